```python
import math
import jax, jax.numpy as jnp
from jax import lax
import numpy as np

D_MODEL = 2048
BATCH = 4
SEQ = 2048
DEPTH = 1
DEC_BATCH = 128
DEC_SEQ = 4
PAST_LEN = 16384
PAGE_SIZE = 128

MIX_WIDTH = D_MODEL
S5_WIDTH = MIX_WIDTH // 2
S5_GROUP = 16
S5_GROUPS = S5_WIDTH // S5_GROUP
S5_STATE = 64
GDN_WIDTH = MIX_WIDTH - S5_WIDTH
GDN_HEAD_DIM = 128
GDN_HEADS = GDN_WIDTH // GDN_HEAD_DIM
GDN_CONV = 4
GDN_CHUNK = 64
D_FF = ((8 * D_MODEL // 3 + 255) // 256) * 256
OFF_QKV = S5_WIDTH
OFF_GATE = OFF_QKV + 3 * GDN_WIDTH
OFF_BETA = OFF_GATE + GDN_WIDTH
OFF_DECAY = OFF_BETA + GDN_HEADS
MIX_IN_COLS = OFF_DECAY + GDN_HEADS
DEEP_ALPHA = (2.0 * DEPTH) ** 0.25
DEEP_BETA = (8.0 * DEPTH) ** -0.25
LN_EPS = 1e-5
NORM_EPS = 1e-6
F32 = jnp.float32

kernel_name = 'hymba_s5_gdn_macaron_deepnorm_step'


def layer_norm(x, g, b):
    x = x.astype(F32)
    mu = jnp.mean(x, -1, keepdims=True)
    var = jnp.mean(jnp.square(x - mu), -1, keepdims=True)
    return (x - mu) * lax.rsqrt(var + LN_EPS) * g.astype(F32) + b.astype(F32)


def swiglu(x, w_in, w_out):
    gate, up = jnp.split(x @ w_in, 2, axis=-1)
    return (jax.nn.silu(gate) * up) @ w_out


def s5_discretize(lam_re, lam_im, log_dt, b_re, b_im):
    dt = jnp.exp(log_dt.astype(F32))[:, None]
    lr, li = lam_re.astype(F32), lam_im.astype(F32)
    mag = jnp.exp(lr * dt)
    ab_re = mag * jnp.cos(li * dt)
    ab_im = mag * jnp.sin(li * dt)
    nr, ni = ab_re - 1.0, ab_im
    den = lr * lr + li * li
    c_re = (nr * lr + ni * li) / den
    c_im = (ni * lr - nr * li) / den
    b_re, b_im = b_re.astype(F32), b_im.astype(F32)
    bb_re = c_re[..., None] * b_re - c_im[..., None] * b_im
    bb_im = c_re[..., None] * b_im + c_im[..., None] * b_re
    return ab_re, ab_im, bb_re, bb_im


def s5_combine(e1, e2):
    a1r, a1i, b1r, b1i = e1
    a2r, a2i, b2r, b2i = e2
    return (a2r * a1r - a2i * a1i, a2r * a1i + a2i * a1r,
            a2r * b1r - a2i * b1i + b2r, a2r * b1i + a2i * b1r + b2i)


def s5_mixer(u, h0_re, h0_im, w):
    bn, t, _ = u.shape
    ug = u.reshape(bn, t, S5_GROUPS, S5_GROUP)
    ab_re, ab_im, bb_re, bb_im = s5_discretize(w['s5_lambda_re'], w['s5_lambda_im'], w['s5_log_dt'], w['s5_b_re'], w['s5_b_im'])
    bu_re = jnp.einsum('btgh,gph->btgp', ug, bb_re)
    bu_im = jnp.einsum('btgh,gph->btgp', ug, bb_im)
    h0_re, h0_im = h0_re.astype(F32), h0_im.astype(F32)
    bu_re = bu_re.at[:, 0].add(ab_re * h0_re - ab_im * h0_im)
    bu_im = bu_im.at[:, 0].add(ab_re * h0_im + ab_im * h0_re)
    a_re = jnp.broadcast_to(ab_re, bu_re.shape)
    a_im = jnp.broadcast_to(ab_im, bu_im.shape)
    _, _, h_re, h_im = lax.associative_scan(s5_combine, (a_re, a_im, bu_re, bu_im), axis=1)
    d = w['s5_d'].astype(F32).reshape(S5_GROUPS, S5_GROUP)
    y = (jnp.einsum('btgp,ghp->btgh', h_re, w['s5_c_re'].astype(F32))
         - jnp.einsum('btgp,ghp->btgh', h_im, w['s5_c_im'].astype(F32)) + d * ug)
    z = jax.nn.gelu(y.reshape(bn, t, S5_WIDTH))
    out = z * jax.nn.sigmoid(z @ w['s5_glu_w'] + w['s5_glu_b'])
    return out, h_re[:, -1], h_im[:, -1]


def causal_short_conv(x, buf, cw):
    t = x.shape[1]
    xp = jnp.concatenate([buf.astype(F32), x], axis=1)
    out = xp[:, 0:t] * cw[0]
    for j in range(1, GDN_CONV):
        out = out + xp[:, j:j + t] * cw[j]
    return out, xp[:, -(GDN_CONV - 1):]


def l2norm(x):
    return x * lax.rsqrt(jnp.sum(x * x, -1, keepdims=True) + NORM_EPS)


def gated_delta_chunked(q, k, v, g, beta, s0):
    bn, t, nh, dk = q.shape
    c = min(GDN_CHUNK, t)
    n = -(-t // c)
    pad = n * c - t

    def blocks(a):
        a = jnp.moveaxis(a, 2, 1)
        if pad:
            a = jnp.pad(a, [(0, 0), (0, 0), (0, pad)] + [(0, 0)] * (a.ndim - 3))
        return a.reshape(a.shape[:2] + (n, c) + a.shape[3:])

    q, k, v, g, beta = (blocks(a) for a in (q, k, v, g, beta))
    gc = jnp.cumsum(g, axis=-1)
    pos = jnp.arange(c)
    causal = pos[:, None] >= pos[None, :]
    strict = (pos[:, None] > pos[None, :]).astype(F32)
    decay = jnp.exp(jnp.where(causal, gc[..., :, None] - gc[..., None, :], -jnp.inf))
    kb = k * beta[..., None]
    m = jnp.einsum('bhnid,bhnjd->bhnij', kb, k) * decay * strict
    eye = jnp.eye(c, dtype=F32)
    tinv = lax.linalg.triangular_solve(eye + m, jnp.broadcast_to(eye, m.shape), left_side=True, lower=True)
    u = tinv @ (v * beta[..., None])
    wk = tinv @ (kb * jnp.exp(gc)[..., None])
    attn = jnp.einsum('bhnid,bhnjd->bhnij', q, k) * decay
    qg = q * jnp.exp(gc)[..., None]
    kd = k * jnp.exp(gc[..., -1:] - gc)[..., None]
    glast = jnp.exp(gc[..., -1])

    def step(s, xs):
        u_i, w_i, qg_i, kd_i, a_i, gl_i = xs
        v_new = u_i - w_i @ s
        o_i = qg_i @ s + a_i @ v_new
        s = s * gl_i[..., None, None] + jnp.einsum('bhcd,bhce->bhde', kd_i, v_new)
        return s, o_i

    xs = tuple(jnp.moveaxis(a, 2, 0) for a in (u, wk, qg, kd, attn, glast))
    s_fin, o = lax.scan(step, s0.astype(F32), xs)
    o = jnp.moveaxis(o, 0, 2).reshape(bn, nh, n * c, -1)[:, :, :t]
    return jnp.moveaxis(o, 1, 2), s_fin


def gdn_mixer(qkv, gate, b_raw, a_raw, s0, conv_buf, w):
    bn, t, _ = qkv.shape
    qkv_c, new_buf = causal_short_conv(qkv, conv_buf, w['gdn_conv_w'].astype(F32))
    q, k, v = jnp.split(jax.nn.silu(qkv_c), 3, axis=-1)
    shp = (bn, t, GDN_HEADS, GDN_HEAD_DIM)
    q = l2norm(q.reshape(shp)) * (GDN_HEAD_DIM ** -0.5)
    k = l2norm(k.reshape(shp))
    v = v.reshape(shp)
    beta = jax.nn.sigmoid(b_raw)
    g = -jnp.exp(w['gdn_a_log'].astype(F32)) * jax.nn.softplus(a_raw + w['gdn_dt_bias'].astype(F32))
    o, s_fin = gated_delta_chunked(q, k, v, g, beta, s0)
    o = o * lax.rsqrt(jnp.mean(o * o, -1, keepdims=True) + NORM_EPS) * w['gdn_norm_w'].astype(F32)
    o = o * jax.nn.silu(gate.reshape(shp))
    return o.reshape(bn, t, GDN_WIDTH), s_fin, new_buf


def decoder_layer(x, s5_re, s5_im, gdn_s, conv_buf, w):
    x = layer_norm(DEEP_ALPHA * x + 0.5 * swiglu(x, w['ffn1_w_in'], w['ffn1_w_out']), w['ln1_g'], w['ln1_b'])
    proj = x @ w['w_mix_in']
    y_s5, n_re, n_im = s5_mixer(proj[..., :S5_WIDTH], s5_re, s5_im, w)
    y_gdn, n_s, n_buf = gdn_mixer(proj[..., OFF_QKV:OFF_GATE], proj[..., OFF_GATE:OFF_BETA],
                                  proj[..., OFF_BETA:OFF_DECAY], proj[..., OFF_DECAY:MIX_IN_COLS],
                                  gdn_s, conv_buf, w)
    mix = jnp.concatenate([y_s5, y_gdn], axis=-1) @ w['w_mix_out']
    x = layer_norm(DEEP_ALPHA * x + mix, w['ln2_g'], w['ln2_b'])
    x = layer_norm(DEEP_ALPHA * x + 0.5 * swiglu(x, w['ffn2_w_in'], w['ffn2_w_out']), w['ln3_g'], w['ln3_b'])
    return x, n_re, n_im, n_s, n_buf


def setup_inputs(seed: int = 0) -> dict:
    key = jax.random.key(seed)
    ks = iter(jax.random.split(key, 40))
    L = DEPTH
    nrm = lambda shape, s: jax.random.normal(next(ks), shape, F32) * s
    gain = lambda shape: 1.0 + nrm(shape, 0.02)
    lam_im = jnp.pi * jnp.arange(S5_STATE, dtype=F32)
    dt = jnp.exp(jax.random.uniform(next(ks), (L, GDN_HEADS), F32, math.log(1e-3), math.log(1e-1)))
    return {
        'x_prompt': nrm((BATCH, SEQ, D_MODEL), 1.0),
        'x_sample': nrm((DEC_BATCH, DEC_SEQ, D_MODEL), 1.0),
        'state_s5_re': nrm((L, DEC_BATCH, S5_GROUPS, S5_STATE), 0.1),
        'state_s5_im': nrm((L, DEC_BATCH, S5_GROUPS, S5_STATE), 0.1),
        'state_gdn': nrm((L, DEC_BATCH, GDN_HEADS, GDN_HEAD_DIM, GDN_HEAD_DIM), 0.3),
        'state_conv': nrm((L, DEC_BATCH, GDN_CONV - 1, 3 * GDN_WIDTH), 1.0),
        'ln1_g': gain((L, D_MODEL)),
        'ln1_b': nrm((L, D_MODEL), 0.02),
        'ffn1_w_in': nrm((L, D_MODEL, 2 * D_FF), D_MODEL ** -0.5),
        'ffn1_w_out': nrm((L, D_FF, D_MODEL), D_FF ** -0.5 * DEEP_BETA),
        'w_mix_in': nrm((L, D_MODEL, MIX_IN_COLS), D_MODEL ** -0.5),
        's5_lambda_re': -0.5 + nrm((L, S5_GROUPS, S5_STATE), 0.01),
        's5_lambda_im': lam_im + nrm((L, S5_GROUPS, S5_STATE), 0.01),
        's5_log_dt': jax.random.uniform(next(ks), (L, S5_GROUPS), F32, math.log(1e-3), math.log(1e-1)),
        's5_b_re': nrm((L, S5_GROUPS, S5_STATE, S5_GROUP), (2 * S5_GROUP) ** -0.5),
        's5_b_im': nrm((L, S5_GROUPS, S5_STATE, S5_GROUP), (2 * S5_GROUP) ** -0.5),
        's5_c_re': nrm((L, S5_GROUPS, S5_GROUP, S5_STATE), (2 * S5_STATE) ** -0.5),
        's5_c_im': nrm((L, S5_GROUPS, S5_GROUP, S5_STATE), (2 * S5_STATE) ** -0.5),
        's5_d': nrm((L, S5_WIDTH), 1.0),
        's5_glu_w': nrm((L, S5_WIDTH, S5_WIDTH), S5_WIDTH ** -0.5),
        's5_glu_b': nrm((L, S5_WIDTH), 0.02),
        'gdn_conv_w': nrm((L, GDN_CONV, 3 * GDN_WIDTH), GDN_CONV ** -0.5),
        'gdn_a_log': jnp.log(jax.random.uniform(next(ks), (L, GDN_HEADS), F32, 1.0, 16.0)),
        'gdn_dt_bias': dt + jnp.log(-jnp.expm1(-dt)),
        'gdn_norm_w': gain((L, GDN_HEAD_DIM)),
        'w_mix_out': nrm((L, MIX_WIDTH, D_MODEL), MIX_WIDTH ** -0.5 * DEEP_BETA),
        'ln2_g': gain((L, D_MODEL)),
        'ln2_b': nrm((L, D_MODEL), 0.02),
        'ffn2_w_in': nrm((L, D_MODEL, 2 * D_FF), D_MODEL ** -0.5),
        'ffn2_w_out': nrm((L, D_FF, D_MODEL), D_FF ** -0.5 * DEEP_BETA),
        'ln3_g': gain((L, D_MODEL)),
        'ln3_b': nrm((L, D_MODEL), 0.02),
    }


def reference(x_prompt, x_sample, state_s5_re, state_s5_im, state_gdn, state_conv,
              ln1_g, ln1_b, ffn1_w_in, ffn1_w_out, w_mix_in,
              s5_lambda_re, s5_lambda_im, s5_log_dt, s5_b_re, s5_b_im, s5_c_re, s5_c_im,
              s5_d, s5_glu_w, s5_glu_b, gdn_conv_w, gdn_a_log, gdn_dt_bias, gdn_norm_w,
              w_mix_out, ln2_g, ln2_b, ffn2_w_in, ffn2_w_out, ln3_g, ln3_b):
    bp = x_prompt.shape[0]
    yp = x_prompt.astype(F32)
    ys = x_sample.astype(F32)
    p_re, p_im, p_s, p_buf = [], [], [], []
    s_re, s_im, s_s, s_buf = [], [], [], []
    for l in range(DEPTH):
        w = {
            'ln1_g': ln1_g[l], 'ln1_b': ln1_b[l], 'ffn1_w_in': ffn1_w_in[l], 'ffn1_w_out': ffn1_w_out[l],
            'w_mix_in': w_mix_in[l], 's5_lambda_re': s5_lambda_re[l], 's5_lambda_im': s5_lambda_im[l],
            's5_log_dt': s5_log_dt[l], 's5_b_re': s5_b_re[l], 's5_b_im': s5_b_im[l],
            's5_c_re': s5_c_re[l], 's5_c_im': s5_c_im[l], 's5_d': s5_d[l],
            's5_glu_w': s5_glu_w[l], 's5_glu_b': s5_glu_b[l], 'gdn_conv_w': gdn_conv_w[l],
            'gdn_a_log': gdn_a_log[l], 'gdn_dt_bias': gdn_dt_bias[l], 'gdn_norm_w': gdn_norm_w[l],
            'w_mix_out': w_mix_out[l], 'ln2_g': ln2_g[l], 'ln2_b': ln2_b[l],
            'ffn2_w_in': ffn2_w_in[l], 'ffn2_w_out': ffn2_w_out[l], 'ln3_g': ln3_g[l], 'ln3_b': ln3_b[l],
        }
        z_ssm = jnp.zeros((bp, S5_GROUPS, S5_STATE), F32)
        z_gdn = jnp.zeros((bp, GDN_HEADS, GDN_HEAD_DIM, GDN_HEAD_DIM), F32)
        z_buf = jnp.zeros((bp, GDN_CONV - 1, 3 * GDN_WIDTH), F32)
        yp, a, b, c, d = decoder_layer(yp, z_ssm, z_ssm, z_gdn, z_buf, w)
        p_re.append(a); p_im.append(b); p_s.append(c); p_buf.append(d)
        ys, a, b, c, d = decoder_layer(ys, state_s5_re[l], state_s5_im[l], state_gdn[l], state_conv[l], w)
        s_re.append(a); s_im.append(b); s_s.append(c); s_buf.append(d)
    return (yp.astype(x_prompt.dtype), ys.astype(x_sample.dtype),
            jnp.stack(p_re), jnp.stack(p_im), jnp.stack(p_s), jnp.stack(p_buf),
            jnp.stack(s_re), jnp.stack(s_im), jnp.stack(s_s), jnp.stack(s_buf))
```

```python
import functools
import math

import jax
import jax.numpy as jnp
from jax import lax
from jax.experimental import pallas as pl
from jax.experimental.pallas import tpu as pltpu

F32 = jnp.float32
BF16 = jnp.bfloat16
HIGHEST = lax.Precision.HIGHEST

D_MODEL = 2048
S5_WIDTH = 1024
S5_GROUP = 16
S5_GROUPS = 64
S5_STATE = 64
S5_HID = S5_GROUPS * S5_STATE
GDN_WIDTH = 1024
GDN_HEAD_DIM = 128
GDN_HEADS = 8
GDN_CONV = 4
D_FF = 5632
MIX_MAIN = 5120
DEEP_ALPHA = 2.0 ** 0.25
LN_EPS = 1e-5
NORM_EPS = 1e-6

SUBLANES = 8
LANES = 128
SLAB = 8
SLAB_REAL = 4
GDN_TILE = 128
S5_TILE = 256
S5_LCHUNK = 256
S5_BLOCKS = 8
VMEM_LIMIT = 56 * 1024 * 1024

NT_DIMS = (((1,), (1,)), ((), ()))


def _dot(a, b, **kw):
    return jnp.dot(a, b, preferred_element_type=F32, **kw)


def _silu(x):
    return x * jax.nn.sigmoid(x)


def _layer_norm(y, g, b):
    mu = jnp.mean(y, axis=-1, keepdims=True)
    d = y - mu
    var = jnp.mean(d * d, axis=-1, keepdims=True)
    return d * lax.rsqrt(var + LN_EPS) * g + b


def _slab_row_selector(rows, nseq, slab_row, first, *, transpose):
    shape = (rows, nseq) if transpose else (nseq, rows)
    r = lax.broadcasted_iota(jnp.int32, shape, 0 if transpose else 1)
    b = lax.broadcasted_iota(jnp.int32, shape, 1 if transpose else 0)
    return (r == SLAB * (b - first) + slab_row).astype(F32)


def _params(*sem):
    return pltpu.CompilerParams(dimension_semantics=sem, vmem_limit_bytes=VMEM_LIMIT)


def _ffn_kernel(x_ref, wg_ref, wu_ref, wo_ref, g_ref, b_ref, o_ref, acc_ref, xb_ref):
    j = pl.program_id(1)

    @pl.when(j == 0)
    def _():
        acc_ref[...] = jnp.zeros_like(acc_ref)
        xb_ref[...] = x_ref[...].astype(BF16)

    xb = xb_ref[...]
    gate = _dot(xb, wg_ref[...])
    up = _dot(xb, wu_ref[...])
    h = _silu(gate) * up
    acc_ref[...] += _dot(h.astype(BF16), wo_ref[...])

    @pl.when(j == pl.num_programs(1) - 1)
    def _():
        y = DEEP_ALPHA * x_ref[...] + 0.5 * acc_ref[...]
        o_ref[...] = _layer_norm(y, g_ref[...], b_ref[...])


def _ffn_ln(x, w_in, w_out, g, b, *, tm=512, tf=512):
    n = x.shape[0]
    nff = D_FF // tf
    return pl.pallas_call(
        _ffn_kernel,
        grid=(n // tm, nff),
        in_specs=[
            pl.BlockSpec((tm, D_MODEL), lambda i, j: (i, 0)),
            pl.BlockSpec((D_MODEL, tf), lambda i, j: (0, j)),
            pl.BlockSpec((D_MODEL, tf), lambda i, j: (0, j + nff)),
            pl.BlockSpec((tf, D_MODEL), lambda i, j: (j, 0)),
            pl.BlockSpec((1, D_MODEL), lambda i, j: (0, 0)),
            pl.BlockSpec((1, D_MODEL), lambda i, j: (0, 0)),
        ],
        out_specs=pl.BlockSpec((tm, D_MODEL), lambda i, j: (i, 0)),
        out_shape=jax.ShapeDtypeStruct((n, D_MODEL), F32),
        scratch_shapes=[pltpu.VMEM((tm, D_MODEL), F32), pltpu.VMEM((tm, D_MODEL), BF16)],
        compiler_params=_params("parallel", "arbitrary"),
        name="ffn_ln",
    )(x, w_in, w_in, w_out, g, b)


def _mixin_kernel(x_ref, w_ref, o_ref):
    o_ref[...] = _dot(x_ref[...].astype(BF16), w_ref[...])


def _mixin(x, w, *, tm=512, tn=1024):
    n = x.shape[0]
    return pl.pallas_call(
        _mixin_kernel,
        grid=(n // tm, MIX_MAIN // tn),
        in_specs=[pl.BlockSpec((tm, D_MODEL), lambda i, j: (i, 0)),
                  pl.BlockSpec((D_MODEL, tn), lambda i, j: (0, j))],
        out_specs=pl.BlockSpec((tm, tn), lambda i, j: (i, j)),
        out_shape=jax.ShapeDtypeStruct((n, MIX_MAIN), F32),
        compiler_params=_params("parallel", "arbitrary"),
        name="mix_in",
    )(x, w)


def _mixin_tail_kernel(x_ref, w_ref, wt_ref, o_ref, ot_ref):
    xb = x_ref[...].astype(BF16)
    o_ref[...] = _dot(xb, w_ref[...])
    ot_ref[...] = lax.dot_general(wt_ref[...], xb, NT_DIMS, preferred_element_type=F32)


def _mixin_tail(x, w_tail, w_tail_t, *, tm=512):
    n = x.shape[0]
    return pl.pallas_call(
        _mixin_tail_kernel,
        grid=(n // tm,),
        in_specs=[pl.BlockSpec((tm, D_MODEL), lambda i: (i, 0)),
                  pl.BlockSpec((D_MODEL, LANES), lambda i: (0, 0)),
                  pl.BlockSpec((2 * GDN_HEADS, D_MODEL), lambda i: (0, 0))],
        out_specs=[pl.BlockSpec((tm, LANES), lambda i: (i, 0)),
                   pl.BlockSpec((2 * GDN_HEADS, tm), lambda i: (0, i))],
        out_shape=[jax.ShapeDtypeStruct((n, LANES), F32),
                   jax.ShapeDtypeStruct((2 * GDN_HEADS, n), F32)],
        compiler_params=_params("parallel"),
        name="mix_in_tail",
    )(x, w_tail, w_tail_t)


def _mixout_kernel(ya_ref, yb_ref, x_ref, w_ref, g_ref, b_ref, o_ref):
    mix = (_dot(ya_ref[...].astype(BF16), w_ref[0:S5_WIDTH, :])
           + _dot(yb_ref[...].astype(BF16), w_ref[S5_WIDTH:, :]))
    o_ref[...] = _layer_norm(DEEP_ALPHA * x_ref[...] + mix, g_ref[...], b_ref[...])


def _mixout_ln(ya, yb, x, w, g, b, *, tm=512):
    n = x.shape[0]
    return pl.pallas_call(
        _mixout_kernel,
        grid=(n // tm,),
        in_specs=[pl.BlockSpec((tm, S5_WIDTH), lambda i: (i, 0)),
                  pl.BlockSpec((tm, GDN_WIDTH), lambda i: (i, 0)),
                  pl.BlockSpec((tm, D_MODEL), lambda i: (i, 0)),
                  pl.BlockSpec((D_MODEL, D_MODEL), lambda i: (0, 0)),
                  pl.BlockSpec((1, D_MODEL), lambda i: (0, 0)),
                  pl.BlockSpec((1, D_MODEL), lambda i: (0, 0))],
        out_specs=pl.BlockSpec((tm, D_MODEL), lambda i: (i, 0)),
        out_shape=jax.ShapeDtypeStruct((n, D_MODEL), F32),
        compiler_params=_params("parallel"),
        name="mix_out_ln",
    )(ya, yb, x, w, g, b)


def _s5_disc_kernel(lre_ref, lim_ref, ldt_ref, c_ref, coef_ref):
    lr, li = lre_ref[...], lim_ref[...]
    dt = jnp.exp(ldt_ref[...])
    mag = jnp.exp(lr * dt)
    ar = mag * jnp.cos(li * dt)
    ai = mag * jnp.sin(li * dt)
    nr, ni = ar - 1.0, ai
    den = lr * lr + li * li
    c_ref[0:1, :] = (nr * lr + ni * li) / den
    c_ref[1:2, :] = (ni * lr - nr * li) / den

    def cmul(x, y):
        return x[0] * y[0] - x[1] * y[1], x[0] * y[1] + x[1] * y[0]

    a1 = (ar, ai)
    pows = [a1]
    for _ in range(SUBLANES - 1):
        pows.append(cmul(pows[-1], a1))
    coef_ref[...] = jnp.zeros_like(coef_ref)
    width = lr.shape[-1]
    for k, s in enumerate((1, 2, 4)):
        for part in range(2):
            coef_ref[2 * k + part, s:SUBLANES, :] = jnp.broadcast_to(pows[s - 1][part], (SUBLANES - s, width))
    for r in range(SUBLANES):
        coef_ref[6, r:r + 1, :] = pows[r][0]
        coef_ref[7, r:r + 1, :] = pows[r][1]


def _s5_disc(lre, lim, ldt):
    return pl.pallas_call(
        _s5_disc_kernel,
        out_shape=[jax.ShapeDtypeStruct((2, S5_HID), F32),
                   jax.ShapeDtypeStruct((8, SUBLANES, S5_HID), F32)],
        name="s5_disc",
    )(lre, lim, ldt)


def _gelu_tanh(y):
    return 0.5 * y * (1.0 + jnp.tanh(math.sqrt(2.0 / math.pi) * (y + 0.044715 * (y * y * y))))


def _s5_kernel(u_ref, h0re_ref, h0im_ref, c_ref, coef_ref, bre_ref, bim_ref, cwre_ref, cwim_ref,
               d_ref, gw_ref, gb_ref, y_ref, sre_ref, sim_ref, hre, him, ysc, car_re, car_im,
               *, chain, tt):
    u = u_ref[...]
    ub = u.astype(BF16)
    bw = S5_HID // S5_BLOCKS
    gw = S5_WIDTH // S5_BLOCKS
    for j in range(S5_BLOCKS):
        uj = ub[:, j * gw:(j + 1) * gw]
        r = _dot(uj, bre_ref[j])
        im = _dot(uj, bim_ref[j])
        cr = c_ref[0:1, j * bw:(j + 1) * bw]
        ci = c_ref[1:2, j * bw:(j + 1) * bw]
        hre[:, j * bw:(j + 1) * bw] = cr * r - ci * im
        him[:, j * bw:(j + 1) * bw] = cr * im + ci * r

    nslab = tt // SLAB
    if chain:
        @pl.when(pl.program_id(1) == 0)
        def _():
            car_re[...] = jnp.broadcast_to(h0re_ref[...], car_re.shape)
            car_im[...] = jnp.broadcast_to(h0im_ref[...], car_im.shape)
    else:
        first = (pl.program_id(0) % (h0re_ref.shape[0] // nslab)) * nslab
        place = _slab_row_selector(tt, h0re_ref.shape[0], SLAB - SLAB_REAL - 1, first, transpose=True)
        hre[...] += _dot(place, h0re_ref[...], precision=HIGHEST)
        him[...] += _dot(place, h0im_ref[...], precision=HIGHEST)

    for c in range(S5_HID // S5_LCHUNK):
        sl = slice(c * S5_LCHUNK, (c + 1) * S5_LCHUNK)
        steps = [(coef_ref[2 * k, :, sl], coef_ref[2 * k + 1, :, sl], s) for k, s in enumerate((1, 2, 4))]
        pr, pi = coef_ref[6, :, sl], coef_ref[7, :, sl]

        def body(rb, carry, sl=sl, steps=steps, pr=pr, pi=pi):
            rows = pl.ds(pl.multiple_of(rb * SUBLANES, SUBLANES), SUBLANES)
            xr, xi = hre[rows, sl], him[rows, sl]
            for a_r, a_i, s in steps:
                sr = pltpu.roll(xr, s, 0)
                si = pltpu.roll(xi, s, 0)
                xr, xi = xr + a_r * sr - a_i * si, xi + a_r * si + a_i * sr
            if chain:
                kr, ki = carry
                xr, xi = xr + pr * kr - pi * ki, xi + pr * ki + pi * kr
                carry = (jnp.broadcast_to(xr[SUBLANES - 1:SUBLANES, :], xr.shape),
                         jnp.broadcast_to(xi[SUBLANES - 1:SUBLANES, :], xi.shape))
            hre[rows, sl] = xr
            him[rows, sl] = xi
            return carry

        if chain:
            carry = lax.fori_loop(0, tt // SUBLANES, body, (car_re[:, sl], car_im[:, sl]))
            car_re[:, sl] = carry[0]
            car_im[:, sl] = carry[1]
        else:
            lax.fori_loop(0, tt // SUBLANES, body, 0)

    if chain:
        @pl.when(pl.program_id(1) == pl.num_programs(1) - 1)
        def _():
            sre_ref[...] = car_re[0:1, :]
            sim_ref[...] = car_im[0:1, :]
    else:
        last = _slab_row_selector(tt, nslab, SLAB - 1, 0, transpose=False)
        sre_ref[...] = _dot(last, hre[...], precision=HIGHEST)
        sim_ref[...] = _dot(last, him[...], precision=HIGHEST)

    for j in range(S5_BLOCKS):
        hr = hre[:, j * bw:(j + 1) * bw].astype(BF16)
        hi = him[:, j * bw:(j + 1) * bw].astype(BF16)
        ysc[:, j * gw:(j + 1) * gw] = _dot(hr, cwre_ref[j]) - _dot(hi, cwim_ref[j])
    z = _gelu_tanh(ysc[...] + d_ref[...] * u)
    gl = _dot(z.astype(BF16), gw_ref[...]) + gb_ref[...]
    y_ref[...] = z * jax.nn.sigmoid(gl)


def _s5_mixer(proj, h0re, h0im, sw, *, chain, nseq, tt=S5_TILE):
    rows = proj.shape[0]
    const = lambda shape: pl.BlockSpec(shape, lambda *_: (0,) * len(shape))
    if chain:
        nt = rows // nseq // tt
        grid = (nseq, nt)
        u_spec = pl.BlockSpec((tt, S5_WIDTH), lambda b, t: (b * nt + t, 0))
        y_spec = pl.BlockSpec((tt, S5_WIDTH), lambda b, t: (b * nt + t, 0))
        st_spec = pl.BlockSpec((None, 1, S5_HID), lambda b, t: (b, 0, 0))
        st_shape = jax.ShapeDtypeStruct((nseq, 1, S5_HID), F32)
        h0_spec = st_spec
        sem = ("parallel", "arbitrary")
    else:
        grid = (rows // tt,)
        u_spec = pl.BlockSpec((tt, S5_WIDTH), lambda i: (i, 0))
        y_spec = pl.BlockSpec((tt, S5_WIDTH), lambda i: (i, 0))
        st_spec = pl.BlockSpec((tt // SLAB, S5_HID), lambda i: (i, 0))
        st_shape = jax.ShapeDtypeStruct((rows // SLAB, S5_HID), F32)
        per = LANES // (tt // SLAB)
        h0_spec = pl.BlockSpec((LANES, S5_HID), lambda i: (i // per, 0))
        sem = ("parallel",)
    bw = S5_HID // S5_BLOCKS
    gw = S5_WIDTH // S5_BLOCKS
    return pl.pallas_call(
        functools.partial(_s5_kernel, chain=chain, tt=tt),
        grid=grid,
        in_specs=[u_spec, h0_spec, h0_spec,
                  const((2, S5_HID)), const((8, SUBLANES, S5_HID)),
                  const((S5_BLOCKS, gw, bw)), const((S5_BLOCKS, gw, bw)),
                  const((S5_BLOCKS, bw, gw)), const((S5_BLOCKS, bw, gw)),
                  const((1, S5_WIDTH)), const((S5_WIDTH, S5_WIDTH)), const((1, S5_WIDTH))],
        out_specs=[y_spec, st_spec, st_spec],
        out_shape=[jax.ShapeDtypeStruct((rows, S5_WIDTH), F32), st_shape, st_shape],
        scratch_shapes=[pltpu.VMEM((tt, S5_HID), F32), pltpu.VMEM((tt, S5_HID), F32),
                        pltpu.VMEM((tt, S5_WIDTH), F32),
                        pltpu.VMEM((SUBLANES, S5_HID), F32), pltpu.VMEM((SUBLANES, S5_HID), F32)],
        compiler_params=_params(*sem),
        name="s5_chain" if chain else "s5_slab",
    )(proj, h0re, h0im, sw["c"], sw["coef"], sw["b_re"], sw["b_im"], sw["c_re"], sw["c_im"],
      sw["d"], sw["glu_w"], sw["glu_b"])


def _softplus(x):
    return jnp.maximum(x, 0.0) + jnp.log1p(jnp.exp(-jnp.abs(x)))


def _segment_masks(seg):
    ri = lax.broadcasted_iota(jnp.int32, (GDN_TILE, GDN_TILE), 0)
    ci = lax.broadcasted_iota(jnp.int32, (GDN_TILE, GDN_TILE), 1)
    same = (ri // seg) == (ci // seg)
    causal = (ri >= ci) & same
    strict = (ri > ci) & same
    return same, causal, strict


def _gdn_gates(bd, bdt, alog_row, dtb_row, alog_col, dtb_col, seg, slab):
    same, causal, strict = _segment_masks(seg)
    beta = jax.nn.sigmoid(bd)
    g = -jnp.exp(alog_row) * _softplus(bd + dtb_row)
    gt = -jnp.exp(alog_col) * _softplus(bdt[GDN_HEADS:, :] + dtb_col)
    if slab:
        rreal = (lax.broadcasted_iota(jnp.int32, (GDN_TILE, 1), 0) % SLAB) >= SLAB - SLAB_REAL
        creal = (lax.broadcasted_iota(jnp.int32, (1, GDN_TILE), 1) % SLAB) >= SLAB - SLAB_REAL
        beta = jnp.where(rreal, beta, 0.0)
        g = jnp.where(rreal, g, 0.0)
        gt = jnp.where(creal, gt, 0.0)
    gc = _dot(causal.astype(F32), g, precision=HIGHEST)
    gl = _dot(same.astype(F32), g, precision=HIGHEST)
    ri = lax.broadcasted_iota(jnp.int32, (GDN_TILE, GDN_TILE), 0)
    ci = lax.broadcasted_iota(jnp.int32, (GDN_TILE, GDN_TILE), 1)
    upper = (ri <= ci) & same
    gct = _dot(gt, upper.astype(F32), precision=HIGHEST)
    return beta, gc, gl, gct, causal, strict


def _merge_masks(top):
    ri = lax.broadcasted_iota(jnp.int32, (GDN_TILE, GDN_TILE), 0)
    ci = lax.broadcasted_iota(jnp.int32, (GDN_TILE, GDN_TILE), 1)
    masks = []
    s = 1
    while s < top:
        masks.append(((ri // (2 * s)) == (ci // (2 * s))) & ((ri // s) != (ci // s)))
        s *= 2
    return masks


def _unit_lower_inverse(m, masks):
    e = -jnp.where(masks[0], m, 0.0)
    for mask in masks[1:]:
        c = jnp.where(mask, m, 0.0)
        eb = e.astype(BF16)
        x = c + _dot(eb, c.astype(BF16))
        e = e - (x + _dot(x.astype(BF16), eb))
    return e


def _gdn_intra(q, k, v, beta, gcol, grow, glcol, causal, strict, masks):
    qn = q * lax.rsqrt(jnp.sum(q * q, axis=-1, keepdims=True) + NORM_EPS) * (GDN_HEAD_DIM ** -0.5)
    kn = k * lax.rsqrt(jnp.sum(k * k, axis=-1, keepdims=True) + NORM_EPS)
    decay = jnp.exp(jnp.where(causal, gcol - grow, -jnp.inf))
    kb = kn * beta
    knb = kn.astype(BF16)
    kk = lax.dot_general(kb.astype(BF16), knb, NT_DIMS, preferred_element_type=F32)
    y = _unit_lower_inverse(kk * jnp.where(strict, decay, 0.0), masks)
    eg = jnp.exp(gcol)
    rhs = jnp.concatenate([v * beta, kb * eg], axis=1)
    uw = rhs + _dot(y.astype(BF16), rhs.astype(BF16))
    u, wk = uw[:, :GDN_HEAD_DIM], uw[:, GDN_HEAD_DIM:]
    attn = lax.dot_general(qn.astype(BF16), knb, NT_DIMS, preferred_element_type=F32) * decay
    qg = qn * eg
    kd = kn * jnp.exp(glcol - gcol)
    return u, wk, attn, qg, kd


def _gdn_out(o, gate, norm_w):
    o = o * lax.rsqrt(jnp.mean(o * o, axis=-1, keepdims=True) + NORM_EPS) * norm_w
    return o * _silu(gate)


def _conv4(win_ref, cw, width_slice):
    out = cw[0:1, :] * win_ref[pl.ds(SUBLANES - 3, GDN_TILE), width_slice]
    for j in range(1, GDN_CONV):
        out = out + cw[j:j + 1, :] * win_ref[pl.ds(SUBLANES - 3 + j, GDN_TILE), width_slice]
    return out


def _gdn_chain_kernel(q_ref, k_ref, v_ref, gate_ref, bd_ref, bdt_ref, buf_ref, s0_ref, cw_ref,
                      alr_ref, dbr_ref, alc_ref, dbc_ref, nw_ref,
                      y_ref, sout_ref, bufout_ref, win, s_sc):
    t = pl.program_id(1)
    w3 = 3 * GDN_WIDTH

    @pl.when(t == 0)
    def _():
        win[0:SUBLANES, :] = jnp.zeros((SUBLANES, w3), F32)
        win[SUBLANES - 3:SUBLANES, :] = buf_ref[...]
        s_sc[...] = s0_ref[...]

    win[SUBLANES:, 0:GDN_WIDTH] = q_ref[...]
    win[SUBLANES:, GDN_WIDTH:2 * GDN_WIDTH] = k_ref[...]
    win[SUBLANES:, 2 * GDN_WIDTH:] = v_ref[...]

    beta, gc, gl, gct, causal, strict = _gdn_gates(
        bd_ref[...], bdt_ref[...], alr_ref[...], dbr_ref[...], alc_ref[...], dbc_ref[...],
        GDN_TILE, False)
    masks = _merge_masks(GDN_TILE)
    for h in range(GDN_HEADS):
        hs = slice(h * GDN_HEAD_DIM, (h + 1) * GDN_HEAD_DIM)
        qkv = []
        for part in range(3):
            cs = slice(part * GDN_WIDTH + h * GDN_HEAD_DIM, part * GDN_WIDTH + (h + 1) * GDN_HEAD_DIM)
            qkv.append(_silu(_conv4(win, cw_ref[:, cs], cs)))
        gcol = gc[:, GDN_HEADS + h:GDN_HEADS + h + 1]
        glcol = gl[:, GDN_HEADS + h:GDN_HEADS + h + 1]
        u, wk, attn, qg, kd = _gdn_intra(qkv[0], qkv[1], qkv[2], beta[:, h:h + 1], gcol,
                                         gct[h:h + 1, :], glcol, causal, strict, masks)
        s = s_sc[h]
        a = _dot(jnp.concatenate([wk, qg], axis=0).astype(BF16), s.astype(BF16))
        v_new = u - a[:GDN_TILE]
        vnb = v_new.astype(BF16)
        o = a[GDN_TILE:] + _dot(attn.astype(BF16), vnb)
        s_sc[h] = s * jnp.exp(glcol[0:1, :]) + _dot(kd.T.astype(BF16), vnb)
        y_ref[:, hs] = _gdn_out(o, gate_ref[:, hs], nw_ref[...])

    win[0:SUBLANES, :] = win[GDN_TILE:GDN_TILE + SUBLANES, :]

    @pl.when(t == pl.num_programs(1) - 1)
    def _():
        sout_ref[...] = s_sc[...]
        bufout_ref[...] = win[GDN_TILE + SUBLANES - 3:GDN_TILE + SUBLANES, :]


def _gdn_chain(proj, bd, bdt, conv_buf, s0, gw, *, nseq):
    rows = proj.shape[0]
    nt = rows // nseq // GDN_TILE
    w3 = 3 * GDN_WIDTH
    const = lambda shape: pl.BlockSpec(shape, lambda *_: (0,) * len(shape))
    col = lambda cb: pl.BlockSpec((GDN_TILE, GDN_WIDTH), lambda b, t, cb=cb: (b * nt + t, cb))
    return pl.pallas_call(
        _gdn_chain_kernel,
        grid=(nseq, nt),
        in_specs=[col(1), col(2), col(3), col(4),
                  pl.BlockSpec((GDN_TILE, LANES), lambda b, t: (b * nt + t, 0)),
                  pl.BlockSpec((2 * GDN_HEADS, GDN_TILE), lambda b, t: (0, b * nt + t)),
                  pl.BlockSpec((None, GDN_CONV - 1, w3), lambda b, t: (b, 0, 0)),
                  pl.BlockSpec((None, GDN_HEADS, GDN_HEAD_DIM, GDN_HEAD_DIM), lambda b, t: (b, 0, 0, 0)),
                  const((GDN_CONV, w3)), const((1, LANES)), const((1, LANES)),
                  const((GDN_HEADS, 1)), const((GDN_HEADS, 1)), const((1, GDN_HEAD_DIM))],
        out_specs=[pl.BlockSpec((GDN_TILE, GDN_WIDTH), lambda b, t: (b * nt + t, 0)),
                   pl.BlockSpec((None, GDN_HEADS, GDN_HEAD_DIM, GDN_HEAD_DIM), lambda b, t: (b, 0, 0, 0)),
                   pl.BlockSpec((None, GDN_CONV - 1, w3), lambda b, t: (b, 0, 0))],
        out_shape=[jax.ShapeDtypeStruct((rows, GDN_WIDTH), F32),
                   jax.ShapeDtypeStruct((nseq, GDN_HEADS, GDN_HEAD_DIM, GDN_HEAD_DIM), F32),
                   jax.ShapeDtypeStruct((nseq, GDN_CONV - 1, w3), F32)],
        scratch_shapes=[pltpu.VMEM((GDN_TILE + SUBLANES, w3), F32),
                        pltpu.VMEM((GDN_HEADS, GDN_HEAD_DIM, GDN_HEAD_DIM), F32)],
        compiler_params=_params("parallel", "arbitrary"),
        name="gdn_chain",
    )(proj, proj, proj, proj, bd, bdt, conv_buf, s0, gw["conv_w"], gw["alog_row"], gw["dtb_row"],
      gw["alog_col"], gw["dtb_col"], gw["norm_w"])


def _gdn_slab_kernel(q_ref, k_ref, v_ref, gate_ref, bd_ref, bdt_ref, bq_ref, bk_ref, bv_ref, s0_ref,
                     cwq_ref, cwk_ref, cwv_ref, alr_ref, dbr_ref, alc_ref, dbc_ref, nw_ref,
                     y_ref, sout_ref, oq_ref, ok_ref, ov_ref,
                     win, wkqg, u_sc, vn_sc, qgs_sc, kdt_sc, egl_sc):
    h = pl.program_id(1)
    nslab = GDN_TILE // SLAB
    first = (pl.program_id(0) % (bq_ref.shape[1] // nslab)) * nslab
    real = (lax.broadcasted_iota(jnp.int32, (GDN_TILE, 1), 0) % SLAB) >= SLAB - SLAB_REAL
    qkv = []
    for part, (x_ref, b_ref, cw_ref, o_ref) in enumerate(
            ((q_ref, bq_ref, cwq_ref, oq_ref), (k_ref, bk_ref, cwk_ref, ok_ref), (v_ref, bv_ref, cwv_ref, ov_ref))):
        cs = slice(part * GDN_HEAD_DIM, (part + 1) * GDN_HEAD_DIM)
        x = x_ref[...]
        for i in range(GDN_CONV - 1):
            place = _slab_row_selector(GDN_TILE, b_ref.shape[1], 1 + i, first, transpose=True)
            x = x + _dot(place, b_ref[i], precision=HIGHEST)
        win[0:SUBLANES, cs] = jnp.zeros((SUBLANES, GDN_HEAD_DIM), F32)
        win[SUBLANES:, cs] = x
        conv = _conv4(win, cw_ref[...], cs)
        qkv.append(jnp.where(real, _silu(conv), 0.0))
        for i in range(GDN_CONV - 1):
            take = _slab_row_selector(GDN_TILE, nslab, SLAB - 3 + i, 0, transpose=False)
            o_ref[i] = _dot(take, x, precision=HIGHEST)

    beta, gc, gl, gct, causal, strict = _gdn_gates(
        bd_ref[...], bdt_ref[...], alr_ref[...], dbr_ref[...], alc_ref[...], dbc_ref[...], SLAB, True)
    lane = lax.broadcasted_iota(jnp.int32, (1, LANES), 1)
    sub = lax.broadcasted_iota(jnp.int32, (SUBLANES, 1), 0)
    pick = lambda a, idx: jnp.sum(jnp.where(lane == idx, a, 0.0), axis=1, keepdims=True)
    beta_col = pick(beta, h)
    gcol = pick(gc, GDN_HEADS + h)
    glcol = pick(gl, GDN_HEADS + h)
    grow = jnp.sum(jnp.where(sub == h, gct, 0.0), axis=0, keepdims=True)
    masks = _merge_masks(SLAB_REAL)
    u, wk, attn, qg, kd = _gdn_intra(qkv[0], qkv[1], qkv[2], beta_col, gcol, grow, glcol,
                                     causal, strict, masks)
    wkqg[0] = wk
    wkqg[1] = qg
    u_sc[...] = u
    kdt_sc[...] = kd.T
    egl_sc[...] = jnp.broadcast_to(jnp.exp(glcol), egl_sc.shape)

    def apply_state(i, _):
        rows = pl.ds(pl.multiple_of(i * SLAB, SLAB), SLAB)
        lhs = jnp.concatenate([wkqg[0, rows, :], wkqg[1, rows, :]], axis=0).astype(BF16)
        res = _dot(lhs, s0_ref[i, 0].astype(BF16))
        vn_sc[rows, :] = u_sc[rows, :] - res[:SLAB]
        qgs_sc[rows, :] = res[SLAB:]
        return 0

    lax.fori_loop(0, nslab, apply_state, 0)
    v_new = vn_sc[...]
    vnb = v_new.astype(BF16)
    o = qgs_sc[...] + _dot(attn.astype(BF16), vnb)
    y_ref[...] = _gdn_out(o, gate_ref[...], nw_ref[...])

    def update_state(i, _):
        in_slab = (lane // SLAB) == i
        lhs = jnp.where(in_slab, kdt_sc[...], 0.0).astype(BF16)
        upd = _dot(lhs, vn_sc[...].astype(BF16))
        egl = egl_sc[pl.ds(pl.multiple_of(i * SLAB, SLAB), 1), :]
        sout_ref[i, 0] = s0_ref[i, 0] * egl + upd
        return 0

    lax.fori_loop(0, nslab, update_state, 0)


def _gdn_slab(proj, bd, bdt, conv_buf, s0, gw):
    rows = proj.shape[0]
    nb = rows // SLAB
    nslab = GDN_TILE // SLAB
    hb = GDN_WIDTH // GDN_HEAD_DIM
    const = lambda shape: pl.BlockSpec(shape, lambda *_: (0,) * len(shape))
    col = lambda g: pl.BlockSpec((GDN_TILE, GDN_HEAD_DIM), lambda i, h, g=g: (i, g * hb + h))
    per = LANES // nslab
    buf = lambda g: pl.BlockSpec((GDN_CONV - 1, LANES, GDN_HEAD_DIM), lambda i, h, g=g: (0, i // per, g * hb + h))
    cwb = lambda g: pl.BlockSpec((GDN_CONV, GDN_HEAD_DIM), lambda i, h, g=g: (0, g * hb + h))
    st = pl.BlockSpec((nslab, 1, GDN_HEAD_DIM, GDN_HEAD_DIM), lambda i, h: (i, h, 0, 0))
    obuf = pl.BlockSpec((GDN_CONV - 1, nslab, GDN_HEAD_DIM), lambda i, h: (0, i, h))
    tile = (GDN_TILE, GDN_HEAD_DIM)
    return pl.pallas_call(
        _gdn_slab_kernel,
        grid=(rows // GDN_TILE, GDN_HEADS),
        in_specs=[col(1), col(2), col(3), col(4),
                  pl.BlockSpec((GDN_TILE, LANES), lambda i, h: (i, 0)),
                  pl.BlockSpec((2 * GDN_HEADS, GDN_TILE), lambda i, h: (0, i)),
                  buf(0), buf(1), buf(2), st, cwb(0), cwb(1), cwb(2),
                  const((1, LANES)), const((1, LANES)), const((GDN_HEADS, 1)), const((GDN_HEADS, 1)),
                  const((1, GDN_HEAD_DIM))],
        out_specs=[pl.BlockSpec(tile, lambda i, h: (i, h)), st, obuf, obuf, obuf],
        out_shape=[jax.ShapeDtypeStruct((rows, GDN_WIDTH), F32),
                   jax.ShapeDtypeStruct(s0.shape, F32)]
                  + [jax.ShapeDtypeStruct((GDN_CONV - 1, nb, GDN_WIDTH), F32)] * 3,
        scratch_shapes=[pltpu.VMEM((GDN_TILE + SUBLANES, 3 * GDN_HEAD_DIM), F32),
                        pltpu.VMEM((2,) + tile, F32), pltpu.VMEM(tile, F32), pltpu.VMEM(tile, F32),
                        pltpu.VMEM(tile, F32), pltpu.VMEM(tile, F32), pltpu.VMEM(tile, F32)],
        compiler_params=_params("parallel", "arbitrary"),
        name="gdn_slab",
    )(proj, proj, proj, proj, bd, bdt, conv_buf, conv_buf, conv_buf, s0,
      gw["conv_w"], gw["conv_w"], gw["conv_w"], gw["alog_row"], gw["dtb_row"],
      gw["alog_col"], gw["dtb_col"], gw["norm_w"])


def _block_diag(w):
    per = S5_GROUPS // S5_BLOCKS
    g, a, b = w.shape
    w = w.reshape(S5_BLOCKS, per, a, b)
    eye = jnp.eye(per, dtype=w.dtype)
    return jnp.einsum("jgab,gk->jgakb", w, eye).reshape(S5_BLOCKS, per * a, per * b)


def _lane_pad(v, offset):
    return jnp.zeros((1, LANES), F32).at[0, offset:offset + v.shape[0]].set(v.astype(F32))


def _layer_weights(l, ln1_g, ln1_b, ffn1_w_in, ffn1_w_out, w_mix_in, s5_lambda_re, s5_lambda_im, s5_log_dt,
                   s5_b_re, s5_b_im, s5_c_re, s5_c_im, s5_d, s5_glu_w, s5_glu_b, gdn_conv_w, gdn_a_log,
                   gdn_dt_bias, gdn_norm_w, w_mix_out, ln2_g, ln2_b, ffn2_w_in, ffn2_w_out, ln3_g, ln3_b):
    row = lambda v: v[l].reshape(1, -1).astype(F32)
    w = {
        "ln1": (row(ln1_g), row(ln1_b)), "ln2": (row(ln2_g), row(ln2_b)), "ln3": (row(ln3_g), row(ln3_b)),
        "ffn1": (ffn1_w_in[l].astype(BF16), ffn1_w_out[l].astype(BF16)),
        "ffn2": (ffn2_w_in[l].astype(BF16), ffn2_w_out[l].astype(BF16)),
        "mix_in": w_mix_in[l][:, :MIX_MAIN].astype(BF16),
        "mix_out": w_mix_out[l].astype(BF16),
    }
    tail = w_mix_in[l][:, MIX_MAIN:]
    w["mix_tail"] = jnp.pad(tail, ((0, 0), (0, LANES - tail.shape[1]))).astype(BF16)
    w["mix_tail_t"] = tail.T.astype(BF16)
    c, coef = _s5_disc(s5_lambda_re[l].reshape(1, S5_HID), s5_lambda_im[l].reshape(1, S5_HID),
                       jnp.repeat(s5_log_dt[l], S5_STATE).reshape(1, S5_HID))
    w["s5"] = {
        "c": c, "coef": coef,
        "b_re": _block_diag(jnp.swapaxes(s5_b_re[l], 1, 2)).astype(BF16),
        "b_im": _block_diag(jnp.swapaxes(s5_b_im[l], 1, 2)).astype(BF16),
        "c_re": _block_diag(jnp.swapaxes(s5_c_re[l], 1, 2)).astype(BF16),
        "c_im": _block_diag(jnp.swapaxes(s5_c_im[l], 1, 2)).astype(BF16),
        "d": row(s5_d), "glu_w": s5_glu_w[l].astype(BF16), "glu_b": row(s5_glu_b),
    }
    w["gdn"] = {
        "conv_w": gdn_conv_w[l].astype(F32),
        "alog_row": _lane_pad(gdn_a_log[l], GDN_HEADS), "dtb_row": _lane_pad(gdn_dt_bias[l], GDN_HEADS),
        "alog_col": gdn_a_log[l].reshape(GDN_HEADS, 1).astype(F32),
        "dtb_col": gdn_dt_bias[l].reshape(GDN_HEADS, 1).astype(F32),
        "norm_w": row(gdn_norm_w),
    }
    return w


def _mix_projections(x, w):
    proj = _mixin(x, w["mix_in"])
    bd, bdt = _mixin_tail(x, w["mix_tail"], w["mix_tail_t"])
    return proj, bd, bdt


def _prompt_layer(x, w, nseq):
    x = _ffn_ln(x, *w["ffn1"], *w["ln1"])
    proj, bd, bdt = _mix_projections(x, w)
    z_s5 = jnp.zeros((nseq, 1, S5_HID), F32)
    y_s5, n_re, n_im = _s5_mixer(proj, z_s5, z_s5, w["s5"], chain=True, nseq=nseq)
    z_gdn = jnp.zeros((nseq, GDN_HEADS, GDN_HEAD_DIM, GDN_HEAD_DIM), F32)
    z_buf = jnp.zeros((nseq, GDN_CONV - 1, 3 * GDN_WIDTH), F32)
    y_gdn, n_s, n_buf = _gdn_chain(proj, bd, bdt, z_buf, z_gdn, w["gdn"], nseq=nseq)
    x = _mixout_ln(y_s5, y_gdn, x, w["mix_out"], *w["ln2"])
    x = _ffn_ln(x, *w["ffn2"], *w["ln3"])
    shape = (nseq, S5_GROUPS, S5_STATE)
    return x, n_re.reshape(shape), n_im.reshape(shape), n_s, n_buf


def _sample_layer(x, s5_re, s5_im, gdn_s, conv_buf, w, nb, t):
    x = _ffn_ln(x, *w["ffn1"], *w["ln1"])
    xs = jnp.pad(x.reshape(nb, t, D_MODEL), ((0, 0), (SLAB - t, 0), (0, 0))).reshape(nb * SLAB, D_MODEL)
    proj, bd, bdt = _mix_projections(xs, w)
    y_s5, n_re, n_im = _s5_mixer(proj, s5_re.reshape(nb, S5_HID).astype(F32),
                                 s5_im.reshape(nb, S5_HID).astype(F32), w["s5"], chain=False, nseq=nb)
    y_gdn, n_s, bq, bk, bv = _gdn_slab(proj, bd, bdt, jnp.swapaxes(conv_buf.astype(F32), 0, 1),
                                       gdn_s.astype(F32), w["gdn"])
    xs = _mixout_ln(y_s5, y_gdn, xs, w["mix_out"], *w["ln2"])
    x = xs.reshape(nb, SLAB, D_MODEL)[:, SLAB - t:].reshape(nb * t, D_MODEL)
    x = _ffn_ln(x, *w["ffn2"], *w["ln3"])
    shape = (nb, S5_GROUPS, S5_STATE)
    n_buf = jnp.swapaxes(jnp.concatenate([bq, bk, bv], axis=-1), 0, 1)
    return x, n_re.reshape(shape), n_im.reshape(shape), n_s, n_buf


def kernel(x_prompt, x_sample, state_s5_re, state_s5_im, state_gdn, state_conv, ln1_g, ln1_b, ffn1_w_in, ffn1_w_out, w_mix_in, s5_lambda_re, s5_lambda_im, s5_log_dt, s5_b_re, s5_b_im, s5_c_re, s5_c_im, s5_d, s5_glu_w, s5_glu_b, gdn_conv_w, gdn_a_log, gdn_dt_bias, gdn_norm_w, w_mix_out, ln2_g, ln2_b, ffn2_w_in, ffn2_w_out, ln3_g, ln3_b):
    bp, tp, _ = x_prompt.shape
    bs, ts, _ = x_sample.shape
    assert ts == SLAB_REAL and tp % S5_TILE == 0 and (bs * SLAB) % S5_TILE == 0
    depth = ln1_g.shape[0]
    yp = x_prompt.astype(F32).reshape(bp * tp, D_MODEL)
    ys = x_sample.astype(F32).reshape(bs * ts, D_MODEL)
    outs = [[] for _ in range(8)]
    for l in range(depth):
        w = _layer_weights(l, ln1_g, ln1_b, ffn1_w_in, ffn1_w_out, w_mix_in, s5_lambda_re, s5_lambda_im,
                           s5_log_dt, s5_b_re, s5_b_im, s5_c_re, s5_c_im, s5_d, s5_glu_w, s5_glu_b,
                           gdn_conv_w, gdn_a_log, gdn_dt_bias, gdn_norm_w, w_mix_out, ln2_g, ln2_b,
                           ffn2_w_in, ffn2_w_out, ln3_g, ln3_b)
        yp, *p_state = _prompt_layer(yp, w, bp)
        ys, *s_state = _sample_layer(ys, state_s5_re[l], state_s5_im[l], state_gdn[l], state_conv[l], w, bs, ts)
        for acc, val in zip(outs, p_state + s_state):
            acc.append(val)
    return (yp.reshape(x_prompt.shape).astype(x_prompt.dtype), ys.reshape(x_sample.shape).astype(x_sample.dtype),
            *(jnp.stack(o) for o in outs))
```

```python
import functools
import math

import jax
import jax.numpy as jnp
from jax import lax
from jax.experimental import pallas as pl
from jax.experimental.pallas import tpu as pltpu

F32 = jnp.float32
BF16 = jnp.bfloat16
HIGHEST = lax.Precision.HIGHEST

D_MODEL = 2048
S5_WIDTH = 1024
S5_GROUP = 16
S5_GROUPS = 64
S5_STATE = 64
S5_HID = S5_GROUPS * S5_STATE
GDN_WIDTH = 1024
GDN_HEAD_DIM = 128
GDN_HEADS = 8
GDN_CONV = 4
D_FF = 5632
MIX_MAIN = 5120
DEEP_ALPHA = 2.0 ** 0.25
LN_EPS = 1e-5
NORM_EPS = 1e-6

SUBLANES = 8
LANES = 128
SLAB = 8
SLAB_REAL = 4
GDN_TILE = 128
S5_TILE = 256
S5_LCHUNK = 256
S5_BLOCKS = 8
VMEM_LIMIT = 56 * 1024 * 1024

NT_DIMS = (((1,), (1,)), ((), ()))


def _dot(a, b, **kw):
    return jnp.dot(a, b, preferred_element_type=F32, **kw)


def _silu(x):
    return x * jax.nn.sigmoid(x)


def _layer_norm(y, g, b):
    mu = jnp.mean(y, axis=-1, keepdims=True)
    d = y - mu
    var = jnp.mean(d * d, axis=-1, keepdims=True)
    return d * lax.rsqrt(var + LN_EPS) * g + b


def _slab_row_selector(rows, nseq, slab_row, first, *, transpose):
    shape = (rows, nseq) if transpose else (nseq, rows)
    r = lax.broadcasted_iota(jnp.int32, shape, 0 if transpose else 1)
    b = lax.broadcasted_iota(jnp.int32, shape, 1 if transpose else 0)
    return (r == SLAB * (b - first) + slab_row).astype(F32)


def _params(*sem):
    return pltpu.CompilerParams(dimension_semantics=sem, vmem_limit_bytes=VMEM_LIMIT)


def _ffn_kernel(x_ref, wg_ref, wu_ref, wo_ref, g_ref, b_ref, o_ref, acc_ref, xb_ref):
    j = pl.program_id(1)

    @pl.when(j == 0)
    def _():
        acc_ref[...] = jnp.zeros_like(acc_ref)
        xb_ref[...] = x_ref[...].astype(BF16)

    xb = xb_ref[...]
    gate = _dot(xb, wg_ref[...])
    up = _dot(xb, wu_ref[...])
    h = _silu(gate) * up
    acc_ref[...] += _dot(h.astype(BF16), wo_ref[...])

    @pl.when(j == pl.num_programs(1) - 1)
    def _():
        y = DEEP_ALPHA * x_ref[...] + 0.5 * acc_ref[...]
        o_ref[...] = _layer_norm(y, g_ref[...], b_ref[...])


def _ffn_ln(x, w_in, w_out, g, b, *, tm=512, tf=512):
    n = x.shape[0]
    nff = D_FF // tf
    return pl.pallas_call(
        _ffn_kernel,
        grid=(n // tm, nff),
        in_specs=[
            pl.BlockSpec((tm, D_MODEL), lambda i, j: (i, 0)),
            pl.BlockSpec((D_MODEL, tf), lambda i, j: (0, j)),
            pl.BlockSpec((D_MODEL, tf), lambda i, j: (0, j + nff)),
            pl.BlockSpec((tf, D_MODEL), lambda i, j: (j, 0)),
            pl.BlockSpec((1, D_MODEL), lambda i, j: (0, 0)),
            pl.BlockSpec((1, D_MODEL), lambda i, j: (0, 0)),
        ],
        out_specs=pl.BlockSpec((tm, D_MODEL), lambda i, j: (i, 0)),
        out_shape=jax.ShapeDtypeStruct((n, D_MODEL), F32),
        scratch_shapes=[pltpu.VMEM((tm, D_MODEL), F32), pltpu.VMEM((tm, D_MODEL), BF16)],
        compiler_params=_params("parallel", "arbitrary"),
        name="ffn_ln",
    )(x, w_in, w_in, w_out, g, b)


def _mixin_kernel(x_ref, w_ref, o_ref):
    o_ref[...] = _dot(x_ref[...].astype(BF16), w_ref[...])


def _mixin(x, w, *, tm=512, tn=1024):
    n = x.shape[0]
    return pl.pallas_call(
        _mixin_kernel,
        grid=(n // tm, MIX_MAIN // tn),
        in_specs=[pl.BlockSpec((tm, D_MODEL), lambda i, j: (i, 0)),
                  pl.BlockSpec((D_MODEL, tn), lambda i, j: (0, j))],
        out_specs=pl.BlockSpec((tm, tn), lambda i, j: (i, j)),
        out_shape=jax.ShapeDtypeStruct((n, MIX_MAIN), F32),
        compiler_params=_params("parallel", "arbitrary"),
        name="mix_in",
    )(x, w)


def _mixin_tail_kernel(x_ref, w_ref, wt_ref, o_ref, ot_ref):
    xb = x_ref[...].astype(BF16)
    o_ref[...] = _dot(xb, w_ref[...])
    ot_ref[...] = lax.dot_general(wt_ref[...], xb, NT_DIMS, preferred_element_type=F32)


def _mixin_tail(x, w_tail, w_tail_t, *, tm=512):
    n = x.shape[0]
    return pl.pallas_call(
        _mixin_tail_kernel,
        grid=(n // tm,),
        in_specs=[pl.BlockSpec((tm, D_MODEL), lambda i: (i, 0)),
                  pl.BlockSpec((D_MODEL, LANES), lambda i: (0, 0)),
                  pl.BlockSpec((2 * GDN_HEADS, D_MODEL), lambda i: (0, 0))],
        out_specs=[pl.BlockSpec((tm, LANES), lambda i: (i, 0)),
                   pl.BlockSpec((2 * GDN_HEADS, tm), lambda i: (0, i))],
        out_shape=[jax.ShapeDtypeStruct((n, LANES), F32),
                   jax.ShapeDtypeStruct((2 * GDN_HEADS, n), F32)],
        compiler_params=_params("parallel"),
        name="mix_in_tail",
    )(x, w_tail, w_tail_t)


def _mixout_kernel(ya_ref, yb_ref, x_ref, w_ref, g_ref, b_ref, o_ref):
    mix = (_dot(ya_ref[...].astype(BF16), w_ref[0:S5_WIDTH, :])
           + _dot(yb_ref[...].astype(BF16), w_ref[S5_WIDTH:, :]))
    o_ref[...] = _layer_norm(DEEP_ALPHA * x_ref[...] + mix, g_ref[...], b_ref[...])


def _mixout_ln(ya, yb, x, w, g, b, *, tm=512):
    n = x.shape[0]
    return pl.pallas_call(
        _mixout_kernel,
        grid=(n // tm,),
        in_specs=[pl.BlockSpec((tm, S5_WIDTH), lambda i: (i, 0)),
                  pl.BlockSpec((tm, GDN_WIDTH), lambda i: (i, 0)),
                  pl.BlockSpec((tm, D_MODEL), lambda i: (i, 0)),
                  pl.BlockSpec((D_MODEL, D_MODEL), lambda i: (0, 0)),
                  pl.BlockSpec((1, D_MODEL), lambda i: (0, 0)),
                  pl.BlockSpec((1, D_MODEL), lambda i: (0, 0))],
        out_specs=pl.BlockSpec((tm, D_MODEL), lambda i: (i, 0)),
        out_shape=jax.ShapeDtypeStruct((n, D_MODEL), F32),
        compiler_params=_params("parallel"),
        name="mix_out_ln",
    )(ya, yb, x, w, g, b)


def _s5_disc_kernel(lre_ref, lim_ref, ldt_ref, c_ref, coef_ref):
    lr, li = lre_ref[...], lim_ref[...]
    dt = jnp.exp(ldt_ref[...])
    mag = jnp.exp(lr * dt)
    ar = mag * jnp.cos(li * dt)
    ai = mag * jnp.sin(li * dt)
    nr, ni = ar - 1.0, ai
    den = lr * lr + li * li
    c_ref[0:1, :] = (nr * lr + ni * li) / den
    c_ref[1:2, :] = (ni * lr - nr * li) / den

    def cmul(x, y):
        return x[0] * y[0] - x[1] * y[1], x[0] * y[1] + x[1] * y[0]

    a1 = (ar, ai)
    pows = [a1]
    for _ in range(SUBLANES - 1):
        pows.append(cmul(pows[-1], a1))
    coef_ref[...] = jnp.zeros_like(coef_ref)
    width = lr.shape[-1]
    for k, s in enumerate((1, 2, 4)):
        for part in range(2):
            coef_ref[2 * k + part, s:SUBLANES, :] = jnp.broadcast_to(pows[s - 1][part], (SUBLANES - s, width))
    for r in range(SUBLANES):
        coef_ref[6, r:r + 1, :] = pows[r][0]
        coef_ref[7, r:r + 1, :] = pows[r][1]


def _s5_disc(lre, lim, ldt):
    return pl.pallas_call(
        _s5_disc_kernel,
        out_shape=[jax.ShapeDtypeStruct((2, S5_HID), F32),
                   jax.ShapeDtypeStruct((8, SUBLANES, S5_HID), F32)],
        name="s5_disc",
    )(lre, lim, ldt)


def _gelu_tanh(y):
    return 0.5 * y * (1.0 + jnp.tanh(math.sqrt(2.0 / math.pi) * (y + 0.044715 * (y * y * y))))


def _s5_kernel(u_ref, h0re_ref, h0im_ref, c_ref, coef_ref, bre_ref, bim_ref, cwre_ref, cwim_ref,
               d_ref, gw_ref, gb_ref, y_ref, sre_ref, sim_ref, hre, him, ysc, car_re, car_im,
               *, chain, tt):
    u = u_ref[...]
    ub = u.astype(BF16)
    bw = S5_HID // S5_BLOCKS
    gw = S5_WIDTH // S5_BLOCKS
    for j in range(S5_BLOCKS):
        uj = ub[:, j * gw:(j + 1) * gw]
        r = _dot(uj, bre_ref[j])
        im = _dot(uj, bim_ref[j])
        cr = c_ref[0:1, j * bw:(j + 1) * bw]
        ci = c_ref[1:2, j * bw:(j + 1) * bw]
        hre[:, j * bw:(j + 1) * bw] = cr * r - ci * im
        him[:, j * bw:(j + 1) * bw] = cr * im + ci * r

    nslab = tt // SLAB
    if chain:
        @pl.when(pl.program_id(1) == 0)
        def _():
            car_re[...] = jnp.broadcast_to(h0re_ref[...], car_re.shape)
            car_im[...] = jnp.broadcast_to(h0im_ref[...], car_im.shape)
    else:
        first = (pl.program_id(0) % (h0re_ref.shape[0] // nslab)) * nslab
        place = _slab_row_selector(tt, h0re_ref.shape[0], SLAB - SLAB_REAL - 1, first, transpose=True)
        hre[...] += _dot(place, h0re_ref[...], precision=HIGHEST)
        him[...] += _dot(place, h0im_ref[...], precision=HIGHEST)

    for c in range(S5_HID // S5_LCHUNK):
        sl = slice(c * S5_LCHUNK, (c + 1) * S5_LCHUNK)
        steps = [(coef_ref[2 * k, :, sl], coef_ref[2 * k + 1, :, sl], s) for k, s in enumerate((1, 2, 4))]
        pr, pi = coef_ref[6, :, sl], coef_ref[7, :, sl]

        def body(rb, carry, sl=sl, steps=steps, pr=pr, pi=pi):
            rows = pl.ds(pl.multiple_of(rb * SUBLANES, SUBLANES), SUBLANES)
            xr, xi = hre[rows, sl], him[rows, sl]
            for a_r, a_i, s in steps:
                sr = pltpu.roll(xr, s, 0)
                si = pltpu.roll(xi, s, 0)
                xr, xi = xr + a_r * sr - a_i * si, xi + a_r * si + a_i * sr
            if chain:
                kr, ki = carry
                xr, xi = xr + pr * kr - pi * ki, xi + pr * ki + pi * kr
                carry = (jnp.broadcast_to(xr[SUBLANES - 1:SUBLANES, :], xr.shape),
                         jnp.broadcast_to(xi[SUBLANES - 1:SUBLANES, :], xi.shape))
            hre[rows, sl] = xr
            him[rows, sl] = xi
            return carry

        if chain:
            carry = lax.fori_loop(0, tt // SUBLANES, body, (car_re[:, sl], car_im[:, sl]))
            car_re[:, sl] = carry[0]
            car_im[:, sl] = carry[1]
        else:
            lax.fori_loop(0, tt // SUBLANES, body, 0)

    if chain:
        @pl.when(pl.program_id(1) == pl.num_programs(1) - 1)
        def _():
            sre_ref[...] = car_re[0:1, :]
            sim_ref[...] = car_im[0:1, :]
    else:
        last = _slab_row_selector(tt, nslab, SLAB - 1, 0, transpose=False)
        sre_ref[...] = _dot(last, hre[...], precision=HIGHEST)
        sim_ref[...] = _dot(last, him[...], precision=HIGHEST)

    for j in range(S5_BLOCKS):
        hr = hre[:, j * bw:(j + 1) * bw].astype(BF16)
        hi = him[:, j * bw:(j + 1) * bw].astype(BF16)
        ysc[:, j * gw:(j + 1) * gw] = _dot(hr, cwre_ref[j]) - _dot(hi, cwim_ref[j])
    z = _gelu_tanh(ysc[...] + d_ref[...] * u)
    gl = _dot(z.astype(BF16), gw_ref[...]) + gb_ref[...]
    y_ref[...] = z * jax.nn.sigmoid(gl)


def _s5_mixer(proj, h0re, h0im, sw, *, chain, nseq, tt=S5_TILE):
    rows = proj.shape[0]
    const = lambda shape: pl.BlockSpec(shape, lambda *_: (0,) * len(shape))
    if chain:
        nt = rows // nseq // tt
        grid = (nseq, nt)
        u_spec = pl.BlockSpec((tt, S5_WIDTH), lambda b, t: (b * nt + t, 0))
        y_spec = pl.BlockSpec((tt, S5_WIDTH), lambda b, t: (b * nt + t, 0))
        st_spec = pl.BlockSpec((None, 1, S5_HID), lambda b, t: (b, 0, 0))
        st_shape = jax.ShapeDtypeStruct((nseq, 1, S5_HID), F32)
        h0_spec = st_spec
        sem = ("parallel", "arbitrary")
    else:
        grid = (rows // tt,)
        u_spec = pl.BlockSpec((tt, S5_WIDTH), lambda i: (i, 0))
        y_spec = pl.BlockSpec((tt, S5_WIDTH), lambda i: (i, 0))
        st_spec = pl.BlockSpec((tt // SLAB, S5_HID), lambda i: (i, 0))
        st_shape = jax.ShapeDtypeStruct((rows // SLAB, S5_HID), F32)
        per = LANES // (tt // SLAB)
        h0_spec = pl.BlockSpec((LANES, S5_HID), lambda i: (i // per, 0))
        sem = ("parallel",)
    bw = S5_HID // S5_BLOCKS
    gw = S5_WIDTH // S5_BLOCKS
    return pl.pallas_call(
        functools.partial(_s5_kernel, chain=chain, tt=tt),
        grid=grid,
        in_specs=[u_spec, h0_spec, h0_spec,
                  const((2, S5_HID)), const((8, SUBLANES, S5_HID)),
                  const((S5_BLOCKS, gw, bw)), const((S5_BLOCKS, gw, bw)),
                  const((S5_BLOCKS, bw, gw)), const((S5_BLOCKS, bw, gw)),
                  const((1, S5_WIDTH)), const((S5_WIDTH, S5_WIDTH)), const((1, S5_WIDTH))],
        out_specs=[y_spec, st_spec, st_spec],
        out_shape=[jax.ShapeDtypeStruct((rows, S5_WIDTH), F32), st_shape, st_shape],
        scratch_shapes=[pltpu.VMEM((tt, S5_HID), F32), pltpu.VMEM((tt, S5_HID), F32),
                        pltpu.VMEM((tt, S5_WIDTH), F32),
                        pltpu.VMEM((SUBLANES, S5_HID), F32), pltpu.VMEM((SUBLANES, S5_HID), F32)],
        compiler_params=_params(*sem),
        name="s5_chain" if chain else "s5_slab",
    )(proj, h0re, h0im, sw["c"], sw["coef"], sw["b_re"], sw["b_im"], sw["c_re"], sw["c_im"],
      sw["d"], sw["glu_w"], sw["glu_b"])


def _softplus(x):
    return jnp.maximum(x, 0.0) + jnp.log1p(jnp.exp(-jnp.abs(x)))


def _segment_masks(seg):
    ri = lax.broadcasted_iota(jnp.int32, (GDN_TILE, GDN_TILE), 0)
    ci = lax.broadcasted_iota(jnp.int32, (GDN_TILE, GDN_TILE), 1)
    same = (ri // seg) == (ci // seg)
    causal = (ri >= ci) & same
    strict = (ri > ci) & same
    return same, causal, strict


def _gdn_gates(bd, bdt, alog_row, dtb_row, alog_col, dtb_col, seg, slab):
    same, causal, strict = _segment_masks(seg)
    beta = jax.nn.sigmoid(bd)
    g = -jnp.exp(alog_row) * _softplus(bd + dtb_row)
    gt = -jnp.exp(alog_col) * _softplus(bdt[GDN_HEADS:, :] + dtb_col)
    if slab:
        rreal = (lax.broadcasted_iota(jnp.int32, (GDN_TILE, 1), 0) % SLAB) >= SLAB - SLAB_REAL
        creal = (lax.broadcasted_iota(jnp.int32, (1, GDN_TILE), 1) % SLAB) >= SLAB - SLAB_REAL
        beta = jnp.where(rreal, beta, 0.0)
        g = jnp.where(rreal, g, 0.0)
        gt = jnp.where(creal, gt, 0.0)
    gc = _dot(causal.astype(F32), g, precision=HIGHEST)
    gl = _dot(same.astype(F32), g, precision=HIGHEST)
    ri = lax.broadcasted_iota(jnp.int32, (GDN_TILE, GDN_TILE), 0)
    ci = lax.broadcasted_iota(jnp.int32, (GDN_TILE, GDN_TILE), 1)
    upper = (ri <= ci) & same
    gct = _dot(gt, upper.astype(F32), precision=HIGHEST)
    return beta, gc, gl, gct, causal, strict


def _merge_masks(top):
    ri = lax.broadcasted_iota(jnp.int32, (GDN_TILE, GDN_TILE), 0)
    ci = lax.broadcasted_iota(jnp.int32, (GDN_TILE, GDN_TILE), 1)
    masks = []
    s = 1
    while s < top:
        masks.append(((ri // (2 * s)) == (ci // (2 * s))) & ((ri // s) != (ci // s)))
        s *= 2
    return masks


def _unit_lower_inverse(ms, masks):
    es = [-jnp.where(masks[0], m, 0.0) for m in ms]
    for mask in masks[1:]:
        cs = [jnp.where(mask, m, 0.0) for m in ms]
        ebs = [e.astype(BF16) for e in es]
        xs = [c + _dot(eb, c.astype(BF16)) for c, eb in zip(cs, ebs)]
        es = [e - (x + _dot(x.astype(BF16), eb)) for e, x, eb in zip(es, xs, ebs)]
    return es


def _gdn_intra(qs, ks, vs, betas, gcols, grows, glcols, causal, strict, masks):
    n = range(len(qs))
    qn = [q * lax.rsqrt(jnp.sum(q * q, axis=-1, keepdims=True) + NORM_EPS) * (GDN_HEAD_DIM ** -0.5) for q in qs]
    kn = [k * lax.rsqrt(jnp.sum(k * k, axis=-1, keepdims=True) + NORM_EPS) for k in ks]
    decay = [jnp.exp(jnp.where(causal, gcols[i] - grows[i], -jnp.inf)) for i in n]
    kb = [kn[i] * betas[i] for i in n]
    knb = [k.astype(BF16) for k in kn]
    kk = [lax.dot_general(kb[i].astype(BF16), knb[i], NT_DIMS, preferred_element_type=F32) for i in n]
    ys = _unit_lower_inverse([kk[i] * jnp.where(strict, decay[i], 0.0) for i in n], masks)
    eg = [jnp.exp(g) for g in gcols]
    rhs = [jnp.concatenate([vs[i] * betas[i], kb[i] * eg[i]], axis=1) for i in n]
    uw = [rhs[i] + _dot(ys[i].astype(BF16), rhs[i].astype(BF16)) for i in n]
    attn = [lax.dot_general(qn[i].astype(BF16), knb[i], NT_DIMS, preferred_element_type=F32) * decay[i] for i in n]
    qg = [qn[i] * eg[i] for i in n]
    kd = [kn[i] * jnp.exp(glcols[i] - gcols[i]) for i in n]
    return [(uw[i][:, :GDN_HEAD_DIM], uw[i][:, GDN_HEAD_DIM:], attn[i], qg[i], kd[i]) for i in n]


def _gdn_out(o, gate, norm_w):
    o = o * lax.rsqrt(jnp.mean(o * o, axis=-1, keepdims=True) + NORM_EPS) * norm_w
    return o * _silu(gate)


def _conv4(win_ref, cw, width_slice):
    out = cw[0:1, :] * win_ref[pl.ds(SUBLANES - 3, GDN_TILE), width_slice]
    for j in range(1, GDN_CONV):
        out = out + cw[j:j + 1, :] * win_ref[pl.ds(SUBLANES - 3 + j, GDN_TILE), width_slice]
    return out


def _gdn_chain_kernel(q_ref, k_ref, v_ref, gate_ref, bd_ref, bdt_ref, buf_ref, s0_ref, cw_ref,
                      alr_ref, dbr_ref, alc_ref, dbc_ref, nw_ref,
                      y_ref, sout_ref, bufout_ref, win, s_sc):
    t = pl.program_id(1)
    w3 = 3 * GDN_WIDTH

    @pl.when(t == 0)
    def _():
        win[0:SUBLANES, :] = jnp.zeros((SUBLANES, w3), F32)
        win[SUBLANES - 3:SUBLANES, :] = buf_ref[...]
        s_sc[...] = s0_ref[...]

    win[SUBLANES:, 0:GDN_WIDTH] = q_ref[...]
    win[SUBLANES:, GDN_WIDTH:2 * GDN_WIDTH] = k_ref[...]
    win[SUBLANES:, 2 * GDN_WIDTH:] = v_ref[...]

    beta, gc, gl, gct, causal, strict = _gdn_gates(
        bd_ref[...], bdt_ref[...], alr_ref[...], dbr_ref[...], alc_ref[...], dbc_ref[...],
        GDN_TILE, False)
    masks = _merge_masks(GDN_TILE)
    heads = range(GDN_HEADS)
    qkv = [[], [], []]
    for part in range(3):
        for h in heads:
            cs = slice(part * GDN_WIDTH + h * GDN_HEAD_DIM, part * GDN_WIDTH + (h + 1) * GDN_HEAD_DIM)
            qkv[part].append(_silu(_conv4(win, cw_ref[:, cs], cs)))
    gcols = [gc[:, GDN_HEADS + h:GDN_HEADS + h + 1] for h in heads]
    glcols = [gl[:, GDN_HEADS + h:GDN_HEADS + h + 1] for h in heads]
    intra = _gdn_intra(qkv[0], qkv[1], qkv[2], [beta[:, h:h + 1] for h in heads], gcols,
                       [gct[h:h + 1, :] for h in heads], glcols, causal, strict, masks)
    s_old = [s_sc[h] for h in heads]
    a = [_dot(jnp.concatenate([intra[h][1], intra[h][3]], axis=0).astype(BF16), s_old[h].astype(BF16))
         for h in heads]
    vnb = [(intra[h][0] - a[h][:GDN_TILE]).astype(BF16) for h in heads]
    o = [a[h][GDN_TILE:] + _dot(intra[h][2].astype(BF16), vnb[h]) for h in heads]
    for h in heads:
        s_sc[h] = s_old[h] * jnp.exp(glcols[h][0:1, :]) + _dot(intra[h][4].T.astype(BF16), vnb[h])
    for h in heads:
        hs = slice(h * GDN_HEAD_DIM, (h + 1) * GDN_HEAD_DIM)
        y_ref[:, hs] = _gdn_out(o[h], gate_ref[:, hs], nw_ref[...])

    win[0:SUBLANES, :] = win[GDN_TILE:GDN_TILE + SUBLANES, :]

    @pl.when(t == pl.num_programs(1) - 1)
    def _():
        sout_ref[...] = s_sc[...]
        bufout_ref[...] = win[GDN_TILE + SUBLANES - 3:GDN_TILE + SUBLANES, :]


def _gdn_chain(proj, bd, bdt, conv_buf, s0, gw, *, nseq):
    rows = proj.shape[0]
    nt = rows // nseq // GDN_TILE
    w3 = 3 * GDN_WIDTH
    const = lambda shape: pl.BlockSpec(shape, lambda *_: (0,) * len(shape))
    col = lambda cb: pl.BlockSpec((GDN_TILE, GDN_WIDTH), lambda b, t, cb=cb: (b * nt + t, cb))
    return pl.pallas_call(
        _gdn_chain_kernel,
        grid=(nseq, nt),
        in_specs=[col(1), col(2), col(3), col(4),
                  pl.BlockSpec((GDN_TILE, LANES), lambda b, t: (b * nt + t, 0)),
                  pl.BlockSpec((2 * GDN_HEADS, GDN_TILE), lambda b, t: (0, b * nt + t)),
                  pl.BlockSpec((None, GDN_CONV - 1, w3), lambda b, t: (b, 0, 0)),
                  pl.BlockSpec((None, GDN_HEADS, GDN_HEAD_DIM, GDN_HEAD_DIM), lambda b, t: (b, 0, 0, 0)),
                  const((GDN_CONV, w3)), const((1, LANES)), const((1, LANES)),
                  const((GDN_HEADS, 1)), const((GDN_HEADS, 1)), const((1, GDN_HEAD_DIM))],
        out_specs=[pl.BlockSpec((GDN_TILE, GDN_WIDTH), lambda b, t: (b * nt + t, 0)),
                   pl.BlockSpec((None, GDN_HEADS, GDN_HEAD_DIM, GDN_HEAD_DIM), lambda b, t: (b, 0, 0, 0)),
                   pl.BlockSpec((None, GDN_CONV - 1, w3), lambda b, t: (b, 0, 0))],
        out_shape=[jax.ShapeDtypeStruct((rows, GDN_WIDTH), F32),
                   jax.ShapeDtypeStruct((nseq, GDN_HEADS, GDN_HEAD_DIM, GDN_HEAD_DIM), F32),
                   jax.ShapeDtypeStruct((nseq, GDN_CONV - 1, w3), F32)],
        scratch_shapes=[pltpu.VMEM((GDN_TILE + SUBLANES, w3), F32),
                        pltpu.VMEM((GDN_HEADS, GDN_HEAD_DIM, GDN_HEAD_DIM), F32)],
        compiler_params=_params("parallel", "arbitrary"),
        name="gdn_chain",
    )(proj, proj, proj, proj, bd, bdt, conv_buf, s0, gw["conv_w"], gw["alog_row"], gw["dtb_row"],
      gw["alog_col"], gw["dtb_col"], gw["norm_w"])


def _gdn_slab_kernel(q_ref, k_ref, v_ref, gate_ref, bd_ref, bdt_ref, bq_ref, bk_ref, bv_ref, s0_ref,
                     cwq_ref, cwk_ref, cwv_ref, alr_ref, dbr_ref, alc_ref, dbc_ref, nw_ref,
                     y_ref, sout_ref, oq_ref, ok_ref, ov_ref,
                     win):
    h = pl.program_id(1)
    nslab = GDN_TILE // SLAB
    first = (pl.program_id(0) % (bq_ref.shape[1] // nslab)) * nslab
    real = (lax.broadcasted_iota(jnp.int32, (GDN_TILE, 1), 0) % SLAB) >= SLAB - SLAB_REAL
    qkv = []
    for part, (x_ref, b_ref, cw_ref, o_ref) in enumerate(
            ((q_ref, bq_ref, cwq_ref, oq_ref), (k_ref, bk_ref, cwk_ref, ok_ref), (v_ref, bv_ref, cwv_ref, ov_ref))):
        cs = slice(part * GDN_HEAD_DIM, (part + 1) * GDN_HEAD_DIM)
        x = x_ref[...]
        for i in range(GDN_CONV - 1):
            place = _slab_row_selector(GDN_TILE, b_ref.shape[1], 1 + i, first, transpose=True)
            x = x + _dot(place, b_ref[i], precision=HIGHEST)
        win[0:SUBLANES, cs] = jnp.zeros((SUBLANES, GDN_HEAD_DIM), F32)
        win[SUBLANES:, cs] = x
        conv = _conv4(win, cw_ref[...], cs)
        qkv.append(jnp.where(real, _silu(conv), 0.0))
        for i in range(GDN_CONV - 1):
            take = _slab_row_selector(GDN_TILE, nslab, SLAB - 3 + i, 0, transpose=False)
            o_ref[i] = _dot(take, x, precision=HIGHEST)

    beta, gc, gl, gct, causal, strict = _gdn_gates(
        bd_ref[...], bdt_ref[...], alr_ref[...], dbr_ref[...], alc_ref[...], dbc_ref[...], SLAB, True)
    lane = lax.broadcasted_iota(jnp.int32, (1, LANES), 1)
    sub = lax.broadcasted_iota(jnp.int32, (SUBLANES, 1), 0)
    pick = lambda a, idx: jnp.sum(jnp.where(lane == idx, a, 0.0), axis=1, keepdims=True)
    beta_col = pick(beta, h)
    gcol = pick(gc, GDN_HEADS + h)
    glcol = pick(gl, GDN_HEADS + h)
    grow = jnp.sum(jnp.where(sub == h, gct, 0.0), axis=0, keepdims=True)
    masks = _merge_masks(SLAB_REAL)
    (u, wk, attn, qg, kd), = _gdn_intra([qkv[0]], [qkv[1]], [qkv[2]], [beta_col], [gcol], [grow], [glcol],
                                        causal, strict, masks)
    res = []
    for i in range(nslab):
        rows = slice(i * SLAB, (i + 1) * SLAB)
        lhs = jnp.concatenate([wk[rows], qg[rows]], axis=0).astype(BF16)
        res.append(_dot(lhs, s0_ref[i, 0].astype(BF16)))
    v_new = u - jnp.concatenate([r[:SLAB] for r in res], axis=0)
    vnb = v_new.astype(BF16)
    o = jnp.concatenate([r[SLAB:] for r in res], axis=0) + _dot(attn.astype(BF16), vnb)
    y_ref[...] = _gdn_out(o, gate_ref[...], nw_ref[...])
    kdt = kd.T
    egl = jnp.exp(glcol)
    for i in range(nslab):
        in_slab = (lane // SLAB) == i
        upd = _dot(jnp.where(in_slab, kdt, 0.0).astype(BF16), vnb)
        sout_ref[i, 0] = s0_ref[i, 0] * egl[i * SLAB:i * SLAB + 1, :] + upd


def _gdn_slab(proj, bd, bdt, conv_buf, s0, gw):
    rows = proj.shape[0]
    nb = rows // SLAB
    nslab = GDN_TILE // SLAB
    hb = GDN_WIDTH // GDN_HEAD_DIM
    const = lambda shape: pl.BlockSpec(shape, lambda *_: (0,) * len(shape))
    col = lambda g: pl.BlockSpec((GDN_TILE, GDN_HEAD_DIM), lambda i, h, g=g: (i, g * hb + h))
    per = LANES // nslab
    buf = lambda g: pl.BlockSpec((GDN_CONV - 1, LANES, GDN_HEAD_DIM), lambda i, h, g=g: (0, i // per, g * hb + h))
    cwb = lambda g: pl.BlockSpec((GDN_CONV, GDN_HEAD_DIM), lambda i, h, g=g: (0, g * hb + h))
    st = pl.BlockSpec((nslab, 1, GDN_HEAD_DIM, GDN_HEAD_DIM), lambda i, h: (i, h, 0, 0))
    obuf = pl.BlockSpec((GDN_CONV - 1, nslab, GDN_HEAD_DIM), lambda i, h: (0, i, h))
    tile = (GDN_TILE, GDN_HEAD_DIM)
    return pl.pallas_call(
        _gdn_slab_kernel,
        grid=(rows // GDN_TILE, GDN_HEADS),
        in_specs=[col(1), col(2), col(3), col(4),
                  pl.BlockSpec((GDN_TILE, LANES), lambda i, h: (i, 0)),
                  pl.BlockSpec((2 * GDN_HEADS, GDN_TILE), lambda i, h: (0, i)),
                  buf(0), buf(1), buf(2), st, cwb(0), cwb(1), cwb(2),
                  const((1, LANES)), const((1, LANES)), const((GDN_HEADS, 1)), const((GDN_HEADS, 1)),
                  const((1, GDN_HEAD_DIM))],
        out_specs=[pl.BlockSpec(tile, lambda i, h: (i, h)), st, obuf, obuf, obuf],
        out_shape=[jax.ShapeDtypeStruct((rows, GDN_WIDTH), F32),
                   jax.ShapeDtypeStruct(s0.shape, F32)]
                  + [jax.ShapeDtypeStruct((GDN_CONV - 1, nb, GDN_WIDTH), F32)] * 3,
        scratch_shapes=[pltpu.VMEM((GDN_TILE + SUBLANES, 3 * GDN_HEAD_DIM), F32)],
        compiler_params=_params("parallel", "arbitrary"),
        name="gdn_slab",
    )(proj, proj, proj, proj, bd, bdt, conv_buf, conv_buf, conv_buf, s0,
      gw["conv_w"], gw["conv_w"], gw["conv_w"], gw["alog_row"], gw["dtb_row"],
      gw["alog_col"], gw["dtb_col"], gw["norm_w"])


def _block_diag(w):
    per = S5_GROUPS // S5_BLOCKS
    g, a, b = w.shape
    w = w.reshape(S5_BLOCKS, per, a, b)
    eye = jnp.eye(per, dtype=w.dtype)
    return jnp.einsum("jgab,gk->jgakb", w, eye).reshape(S5_BLOCKS, per * a, per * b)


def _lane_pad(v, offset):
    return jnp.zeros((1, LANES), F32).at[0, offset:offset + v.shape[0]].set(v.astype(F32))


def _layer_weights(l, ln1_g, ln1_b, ffn1_w_in, ffn1_w_out, w_mix_in, s5_lambda_re, s5_lambda_im, s5_log_dt,
                   s5_b_re, s5_b_im, s5_c_re, s5_c_im, s5_d, s5_glu_w, s5_glu_b, gdn_conv_w, gdn_a_log,
                   gdn_dt_bias, gdn_norm_w, w_mix_out, ln2_g, ln2_b, ffn2_w_in, ffn2_w_out, ln3_g, ln3_b):
    row = lambda v: v[l].reshape(1, -1).astype(F32)
    w = {
        "ln1": (row(ln1_g), row(ln1_b)), "ln2": (row(ln2_g), row(ln2_b)), "ln3": (row(ln3_g), row(ln3_b)),
        "ffn1": (ffn1_w_in[l].astype(BF16), ffn1_w_out[l].astype(BF16)),
        "ffn2": (ffn2_w_in[l].astype(BF16), ffn2_w_out[l].astype(BF16)),
        "mix_in": w_mix_in[l][:, :MIX_MAIN].astype(BF16),
        "mix_out": w_mix_out[l].astype(BF16),
    }
    tail = w_mix_in[l][:, MIX_MAIN:]
    w["mix_tail"] = jnp.pad(tail, ((0, 0), (0, LANES - tail.shape[1]))).astype(BF16)
    w["mix_tail_t"] = tail.T.astype(BF16)
    c, coef = _s5_disc(s5_lambda_re[l].reshape(1, S5_HID), s5_lambda_im[l].reshape(1, S5_HID),
                       jnp.repeat(s5_log_dt[l], S5_STATE).reshape(1, S5_HID))
    w["s5"] = {
        "c": c, "coef": coef,
        "b_re": _block_diag(jnp.swapaxes(s5_b_re[l], 1, 2)).astype(BF16),
        "b_im": _block_diag(jnp.swapaxes(s5_b_im[l], 1, 2)).astype(BF16),
        "c_re": _block_diag(jnp.swapaxes(s5_c_re[l], 1, 2)).astype(BF16),
        "c_im": _block_diag(jnp.swapaxes(s5_c_im[l], 1, 2)).astype(BF16),
        "d": row(s5_d), "glu_w": s5_glu_w[l].astype(BF16), "glu_b": row(s5_glu_b),
    }
    w["gdn"] = {
        "conv_w": gdn_conv_w[l].astype(F32),
        "alog_row": _lane_pad(gdn_a_log[l], GDN_HEADS), "dtb_row": _lane_pad(gdn_dt_bias[l], GDN_HEADS),
        "alog_col": gdn_a_log[l].reshape(GDN_HEADS, 1).astype(F32),
        "dtb_col": gdn_dt_bias[l].reshape(GDN_HEADS, 1).astype(F32),
        "norm_w": row(gdn_norm_w),
    }
    return w


def _mix_projections(x, w):
    proj = _mixin(x, w["mix_in"])
    bd, bdt = _mixin_tail(x, w["mix_tail"], w["mix_tail_t"])
    return proj, bd, bdt


def _prompt_layer(x, w, nseq):
    x = _ffn_ln(x, *w["ffn1"], *w["ln1"])
    proj, bd, bdt = _mix_projections(x, w)
    z_s5 = jnp.zeros((nseq, 1, S5_HID), F32)
    y_s5, n_re, n_im = _s5_mixer(proj, z_s5, z_s5, w["s5"], chain=True, nseq=nseq)
    z_gdn = jnp.zeros((nseq, GDN_HEADS, GDN_HEAD_DIM, GDN_HEAD_DIM), F32)
    z_buf = jnp.zeros((nseq, GDN_CONV - 1, 3 * GDN_WIDTH), F32)
    y_gdn, n_s, n_buf = _gdn_chain(proj, bd, bdt, z_buf, z_gdn, w["gdn"], nseq=nseq)
    x = _mixout_ln(y_s5, y_gdn, x, w["mix_out"], *w["ln2"])
    x = _ffn_ln(x, *w["ffn2"], *w["ln3"])
    shape = (nseq, S5_GROUPS, S5_STATE)
    return x, n_re.reshape(shape), n_im.reshape(shape), n_s, n_buf


def _sample_layer(x, s5_re, s5_im, gdn_s, conv_buf, w, nb, t):
    x = _ffn_ln(x, *w["ffn1"], *w["ln1"])
    xs = jnp.pad(x.reshape(nb, t, D_MODEL), ((0, 0), (SLAB - t, 0), (0, 0))).reshape(nb * SLAB, D_MODEL)
    proj, bd, bdt = _mix_projections(xs, w)
    y_s5, n_re, n_im = _s5_mixer(proj, s5_re.reshape(nb, S5_HID).astype(F32),
                                 s5_im.reshape(nb, S5_HID).astype(F32), w["s5"], chain=False, nseq=nb)
    y_gdn, n_s, bq, bk, bv = _gdn_slab(proj, bd, bdt, jnp.swapaxes(conv_buf.astype(F32), 0, 1),
                                       gdn_s.astype(F32), w["gdn"])
    xs = _mixout_ln(y_s5, y_gdn, xs, w["mix_out"], *w["ln2"])
    x = xs.reshape(nb, SLAB, D_MODEL)[:, SLAB - t:].reshape(nb * t, D_MODEL)
    x = _ffn_ln(x, *w["ffn2"], *w["ln3"])
    shape = (nb, S5_GROUPS, S5_STATE)
    n_buf = jnp.swapaxes(jnp.concatenate([bq, bk, bv], axis=-1), 0, 1)
    return x, n_re.reshape(shape), n_im.reshape(shape), n_s, n_buf


def kernel(x_prompt, x_sample, state_s5_re, state_s5_im, state_gdn, state_conv, ln1_g, ln1_b, ffn1_w_in, ffn1_w_out, w_mix_in, s5_lambda_re, s5_lambda_im, s5_log_dt, s5_b_re, s5_b_im, s5_c_re, s5_c_im, s5_d, s5_glu_w, s5_glu_b, gdn_conv_w, gdn_a_log, gdn_dt_bias, gdn_norm_w, w_mix_out, ln2_g, ln2_b, ffn2_w_in, ffn2_w_out, ln3_g, ln3_b):
    bp, tp, _ = x_prompt.shape
    bs, ts, _ = x_sample.shape
    assert ts == SLAB_REAL and tp % S5_TILE == 0 and (bs * SLAB) % S5_TILE == 0
    depth = ln1_g.shape[0]
    yp = x_prompt.astype(F32).reshape(bp * tp, D_MODEL)
    ys = x_sample.astype(F32).reshape(bs * ts, D_MODEL)
    outs = [[] for _ in range(8)]
    for l in range(depth):
        w = _layer_weights(l, ln1_g, ln1_b, ffn1_w_in, ffn1_w_out, w_mix_in, s5_lambda_re, s5_lambda_im,
                           s5_log_dt, s5_b_re, s5_b_im, s5_c_re, s5_c_im, s5_d, s5_glu_w, s5_glu_b,
                           gdn_conv_w, gdn_a_log, gdn_dt_bias, gdn_norm_w, w_mix_out, ln2_g, ln2_b,
                           ffn2_w_in, ffn2_w_out, ln3_g, ln3_b)
        yp, *p_state = _prompt_layer(yp, w, bp)
        ys, *s_state = _sample_layer(ys, state_s5_re[l], state_s5_im[l], state_gdn[l], state_conv[l], w, bs, ts)
        for acc, val in zip(outs, p_state + s_state):
            acc.append(val)
    return (yp.reshape(x_prompt.shape).astype(x_prompt.dtype), ys.reshape(x_sample.shape).astype(x_sample.dtype),
            *(jnp.stack(o) for o in outs))
```

```python
import functools
import math

import jax
import jax.numpy as jnp
from jax import lax
from jax.experimental import pallas as pl
from jax.experimental.pallas import tpu as pltpu

F32 = jnp.float32
BF16 = jnp.bfloat16
HIGHEST = lax.Precision.HIGHEST

D_MODEL = 2048
S5_WIDTH = 1024
S5_GROUP = 16
S5_GROUPS = 64
S5_STATE = 64
S5_HID = S5_GROUPS * S5_STATE
GDN_WIDTH = 1024
GDN_HEAD_DIM = 128
GDN_HEADS = 8
GDN_CONV = 4
D_FF = 5632
MIX_MAIN = 5120
DEEP_ALPHA = 2.0 ** 0.25
LN_EPS = 1e-5
NORM_EPS = 1e-6

SUBLANES = 8
LANES = 128
SLAB = 8
SLAB_REAL = 4
GDN_TILE = 128
S5_TILE = 256
S5_LCHUNK = 512
S5_SEG = S5_TILE // SUBLANES
S5_BLOCKS = 8
VMEM_LIMIT = 56 * 1024 * 1024

NT_DIMS = (((1,), (1,)), ((), ()))


def _dot(a, b, **kw):
    return jnp.dot(a, b, preferred_element_type=F32, **kw)


def _silu(x):
    return x * jax.nn.sigmoid(x)


def _layer_norm(y, g, b):
    mu = jnp.mean(y, axis=-1, keepdims=True)
    d = y - mu
    var = jnp.mean(d * d, axis=-1, keepdims=True)
    return d * lax.rsqrt(var + LN_EPS) * g + b


def _slab_row_selector(rows, nseq, slab_row, first, *, transpose):
    shape = (rows, nseq) if transpose else (nseq, rows)
    r = lax.broadcasted_iota(jnp.int32, shape, 0 if transpose else 1)
    b = lax.broadcasted_iota(jnp.int32, shape, 1 if transpose else 0)
    return (r == SLAB * (b - first) + slab_row).astype(F32)


def _params(*sem):
    return pltpu.CompilerParams(dimension_semantics=sem, vmem_limit_bytes=VMEM_LIMIT)


def _ffn_kernel(x_ref, wg_ref, wu_ref, wo_ref, g_ref, b_ref, o_ref, acc_ref, xb_ref):
    j = pl.program_id(1)

    @pl.when(j == 0)
    def _():
        acc_ref[...] = jnp.zeros_like(acc_ref)
        xb_ref[...] = x_ref[...].astype(BF16)

    xb = xb_ref[...]
    gate = _dot(xb, wg_ref[...])
    up = _dot(xb, wu_ref[...])
    h = _silu(gate) * up
    acc_ref[...] += _dot(h.astype(BF16), wo_ref[...])

    @pl.when(j == pl.num_programs(1) - 1)
    def _():
        y = DEEP_ALPHA * x_ref[...] + 0.5 * acc_ref[...]
        o_ref[...] = _layer_norm(y, g_ref[...], b_ref[...])


def _ffn_ln(x, w_in, w_out, g, b, *, tm=512, tf=512):
    n = x.shape[0]
    nff = D_FF // tf
    return pl.pallas_call(
        _ffn_kernel,
        grid=(n // tm, nff),
        in_specs=[
            pl.BlockSpec((tm, D_MODEL), lambda i, j: (i, 0)),
            pl.BlockSpec((D_MODEL, tf), lambda i, j: (0, j)),
            pl.BlockSpec((D_MODEL, tf), lambda i, j: (0, j + nff)),
            pl.BlockSpec((tf, D_MODEL), lambda i, j: (j, 0)),
            pl.BlockSpec((1, D_MODEL), lambda i, j: (0, 0)),
            pl.BlockSpec((1, D_MODEL), lambda i, j: (0, 0)),
        ],
        out_specs=pl.BlockSpec((tm, D_MODEL), lambda i, j: (i, 0)),
        out_shape=jax.ShapeDtypeStruct((n, D_MODEL), F32),
        scratch_shapes=[pltpu.VMEM((tm, D_MODEL), F32), pltpu.VMEM((tm, D_MODEL), BF16)],
        compiler_params=_params("parallel", "arbitrary"),
        name="ffn_ln",
    )(x, w_in, w_in, w_out, g, b)


def _mixin_kernel(x_ref, w_ref, o_ref):
    o_ref[...] = _dot(x_ref[...].astype(BF16), w_ref[...])


def _mixin(x, w, *, tm=512, tn=1024):
    n = x.shape[0]
    return pl.pallas_call(
        _mixin_kernel,
        grid=(n // tm, MIX_MAIN // tn),
        in_specs=[pl.BlockSpec((tm, D_MODEL), lambda i, j: (i, 0)),
                  pl.BlockSpec((D_MODEL, tn), lambda i, j: (0, j))],
        out_specs=pl.BlockSpec((tm, tn), lambda i, j: (i, j)),
        out_shape=jax.ShapeDtypeStruct((n, MIX_MAIN), F32),
        compiler_params=_params("parallel", "arbitrary"),
        name="mix_in",
    )(x, w)


def _mixin_tail_kernel(x_ref, w_ref, wt_ref, o_ref, ot_ref):
    xb = x_ref[...].astype(BF16)
    o_ref[...] = _dot(xb, w_ref[...])
    ot_ref[...] = lax.dot_general(wt_ref[...], xb, NT_DIMS, preferred_element_type=F32)


def _mixin_tail(x, w_tail, w_tail_t, *, tm=512):
    n = x.shape[0]
    return pl.pallas_call(
        _mixin_tail_kernel,
        grid=(n // tm,),
        in_specs=[pl.BlockSpec((tm, D_MODEL), lambda i: (i, 0)),
                  pl.BlockSpec((D_MODEL, LANES), lambda i: (0, 0)),
                  pl.BlockSpec((2 * GDN_HEADS, D_MODEL), lambda i: (0, 0))],
        out_specs=[pl.BlockSpec((tm, LANES), lambda i: (i, 0)),
                   pl.BlockSpec((2 * GDN_HEADS, tm), lambda i: (0, i))],
        out_shape=[jax.ShapeDtypeStruct((n, LANES), F32),
                   jax.ShapeDtypeStruct((2 * GDN_HEADS, n), F32)],
        compiler_params=_params("parallel"),
        name="mix_in_tail",
    )(x, w_tail, w_tail_t)


def _mixout_kernel(ya_ref, yb_ref, x_ref, w_ref, g_ref, b_ref, o_ref):
    mix = (_dot(ya_ref[...].astype(BF16), w_ref[0:S5_WIDTH, :])
           + _dot(yb_ref[...].astype(BF16), w_ref[S5_WIDTH:, :]))
    o_ref[...] = _layer_norm(DEEP_ALPHA * x_ref[...] + mix, g_ref[...], b_ref[...])


def _mixout_ln(ya, yb, x, w, g, b, *, tm=512):
    n = x.shape[0]
    return pl.pallas_call(
        _mixout_kernel,
        grid=(n // tm,),
        in_specs=[pl.BlockSpec((tm, S5_WIDTH), lambda i: (i, 0)),
                  pl.BlockSpec((tm, GDN_WIDTH), lambda i: (i, 0)),
                  pl.BlockSpec((tm, D_MODEL), lambda i: (i, 0)),
                  pl.BlockSpec((D_MODEL, D_MODEL), lambda i: (0, 0)),
                  pl.BlockSpec((1, D_MODEL), lambda i: (0, 0)),
                  pl.BlockSpec((1, D_MODEL), lambda i: (0, 0))],
        out_specs=pl.BlockSpec((tm, D_MODEL), lambda i: (i, 0)),
        out_shape=jax.ShapeDtypeStruct((n, D_MODEL), F32),
        compiler_params=_params("parallel"),
        name="mix_out_ln",
    )(ya, yb, x, w, g, b)


def _s5_disc_kernel(lre_ref, lim_ref, ldt_ref, c_ref, coef_ref, pt_ref):
    lr, li = lre_ref[...], lim_ref[...]
    dt = jnp.exp(ldt_ref[...])
    mag = jnp.exp(lr * dt)
    ar = mag * jnp.cos(li * dt)
    ai = mag * jnp.sin(li * dt)
    nr, ni = ar - 1.0, ai
    den = lr * lr + li * li
    c_ref[0:1, :] = (nr * lr + ni * li) / den
    c_ref[1:2, :] = (ni * lr - nr * li) / den

    def cmul(x, y):
        return x[0] * y[0] - x[1] * y[1], x[0] * y[1] + x[1] * y[0]

    width = lr.shape[-1]
    p = (ar, ai)
    for j in range(S5_SEG):
        pt_ref[0, j:j + 1, :] = p[0]
        pt_ref[1, j:j + 1, :] = p[1]
        if j + 1 < S5_SEG:
            p = cmul(p, (ar, ai))
    coef_ref[...] = jnp.zeros_like(coef_ref)
    for part, v in enumerate((ar, ai)):
        coef_ref[part] = jnp.broadcast_to(v, (SUBLANES, width))
    for k, s in enumerate((1, 2, 4)):
        for part in range(2):
            coef_ref[2 + 2 * k + part, s:SUBLANES, :] = jnp.broadcast_to(p[part], (SUBLANES - s, width))
        p = cmul(p, p)


def _s5_disc(lre, lim, ldt):
    return pl.pallas_call(
        _s5_disc_kernel,
        out_shape=[jax.ShapeDtypeStruct((2, S5_HID), F32),
                   jax.ShapeDtypeStruct((8, SUBLANES, S5_HID), F32),
                   jax.ShapeDtypeStruct((2, S5_SEG, S5_HID), F32)],
        name="s5_disc",
    )(lre, lim, ldt)


def _gelu_tanh(y):
    return 0.5 * y * (1.0 + jnp.tanh(math.sqrt(2.0 / math.pi) * (y + 0.044715 * (y * y * y))))


def _scan_layout(tt, group, *, inverse):
    p = lax.broadcasted_iota(jnp.int32, (tt, tt), 1 if inverse else 0)
    n = lax.broadcasted_iota(jnp.int32, (tt, tt), 0 if inverse else 1)
    rem = p % group
    src = (p - rem) + (rem % SUBLANES) * (group // SUBLANES) + rem // SUBLANES
    return (n == src).astype(F32).astype(BF16)


def _cmul_add(ar, ai, xr, xi, br, bi):
    return ar * xr - ai * xi + br, ar * xi + ai * xr + bi


def _s5_kernel(u_ref, h0re_ref, h0im_ref, c_ref, coef_ref, pt_ref, bre_ref, bim_ref, cwre_ref, cwim_ref,
               d_ref, gw_ref, gb_ref, y_ref, sre_ref, sim_ref, hre, him, ysc, car_re, car_im,
               *, chain, tt):
    group = tt if chain else SLAB * SUBLANES
    seg = group // SUBLANES
    u = u_ref[...]
    ub = _dot(_scan_layout(tt, group, inverse=False), u.astype(BF16)).astype(BF16)
    bw = S5_HID // S5_BLOCKS
    gw = S5_WIDTH // S5_BLOCKS
    for j in range(S5_BLOCKS):
        uj = ub[:, j * gw:(j + 1) * gw]
        r = _dot(uj, bre_ref[j])
        im = _dot(uj, bim_ref[j])
        cr = c_ref[0:1, j * bw:(j + 1) * bw]
        ci = c_ref[1:2, j * bw:(j + 1) * bw]
        hre[:, j * bw:(j + 1) * bw] = cr * r - ci * im
        him[:, j * bw:(j + 1) * bw] = cr * im + ci * r

    if chain:
        @pl.when(pl.program_id(1) == 0)
        def _():
            car_re[...] = jnp.broadcast_to(h0re_ref[...], car_re.shape)
            car_im[...] = jnp.broadcast_to(h0im_ref[...], car_im.shape)

    sub = lax.broadcasted_iota(jnp.int32, (SUBLANES, 1), 0)
    for c in range(S5_HID // S5_LCHUNK):
        sl = slice(c * S5_LCHUNK, (c + 1) * S5_LCHUNK)
        ar, ai = coef_ref[0, :, sl], coef_ref[1, :, sl]
        if not chain:
            for g in range(tt // group):
                srows = slice(g * SUBLANES, (g + 1) * SUBLANES)
                xr, xi = h0re_ref[srows, sl], h0im_ref[srows, sl]
                for j in range(SLAB - SLAB_REAL, SLAB):
                    rows = slice(g * group + j * SUBLANES, g * group + (j + 1) * SUBLANES)
                    xr, xi = _cmul_add(ar, ai, xr, xi, hre[rows, sl], him[rows, sl])
                    hre[rows, sl] = xr
                    him[rows, sl] = xi
                sre_ref[srows, sl] = xr
                sim_ref[srows, sl] = xi
            continue

        def local_scan(j, x, sl=sl, ar=ar, ai=ai):
            rows = pl.ds(pl.multiple_of(j * SUBLANES, SUBLANES), SUBLANES)
            xr, xi = _cmul_add(ar, ai, x[0], x[1], hre[rows, sl], him[rows, sl])
            hre[rows, sl] = xr
            him[rows, sl] = xi
            return xr, xi

        zero = jnp.zeros((SUBLANES, S5_LCHUNK), F32)
        fr, fi = lax.fori_loop(0, seg, local_scan, (zero, zero), unroll=4)
        kr = jnp.where(sub == 0, car_re[:, sl], pltpu.roll(fr, 1, 0))
        ki = jnp.where(sub == 0, car_im[:, sl], pltpu.roll(fi, 1, 0))
        for k, s in enumerate((1, 2, 4)):
            pr, pi = coef_ref[2 + 2 * k, :, sl], coef_ref[3 + 2 * k, :, sl]
            kr, ki = _cmul_add(pr, pi, pltpu.roll(kr, s, 0), pltpu.roll(ki, s, 0), kr, ki)
        outr, outi = _cmul_add(coef_ref[2, :, sl], coef_ref[3, :, sl], kr, ki, fr, fi)
        car_re[:, sl] = jnp.broadcast_to(outr[SUBLANES - 1:SUBLANES, :], outr.shape)
        car_im[:, sl] = jnp.broadcast_to(outi[SUBLANES - 1:SUBLANES, :], outi.shape)

        def add_carry(j, _, sl=sl, kr=kr, ki=ki):
            rows = pl.ds(pl.multiple_of(j * SUBLANES, SUBLANES), SUBLANES)
            pr = jnp.broadcast_to(pt_ref[0, pl.ds(j, 1), sl], kr.shape)
            pi = jnp.broadcast_to(pt_ref[1, pl.ds(j, 1), sl], kr.shape)
            xr, xi = _cmul_add(pr, pi, kr, ki, hre[rows, sl], him[rows, sl])
            hre[rows, sl] = xr
            him[rows, sl] = xi
            return 0

        lax.fori_loop(0, seg, add_carry, 0, unroll=4)

    if chain:
        @pl.when(pl.program_id(1) == pl.num_programs(1) - 1)
        def _():
            sre_ref[...] = car_re[0:1, :]
            sim_ref[...] = car_im[0:1, :]

    for j in range(S5_BLOCKS):
        hr = hre[:, j * bw:(j + 1) * bw].astype(BF16)
        hi = him[:, j * bw:(j + 1) * bw].astype(BF16)
        ysc[:, j * gw:(j + 1) * gw] = _dot(hr, cwre_ref[j]) - _dot(hi, cwim_ref[j])
    ys = ysc[...]
    back = _scan_layout(tt, group, inverse=True)
    y = jnp.zeros_like(ys)
    for _ in range(3):
        piece = ys.astype(BF16)
        y = y + _dot(back, piece)
        ys = ys - piece.astype(F32)
    z = _gelu_tanh(y + d_ref[...] * u)
    gl = _dot(z.astype(BF16), gw_ref[...]) + gb_ref[...]
    y_ref[...] = z * jax.nn.sigmoid(gl)


def _s5_mixer(proj, h0re, h0im, sw, *, chain, nseq, tt=S5_TILE):
    rows = proj.shape[0]
    const = lambda shape: pl.BlockSpec(shape, lambda *_: (0,) * len(shape))
    if chain:
        assert tt == S5_SEG * SUBLANES
        nt = rows // nseq // tt
        grid = (nseq, nt)
        u_spec = pl.BlockSpec((tt, S5_WIDTH), lambda b, t: (b * nt + t, 0))
        st_spec = pl.BlockSpec((None, 1, S5_HID), lambda b, t: (b, 0, 0))
        st_shape = jax.ShapeDtypeStruct((nseq, 1, S5_HID), F32)
        sem = ("parallel", "arbitrary")
    else:
        grid = (rows // tt,)
        u_spec = pl.BlockSpec((tt, S5_WIDTH), lambda i: (i, 0))
        st_spec = pl.BlockSpec((tt // SLAB, S5_HID), lambda i: (i, 0))
        st_shape = jax.ShapeDtypeStruct((rows // SLAB, S5_HID), F32)
        sem = ("parallel",)
    y_spec = u_spec
    bw = S5_HID // S5_BLOCKS
    gw = S5_WIDTH // S5_BLOCKS
    return pl.pallas_call(
        functools.partial(_s5_kernel, chain=chain, tt=tt),
        grid=grid,
        in_specs=[u_spec, st_spec, st_spec,
                  const((2, S5_HID)), const((8, SUBLANES, S5_HID)), const((2, S5_SEG, S5_HID)),
                  const((S5_BLOCKS, gw, bw)), const((S5_BLOCKS, gw, bw)),
                  const((S5_BLOCKS, bw, gw)), const((S5_BLOCKS, bw, gw)),
                  const((1, S5_WIDTH)), const((S5_WIDTH, S5_WIDTH)), const((1, S5_WIDTH))],
        out_specs=[y_spec, st_spec, st_spec],
        out_shape=[jax.ShapeDtypeStruct((rows, S5_WIDTH), F32), st_shape, st_shape],
        scratch_shapes=[pltpu.VMEM((tt, S5_HID), F32), pltpu.VMEM((tt, S5_HID), F32),
                        pltpu.VMEM((tt, S5_WIDTH), F32),
                        pltpu.VMEM((SUBLANES, S5_HID), F32), pltpu.VMEM((SUBLANES, S5_HID), F32)],
        compiler_params=_params(*sem),
        name="s5_chain" if chain else "s5_slab",
    )(proj, h0re, h0im, sw["c"], sw["coef"], sw["pt"], sw["b_re"], sw["b_im"], sw["c_re"], sw["c_im"],
      sw["d"], sw["glu_w"], sw["glu_b"])


def _softplus(x):
    return jnp.maximum(x, 0.0) + jnp.log1p(jnp.exp(-jnp.abs(x)))


def _segment_masks(seg):
    ri = lax.broadcasted_iota(jnp.int32, (GDN_TILE, GDN_TILE), 0)
    ci = lax.broadcasted_iota(jnp.int32, (GDN_TILE, GDN_TILE), 1)
    same = (ri // seg) == (ci // seg)
    causal = (ri >= ci) & same
    strict = (ri > ci) & same
    return same, causal, strict


def _gdn_gates(bd, bdt, alog_row, dtb_row, alog_col, dtb_col, seg, slab):
    same, causal, strict = _segment_masks(seg)
    beta = jax.nn.sigmoid(bd)
    g = -jnp.exp(alog_row) * _softplus(bd + dtb_row)
    gt = -jnp.exp(alog_col) * _softplus(bdt[GDN_HEADS:, :] + dtb_col)
    if slab:
        rreal = (lax.broadcasted_iota(jnp.int32, (GDN_TILE, 1), 0) % SLAB) >= SLAB - SLAB_REAL
        creal = (lax.broadcasted_iota(jnp.int32, (1, GDN_TILE), 1) % SLAB) >= SLAB - SLAB_REAL
        beta = jnp.where(rreal, beta, 0.0)
        g = jnp.where(rreal, g, 0.0)
        gt = jnp.where(creal, gt, 0.0)
    gc = _dot(causal.astype(F32), g, precision=HIGHEST)
    gl = _dot(same.astype(F32), g, precision=HIGHEST)
    ri = lax.broadcasted_iota(jnp.int32, (GDN_TILE, GDN_TILE), 0)
    ci = lax.broadcasted_iota(jnp.int32, (GDN_TILE, GDN_TILE), 1)
    upper = (ri <= ci) & same
    gct = _dot(gt, upper.astype(F32), precision=HIGHEST)
    return beta, gc, gl, gct, causal, strict


def _merge_masks(top):
    ri = lax.broadcasted_iota(jnp.int32, (GDN_TILE, GDN_TILE), 0)
    ci = lax.broadcasted_iota(jnp.int32, (GDN_TILE, GDN_TILE), 1)
    masks = []
    s = 1
    while s < top:
        masks.append(((ri // (2 * s)) == (ci // (2 * s))) & ((ri // s) != (ci // s)))
        s *= 2
    return masks


def _unit_lower_inverse(ms, masks):
    es = [-jnp.where(masks[0], m, 0.0) for m in ms]
    for mask in masks[1:]:
        cs = [jnp.where(mask, m, 0.0) for m in ms]
        ebs = [e.astype(BF16) for e in es]
        xs = [c + _dot(eb, c.astype(BF16)) for c, eb in zip(cs, ebs)]
        es = [e - (x + _dot(x.astype(BF16), eb)) for e, x, eb in zip(es, xs, ebs)]
    return es


def _gdn_intra(qs, ks, vs, betas, gcols, grows, glcols, causal, strict, masks):
    n = range(len(qs))
    qn = [q * lax.rsqrt(jnp.sum(q * q, axis=-1, keepdims=True) + NORM_EPS) * (GDN_HEAD_DIM ** -0.5) for q in qs]
    kn = [k * lax.rsqrt(jnp.sum(k * k, axis=-1, keepdims=True) + NORM_EPS) for k in ks]
    decay = [jnp.exp(jnp.where(causal, gcols[i] - grows[i], -jnp.inf)) for i in n]
    kb = [kn[i] * betas[i] for i in n]
    knb = [k.astype(BF16) for k in kn]
    kk = [lax.dot_general(kb[i].astype(BF16), knb[i], NT_DIMS, preferred_element_type=F32) for i in n]
    ys = _unit_lower_inverse([kk[i] * jnp.where(strict, decay[i], 0.0) for i in n], masks)
    eg = [jnp.exp(g) for g in gcols]
    rhs = [jnp.concatenate([vs[i] * betas[i], kb[i] * eg[i]], axis=1) for i in n]
    uw = [rhs[i] + _dot(ys[i].astype(BF16), rhs[i].astype(BF16)) for i in n]
    attn = [lax.dot_general(qn[i].astype(BF16), knb[i], NT_DIMS, preferred_element_type=F32) * decay[i] for i in n]
    qg = [qn[i] * eg[i] for i in n]
    kd = [kn[i] * jnp.exp(glcols[i] - gcols[i]) for i in n]
    return [(uw[i][:, :GDN_HEAD_DIM], uw[i][:, GDN_HEAD_DIM:], attn[i], qg[i], kd[i]) for i in n]


def _gdn_out(o, gate, norm_w):
    o = o * lax.rsqrt(jnp.mean(o * o, axis=-1, keepdims=True) + NORM_EPS) * norm_w
    return o * _silu(gate)


def _conv4(win_ref, cw, width_slice):
    out = cw[0:1, :] * win_ref[pl.ds(SUBLANES - 3, GDN_TILE), width_slice]
    for j in range(1, GDN_CONV):
        out = out + cw[j:j + 1, :] * win_ref[pl.ds(SUBLANES - 3 + j, GDN_TILE), width_slice]
    return out


def _gdn_chain_kernel(q_ref, k_ref, v_ref, gate_ref, bd_ref, bdt_ref, buf_ref, s0_ref, cw_ref,
                      alr_ref, dbr_ref, alc_ref, dbc_ref, nw_ref,
                      y_ref, sout_ref, bufout_ref, win, s_sc):
    t = pl.program_id(1)
    w3 = 3 * GDN_WIDTH

    @pl.when(t == 0)
    def _():
        win[0:SUBLANES, :] = jnp.zeros((SUBLANES, w3), F32)
        win[SUBLANES - 3:SUBLANES, :] = buf_ref[...]
        s_sc[...] = s0_ref[...]

    win[SUBLANES:, 0:GDN_WIDTH] = q_ref[...]
    win[SUBLANES:, GDN_WIDTH:2 * GDN_WIDTH] = k_ref[...]
    win[SUBLANES:, 2 * GDN_WIDTH:] = v_ref[...]

    beta, gc, gl, gct, causal, strict = _gdn_gates(
        bd_ref[...], bdt_ref[...], alr_ref[...], dbr_ref[...], alc_ref[...], dbc_ref[...],
        GDN_TILE, False)
    masks = _merge_masks(GDN_TILE)
    heads = range(GDN_HEADS)
    qkv = [[], [], []]
    for part in range(3):
        for h in heads:
            cs = slice(part * GDN_WIDTH + h * GDN_HEAD_DIM, part * GDN_WIDTH + (h + 1) * GDN_HEAD_DIM)
            qkv[part].append(_silu(_conv4(win, cw_ref[:, cs], cs)))
    gcols = [gc[:, GDN_HEADS + h:GDN_HEADS + h + 1] for h in heads]
    glcols = [gl[:, GDN_HEADS + h:GDN_HEADS + h + 1] for h in heads]
    intra = _gdn_intra(qkv[0], qkv[1], qkv[2], [beta[:, h:h + 1] for h in heads], gcols,
                       [gct[h:h + 1, :] for h in heads], glcols, causal, strict, masks)
    s_old = [s_sc[h] for h in heads]
    a = [_dot(jnp.concatenate([intra[h][1], intra[h][3]], axis=0).astype(BF16), s_old[h].astype(BF16))
         for h in heads]
    vnb = [(intra[h][0] - a[h][:GDN_TILE]).astype(BF16) for h in heads]
    o = [a[h][GDN_TILE:] + _dot(intra[h][2].astype(BF16), vnb[h]) for h in heads]
    for h in heads:
        s_sc[h] = s_old[h] * jnp.exp(glcols[h][0:1, :]) + _dot(intra[h][4].T.astype(BF16), vnb[h])
    for h in heads:
        hs = slice(h * GDN_HEAD_DIM, (h + 1) * GDN_HEAD_DIM)
        y_ref[:, hs] = _gdn_out(o[h], gate_ref[:, hs], nw_ref[...])

    win[0:SUBLANES, :] = win[GDN_TILE:GDN_TILE + SUBLANES, :]

    @pl.when(t == pl.num_programs(1) - 1)
    def _():
        sout_ref[...] = s_sc[...]
        bufout_ref[...] = win[GDN_TILE + SUBLANES - 3:GDN_TILE + SUBLANES, :]


def _gdn_chain(proj, bd, bdt, conv_buf, s0, gw, *, nseq):
    rows = proj.shape[0]
    nt = rows // nseq // GDN_TILE
    w3 = 3 * GDN_WIDTH
    const = lambda shape: pl.BlockSpec(shape, lambda *_: (0,) * len(shape))
    col = lambda cb: pl.BlockSpec((GDN_TILE, GDN_WIDTH), lambda b, t, cb=cb: (b * nt + t, cb))
    return pl.pallas_call(
        _gdn_chain_kernel,
        grid=(nseq, nt),
        in_specs=[col(1), col(2), col(3), col(4),
                  pl.BlockSpec((GDN_TILE, LANES), lambda b, t: (b * nt + t, 0)),
                  pl.BlockSpec((2 * GDN_HEADS, GDN_TILE), lambda b, t: (0, b * nt + t)),
                  pl.BlockSpec((None, GDN_CONV - 1, w3), lambda b, t: (b, 0, 0)),
                  pl.BlockSpec((None, GDN_HEADS, GDN_HEAD_DIM, GDN_HEAD_DIM), lambda b, t: (b, 0, 0, 0)),
                  const((GDN_CONV, w3)), const((1, LANES)), const((1, LANES)),
                  const((GDN_HEADS, 1)), const((GDN_HEADS, 1)), const((1, GDN_HEAD_DIM))],
        out_specs=[pl.BlockSpec((GDN_TILE, GDN_WIDTH), lambda b, t: (b * nt + t, 0)),
                   pl.BlockSpec((None, GDN_HEADS, GDN_HEAD_DIM, GDN_HEAD_DIM), lambda b, t: (b, 0, 0, 0)),
                   pl.BlockSpec((None, GDN_CONV - 1, w3), lambda b, t: (b, 0, 0))],
        out_shape=[jax.ShapeDtypeStruct((rows, GDN_WIDTH), F32),
                   jax.ShapeDtypeStruct((nseq, GDN_HEADS, GDN_HEAD_DIM, GDN_HEAD_DIM), F32),
                   jax.ShapeDtypeStruct((nseq, GDN_CONV - 1, w3), F32)],
        scratch_shapes=[pltpu.VMEM((GDN_TILE + SUBLANES, w3), F32),
                        pltpu.VMEM((GDN_HEADS, GDN_HEAD_DIM, GDN_HEAD_DIM), F32)],
        compiler_params=_params("parallel", "arbitrary"),
        name="gdn_chain",
    )(proj, proj, proj, proj, bd, bdt, conv_buf, s0, gw["conv_w"], gw["alog_row"], gw["dtb_row"],
      gw["alog_col"], gw["dtb_col"], gw["norm_w"])


def _gdn_slab_kernel(q_ref, k_ref, v_ref, gate_ref, bd_ref, bdt_ref, bq_ref, bk_ref, bv_ref, s0_ref,
                     cwq_ref, cwk_ref, cwv_ref, alr_ref, dbr_ref, alc_ref, dbc_ref, nw_ref,
                     y_ref, sout_ref, oq_ref, ok_ref, ov_ref,
                     win):
    h = pl.program_id(1)
    nslab = GDN_TILE // SLAB
    first = (pl.program_id(0) % (bq_ref.shape[1] // nslab)) * nslab
    real = (lax.broadcasted_iota(jnp.int32, (GDN_TILE, 1), 0) % SLAB) >= SLAB - SLAB_REAL
    qkv = []
    for part, (x_ref, b_ref, cw_ref, o_ref) in enumerate(
            ((q_ref, bq_ref, cwq_ref, oq_ref), (k_ref, bk_ref, cwk_ref, ok_ref), (v_ref, bv_ref, cwv_ref, ov_ref))):
        cs = slice(part * GDN_HEAD_DIM, (part + 1) * GDN_HEAD_DIM)
        x = x_ref[...]
        for i in range(GDN_CONV - 1):
            place = _slab_row_selector(GDN_TILE, b_ref.shape[1], 1 + i, first, transpose=True)
            x = x + _dot(place, b_ref[i], precision=HIGHEST)
        win[0:SUBLANES, cs] = jnp.zeros((SUBLANES, GDN_HEAD_DIM), F32)
        win[SUBLANES:, cs] = x
        conv = _conv4(win, cw_ref[...], cs)
        qkv.append(jnp.where(real, _silu(conv), 0.0))
        for i in range(GDN_CONV - 1):
            take = _slab_row_selector(GDN_TILE, nslab, SLAB - 3 + i, 0, transpose=False)
            o_ref[i] = _dot(take, x, precision=HIGHEST)

    beta, gc, gl, gct, causal, strict = _gdn_gates(
        bd_ref[...], bdt_ref[...], alr_ref[...], dbr_ref[...], alc_ref[...], dbc_ref[...], SLAB, True)
    lane = lax.broadcasted_iota(jnp.int32, (1, LANES), 1)
    sub = lax.broadcasted_iota(jnp.int32, (SUBLANES, 1), 0)
    pick = lambda a, idx: jnp.sum(jnp.where(lane == idx, a, 0.0), axis=1, keepdims=True)
    beta_col = pick(beta, h)
    gcol = pick(gc, GDN_HEADS + h)
    glcol = pick(gl, GDN_HEADS + h)
    grow = jnp.sum(jnp.where(sub == h, gct, 0.0), axis=0, keepdims=True)
    masks = _merge_masks(SLAB_REAL)
    (u, wk, attn, qg, kd), = _gdn_intra([qkv[0]], [qkv[1]], [qkv[2]], [beta_col], [gcol], [grow], [glcol],
                                        causal, strict, masks)
    res = []
    for i in range(nslab):
        rows = slice(i * SLAB, (i + 1) * SLAB)
        lhs = jnp.concatenate([wk[rows], qg[rows]], axis=0).astype(BF16)
        res.append(_dot(lhs, s0_ref[i, 0].astype(BF16)))
    v_new = u - jnp.concatenate([r[:SLAB] for r in res], axis=0)
    vnb = v_new.astype(BF16)
    o = jnp.concatenate([r[SLAB:] for r in res], axis=0) + _dot(attn.astype(BF16), vnb)
    y_ref[...] = _gdn_out(o, gate_ref[...], nw_ref[...])
    kdt = kd.T
    egl = jnp.exp(glcol)
    for i in range(nslab):
        in_slab = (lane // SLAB) == i
        upd = _dot(jnp.where(in_slab, kdt, 0.0).astype(BF16), vnb)
        sout_ref[i, 0] = s0_ref[i, 0] * egl[i * SLAB:i * SLAB + 1, :] + upd


def _gdn_slab(proj, bd, bdt, conv_buf, s0, gw):
    rows = proj.shape[0]
    nb = rows // SLAB
    nslab = GDN_TILE // SLAB
    hb = GDN_WIDTH // GDN_HEAD_DIM
    const = lambda shape: pl.BlockSpec(shape, lambda *_: (0,) * len(shape))
    col = lambda g: pl.BlockSpec((GDN_TILE, GDN_HEAD_DIM), lambda i, h, g=g: (i, g * hb + h))
    per = LANES // nslab
    buf = lambda g: pl.BlockSpec((GDN_CONV - 1, LANES, GDN_HEAD_DIM), lambda i, h, g=g: (0, i // per, g * hb + h))
    cwb = lambda g: pl.BlockSpec((GDN_CONV, GDN_HEAD_DIM), lambda i, h, g=g: (0, g * hb + h))
    st = pl.BlockSpec((nslab, 1, GDN_HEAD_DIM, GDN_HEAD_DIM), lambda i, h: (i, h, 0, 0))
    obuf = pl.BlockSpec((GDN_CONV - 1, nslab, GDN_HEAD_DIM), lambda i, h: (0, i, h))
    tile = (GDN_TILE, GDN_HEAD_DIM)
    return pl.pallas_call(
        _gdn_slab_kernel,
        grid=(rows // GDN_TILE, GDN_HEADS),
        in_specs=[col(1), col(2), col(3), col(4),
                  pl.BlockSpec((GDN_TILE, LANES), lambda i, h: (i, 0)),
                  pl.BlockSpec((2 * GDN_HEADS, GDN_TILE), lambda i, h: (0, i)),
                  buf(0), buf(1), buf(2), st, cwb(0), cwb(1), cwb(2),
                  const((1, LANES)), const((1, LANES)), const((GDN_HEADS, 1)), const((GDN_HEADS, 1)),
                  const((1, GDN_HEAD_DIM))],
        out_specs=[pl.BlockSpec(tile, lambda i, h: (i, h)), st, obuf, obuf, obuf],
        out_shape=[jax.ShapeDtypeStruct((rows, GDN_WIDTH), F32),
                   jax.ShapeDtypeStruct(s0.shape, F32)]
                  + [jax.ShapeDtypeStruct((GDN_CONV - 1, nb, GDN_WIDTH), F32)] * 3,
        scratch_shapes=[pltpu.VMEM((GDN_TILE + SUBLANES, 3 * GDN_HEAD_DIM), F32)],
        compiler_params=_params("parallel", "arbitrary"),
        name="gdn_slab",
    )(proj, proj, proj, proj, bd, bdt, conv_buf, conv_buf, conv_buf, s0,
      gw["conv_w"], gw["conv_w"], gw["conv_w"], gw["alog_row"], gw["dtb_row"],
      gw["alog_col"], gw["dtb_col"], gw["norm_w"])


def _block_diag(w):
    per = S5_GROUPS // S5_BLOCKS
    g, a, b = w.shape
    w = w.reshape(S5_BLOCKS, per, a, b)
    eye = jnp.eye(per, dtype=w.dtype)
    return jnp.einsum("jgab,gk->jgakb", w, eye).reshape(S5_BLOCKS, per * a, per * b)


def _lane_pad(v, offset):
    return jnp.zeros((1, LANES), F32).at[0, offset:offset + v.shape[0]].set(v.astype(F32))


def _layer_weights(l, ln1_g, ln1_b, ffn1_w_in, ffn1_w_out, w_mix_in, s5_lambda_re, s5_lambda_im, s5_log_dt,
                   s5_b_re, s5_b_im, s5_c_re, s5_c_im, s5_d, s5_glu_w, s5_glu_b, gdn_conv_w, gdn_a_log,
                   gdn_dt_bias, gdn_norm_w, w_mix_out, ln2_g, ln2_b, ffn2_w_in, ffn2_w_out, ln3_g, ln3_b):
    row = lambda v: v[l].reshape(1, -1).astype(F32)
    w = {
        "ln1": (row(ln1_g), row(ln1_b)), "ln2": (row(ln2_g), row(ln2_b)), "ln3": (row(ln3_g), row(ln3_b)),
        "ffn1": (ffn1_w_in[l].astype(BF16), ffn1_w_out[l].astype(BF16)),
        "ffn2": (ffn2_w_in[l].astype(BF16), ffn2_w_out[l].astype(BF16)),
        "mix_in": w_mix_in[l][:, :MIX_MAIN].astype(BF16),
        "mix_out": w_mix_out[l].astype(BF16),
    }
    tail = w_mix_in[l][:, MIX_MAIN:]
    w["mix_tail"] = jnp.pad(tail, ((0, 0), (0, LANES - tail.shape[1]))).astype(BF16)
    w["mix_tail_t"] = tail.T.astype(BF16)
    c, coef, pt = _s5_disc(s5_lambda_re[l].reshape(1, S5_HID), s5_lambda_im[l].reshape(1, S5_HID),
                       jnp.repeat(s5_log_dt[l], S5_STATE).reshape(1, S5_HID))
    w["s5"] = {
        "c": c, "coef": coef, "pt": pt,
        "b_re": _block_diag(jnp.swapaxes(s5_b_re[l], 1, 2)).astype(BF16),
        "b_im": _block_diag(jnp.swapaxes(s5_b_im[l], 1, 2)).astype(BF16),
        "c_re": _block_diag(jnp.swapaxes(s5_c_re[l], 1, 2)).astype(BF16),
        "c_im": _block_diag(jnp.swapaxes(s5_c_im[l], 1, 2)).astype(BF16),
        "d": row(s5_d), "glu_w": s5_glu_w[l].astype(BF16), "glu_b": row(s5_glu_b),
    }
    w["gdn"] = {
        "conv_w": gdn_conv_w[l].astype(F32),
        "alog_row": _lane_pad(gdn_a_log[l], GDN_HEADS), "dtb_row": _lane_pad(gdn_dt_bias[l], GDN_HEADS),
        "alog_col": gdn_a_log[l].reshape(GDN_HEADS, 1).astype(F32),
        "dtb_col": gdn_dt_bias[l].reshape(GDN_HEADS, 1).astype(F32),
        "norm_w": row(gdn_norm_w),
    }
    return w


def _mix_projections(x, w):
    proj = _mixin(x, w["mix_in"])
    bd, bdt = _mixin_tail(x, w["mix_tail"], w["mix_tail_t"])
    return proj, bd, bdt


def _prompt_layer(x, w, nseq):
    x = _ffn_ln(x, *w["ffn1"], *w["ln1"])
    proj, bd, bdt = _mix_projections(x, w)
    z_s5 = jnp.zeros((nseq, 1, S5_HID), F32)
    y_s5, n_re, n_im = _s5_mixer(proj, z_s5, z_s5, w["s5"], chain=True, nseq=nseq)
    z_gdn = jnp.zeros((nseq, GDN_HEADS, GDN_HEAD_DIM, GDN_HEAD_DIM), F32)
    z_buf = jnp.zeros((nseq, GDN_CONV - 1, 3 * GDN_WIDTH), F32)
    y_gdn, n_s, n_buf = _gdn_chain(proj, bd, bdt, z_buf, z_gdn, w["gdn"], nseq=nseq)
    x = _mixout_ln(y_s5, y_gdn, x, w["mix_out"], *w["ln2"])
    x = _ffn_ln(x, *w["ffn2"], *w["ln3"])
    shape = (nseq, S5_GROUPS, S5_STATE)
    return x, n_re.reshape(shape), n_im.reshape(shape), n_s, n_buf


def _sample_layer(x, s5_re, s5_im, gdn_s, conv_buf, w, nb, t):
    x = _ffn_ln(x, *w["ffn1"], *w["ln1"])
    xs = jnp.pad(x.reshape(nb, t, D_MODEL), ((0, 0), (SLAB - t, 0), (0, 0))).reshape(nb * SLAB, D_MODEL)
    proj, bd, bdt = _mix_projections(xs, w)
    y_s5, n_re, n_im = _s5_mixer(proj, s5_re.reshape(nb, S5_HID).astype(F32),
                                 s5_im.reshape(nb, S5_HID).astype(F32), w["s5"], chain=False, nseq=nb)
    y_gdn, n_s, bq, bk, bv = _gdn_slab(proj, bd, bdt, jnp.swapaxes(conv_buf.astype(F32), 0, 1),
                                       gdn_s.astype(F32), w["gdn"])
    xs = _mixout_ln(y_s5, y_gdn, xs, w["mix_out"], *w["ln2"])
    x = xs.reshape(nb, SLAB, D_MODEL)[:, SLAB - t:].reshape(nb * t, D_MODEL)
    x = _ffn_ln(x, *w["ffn2"], *w["ln3"])
    shape = (nb, S5_GROUPS, S5_STATE)
    n_buf = jnp.swapaxes(jnp.concatenate([bq, bk, bv], axis=-1), 0, 1)
    return x, n_re.reshape(shape), n_im.reshape(shape), n_s, n_buf


def kernel(x_prompt, x_sample, state_s5_re, state_s5_im, state_gdn, state_conv, ln1_g, ln1_b, ffn1_w_in, ffn1_w_out, w_mix_in, s5_lambda_re, s5_lambda_im, s5_log_dt, s5_b_re, s5_b_im, s5_c_re, s5_c_im, s5_d, s5_glu_w, s5_glu_b, gdn_conv_w, gdn_a_log, gdn_dt_bias, gdn_norm_w, w_mix_out, ln2_g, ln2_b, ffn2_w_in, ffn2_w_out, ln3_g, ln3_b):
    bp, tp, _ = x_prompt.shape
    bs, ts, _ = x_sample.shape
    assert ts == SLAB_REAL and tp % S5_TILE == 0 and (bs * SLAB) % S5_TILE == 0
    depth = ln1_g.shape[0]
    yp = x_prompt.astype(F32).reshape(bp * tp, D_MODEL)
    ys = x_sample.astype(F32).reshape(bs * ts, D_MODEL)
    outs = [[] for _ in range(8)]
    for l in range(depth):
        w = _layer_weights(l, ln1_g, ln1_b, ffn1_w_in, ffn1_w_out, w_mix_in, s5_lambda_re, s5_lambda_im,
                           s5_log_dt, s5_b_re, s5_b_im, s5_c_re, s5_c_im, s5_d, s5_glu_w, s5_glu_b,
                           gdn_conv_w, gdn_a_log, gdn_dt_bias, gdn_norm_w, w_mix_out, ln2_g, ln2_b,
                           ffn2_w_in, ffn2_w_out, ln3_g, ln3_b)
        yp, *p_state = _prompt_layer(yp, w, bp)
        ys, *s_state = _sample_layer(ys, state_s5_re[l], state_s5_im[l], state_gdn[l], state_conv[l], w, bs, ts)
        for acc, val in zip(outs, p_state + s_state):
            acc.append(val)
    return (yp.reshape(x_prompt.shape).astype(x_prompt.dtype), ys.reshape(x_sample.shape).astype(x_sample.dtype),
            *(jnp.stack(o) for o in outs))
```

```python
import functools
import math

import jax
import jax.numpy as jnp
from jax import lax
from jax.experimental import pallas as pl
from jax.experimental.pallas import tpu as pltpu

F32 = jnp.float32
BF16 = jnp.bfloat16

D_MODEL = 2048
S5_WIDTH = 1024
S5_GROUP = 16
S5_GROUPS = 64
S5_STATE = 64
S5_HID = S5_GROUPS * S5_STATE
GDN_WIDTH = 1024
GDN_HEAD_DIM = 128
GDN_HEADS = 8
GDN_CONV = 4
D_FF = 5632
MIX_MAIN = 5120
DEEP_ALPHA = 2.0 ** 0.25
LN_EPS = 1e-5
NORM_EPS = 1e-6

SUBLANES = 8
LANES = 128
SLAB = 8
SLAB_REAL = 4
GDN_TILE = 128
S5_TILE = 256
S5_LCHUNK = 512
S5_SEG = S5_TILE // SUBLANES
S5_BLOCKS = 8
VMEM_LIMIT = 56 * 1024 * 1024

NT_DIMS = (((1,), (1,)), ((), ()))


def _dot(a, b, **kw):
    return jnp.dot(a, b, preferred_element_type=F32, **kw)


def _silu(x):
    return x * jax.nn.sigmoid(x)


def _layer_norm(y, g, b):
    mu = jnp.mean(y, axis=-1, keepdims=True)
    d = y - mu
    var = jnp.mean(d * d, axis=-1, keepdims=True)
    return d * lax.rsqrt(var + LN_EPS) * g + b


def _slab_row_selector(rows, nseq, slab_row, first, *, transpose):
    shape = (rows, nseq) if transpose else (nseq, rows)
    r = lax.broadcasted_iota(jnp.int32, shape, 0 if transpose else 1)
    b = lax.broadcasted_iota(jnp.int32, shape, 1 if transpose else 0)
    return _as_bf16(r == SLAB * (b - first) + slab_row)


def _as_bf16(mask):
    return mask.astype(F32).astype(BF16)


def _dot_split(a, b, dims=None):
    f32_is_lhs = a.dtype == F32
    x = a if f32_is_lhs else b
    acc = None
    for _ in range(3):
        piece = x.astype(BF16)
        lhs, rhs = (piece, b) if f32_is_lhs else (a, piece)
        d = _dot(lhs, rhs) if dims is None else lax.dot_general(lhs, rhs, dims, preferred_element_type=F32)
        acc = d if acc is None else acc + d
        x = x - piece.astype(F32)
    return acc


def _params(*sem):
    return pltpu.CompilerParams(dimension_semantics=sem, vmem_limit_bytes=VMEM_LIMIT)


def _ffn_kernel(x_ref, wg_ref, wu_ref, wo_ref, g_ref, b_ref, o_ref, acc_ref, xb_ref):
    j = pl.program_id(1)

    @pl.when(j == 0)
    def _():
        acc_ref[...] = jnp.zeros_like(acc_ref)
        xb_ref[...] = x_ref[...].astype(BF16)

    xb = xb_ref[...]
    gate = _dot(xb, wg_ref[...])
    up = _dot(xb, wu_ref[...])
    h = _silu(gate) * up
    acc_ref[...] += _dot(h.astype(BF16), wo_ref[...])

    @pl.when(j == pl.num_programs(1) - 1)
    def _():
        y = DEEP_ALPHA * x_ref[...] + 0.5 * acc_ref[...]
        o_ref[...] = _layer_norm(y, g_ref[...], b_ref[...])


def _ffn_ln(x, w_in, w_out, g, b, *, tm=512, tf=512):
    n = x.shape[0]
    nff = D_FF // tf
    return pl.pallas_call(
        _ffn_kernel,
        grid=(n // tm, nff),
        in_specs=[
            pl.BlockSpec((tm, D_MODEL), lambda i, j: (i, 0)),
            pl.BlockSpec((D_MODEL, tf), lambda i, j: (0, j)),
            pl.BlockSpec((D_MODEL, tf), lambda i, j: (0, j + nff)),
            pl.BlockSpec((tf, D_MODEL), lambda i, j: (j, 0)),
            pl.BlockSpec((1, D_MODEL), lambda i, j: (0, 0)),
            pl.BlockSpec((1, D_MODEL), lambda i, j: (0, 0)),
        ],
        out_specs=pl.BlockSpec((tm, D_MODEL), lambda i, j: (i, 0)),
        out_shape=jax.ShapeDtypeStruct((n, D_MODEL), F32),
        scratch_shapes=[pltpu.VMEM((tm, D_MODEL), F32), pltpu.VMEM((tm, D_MODEL), BF16)],
        compiler_params=_params("parallel", "arbitrary"),
        name="ffn_ln",
    )(x, w_in, w_in, w_out, g, b)


def _mixin_kernel(x_ref, w_ref, wt_ref, o_ref, ot_ref, xb_ref):
    @pl.when(pl.program_id(1) == 0)
    def _():
        xb_ref[...] = x_ref[...].astype(BF16)
        ot_ref[...] = lax.dot_general(wt_ref[...], xb_ref[...], NT_DIMS, preferred_element_type=F32)

    o_ref[...] = _dot(xb_ref[...], w_ref[...])


def _mixin(x, w, w_tail_t, *, tm=1024, tn=1024):
    n = x.shape[0]
    return pl.pallas_call(
        _mixin_kernel,
        grid=(n // tm, MIX_MAIN // tn),
        in_specs=[pl.BlockSpec((tm, D_MODEL), lambda i, j: (i, 0)),
                  pl.BlockSpec((D_MODEL, tn), lambda i, j: (0, j)),
                  pl.BlockSpec((2 * GDN_HEADS, D_MODEL), lambda i, j: (0, 0))],
        out_specs=[pl.BlockSpec((tm, tn), lambda i, j: (i, j)),
                   pl.BlockSpec((2 * GDN_HEADS, tm), lambda i, j: (0, i))],
        out_shape=[jax.ShapeDtypeStruct((n, MIX_MAIN), F32),
                   jax.ShapeDtypeStruct((2 * GDN_HEADS, n), F32)],
        scratch_shapes=[pltpu.VMEM((tm, D_MODEL), BF16)],
        compiler_params=_params("parallel", "arbitrary"),
        name="mix_in",
    )(x, w, w_tail_t)


def _mixout_kernel(ya_ref, yb_ref, x_ref, w_ref, g_ref, b_ref, o_ref):
    mix = (_dot(ya_ref[...].astype(BF16), w_ref[0:S5_WIDTH, :])
           + _dot(yb_ref[...].astype(BF16), w_ref[S5_WIDTH:, :]))
    o_ref[...] = _layer_norm(DEEP_ALPHA * x_ref[...] + mix, g_ref[...], b_ref[...])


def _mixout_ln(ya, yb, x, w, g, b, *, tm=512):
    n = x.shape[0]
    return pl.pallas_call(
        _mixout_kernel,
        grid=(n // tm,),
        in_specs=[pl.BlockSpec((tm, S5_WIDTH), lambda i: (i, 0)),
                  pl.BlockSpec((tm, GDN_WIDTH), lambda i: (i, 0)),
                  pl.BlockSpec((tm, D_MODEL), lambda i: (i, 0)),
                  pl.BlockSpec((D_MODEL, D_MODEL), lambda i: (0, 0)),
                  pl.BlockSpec((1, D_MODEL), lambda i: (0, 0)),
                  pl.BlockSpec((1, D_MODEL), lambda i: (0, 0))],
        out_specs=pl.BlockSpec((tm, D_MODEL), lambda i: (i, 0)),
        out_shape=jax.ShapeDtypeStruct((n, D_MODEL), F32),
        compiler_params=_params("parallel"),
        name="mix_out_ln",
    )(ya, yb, x, w, g, b)


def _s5_disc_kernel(lre_ref, lim_ref, ldt_ref, c_ref, coef_ref, pt_ref):
    lr, li = lre_ref[...], lim_ref[...]
    dt = jnp.exp(ldt_ref[...])
    mag = jnp.exp(lr * dt)
    ar = mag * jnp.cos(li * dt)
    ai = mag * jnp.sin(li * dt)
    nr, ni = ar - 1.0, ai
    den = lr * lr + li * li
    c_ref[0:1, :] = (nr * lr + ni * li) / den
    c_ref[1:2, :] = (ni * lr - nr * li) / den

    def cmul(x, y):
        return x[0] * y[0] - x[1] * y[1], x[0] * y[1] + x[1] * y[0]

    width = lr.shape[-1]
    p = (ar, ai)
    for j in range(S5_SEG):
        pt_ref[0, j:j + 1, :] = p[0]
        pt_ref[1, j:j + 1, :] = p[1]
        if j + 1 < S5_SEG:
            p = cmul(p, (ar, ai))
    coef_ref[...] = jnp.zeros_like(coef_ref)
    for part, v in enumerate((ar, ai)):
        coef_ref[part] = jnp.broadcast_to(v, (SUBLANES, width))
    for k, s in enumerate((1, 2, 4)):
        for part in range(2):
            coef_ref[2 + 2 * k + part, s:SUBLANES, :] = jnp.broadcast_to(p[part], (SUBLANES - s, width))
        p = cmul(p, p)


def _s5_disc(lre, lim, ldt):
    return pl.pallas_call(
        _s5_disc_kernel,
        out_shape=[jax.ShapeDtypeStruct((2, S5_HID), F32),
                   jax.ShapeDtypeStruct((8, SUBLANES, S5_HID), F32),
                   jax.ShapeDtypeStruct((2, S5_SEG, S5_HID), F32)],
        name="s5_disc",
    )(lre, lim, ldt)


def _gelu_tanh(y):
    return 0.5 * y * (1.0 + jnp.tanh(math.sqrt(2.0 / math.pi) * (y + 0.044715 * (y * y * y))))


def _scan_layout(tt, group, *, inverse):
    p = lax.broadcasted_iota(jnp.int32, (tt, tt), 1 if inverse else 0)
    n = lax.broadcasted_iota(jnp.int32, (tt, tt), 0 if inverse else 1)
    rem = p % group
    src = (p - rem) + (rem % SUBLANES) * (group // SUBLANES) + rem // SUBLANES
    return _as_bf16(n == src)


def _cmul_add(ar, ai, xr, xi, br, bi):
    return ar * xr - ai * xi + br, ar * xi + ai * xr + bi


def _s5_kernel(u_ref, h0re_ref, h0im_ref, c_ref, coef_ref, pt_ref, bre_ref, bim_ref, cwre_ref, cwim_ref,
               d_ref, gw_ref, gb_ref, y_ref, sre_ref, sim_ref, hre, him, ysc, car_re, car_im,
               *, chain, tt):
    group = tt if chain else SLAB * SUBLANES
    seg = group // SUBLANES
    u = u_ref[...]
    ub = _dot(_scan_layout(tt, group, inverse=False), u.astype(BF16)).astype(BF16)
    bw = S5_HID // S5_BLOCKS
    gw = S5_WIDTH // S5_BLOCKS
    for j in range(S5_BLOCKS):
        uj = ub[:, j * gw:(j + 1) * gw]
        r = _dot(uj, bre_ref[j])
        im = _dot(uj, bim_ref[j])
        cr = c_ref[0:1, j * bw:(j + 1) * bw]
        ci = c_ref[1:2, j * bw:(j + 1) * bw]
        hre[:, j * bw:(j + 1) * bw] = cr * r - ci * im
        him[:, j * bw:(j + 1) * bw] = cr * im + ci * r

    if chain:
        @pl.when(pl.program_id(1) == 0)
        def _():
            car_re[...] = jnp.broadcast_to(h0re_ref[...], car_re.shape)
            car_im[...] = jnp.broadcast_to(h0im_ref[...], car_im.shape)

    sub = lax.broadcasted_iota(jnp.int32, (SUBLANES, 1), 0)
    for c in range(S5_HID // S5_LCHUNK):
        sl = slice(c * S5_LCHUNK, (c + 1) * S5_LCHUNK)
        ar, ai = coef_ref[0, :, sl], coef_ref[1, :, sl]
        if not chain:
            for g in range(tt // group):
                srows = slice(g * SUBLANES, (g + 1) * SUBLANES)
                xr, xi = h0re_ref[srows, sl], h0im_ref[srows, sl]
                for j in range(SLAB - SLAB_REAL, SLAB):
                    rows = slice(g * group + j * SUBLANES, g * group + (j + 1) * SUBLANES)
                    xr, xi = _cmul_add(ar, ai, xr, xi, hre[rows, sl], him[rows, sl])
                    hre[rows, sl] = xr
                    him[rows, sl] = xi
                sre_ref[srows, sl] = xr
                sim_ref[srows, sl] = xi
            continue

        def local_scan(j, x, sl=sl, ar=ar, ai=ai):
            rows = pl.ds(pl.multiple_of(j * SUBLANES, SUBLANES), SUBLANES)
            xr, xi = _cmul_add(ar, ai, x[0], x[1], hre[rows, sl], him[rows, sl])
            hre[rows, sl] = xr
            him[rows, sl] = xi
            return xr, xi

        zero = jnp.zeros((SUBLANES, S5_LCHUNK), F32)
        fr, fi = lax.fori_loop(0, seg, local_scan, (zero, zero), unroll=4)
        kr = jnp.where(sub == 0, car_re[:, sl], pltpu.roll(fr, 1, 0))
        ki = jnp.where(sub == 0, car_im[:, sl], pltpu.roll(fi, 1, 0))
        for k, s in enumerate((1, 2, 4)):
            pr, pi = coef_ref[2 + 2 * k, :, sl], coef_ref[3 + 2 * k, :, sl]
            kr, ki = _cmul_add(pr, pi, pltpu.roll(kr, s, 0), pltpu.roll(ki, s, 0), kr, ki)
        outr, outi = _cmul_add(coef_ref[2, :, sl], coef_ref[3, :, sl], kr, ki, fr, fi)
        car_re[:, sl] = jnp.broadcast_to(outr[SUBLANES - 1:SUBLANES, :], outr.shape)
        car_im[:, sl] = jnp.broadcast_to(outi[SUBLANES - 1:SUBLANES, :], outi.shape)

        def add_carry(j, _, sl=sl, kr=kr, ki=ki):
            rows = pl.ds(pl.multiple_of(j * SUBLANES, SUBLANES), SUBLANES)
            pr = jnp.broadcast_to(pt_ref[0, pl.ds(j, 1), sl], kr.shape)
            pi = jnp.broadcast_to(pt_ref[1, pl.ds(j, 1), sl], kr.shape)
            xr, xi = _cmul_add(pr, pi, kr, ki, hre[rows, sl], him[rows, sl])
            hre[rows, sl] = xr
            him[rows, sl] = xi
            return 0

        lax.fori_loop(0, seg, add_carry, 0, unroll=4)

    if chain:
        @pl.when(pl.program_id(1) == pl.num_programs(1) - 1)
        def _():
            sre_ref[...] = car_re[0:1, :]
            sim_ref[...] = car_im[0:1, :]

    for j in range(S5_BLOCKS):
        hr = hre[:, j * bw:(j + 1) * bw].astype(BF16)
        hi = him[:, j * bw:(j + 1) * bw].astype(BF16)
        ysc[:, j * gw:(j + 1) * gw] = _dot(hr, cwre_ref[j]) - _dot(hi, cwim_ref[j])
    y = _dot_split(_scan_layout(tt, group, inverse=True), ysc[...])
    z = _gelu_tanh(y + d_ref[...] * u)
    gl = _dot(z.astype(BF16), gw_ref[...]) + gb_ref[...]
    y_ref[...] = z * jax.nn.sigmoid(gl)


def _s5_mixer(proj, h0re, h0im, sw, *, chain, nseq, tt=S5_TILE):
    rows = proj.shape[0]
    const = lambda shape: pl.BlockSpec(shape, lambda *_: (0,) * len(shape))
    if chain:
        assert tt == S5_SEG * SUBLANES
        nt = rows // nseq // tt
        grid = (nseq, nt)
        u_spec = pl.BlockSpec((tt, S5_WIDTH), lambda b, t: (b * nt + t, 0))
        st_spec = pl.BlockSpec((None, 1, S5_HID), lambda b, t: (b, 0, 0))
        st_shape = jax.ShapeDtypeStruct((nseq, 1, S5_HID), F32)
        sem = ("parallel", "arbitrary")
    else:
        grid = (rows // tt,)
        u_spec = pl.BlockSpec((tt, S5_WIDTH), lambda i: (i, 0))
        st_spec = pl.BlockSpec((tt // SLAB, S5_HID), lambda i: (i, 0))
        st_shape = jax.ShapeDtypeStruct((rows // SLAB, S5_HID), F32)
        sem = ("parallel",)
    y_spec = u_spec
    bw = S5_HID // S5_BLOCKS
    gw = S5_WIDTH // S5_BLOCKS
    return pl.pallas_call(
        functools.partial(_s5_kernel, chain=chain, tt=tt),
        grid=grid,
        in_specs=[u_spec, st_spec, st_spec,
                  const((2, S5_HID)), const((8, SUBLANES, S5_HID)), const((2, S5_SEG, S5_HID)),
                  const((S5_BLOCKS, gw, bw)), const((S5_BLOCKS, gw, bw)),
                  const((S5_BLOCKS, bw, gw)), const((S5_BLOCKS, bw, gw)),
                  const((1, S5_WIDTH)), const((S5_WIDTH, S5_WIDTH)), const((1, S5_WIDTH))],
        out_specs=[y_spec, st_spec, st_spec],
        out_shape=[jax.ShapeDtypeStruct((rows, S5_WIDTH), F32), st_shape, st_shape],
        scratch_shapes=[pltpu.VMEM((tt, S5_HID), F32), pltpu.VMEM((tt, S5_HID), F32),
                        pltpu.VMEM((tt, S5_WIDTH), F32),
                        pltpu.VMEM((SUBLANES, S5_HID), F32), pltpu.VMEM((SUBLANES, S5_HID), F32)],
        compiler_params=_params(*sem),
        name="s5_chain" if chain else "s5_slab",
    )(proj, h0re, h0im, sw["c"], sw["coef"], sw["pt"], sw["b_re"], sw["b_im"], sw["c_re"], sw["c_im"],
      sw["d"], sw["glu_w"], sw["glu_b"])


def _softplus(x):
    return jnp.maximum(x, 0.0) + jnp.log1p(jnp.exp(-jnp.abs(x)))


def _segment_masks(seg):
    ri = lax.broadcasted_iota(jnp.int32, (GDN_TILE, GDN_TILE), 0)
    ci = lax.broadcasted_iota(jnp.int32, (GDN_TILE, GDN_TILE), 1)
    same = (ri // seg) == (ci // seg)
    causal = (ri >= ci) & same
    strict = (ri > ci) & same
    return same, causal, strict


def _gdn_gates(bdt, alog_col, dtb_col, seg, slab):
    same, causal, strict = _segment_masks(seg)
    bt = jax.nn.sigmoid(bdt[:GDN_HEADS, :])
    gt = -jnp.exp(alog_col) * _softplus(bdt[GDN_HEADS:, :] + dtb_col)
    if slab:
        creal = (lax.broadcasted_iota(jnp.int32, (1, GDN_TILE), 1) % SLAB) >= SLAB - SLAB_REAL
        bt = jnp.where(creal, bt, 0.0)
        gt = jnp.where(creal, gt, 0.0)
    ri = lax.broadcasted_iota(jnp.int32, (GDN_TILE, GDN_TILE), 0)
    ci = lax.broadcasted_iota(jnp.int32, (GDN_TILE, GDN_TILE), 1)
    gates = jnp.concatenate([bt, gt, jnp.zeros((GDN_TILE - 2 * GDN_HEADS, GDN_TILE), F32)], axis=0)
    lhs = jnp.concatenate([_as_bf16(ri == ci), _as_bf16(causal), _as_bf16(same)], axis=0)
    cols = _dot_split(lhs, gates, NT_DIMS)
    beta, gc, gl = cols[:GDN_TILE], cols[GDN_TILE:2 * GDN_TILE], cols[2 * GDN_TILE:]
    gct = _dot_split(gt, _as_bf16((ri <= ci) & same))
    return beta, gc, gl, gct, causal, strict


def _merge_masks(top):
    ri = lax.broadcasted_iota(jnp.int32, (GDN_TILE, GDN_TILE), 0)
    ci = lax.broadcasted_iota(jnp.int32, (GDN_TILE, GDN_TILE), 1)
    masks = []
    s = 1
    while s < top:
        masks.append(((ri // (2 * s)) == (ci // (2 * s))) & ((ri // s) != (ci // s)))
        s *= 2
    return masks


def _unit_lower_inverse(ms, masks):
    es = [-jnp.where(masks[0], m, 0.0) for m in ms]
    for mask in masks[1:]:
        cs = [jnp.where(mask, m, 0.0) for m in ms]
        ebs = [e.astype(BF16) for e in es]
        xs = [c + _dot(eb, c.astype(BF16)) for c, eb in zip(cs, ebs)]
        es = [e - (x + _dot(x.astype(BF16), eb)) for e, x, eb in zip(es, xs, ebs)]
    return es


def _gdn_intra(qs, ks, vs, betas, gcols, grows, glcols, causal, strict, masks):
    n = range(len(qs))
    qn = [q * lax.rsqrt(jnp.sum(q * q, axis=-1, keepdims=True) + NORM_EPS) * (GDN_HEAD_DIM ** -0.5) for q in qs]
    kn = [k * lax.rsqrt(jnp.sum(k * k, axis=-1, keepdims=True) + NORM_EPS) for k in ks]
    decay = [jnp.exp(jnp.where(causal, gcols[i] - grows[i], -jnp.inf)) for i in n]
    kb = [kn[i] * betas[i] for i in n]
    knb = [k.astype(BF16) for k in kn]
    kk = [lax.dot_general(kb[i].astype(BF16), knb[i], NT_DIMS, preferred_element_type=F32) for i in n]
    ys = _unit_lower_inverse([kk[i] * jnp.where(strict, decay[i], 0.0) for i in n], masks)
    eg = [jnp.exp(g) for g in gcols]
    rhs = [jnp.concatenate([vs[i] * betas[i], kb[i] * eg[i]], axis=1) for i in n]
    uw = [rhs[i] + _dot(ys[i].astype(BF16), rhs[i].astype(BF16)) for i in n]
    attn = [lax.dot_general(qn[i].astype(BF16), knb[i], NT_DIMS, preferred_element_type=F32) * decay[i] for i in n]
    qg = [qn[i] * eg[i] for i in n]
    kd = [kn[i] * jnp.exp(glcols[i] - gcols[i]) for i in n]
    return [(uw[i][:, :GDN_HEAD_DIM], uw[i][:, GDN_HEAD_DIM:], attn[i], qg[i], kd[i]) for i in n]


def _gdn_out(o, gate, norm_w):
    o = o * lax.rsqrt(jnp.mean(o * o, axis=-1, keepdims=True) + NORM_EPS) * norm_w
    return o * _silu(gate)


def _conv4(win_ref, cw, width_slice):
    out = cw[0:1, :] * win_ref[pl.ds(SUBLANES - 3, GDN_TILE), width_slice]
    for j in range(1, GDN_CONV):
        out = out + cw[j:j + 1, :] * win_ref[pl.ds(SUBLANES - 3 + j, GDN_TILE), width_slice]
    return out


def _gdn_chain_kernel(q_ref, k_ref, v_ref, gate_ref, bdt_ref, buf_ref, s0_ref, cw_ref,
                      alc_ref, dbc_ref, nw_ref,
                      y_ref, sout_ref, bufout_ref, win, s_sc):
    t = pl.program_id(1)
    w3 = 3 * GDN_WIDTH

    @pl.when(t == 0)
    def _():
        win[0:SUBLANES, :] = jnp.zeros((SUBLANES, w3), F32)
        win[SUBLANES - 3:SUBLANES, :] = buf_ref[...]
        s_sc[...] = s0_ref[...]

    win[SUBLANES:, 0:GDN_WIDTH] = q_ref[...]
    win[SUBLANES:, GDN_WIDTH:2 * GDN_WIDTH] = k_ref[...]
    win[SUBLANES:, 2 * GDN_WIDTH:] = v_ref[...]

    beta, gc, gl, gct, causal, strict = _gdn_gates(bdt_ref[...], alc_ref[...], dbc_ref[...], GDN_TILE, False)
    masks = _merge_masks(GDN_TILE)
    heads = range(GDN_HEADS)
    qkv = [[], [], []]
    for part in range(3):
        for h in heads:
            cs = slice(part * GDN_WIDTH + h * GDN_HEAD_DIM, part * GDN_WIDTH + (h + 1) * GDN_HEAD_DIM)
            qkv[part].append(_silu(_conv4(win, cw_ref[:, cs], cs)))
    gcols = [gc[:, GDN_HEADS + h:GDN_HEADS + h + 1] for h in heads]
    glcols = [gl[:, GDN_HEADS + h:GDN_HEADS + h + 1] for h in heads]
    intra = _gdn_intra(qkv[0], qkv[1], qkv[2], [beta[:, h:h + 1] for h in heads], gcols,
                       [gct[h:h + 1, :] for h in heads], glcols, causal, strict, masks)
    s_old = [s_sc[h] for h in heads]
    a = [_dot(jnp.concatenate([intra[h][1], intra[h][3]], axis=0).astype(BF16), s_old[h].astype(BF16))
         for h in heads]
    vnb = [(intra[h][0] - a[h][:GDN_TILE]).astype(BF16) for h in heads]
    o = [a[h][GDN_TILE:] + _dot(intra[h][2].astype(BF16), vnb[h]) for h in heads]
    for h in heads:
        s_sc[h] = s_old[h] * jnp.exp(glcols[h][0:1, :]) + _dot(intra[h][4].T.astype(BF16), vnb[h])
    for h in heads:
        hs = slice(h * GDN_HEAD_DIM, (h + 1) * GDN_HEAD_DIM)
        y_ref[:, hs] = _gdn_out(o[h], gate_ref[:, hs], nw_ref[...])

    win[0:SUBLANES, :] = win[GDN_TILE:GDN_TILE + SUBLANES, :]

    @pl.when(t == pl.num_programs(1) - 1)
    def _():
        sout_ref[...] = s_sc[...]
        bufout_ref[...] = win[GDN_TILE + SUBLANES - 3:GDN_TILE + SUBLANES, :]


def _gdn_chain(proj, bdt, conv_buf, s0, gw, *, nseq):
    rows = proj.shape[0]
    nt = rows // nseq // GDN_TILE
    w3 = 3 * GDN_WIDTH
    const = lambda shape: pl.BlockSpec(shape, lambda *_: (0,) * len(shape))
    col = lambda cb: pl.BlockSpec((GDN_TILE, GDN_WIDTH), lambda b, t, cb=cb: (b * nt + t, cb))
    return pl.pallas_call(
        _gdn_chain_kernel,
        grid=(nseq, nt),
        in_specs=[col(1), col(2), col(3), col(4),
                  pl.BlockSpec((2 * GDN_HEADS, GDN_TILE), lambda b, t: (0, b * nt + t)),
                  pl.BlockSpec((None, GDN_CONV - 1, w3), lambda b, t: (b, 0, 0)),
                  pl.BlockSpec((None, GDN_HEADS, GDN_HEAD_DIM, GDN_HEAD_DIM), lambda b, t: (b, 0, 0, 0)),
                  const((GDN_CONV, w3)), const((GDN_HEADS, 1)), const((GDN_HEADS, 1)), const((1, GDN_HEAD_DIM))],
        out_specs=[pl.BlockSpec((GDN_TILE, GDN_WIDTH), lambda b, t: (b * nt + t, 0)),
                   pl.BlockSpec((None, GDN_HEADS, GDN_HEAD_DIM, GDN_HEAD_DIM), lambda b, t: (b, 0, 0, 0)),
                   pl.BlockSpec((None, GDN_CONV - 1, w3), lambda b, t: (b, 0, 0))],
        out_shape=[jax.ShapeDtypeStruct((rows, GDN_WIDTH), F32),
                   jax.ShapeDtypeStruct((nseq, GDN_HEADS, GDN_HEAD_DIM, GDN_HEAD_DIM), F32),
                   jax.ShapeDtypeStruct((nseq, GDN_CONV - 1, w3), F32)],
        scratch_shapes=[pltpu.VMEM((GDN_TILE + SUBLANES, w3), F32),
                        pltpu.VMEM((GDN_HEADS, GDN_HEAD_DIM, GDN_HEAD_DIM), F32)],
        compiler_params=_params("parallel", "arbitrary"),
        name="gdn_chain",
    )(proj, proj, proj, proj, bdt, conv_buf, s0, gw["conv_w"], gw["alog_col"], gw["dtb_col"], gw["norm_w"])


def _gdn_slab_kernel(q_ref, k_ref, v_ref, gate_ref, bdt_ref, bq_ref, bk_ref, bv_ref, s0_ref,
                     cwq_ref, cwk_ref, cwv_ref, alc_ref, dbc_ref, nw_ref,
                     y_ref, sout_ref, oq_ref, ok_ref, ov_ref,
                     win):
    h = pl.program_id(1)
    nslab = GDN_TILE // SLAB
    first = (pl.program_id(0) % (bq_ref.shape[1] // nslab)) * nslab
    real = (lax.broadcasted_iota(jnp.int32, (GDN_TILE, 1), 0) % SLAB) >= SLAB - SLAB_REAL
    qkv = []
    taps = range(GDN_CONV - 1)
    place = [_slab_row_selector(GDN_TILE, bq_ref.shape[1], 1 + i, first, transpose=True) for i in taps]
    take = [_slab_row_selector(GDN_TILE, nslab, SLAB - 3 + i, 0, transpose=False) for i in taps]
    for part, (x_ref, b_ref, cw_ref, o_ref) in enumerate(
            ((q_ref, bq_ref, cwq_ref, oq_ref), (k_ref, bk_ref, cwk_ref, ok_ref), (v_ref, bv_ref, cwv_ref, ov_ref))):
        cs = slice(part * GDN_HEAD_DIM, (part + 1) * GDN_HEAD_DIM)
        x = x_ref[...]
        for i in taps:
            x = x + _dot_split(place[i], b_ref[i])
        win[0:SUBLANES, cs] = jnp.zeros((SUBLANES, GDN_HEAD_DIM), F32)
        win[SUBLANES:, cs] = x
        conv = _conv4(win, cw_ref[...], cs)
        qkv.append(jnp.where(real, _silu(conv), 0.0))
        for i in taps:
            o_ref[i] = _dot_split(take[i], x)

    beta, gc, gl, gct, causal, strict = _gdn_gates(bdt_ref[...], alc_ref[...], dbc_ref[...], SLAB, True)
    lane = lax.broadcasted_iota(jnp.int32, (1, LANES), 1)
    sub = lax.broadcasted_iota(jnp.int32, (SUBLANES, 1), 0)
    pick = lambda a, idx: jnp.sum(jnp.where(lane == idx, a, 0.0), axis=1, keepdims=True)
    beta_col = pick(beta, h)
    gcol = pick(gc, GDN_HEADS + h)
    glcol = pick(gl, GDN_HEADS + h)
    grow = jnp.sum(jnp.where(sub == h, gct, 0.0), axis=0, keepdims=True)
    masks = _merge_masks(SLAB_REAL)
    (u, wk, attn, qg, kd), = _gdn_intra([qkv[0]], [qkv[1]], [qkv[2]], [beta_col], [gcol], [grow], [glcol],
                                        causal, strict, masks)
    res = []
    for i in range(nslab):
        rows = slice(i * SLAB, (i + 1) * SLAB)
        lhs = jnp.concatenate([wk[rows], qg[rows]], axis=0).astype(BF16)
        res.append(_dot(lhs, s0_ref[i, 0].astype(BF16)))
    v_new = u - jnp.concatenate([r[:SLAB] for r in res], axis=0)
    vnb = v_new.astype(BF16)
    o = jnp.concatenate([r[SLAB:] for r in res], axis=0) + _dot(attn.astype(BF16), vnb)
    y_ref[...] = _gdn_out(o, gate_ref[...], nw_ref[...])
    kdt = kd.T
    egl = jnp.exp(glcol)
    for i in range(nslab):
        in_slab = (lane // SLAB) == i
        upd = _dot(jnp.where(in_slab, kdt, 0.0).astype(BF16), vnb)
        sout_ref[i, 0] = s0_ref[i, 0] * egl[i * SLAB:i * SLAB + 1, :] + upd


def _gdn_slab(proj, bdt, conv_buf, s0, gw):
    rows = proj.shape[0]
    nb = rows // SLAB
    nslab = GDN_TILE // SLAB
    hb = GDN_WIDTH // GDN_HEAD_DIM
    const = lambda shape: pl.BlockSpec(shape, lambda *_: (0,) * len(shape))
    col = lambda g: pl.BlockSpec((GDN_TILE, GDN_HEAD_DIM), lambda i, h, g=g: (i, g * hb + h))
    per = LANES // nslab
    buf = lambda g: pl.BlockSpec((GDN_CONV - 1, LANES, GDN_HEAD_DIM), lambda i, h, g=g: (0, i // per, g * hb + h))
    cwb = lambda g: pl.BlockSpec((GDN_CONV, GDN_HEAD_DIM), lambda i, h, g=g: (0, g * hb + h))
    st = pl.BlockSpec((nslab, 1, GDN_HEAD_DIM, GDN_HEAD_DIM), lambda i, h: (i, h, 0, 0))
    obuf = pl.BlockSpec((GDN_CONV - 1, nslab, GDN_HEAD_DIM), lambda i, h: (0, i, h))
    tile = (GDN_TILE, GDN_HEAD_DIM)
    return pl.pallas_call(
        _gdn_slab_kernel,
        grid=(rows // GDN_TILE, GDN_HEADS),
        in_specs=[col(1), col(2), col(3), col(4),
                  pl.BlockSpec((2 * GDN_HEADS, GDN_TILE), lambda i, h: (0, i)),
                  buf(0), buf(1), buf(2), st, cwb(0), cwb(1), cwb(2),
                  const((GDN_HEADS, 1)), const((GDN_HEADS, 1)), const((1, GDN_HEAD_DIM))],
        out_specs=[pl.BlockSpec(tile, lambda i, h: (i, h)), st, obuf, obuf, obuf],
        out_shape=[jax.ShapeDtypeStruct((rows, GDN_WIDTH), F32),
                   jax.ShapeDtypeStruct(s0.shape, F32)]
                  + [jax.ShapeDtypeStruct((GDN_CONV - 1, nb, GDN_WIDTH), F32)] * 3,
        scratch_shapes=[pltpu.VMEM((GDN_TILE + SUBLANES, 3 * GDN_HEAD_DIM), F32)],
        compiler_params=_params("parallel", "arbitrary"),
        name="gdn_slab",
    )(proj, proj, proj, proj, bdt, conv_buf, conv_buf, conv_buf, s0,
      gw["conv_w"], gw["conv_w"], gw["conv_w"], gw["alog_col"], gw["dtb_col"], gw["norm_w"])


def _block_diag(w):
    per = S5_GROUPS // S5_BLOCKS
    g, a, b = w.shape
    w = w.reshape(S5_BLOCKS, per, a, b)
    eye = jnp.eye(per, dtype=w.dtype)
    return jnp.einsum("jgab,gk->jgakb", w, eye).reshape(S5_BLOCKS, per * a, per * b)


def _layer_weights(l, ln1_g, ln1_b, ffn1_w_in, ffn1_w_out, w_mix_in, s5_lambda_re, s5_lambda_im, s5_log_dt,
                   s5_b_re, s5_b_im, s5_c_re, s5_c_im, s5_d, s5_glu_w, s5_glu_b, gdn_conv_w, gdn_a_log,
                   gdn_dt_bias, gdn_norm_w, w_mix_out, ln2_g, ln2_b, ffn2_w_in, ffn2_w_out, ln3_g, ln3_b):
    row = lambda v: v[l].reshape(1, -1).astype(F32)
    w = {
        "ln1": (row(ln1_g), row(ln1_b)), "ln2": (row(ln2_g), row(ln2_b)), "ln3": (row(ln3_g), row(ln3_b)),
        "ffn1": (ffn1_w_in[l].astype(BF16), ffn1_w_out[l].astype(BF16)),
        "ffn2": (ffn2_w_in[l].astype(BF16), ffn2_w_out[l].astype(BF16)),
        "mix_in": w_mix_in[l].astype(BF16),
        "mix_tail_t": w_mix_in[l][:, MIX_MAIN:].T.astype(BF16),
        "mix_out": w_mix_out[l].astype(BF16),
    }
    c, coef, pt = _s5_disc(s5_lambda_re[l].reshape(1, S5_HID), s5_lambda_im[l].reshape(1, S5_HID),
                       jnp.repeat(s5_log_dt[l], S5_STATE).reshape(1, S5_HID))
    w["s5"] = {
        "c": c, "coef": coef, "pt": pt,
        "b_re": _block_diag(jnp.swapaxes(s5_b_re[l], 1, 2)).astype(BF16),
        "b_im": _block_diag(jnp.swapaxes(s5_b_im[l], 1, 2)).astype(BF16),
        "c_re": _block_diag(jnp.swapaxes(s5_c_re[l], 1, 2)).astype(BF16),
        "c_im": _block_diag(jnp.swapaxes(s5_c_im[l], 1, 2)).astype(BF16),
        "d": row(s5_d), "glu_w": s5_glu_w[l].astype(BF16), "glu_b": row(s5_glu_b),
    }
    w["gdn"] = {
        "conv_w": gdn_conv_w[l].astype(F32),
        "alog_col": gdn_a_log[l].reshape(GDN_HEADS, 1).astype(F32),
        "dtb_col": gdn_dt_bias[l].reshape(GDN_HEADS, 1).astype(F32),
        "norm_w": row(gdn_norm_w),
    }
    return w


def _prompt_layer(x, w, nseq):
    x = _ffn_ln(x, *w["ffn1"], *w["ln1"])
    proj, bdt = _mixin(x, w["mix_in"], w["mix_tail_t"])
    z_s5 = jnp.zeros((nseq, 1, S5_HID), F32)
    y_s5, n_re, n_im = _s5_mixer(proj, z_s5, z_s5, w["s5"], chain=True, nseq=nseq)
    z_gdn = jnp.zeros((nseq, GDN_HEADS, GDN_HEAD_DIM, GDN_HEAD_DIM), F32)
    z_buf = jnp.zeros((nseq, GDN_CONV - 1, 3 * GDN_WIDTH), F32)
    y_gdn, n_s, n_buf = _gdn_chain(proj, bdt, z_buf, z_gdn, w["gdn"], nseq=nseq)
    x = _mixout_ln(y_s5, y_gdn, x, w["mix_out"], *w["ln2"])
    x = _ffn_ln(x, *w["ffn2"], *w["ln3"])
    shape = (nseq, S5_GROUPS, S5_STATE)
    return x, n_re.reshape(shape), n_im.reshape(shape), n_s, n_buf


def _sample_layer(x, s5_re, s5_im, gdn_s, conv_buf, w, nb, t):
    x = _ffn_ln(x, *w["ffn1"], *w["ln1"])
    xs = jnp.pad(x.reshape(nb, t, D_MODEL), ((0, 0), (SLAB - t, 0), (0, 0))).reshape(nb * SLAB, D_MODEL)
    proj, bdt = _mixin(xs, w["mix_in"], w["mix_tail_t"])
    y_s5, n_re, n_im = _s5_mixer(proj, s5_re.reshape(nb, S5_HID).astype(F32),
                                 s5_im.reshape(nb, S5_HID).astype(F32), w["s5"], chain=False, nseq=nb)
    y_gdn, n_s, bq, bk, bv = _gdn_slab(proj, bdt, jnp.swapaxes(conv_buf.astype(F32), 0, 1),
                                       gdn_s.astype(F32), w["gdn"])
    xs = _mixout_ln(y_s5, y_gdn, xs, w["mix_out"], *w["ln2"])
    x = xs.reshape(nb, SLAB, D_MODEL)[:, SLAB - t:].reshape(nb * t, D_MODEL)
    x = _ffn_ln(x, *w["ffn2"], *w["ln3"])
    shape = (nb, S5_GROUPS, S5_STATE)
    n_buf = jnp.swapaxes(jnp.concatenate([bq, bk, bv], axis=-1), 0, 1)
    return x, n_re.reshape(shape), n_im.reshape(shape), n_s, n_buf


def kernel(x_prompt, x_sample, state_s5_re, state_s5_im, state_gdn, state_conv, ln1_g, ln1_b, ffn1_w_in, ffn1_w_out, w_mix_in, s5_lambda_re, s5_lambda_im, s5_log_dt, s5_b_re, s5_b_im, s5_c_re, s5_c_im, s5_d, s5_glu_w, s5_glu_b, gdn_conv_w, gdn_a_log, gdn_dt_bias, gdn_norm_w, w_mix_out, ln2_g, ln2_b, ffn2_w_in, ffn2_w_out, ln3_g, ln3_b):
    bp, tp, _ = x_prompt.shape
    bs, ts, _ = x_sample.shape
    assert ts == SLAB_REAL and tp % S5_TILE == 0 and (bs * SLAB) % S5_TILE == 0
    depth = ln1_g.shape[0]
    yp = x_prompt.astype(F32).reshape(bp * tp, D_MODEL)
    ys = x_sample.astype(F32).reshape(bs * ts, D_MODEL)
    outs = [[] for _ in range(8)]
    for l in range(depth):
        w = _layer_weights(l, ln1_g, ln1_b, ffn1_w_in, ffn1_w_out, w_mix_in, s5_lambda_re, s5_lambda_im,
                           s5_log_dt, s5_b_re, s5_b_im, s5_c_re, s5_c_im, s5_d, s5_glu_w, s5_glu_b,
                           gdn_conv_w, gdn_a_log, gdn_dt_bias, gdn_norm_w, w_mix_out, ln2_g, ln2_b,
                           ffn2_w_in, ffn2_w_out, ln3_g, ln3_b)
        yp, *p_state = _prompt_layer(yp, w, bp)
        ys, *s_state = _sample_layer(ys, state_s5_re[l], state_s5_im[l], state_gdn[l], state_conv[l], w, bs, ts)
        for acc, val in zip(outs, p_state + s_state):
            acc.append(val)
    return (yp.reshape(x_prompt.shape).astype(x_prompt.dtype), ys.reshape(x_sample.shape).astype(x_sample.dtype),
            *(o[0][None] if depth == 1 else jnp.stack(o) for o in outs))
```

```python
import functools
import math

import jax
import jax.numpy as jnp
from jax import lax
from jax.experimental import pallas as pl
from jax.experimental.pallas import tpu as pltpu

F32 = jnp.float32
BF16 = jnp.bfloat16

D_MODEL = 2048
S5_WIDTH = 1024
S5_GROUP = 16
S5_GROUPS = 64
S5_STATE = 64
S5_HID = S5_GROUPS * S5_STATE
GDN_WIDTH = 1024
GDN_HEAD_DIM = 128
GDN_HEADS = 8
GDN_CONV = 4
D_FF = 5632
MIX_MAIN = 5120
DEEP_ALPHA = 2.0 ** 0.25
LN_EPS = 1e-5
NORM_EPS = 1e-6

SUBLANES = 8
LANES = 128
SLAB = 8
SLAB_REAL = 4
GDN_TILE = 128
S5_TILE = 256
S5_SEG = S5_TILE // SUBLANES
S5_BLOCKS = 8
VMEM_LIMIT = 56 * 1024 * 1024

NT_DIMS = (((1,), (1,)), ((), ()))


def _dot(a, b, **kw):
    return jnp.dot(a, b, preferred_element_type=F32, **kw)


def _silu(x):
    return x * jax.nn.sigmoid(x)


def _layer_norm(y, g, b):
    mu = jnp.mean(y, axis=-1, keepdims=True)
    d = y - mu
    var = jnp.mean(d * d, axis=-1, keepdims=True)
    return d * lax.rsqrt(var + LN_EPS) * g + b


def _slab_row_selector(rows, nseq, slab_row, first, *, transpose):
    shape = (rows, nseq) if transpose else (nseq, rows)
    r = lax.broadcasted_iota(jnp.int32, shape, 0 if transpose else 1)
    b = lax.broadcasted_iota(jnp.int32, shape, 1 if transpose else 0)
    return _as_bf16(r == SLAB * (b - first) + slab_row)


def _as_bf16(mask):
    return mask.astype(F32).astype(BF16)


def _dot_split(a, b, dims=None):
    f32_is_lhs = a.dtype == F32
    x = a if f32_is_lhs else b
    acc = None
    for _ in range(3):
        piece = x.astype(BF16)
        lhs, rhs = (piece, b) if f32_is_lhs else (a, piece)
        d = _dot(lhs, rhs) if dims is None else lax.dot_general(lhs, rhs, dims, preferred_element_type=F32)
        acc = d if acc is None else acc + d
        x = x - piece.astype(F32)
    return acc


def _params(*sem):
    return pltpu.CompilerParams(dimension_semantics=sem, vmem_limit_bytes=VMEM_LIMIT)


def _ffn_kernel(x_ref, wg_ref, wu_ref, wo_ref, g_ref, b_ref, o_ref, acc_ref, xb_ref):
    j = pl.program_id(1)

    @pl.when(j == 0)
    def _():
        acc_ref[...] = jnp.zeros_like(acc_ref)
        xb_ref[...] = x_ref[...].astype(BF16)

    xb = xb_ref[...]
    gate = _dot(xb, wg_ref[...])
    up = _dot(xb, wu_ref[...])
    h = _silu(gate) * up
    acc_ref[...] += _dot(h.astype(BF16), wo_ref[...])

    @pl.when(j == pl.num_programs(1) - 1)
    def _():
        y = DEEP_ALPHA * x_ref[...] + 0.5 * acc_ref[...]
        o_ref[...] = _layer_norm(y, g_ref[...], b_ref[...])


def _ffn_ln(x, w_in, w_out, g, b, *, tm=512, tf=512):
    n = x.shape[0]
    nff = D_FF // tf
    return pl.pallas_call(
        _ffn_kernel,
        grid=(n // tm, nff),
        in_specs=[
            pl.BlockSpec((tm, D_MODEL), lambda i, j: (i, 0)),
            pl.BlockSpec((D_MODEL, tf), lambda i, j: (0, j)),
            pl.BlockSpec((D_MODEL, tf), lambda i, j: (0, j + nff)),
            pl.BlockSpec((tf, D_MODEL), lambda i, j: (j, 0)),
            pl.BlockSpec((1, D_MODEL), lambda i, j: (0, 0)),
            pl.BlockSpec((1, D_MODEL), lambda i, j: (0, 0)),
        ],
        out_specs=pl.BlockSpec((tm, D_MODEL), lambda i, j: (i, 0)),
        out_shape=jax.ShapeDtypeStruct((n, D_MODEL), F32),
        scratch_shapes=[pltpu.VMEM((tm, D_MODEL), F32), pltpu.VMEM((tm, D_MODEL), BF16)],
        compiler_params=_params("parallel", "arbitrary"),
        name="ffn_ln",
    )(x, w_in, w_in, w_out, g, b)


def _mixin_kernel(x_ref, w_ref, wt_ref, o_ref, ot_ref, xb_ref):
    @pl.when(pl.program_id(1) == 0)
    def _():
        xb_ref[...] = x_ref[...].astype(BF16)
        ot_ref[...] = lax.dot_general(wt_ref[...], xb_ref[...], NT_DIMS, preferred_element_type=F32)

    o_ref[...] = _dot(xb_ref[...], w_ref[...].astype(BF16))


def _mixin(x, w, w_tail_t, *, tm=1024, tn=1024):
    n = x.shape[0]
    return pl.pallas_call(
        _mixin_kernel,
        grid=(n // tm, MIX_MAIN // tn),
        in_specs=[pl.BlockSpec((tm, D_MODEL), lambda i, j: (i, 0)),
                  pl.BlockSpec((D_MODEL, tn), lambda i, j: (0, j)),
                  pl.BlockSpec((2 * GDN_HEADS, D_MODEL), lambda i, j: (0, 0))],
        out_specs=[pl.BlockSpec((tm, tn), lambda i, j: (i, j)),
                   pl.BlockSpec((2 * GDN_HEADS, tm), lambda i, j: (0, i))],
        out_shape=[jax.ShapeDtypeStruct((n, MIX_MAIN), F32),
                   jax.ShapeDtypeStruct((2 * GDN_HEADS, n), F32)],
        scratch_shapes=[pltpu.VMEM((tm, D_MODEL), BF16)],
        compiler_params=_params("parallel", "arbitrary"),
        name="mix_in",
    )(x, w, w_tail_t)


def _mixout_kernel(ya_ref, yb_ref, x_ref, w_ref, g_ref, b_ref, o_ref):
    mix = (_dot(ya_ref[...].astype(BF16), w_ref[0:S5_WIDTH, :])
           + _dot(yb_ref[...].astype(BF16), w_ref[S5_WIDTH:, :]))
    o_ref[...] = _layer_norm(DEEP_ALPHA * x_ref[...] + mix, g_ref[...], b_ref[...])


def _mixout_ln(ya, yb, x, w, g, b, *, tm=512):
    n = x.shape[0]
    return pl.pallas_call(
        _mixout_kernel,
        grid=(n // tm,),
        in_specs=[pl.BlockSpec((tm, S5_WIDTH), lambda i: (i, 0)),
                  pl.BlockSpec((tm, GDN_WIDTH), lambda i: (i, 0)),
                  pl.BlockSpec((tm, D_MODEL), lambda i: (i, 0)),
                  pl.BlockSpec((D_MODEL, D_MODEL), lambda i: (0, 0)),
                  pl.BlockSpec((1, D_MODEL), lambda i: (0, 0)),
                  pl.BlockSpec((1, D_MODEL), lambda i: (0, 0))],
        out_specs=pl.BlockSpec((tm, D_MODEL), lambda i: (i, 0)),
        out_shape=jax.ShapeDtypeStruct((n, D_MODEL), F32),
        compiler_params=_params("parallel"),
        name="mix_out_ln",
    )(ya, yb, x, w, g, b)


def _s5_disc_kernel(lre_ref, lim_ref, ldt_ref, bre_ref, bim_ref, coef_ref, bbre_ref, bbim_ref):
    lr, li = lre_ref[...], lim_ref[...]
    dt = jnp.exp(ldt_ref[...])
    mag = jnp.exp(lr * dt)
    ar = mag * jnp.cos(li * dt)
    ai = mag * jnp.sin(li * dt)
    nr, ni = ar - 1.0, ai
    den = lr * lr + li * li
    c_re = (nr * lr + ni * li) / den
    c_im = (ni * lr - nr * li) / den
    bw = S5_HID // S5_BLOCKS
    for j in range(S5_BLOCKS):
        cr, ci = c_re[:, j * bw:(j + 1) * bw], c_im[:, j * bw:(j + 1) * bw]
        bbre_ref[j] = (cr * bre_ref[j] - ci * bim_ref[j]).astype(BF16)
        bbim_ref[j] = (cr * bim_ref[j] + ci * bre_ref[j]).astype(BF16)

    def cmul(x, y):
        return x[0] * y[0] - x[1] * y[1], x[0] * y[1] + x[1] * y[0]

    width = lr.shape[-1]
    p = (ar, ai)
    for _ in range(S5_SEG - 1):
        p = cmul(p, (ar, ai))
    coef_ref[...] = jnp.zeros_like(coef_ref)
    for part, v in enumerate((ar, ai)):
        coef_ref[part] = jnp.broadcast_to(v, (SUBLANES, width))
    for k, s in enumerate((1, 2, 4)):
        for part in range(2):
            coef_ref[2 + 2 * k + part, s:SUBLANES, :] = jnp.broadcast_to(p[part], (SUBLANES - s, width))
        p = cmul(p, p)


def _s5_disc(lre, lim, ldt, b_re, b_im):
    return pl.pallas_call(
        _s5_disc_kernel,
        out_shape=[jax.ShapeDtypeStruct((8, SUBLANES, S5_HID), F32),
                   jax.ShapeDtypeStruct(b_re.shape, BF16), jax.ShapeDtypeStruct(b_im.shape, BF16)],
        name="s5_disc",
    )(lre, lim, ldt, b_re, b_im)


def _gelu_tanh(y):
    return 0.5 * y * (1.0 + jnp.tanh(math.sqrt(2.0 / math.pi) * (y + 0.044715 * (y * y * y))))


def _scan_layout(tt, group):
    p = jnp.arange(tt)
    rem = p % group
    src = (p - rem) + (rem % SUBLANES) * (group // SUBLANES) + rem // SUBLANES
    return (src[:, None] == jnp.arange(tt)[None, :]).astype(BF16)


def _cmul_add(ar, ai, xr, xi, br, bi):
    return ar * xr - ai * xi + br, ar * xi + ai * xr + bi


def _s5_kernel(u_ref, h0re_ref, h0im_ref, perm_ref, back_ref, coef_ref, bre_ref, bim_ref, cwre_ref, cwim_ref,
               d_ref, gw_ref, gb_ref, y_ref, sre_ref, sim_ref, hre, him, ysc, car_re, car_im,
               *, chain, tt):
    group = tt if chain else SLAB * SUBLANES
    seg = group // SUBLANES
    u = u_ref[...]
    ub = _dot(perm_ref[...], u.astype(BF16)).astype(BF16)
    bw = S5_HID // S5_BLOCKS
    gw = S5_WIDTH // S5_BLOCKS
    if chain:
        @pl.when(pl.program_id(1) == 0)
        def _():
            car_re[...] = jnp.broadcast_to(h0re_ref[...], car_re.shape)
            car_im[...] = jnp.broadcast_to(h0im_ref[...], car_im.shape)

    sub = lax.broadcasted_iota(jnp.int32, (SUBLANES, 1), 0)
    for c in range(S5_BLOCKS):
        sl = slice(c * bw, (c + 1) * bw)
        uc = ub[:, c * gw:(c + 1) * gw]
        hre[:, sl] = _dot(uc, bre_ref[c])
        him[:, sl] = _dot(uc, bim_ref[c])
        ar, ai = coef_ref[0, :, sl], coef_ref[1, :, sl]
        if not chain:
            for g in range(tt // group):
                srows = slice(g * SUBLANES, (g + 1) * SUBLANES)
                xr, xi = h0re_ref[srows, sl], h0im_ref[srows, sl]
                for j in range(SLAB - SLAB_REAL, SLAB):
                    rows = slice(g * group + j * SUBLANES, g * group + (j + 1) * SUBLANES)
                    xr, xi = _cmul_add(ar, ai, xr, xi, hre[rows, sl], him[rows, sl])
                    hre[rows, sl] = xr
                    him[rows, sl] = xi
                sre_ref[srows, sl] = xr
                sim_ref[srows, sl] = xi
        else:
            xr = xi = jnp.zeros((SUBLANES, bw), F32)
            for j in range(seg):
                rows = slice(j * SUBLANES, (j + 1) * SUBLANES)
                xr, xi = _cmul_add(ar, ai, xr, xi, hre[rows, sl], him[rows, sl])
                hre[rows, sl] = xr
                him[rows, sl] = xi
            kr = jnp.where(sub == 0, car_re[:, sl], pltpu.roll(xr, 1, 0))
            ki = jnp.where(sub == 0, car_im[:, sl], pltpu.roll(xi, 1, 0))
            for k, s in enumerate((1, 2, 4)):
                pr, pi = coef_ref[2 + 2 * k, :, sl], coef_ref[3 + 2 * k, :, sl]
                kr, ki = _cmul_add(pr, pi, pltpu.roll(kr, s, 0), pltpu.roll(ki, s, 0), kr, ki)
            outr, outi = _cmul_add(coef_ref[2, :, sl], coef_ref[3, :, sl], kr, ki, xr, xi)
            car_re[:, sl] = jnp.broadcast_to(outr[SUBLANES - 1:SUBLANES, :], outr.shape)
            car_im[:, sl] = jnp.broadcast_to(outi[SUBLANES - 1:SUBLANES, :], outi.shape)
            for j in range(seg):
                rows = slice(j * SUBLANES, (j + 1) * SUBLANES)
                kr, ki = ar * kr - ai * ki, ar * ki + ai * kr
                hre[rows, sl] += kr
                him[rows, sl] += ki
        ysc[:, c * gw:(c + 1) * gw] = (_dot(hre[:, sl].astype(BF16), cwre_ref[c])
                                       - _dot(him[:, sl].astype(BF16), cwim_ref[c]))

    if chain:
        @pl.when(pl.program_id(1) == pl.num_programs(1) - 1)
        def _():
            sre_ref[...] = car_re[0:1, :]
            sim_ref[...] = car_im[0:1, :]

    y = _dot_split(back_ref[...], ysc[...])
    z = _gelu_tanh(y + d_ref[...] * u)
    gl = _dot(z.astype(BF16), gw_ref[...]) + gb_ref[...]
    y_ref[...] = z * jax.nn.sigmoid(gl)


def _s5_mixer(proj, h0re, h0im, sw, *, chain, nseq, tt=S5_TILE):
    rows = proj.shape[0]
    const = lambda shape: pl.BlockSpec(shape, lambda *_: (0,) * len(shape))
    if chain:
        assert tt == S5_SEG * SUBLANES
        nt = rows // nseq // tt
        grid = (nseq, nt)
        u_spec = pl.BlockSpec((tt, S5_WIDTH), lambda b, t: (b * nt + t, 0))
        st_spec = pl.BlockSpec((None, 1, S5_HID), lambda b, t: (b, 0, 0))
        st_shape = jax.ShapeDtypeStruct((nseq, 1, S5_HID), F32)
        sem = ("parallel", "arbitrary")
    else:
        grid = (rows // tt,)
        u_spec = pl.BlockSpec((tt, S5_WIDTH), lambda i: (i, 0))
        st_spec = pl.BlockSpec((tt // SLAB, S5_HID), lambda i: (i, 0))
        st_shape = jax.ShapeDtypeStruct((rows // SLAB, S5_HID), F32)
        sem = ("parallel",)
    y_spec = u_spec
    bw = S5_HID // S5_BLOCKS
    gw = S5_WIDTH // S5_BLOCKS
    perm = _scan_layout(tt, tt if chain else SLAB * SUBLANES)
    return pl.pallas_call(
        functools.partial(_s5_kernel, chain=chain, tt=tt),
        grid=grid,
        in_specs=[u_spec, st_spec, st_spec,
                  const((tt, tt)), const((tt, tt)), const((8, SUBLANES, S5_HID)),
                  const((S5_BLOCKS, gw, bw)), const((S5_BLOCKS, gw, bw)),
                  const((S5_BLOCKS, bw, gw)), const((S5_BLOCKS, bw, gw)),
                  const((1, S5_WIDTH)), const((S5_WIDTH, S5_WIDTH)), const((1, S5_WIDTH))],
        out_specs=[y_spec, st_spec, st_spec],
        out_shape=[jax.ShapeDtypeStruct((rows, S5_WIDTH), F32), st_shape, st_shape],
        scratch_shapes=[pltpu.VMEM((tt, S5_HID), F32), pltpu.VMEM((tt, S5_HID), F32),
                        pltpu.VMEM((tt, S5_WIDTH), F32),
                        pltpu.VMEM((SUBLANES, S5_HID), F32), pltpu.VMEM((SUBLANES, S5_HID), F32)],
        compiler_params=_params(*sem),
        name="s5_chain" if chain else "s5_slab",
    )(proj, h0re, h0im, perm, perm.T, sw["coef"], sw["b_re"], sw["b_im"], sw["c_re"], sw["c_im"],
      sw["d"], sw["glu_w"], sw["glu_b"])


def _softplus(x):
    return jnp.maximum(x, 0.0) + jnp.log1p(jnp.exp(-jnp.abs(x)))


def _segment_masks(seg):
    ri = lax.broadcasted_iota(jnp.int32, (GDN_TILE, GDN_TILE), 0)
    ci = lax.broadcasted_iota(jnp.int32, (GDN_TILE, GDN_TILE), 1)
    same = (ri // seg) == (ci // seg)
    causal = (ri >= ci) & same
    strict = (ri > ci) & same
    return same, causal, strict


def _gdn_gates(bdt, alog_col, dtb_col, seg, slab):
    same, causal, strict = _segment_masks(seg)
    bt = jax.nn.sigmoid(bdt[:GDN_HEADS, :])
    gt = -jnp.exp(alog_col) * _softplus(bdt[GDN_HEADS:, :] + dtb_col)
    if slab:
        creal = (lax.broadcasted_iota(jnp.int32, (1, GDN_TILE), 1) % SLAB) >= SLAB - SLAB_REAL
        bt = jnp.where(creal, bt, 0.0)
        gt = jnp.where(creal, gt, 0.0)
    ri = lax.broadcasted_iota(jnp.int32, (GDN_TILE, GDN_TILE), 0)
    ci = lax.broadcasted_iota(jnp.int32, (GDN_TILE, GDN_TILE), 1)
    gates = jnp.concatenate([bt, gt, jnp.zeros((GDN_TILE - 2 * GDN_HEADS, GDN_TILE), F32)], axis=0)
    lhs = jnp.concatenate([_as_bf16(ri == ci), _as_bf16(causal), _as_bf16(same)], axis=0)
    cols = _dot_split(lhs, gates, NT_DIMS)
    beta, gc, gl = cols[:GDN_TILE], cols[GDN_TILE:2 * GDN_TILE], cols[2 * GDN_TILE:]
    gct = _dot_split(gt, _as_bf16((ri <= ci) & same))
    return beta, gc, gl, gct, causal, strict


def _merge_masks(top):
    ri = lax.broadcasted_iota(jnp.int32, (GDN_TILE, GDN_TILE), 0)
    ci = lax.broadcasted_iota(jnp.int32, (GDN_TILE, GDN_TILE), 1)
    masks = []
    s = 1
    while s < top:
        masks.append(((ri // (2 * s)) == (ci // (2 * s))) & ((ri // s) != (ci // s)))
        s *= 2
    return masks


def _unit_lower_inverse(ms, masks):
    es = [-jnp.where(masks[0], m, 0.0) for m in ms]
    for mask in masks[1:]:
        cs = [jnp.where(mask, m, 0.0) for m in ms]
        ebs = [e.astype(BF16) for e in es]
        xs = [c + _dot(eb, c.astype(BF16)) for c, eb in zip(cs, ebs)]
        es = [e - (x + _dot(x.astype(BF16), eb)) for e, x, eb in zip(es, xs, ebs)]
    return es


def _gdn_intra(qs, ks, vs, betas, gcols, grows, glcols, causal, strict, masks):
    n = range(len(qs))
    qn = [q * lax.rsqrt(jnp.sum(q * q, axis=-1, keepdims=True) + NORM_EPS) * (GDN_HEAD_DIM ** -0.5) for q in qs]
    kn = [k * lax.rsqrt(jnp.sum(k * k, axis=-1, keepdims=True) + NORM_EPS) for k in ks]
    decay = [jnp.exp(jnp.where(causal, gcols[i] - grows[i], -jnp.inf)) for i in n]
    kb = [kn[i] * betas[i] for i in n]
    knb = [k.astype(BF16) for k in kn]
    kk = [lax.dot_general(kb[i].astype(BF16), knb[i], NT_DIMS, preferred_element_type=F32) for i in n]
    ys = _unit_lower_inverse([kk[i] * jnp.where(strict, decay[i], 0.0) for i in n], masks)
    eg = [jnp.exp(g) for g in gcols]
    rhs = [jnp.concatenate([vs[i] * betas[i], kb[i] * eg[i]], axis=1) for i in n]
    uw = [rhs[i] + _dot(ys[i].astype(BF16), rhs[i].astype(BF16)) for i in n]
    attn = [lax.dot_general(qn[i].astype(BF16), knb[i], NT_DIMS, preferred_element_type=F32) * decay[i] for i in n]
    qg = [qn[i] * eg[i] for i in n]
    kd = [kn[i] * jnp.exp(glcols[i] - gcols[i]) for i in n]
    return [(uw[i][:, :GDN_HEAD_DIM], uw[i][:, GDN_HEAD_DIM:], attn[i], qg[i], kd[i]) for i in n]


def _gdn_out(o, gate, norm_w):
    o = o * lax.rsqrt(jnp.mean(o * o, axis=-1, keepdims=True) + NORM_EPS) * norm_w
    return o * _silu(gate)


def _conv4(win_ref, cw, width_slice):
    out = cw[0:1, :] * win_ref[pl.ds(SUBLANES - 3, GDN_TILE), width_slice]
    for j in range(1, GDN_CONV):
        out = out + cw[j:j + 1, :] * win_ref[pl.ds(SUBLANES - 3 + j, GDN_TILE), width_slice]
    return out


def _gdn_chain_kernel(q_ref, k_ref, v_ref, gate_ref, bdt_ref, buf_ref, s0_ref, cw_ref,
                      alc_ref, dbc_ref, nw_ref,
                      y_ref, sout_ref, bufout_ref, win, s_sc):
    t = pl.program_id(1)
    w3 = 3 * GDN_WIDTH

    @pl.when(t == 0)
    def _():
        win[0:SUBLANES, :] = jnp.zeros((SUBLANES, w3), F32)
        win[SUBLANES - 3:SUBLANES, :] = buf_ref[...]
        s_sc[...] = s0_ref[...]

    win[SUBLANES:, 0:GDN_WIDTH] = q_ref[...]
    win[SUBLANES:, GDN_WIDTH:2 * GDN_WIDTH] = k_ref[...]
    win[SUBLANES:, 2 * GDN_WIDTH:] = v_ref[...]

    beta, gc, gl, gct, causal, strict = _gdn_gates(bdt_ref[...], alc_ref[...], dbc_ref[...], GDN_TILE, False)
    masks = _merge_masks(GDN_TILE)
    heads = range(GDN_HEADS)
    qkv = [[], [], []]
    for part in range(3):
        for h in heads:
            cs = slice(part * GDN_WIDTH + h * GDN_HEAD_DIM, part * GDN_WIDTH + (h + 1) * GDN_HEAD_DIM)
            qkv[part].append(_silu(_conv4(win, cw_ref[:, cs], cs)))
    gcols = [gc[:, GDN_HEADS + h:GDN_HEADS + h + 1] for h in heads]
    glcols = [gl[:, GDN_HEADS + h:GDN_HEADS + h + 1] for h in heads]
    intra = _gdn_intra(qkv[0], qkv[1], qkv[2], [beta[:, h:h + 1] for h in heads], gcols,
                       [gct[h:h + 1, :] for h in heads], glcols, causal, strict, masks)
    s_old = [s_sc[h] for h in heads]
    a = [_dot(jnp.concatenate([intra[h][1], intra[h][3]], axis=0).astype(BF16), s_old[h].astype(BF16))
         for h in heads]
    vnb = [(intra[h][0] - a[h][:GDN_TILE]).astype(BF16) for h in heads]
    o = [a[h][GDN_TILE:] + _dot(intra[h][2].astype(BF16), vnb[h]) for h in heads]
    for h in heads:
        s_sc[h] = s_old[h] * jnp.exp(glcols[h][0:1, :]) + _dot(intra[h][4].T.astype(BF16), vnb[h])
    for h in heads:
        hs = slice(h * GDN_HEAD_DIM, (h + 1) * GDN_HEAD_DIM)
        y_ref[:, hs] = _gdn_out(o[h], gate_ref[:, hs], nw_ref[...])

    win[0:SUBLANES, :] = win[GDN_TILE:GDN_TILE + SUBLANES, :]

    @pl.when(t == pl.num_programs(1) - 1)
    def _():
        sout_ref[...] = s_sc[...]
        bufout_ref[...] = win[GDN_TILE + SUBLANES - 3:GDN_TILE + SUBLANES, :]


def _gdn_chain(proj, bdt, conv_buf, s0, gw, *, nseq):
    rows = proj.shape[0]
    nt = rows // nseq // GDN_TILE
    w3 = 3 * GDN_WIDTH
    const = lambda shape: pl.BlockSpec(shape, lambda *_: (0,) * len(shape))
    col = lambda cb: pl.BlockSpec((GDN_TILE, GDN_WIDTH), lambda b, t, cb=cb: (b * nt + t, cb))
    return pl.pallas_call(
        _gdn_chain_kernel,
        grid=(nseq, nt),
        in_specs=[col(1), col(2), col(3), col(4),
                  pl.BlockSpec((2 * GDN_HEADS, GDN_TILE), lambda b, t: (0, b * nt + t)),
                  pl.BlockSpec((None, GDN_CONV - 1, w3), lambda b, t: (b, 0, 0)),
                  pl.BlockSpec((None, GDN_HEADS, GDN_HEAD_DIM, GDN_HEAD_DIM), lambda b, t: (b, 0, 0, 0)),
                  const((GDN_CONV, w3)), const((GDN_HEADS, 1)), const((GDN_HEADS, 1)), const((1, GDN_HEAD_DIM))],
        out_specs=[pl.BlockSpec((GDN_TILE, GDN_WIDTH), lambda b, t: (b * nt + t, 0)),
                   pl.BlockSpec((None, GDN_HEADS, GDN_HEAD_DIM, GDN_HEAD_DIM), lambda b, t: (b, 0, 0, 0)),
                   pl.BlockSpec((None, GDN_CONV - 1, w3), lambda b, t: (b, 0, 0))],
        out_shape=[jax.ShapeDtypeStruct((rows, GDN_WIDTH), F32),
                   jax.ShapeDtypeStruct((nseq, GDN_HEADS, GDN_HEAD_DIM, GDN_HEAD_DIM), F32),
                   jax.ShapeDtypeStruct((nseq, GDN_CONV - 1, w3), F32)],
        scratch_shapes=[pltpu.VMEM((GDN_TILE + SUBLANES, w3), F32),
                        pltpu.VMEM((GDN_HEADS, GDN_HEAD_DIM, GDN_HEAD_DIM), F32)],
        compiler_params=_params("parallel", "arbitrary"),
        name="gdn_chain",
    )(proj, proj, proj, proj, bdt, conv_buf, s0, gw["conv_w"], gw["alog_col"], gw["dtb_col"], gw["norm_w"])


def _gdn_slab_kernel(q_ref, k_ref, v_ref, gate_ref, bdt_ref, bq_ref, bk_ref, bv_ref, s0_ref,
                     cwq_ref, cwk_ref, cwv_ref, alc_ref, dbc_ref, nw_ref,
                     y_ref, sout_ref, oq_ref, ok_ref, ov_ref,
                     win):
    h = pl.program_id(1)
    nslab = GDN_TILE // SLAB
    first = (pl.program_id(0) % (bq_ref.shape[1] // nslab)) * nslab
    real = (lax.broadcasted_iota(jnp.int32, (GDN_TILE, 1), 0) % SLAB) >= SLAB - SLAB_REAL
    qkv = []
    taps = range(GDN_CONV - 1)
    place = [_slab_row_selector(GDN_TILE, bq_ref.shape[1], 1 + i, first, transpose=True) for i in taps]
    take = [_slab_row_selector(GDN_TILE, nslab, SLAB - 3 + i, 0, transpose=False) for i in taps]
    for part, (x_ref, b_ref, cw_ref, o_ref) in enumerate(
            ((q_ref, bq_ref, cwq_ref, oq_ref), (k_ref, bk_ref, cwk_ref, ok_ref), (v_ref, bv_ref, cwv_ref, ov_ref))):
        cs = slice(part * GDN_HEAD_DIM, (part + 1) * GDN_HEAD_DIM)
        x = x_ref[...]
        for i in taps:
            x = x + _dot_split(place[i], b_ref[i])
        win[0:SUBLANES, cs] = jnp.zeros((SUBLANES, GDN_HEAD_DIM), F32)
        win[SUBLANES:, cs] = x
        conv = _conv4(win, cw_ref[...], cs)
        qkv.append(jnp.where(real, _silu(conv), 0.0))
        for i in taps:
            o_ref[i] = _dot_split(take[i], x)

    beta, gc, gl, gct, causal, strict = _gdn_gates(bdt_ref[...], alc_ref[...], dbc_ref[...], SLAB, True)
    lane = lax.broadcasted_iota(jnp.int32, (1, LANES), 1)
    sub = lax.broadcasted_iota(jnp.int32, (SUBLANES, 1), 0)
    pick = lambda a, idx: jnp.sum(jnp.where(lane == idx, a, 0.0), axis=1, keepdims=True)
    beta_col = pick(beta, h)
    gcol = pick(gc, GDN_HEADS + h)
    glcol = pick(gl, GDN_HEADS + h)
    grow = jnp.sum(jnp.where(sub == h, gct, 0.0), axis=0, keepdims=True)
    masks = _merge_masks(SLAB_REAL)
    (u, wk, attn, qg, kd), = _gdn_intra([qkv[0]], [qkv[1]], [qkv[2]], [beta_col], [gcol], [grow], [glcol],
                                        causal, strict, masks)
    res = []
    for i in range(nslab):
        rows = slice(i * SLAB, (i + 1) * SLAB)
        lhs = jnp.concatenate([wk[rows], qg[rows]], axis=0).astype(BF16)
        res.append(_dot(lhs, s0_ref[i, 0].astype(BF16)))
    v_new = u - jnp.concatenate([r[:SLAB] for r in res], axis=0)
    vnb = v_new.astype(BF16)
    o = jnp.concatenate([r[SLAB:] for r in res], axis=0) + _dot(attn.astype(BF16), vnb)
    y_ref[...] = _gdn_out(o, gate_ref[...], nw_ref[...])
    kdt = kd.T
    egl = jnp.exp(glcol)
    for i in range(nslab):
        in_slab = (lane // SLAB) == i
        upd = _dot(jnp.where(in_slab, kdt, 0.0).astype(BF16), vnb)
        sout_ref[i, 0] = s0_ref[i, 0] * egl[i * SLAB:i * SLAB + 1, :] + upd


def _gdn_slab(proj, bdt, conv_buf, s0, gw):
    rows = proj.shape[0]
    nb = rows // SLAB
    nslab = GDN_TILE // SLAB
    hb = GDN_WIDTH // GDN_HEAD_DIM
    const = lambda shape: pl.BlockSpec(shape, lambda *_: (0,) * len(shape))
    col = lambda g: pl.BlockSpec((GDN_TILE, GDN_HEAD_DIM), lambda i, h, g=g: (i, g * hb + h))
    per = LANES // nslab
    buf = lambda g: pl.BlockSpec((GDN_CONV - 1, LANES, GDN_HEAD_DIM), lambda i, h, g=g: (0, i // per, g * hb + h))
    cwb = lambda g: pl.BlockSpec((GDN_CONV, GDN_HEAD_DIM), lambda i, h, g=g: (0, g * hb + h))
    st = pl.BlockSpec((nslab, 1, GDN_HEAD_DIM, GDN_HEAD_DIM), lambda i, h: (i, h, 0, 0))
    obuf = pl.BlockSpec((GDN_CONV - 1, nslab, GDN_HEAD_DIM), lambda i, h: (0, i, h))
    tile = (GDN_TILE, GDN_HEAD_DIM)
    return pl.pallas_call(
        _gdn_slab_kernel,
        grid=(rows // GDN_TILE, GDN_HEADS),
        in_specs=[col(1), col(2), col(3), col(4),
                  pl.BlockSpec((2 * GDN_HEADS, GDN_TILE), lambda i, h: (0, i)),
                  buf(0), buf(1), buf(2), st, cwb(0), cwb(1), cwb(2),
                  const((GDN_HEADS, 1)), const((GDN_HEADS, 1)), const((1, GDN_HEAD_DIM))],
        out_specs=[pl.BlockSpec(tile, lambda i, h: (i, h)), st, obuf, obuf, obuf],
        out_shape=[jax.ShapeDtypeStruct((rows, GDN_WIDTH), F32),
                   jax.ShapeDtypeStruct(s0.shape, F32)]
                  + [jax.ShapeDtypeStruct((GDN_CONV - 1, nb, GDN_WIDTH), F32)] * 3,
        scratch_shapes=[pltpu.VMEM((GDN_TILE + SUBLANES, 3 * GDN_HEAD_DIM), F32)],
        compiler_params=_params("parallel", "arbitrary"),
        name="gdn_slab",
    )(proj, proj, proj, proj, bdt, conv_buf, conv_buf, conv_buf, s0,
      gw["conv_w"], gw["conv_w"], gw["conv_w"], gw["alog_col"], gw["dtb_col"], gw["norm_w"])


def _block_diag(w):
    per = S5_GROUPS // S5_BLOCKS
    g, a, b = w.shape
    w = w.reshape(S5_BLOCKS, per, a, b)
    eye = jnp.eye(per, dtype=w.dtype)
    return jnp.einsum("jgab,gk->jgakb", w, eye).reshape(S5_BLOCKS, per * a, per * b)


def _layer_weights(l, ln1_g, ln1_b, ffn1_w_in, ffn1_w_out, w_mix_in, s5_lambda_re, s5_lambda_im, s5_log_dt,
                   s5_b_re, s5_b_im, s5_c_re, s5_c_im, s5_d, s5_glu_w, s5_glu_b, gdn_conv_w, gdn_a_log,
                   gdn_dt_bias, gdn_norm_w, w_mix_out, ln2_g, ln2_b, ffn2_w_in, ffn2_w_out, ln3_g, ln3_b):
    row = lambda v: v[l].reshape(1, -1).astype(F32)
    w = {
        "ln1": (row(ln1_g), row(ln1_b)), "ln2": (row(ln2_g), row(ln2_b)), "ln3": (row(ln3_g), row(ln3_b)),
        "ffn1": (ffn1_w_in[l].astype(BF16), ffn1_w_out[l].astype(BF16)),
        "ffn2": (ffn2_w_in[l].astype(BF16), ffn2_w_out[l].astype(BF16)),
        "mix_in": w_mix_in[l],
        "mix_tail_t": w_mix_in[l][:, MIX_MAIN:].T.astype(BF16),
        "mix_out": w_mix_out[l].astype(BF16),
    }
    coef, bb_re, bb_im = _s5_disc(
        s5_lambda_re[l].reshape(1, S5_HID).astype(F32), s5_lambda_im[l].reshape(1, S5_HID).astype(F32),
        jnp.repeat(s5_log_dt[l], S5_STATE).reshape(1, S5_HID).astype(F32),
        _block_diag(jnp.swapaxes(s5_b_re[l], 1, 2).astype(F32)),
        _block_diag(jnp.swapaxes(s5_b_im[l], 1, 2).astype(F32)))
    w["s5"] = {
        "coef": coef, "b_re": bb_re, "b_im": bb_im,
        "c_re": _block_diag(jnp.swapaxes(s5_c_re[l], 1, 2)).astype(BF16),
        "c_im": _block_diag(jnp.swapaxes(s5_c_im[l], 1, 2)).astype(BF16),
        "d": row(s5_d), "glu_w": s5_glu_w[l].astype(BF16), "glu_b": row(s5_glu_b),
    }
    w["gdn"] = {
        "conv_w": gdn_conv_w[l].astype(F32),
        "alog_col": gdn_a_log[l].reshape(GDN_HEADS, 1).astype(F32),
        "dtb_col": gdn_dt_bias[l].reshape(GDN_HEADS, 1).astype(F32),
        "norm_w": row(gdn_norm_w),
    }
    return w


def _prompt_layer(x, w, nseq):
    x = _ffn_ln(x, *w["ffn1"], *w["ln1"])
    proj, bdt = _mixin(x, w["mix_in"], w["mix_tail_t"])
    z_s5 = jnp.zeros((nseq, 1, S5_HID), F32)
    y_s5, n_re, n_im = _s5_mixer(proj, z_s5, z_s5, w["s5"], chain=True, nseq=nseq)
    z_gdn = jnp.zeros((nseq, GDN_HEADS, GDN_HEAD_DIM, GDN_HEAD_DIM), F32)
    z_buf = jnp.zeros((nseq, GDN_CONV - 1, 3 * GDN_WIDTH), F32)
    y_gdn, n_s, n_buf = _gdn_chain(proj, bdt, z_buf, z_gdn, w["gdn"], nseq=nseq)
    x = _mixout_ln(y_s5, y_gdn, x, w["mix_out"], *w["ln2"])
    x = _ffn_ln(x, *w["ffn2"], *w["ln3"])
    shape = (nseq, S5_GROUPS, S5_STATE)
    return x, n_re.reshape(shape), n_im.reshape(shape), n_s, n_buf


def _sample_layer(x, s5_re, s5_im, gdn_s, conv_buf, w, nb, t):
    x = _ffn_ln(x, *w["ffn1"], *w["ln1"])
    xs = jnp.pad(x.reshape(nb, t, D_MODEL), ((0, 0), (SLAB - t, 0), (0, 0))).reshape(nb * SLAB, D_MODEL)
    proj, bdt = _mixin(xs, w["mix_in"], w["mix_tail_t"])
    y_s5, n_re, n_im = _s5_mixer(proj, s5_re.reshape(nb, S5_HID).astype(F32),
                                 s5_im.reshape(nb, S5_HID).astype(F32), w["s5"], chain=False, nseq=nb)
    y_gdn, n_s, bq, bk, bv = _gdn_slab(proj, bdt, jnp.swapaxes(conv_buf.astype(F32), 0, 1),
                                       gdn_s.astype(F32), w["gdn"])
    xs = _mixout_ln(y_s5, y_gdn, xs, w["mix_out"], *w["ln2"])
    x = xs.reshape(nb, SLAB, D_MODEL)[:, SLAB - t:].reshape(nb * t, D_MODEL)
    x = _ffn_ln(x, *w["ffn2"], *w["ln3"])
    shape = (nb, S5_GROUPS, S5_STATE)
    n_buf = jnp.swapaxes(jnp.concatenate([bq, bk, bv], axis=-1), 0, 1)
    return x, n_re.reshape(shape), n_im.reshape(shape), n_s, n_buf


def kernel(x_prompt, x_sample, state_s5_re, state_s5_im, state_gdn, state_conv, ln1_g, ln1_b, ffn1_w_in, ffn1_w_out, w_mix_in, s5_lambda_re, s5_lambda_im, s5_log_dt, s5_b_re, s5_b_im, s5_c_re, s5_c_im, s5_d, s5_glu_w, s5_glu_b, gdn_conv_w, gdn_a_log, gdn_dt_bias, gdn_norm_w, w_mix_out, ln2_g, ln2_b, ffn2_w_in, ffn2_w_out, ln3_g, ln3_b):
    bp, tp, _ = x_prompt.shape
    bs, ts, _ = x_sample.shape
    assert ts == SLAB_REAL and tp % S5_TILE == 0 and (bs * SLAB) % S5_TILE == 0
    depth = ln1_g.shape[0]
    yp = x_prompt.astype(F32).reshape(bp * tp, D_MODEL)
    ys = x_sample.astype(F32).reshape(bs * ts, D_MODEL)
    outs = [[] for _ in range(8)]
    for l in range(depth):
        w = _layer_weights(l, ln1_g, ln1_b, ffn1_w_in, ffn1_w_out, w_mix_in, s5_lambda_re, s5_lambda_im,
                           s5_log_dt, s5_b_re, s5_b_im, s5_c_re, s5_c_im, s5_d, s5_glu_w, s5_glu_b,
                           gdn_conv_w, gdn_a_log, gdn_dt_bias, gdn_norm_w, w_mix_out, ln2_g, ln2_b,
                           ffn2_w_in, ffn2_w_out, ln3_g, ln3_b)
        yp, *p_state = _prompt_layer(yp, w, bp)
        ys, *s_state = _sample_layer(ys, state_s5_re[l], state_s5_im[l], state_gdn[l], state_conv[l], w, bs, ts)
        for acc, val in zip(outs, p_state + s_state):
            acc.append(val)
    return (yp.reshape(x_prompt.shape).astype(x_prompt.dtype), ys.reshape(x_sample.shape).astype(x_sample.dtype),
            *(o[0][None] if depth == 1 else jnp.stack(o) for o in outs))
```

```python
import functools
import math

import jax
import jax.numpy as jnp
from jax import lax
from jax.experimental import pallas as pl
from jax.experimental.pallas import tpu as pltpu

F32 = jnp.float32
BF16 = jnp.bfloat16

D_MODEL = 2048
S5_WIDTH = 1024
S5_GROUP = 16
S5_GROUPS = 64
S5_STATE = 64
S5_HID = S5_GROUPS * S5_STATE
GDN_WIDTH = 1024
GDN_HEAD_DIM = 128
GDN_HEADS = 8
GDN_CONV = 4
D_FF = 5632
MIX_MAIN = 5120
DEEP_ALPHA = 2.0 ** 0.25
LN_EPS = 1e-5
NORM_EPS = 1e-6

SUBLANES = 8
LANES = 128
SLAB = 8
SLAB_REAL = 4
GDN_TILE = 128
FFN_TM = 1024
FFN_CHUNK = 512
S5_TILE = 256
S5_SEG = S5_TILE // SUBLANES
S5_BLOCKS = 8
VMEM_LIMIT = 56 * 1024 * 1024

NT_DIMS = (((1,), (1,)), ((), ()))


def _dot(a, b, **kw):
    return jnp.dot(a, b, preferred_element_type=F32, **kw)


def _silu(x):
    return x * jax.nn.sigmoid(x)


def _layer_norm(y, g, b):
    mu = jnp.mean(y, axis=-1, keepdims=True)
    d = y - mu
    var = jnp.mean(d * d, axis=-1, keepdims=True)
    return d * lax.rsqrt(var + LN_EPS) * g + b


def _slab_row_selector(rows, nseq, slab_row, first, *, transpose):
    shape = (rows, nseq) if transpose else (nseq, rows)
    r = lax.broadcasted_iota(jnp.int32, shape, 0 if transpose else 1)
    b = lax.broadcasted_iota(jnp.int32, shape, 1 if transpose else 0)
    return _as_bf16(r == SLAB * (b - first) + slab_row)


def _as_bf16(mask):
    return mask.astype(F32).astype(BF16)


def _dot_split(a, b, dims=None):
    f32_is_lhs = a.dtype == F32
    x = a if f32_is_lhs else b
    acc = None
    for _ in range(3):
        piece = x.astype(BF16)
        lhs, rhs = (piece, b) if f32_is_lhs else (a, piece)
        d = _dot(lhs, rhs) if dims is None else lax.dot_general(lhs, rhs, dims, preferred_element_type=F32)
        acc = d if acc is None else acc + d
        x = x - piece.astype(F32)
    return acc


def _params(*sem):
    return pltpu.CompilerParams(dimension_semantics=sem, vmem_limit_bytes=VMEM_LIMIT)


def _ffn_kernel(x_ref, wg_ref, wu_ref, wo_ref, g_ref, b_ref, o_ref, xb_ref):
    j = pl.program_id(1)

    @pl.when(j == 0)
    def _():
        o_ref[...] = jnp.zeros_like(o_ref)
        xb_ref[...] = x_ref[...].astype(BF16)

    xb = xb_ref[...]
    gate = _dot(xb, wg_ref[...].astype(BF16))
    up = _dot(xb, wu_ref[...].astype(BF16))
    h = (_silu(gate) * up).astype(BF16)
    for c in range(0, D_MODEL, FFN_CHUNK):
        o_ref[:, c:c + FFN_CHUNK] += _dot(h, wo_ref[:, c:c + FFN_CHUNK].astype(BF16))

    @pl.when(j == pl.num_programs(1) - 1)
    def _():
        for r in range(0, o_ref.shape[0], FFN_CHUNK // 2):
            rows = slice(r, r + FFN_CHUNK // 2)
            y = DEEP_ALPHA * x_ref[rows, :] + 0.5 * o_ref[rows, :]
            o_ref[rows, :] = _layer_norm(y, g_ref[...], b_ref[...])


def _ffn_ln(x, w_in, w_out, g, b, *, tf=256):
    n = x.shape[0]
    tm = math.gcd(n, FFN_TM)
    nff = D_FF // tf
    return pl.pallas_call(
        _ffn_kernel,
        grid=(n // tm, nff),
        in_specs=[
            pl.BlockSpec((tm, D_MODEL), lambda i, j: (i, 0)),
            pl.BlockSpec((D_MODEL, tf), lambda i, j: (0, j)),
            pl.BlockSpec((D_MODEL, tf), lambda i, j: (0, j + nff)),
            pl.BlockSpec((tf, D_MODEL), lambda i, j: (j, 0)),
            pl.BlockSpec((1, D_MODEL), lambda i, j: (0, 0)),
            pl.BlockSpec((1, D_MODEL), lambda i, j: (0, 0)),
        ],
        out_specs=pl.BlockSpec((tm, D_MODEL), lambda i, j: (i, 0)),
        out_shape=jax.ShapeDtypeStruct((n, D_MODEL), F32),
        scratch_shapes=[pltpu.VMEM((tm, D_MODEL), BF16)],
        compiler_params=_params("parallel", "arbitrary"),
        name="ffn_ln",
    )(x, w_in, w_in, w_out, g, b)


def _mixin_kernel(x_ref, w_ref, wt_ref, o_ref, ot_ref, xb_ref):
    @pl.when(pl.program_id(1) == 0)
    def _():
        xb_ref[...] = x_ref[...].astype(BF16)
        ot_ref[...] = lax.dot_general(wt_ref[...], xb_ref[...], NT_DIMS, preferred_element_type=F32)

    o_ref[...] = _dot(xb_ref[...], w_ref[...].astype(BF16))


def _mixin(x, w, layer, w_tail_t, *, tm=1024, tn=1024):
    n = x.shape[0]
    return pl.pallas_call(
        _mixin_kernel,
        grid=(n // tm, MIX_MAIN // tn),
        in_specs=[pl.BlockSpec((tm, D_MODEL), lambda i, j: (i, 0)),
                  pl.BlockSpec((None, D_MODEL, tn), lambda i, j: (layer, 0, j)),
                  pl.BlockSpec((2 * GDN_HEADS, D_MODEL), lambda i, j: (0, 0))],
        out_specs=[pl.BlockSpec((tm, tn), lambda i, j: (i, j)),
                   pl.BlockSpec((2 * GDN_HEADS, tm), lambda i, j: (0, i))],
        out_shape=[jax.ShapeDtypeStruct((n, MIX_MAIN), F32),
                   jax.ShapeDtypeStruct((2 * GDN_HEADS, n), F32)],
        scratch_shapes=[pltpu.VMEM((tm, D_MODEL), BF16)],
        compiler_params=_params("parallel", "arbitrary"),
        name="mix_in",
    )(x, w, w_tail_t)


def _mixout_kernel(ya_ref, yb_ref, x_ref, w_ref, g_ref, b_ref, o_ref):
    mix = (_dot(ya_ref[...].astype(BF16), w_ref[0:S5_WIDTH, :])
           + _dot(yb_ref[...].astype(BF16), w_ref[S5_WIDTH:, :]))
    o_ref[...] = _layer_norm(DEEP_ALPHA * x_ref[...] + mix, g_ref[...], b_ref[...])


def _mixout_ln(ya, yb, x, w, g, b, *, tm=512):
    n = x.shape[0]
    return pl.pallas_call(
        _mixout_kernel,
        grid=(n // tm,),
        in_specs=[pl.BlockSpec((tm, S5_WIDTH), lambda i: (i, 0)),
                  pl.BlockSpec((tm, GDN_WIDTH), lambda i: (i, 0)),
                  pl.BlockSpec((tm, D_MODEL), lambda i: (i, 0)),
                  pl.BlockSpec((D_MODEL, D_MODEL), lambda i: (0, 0)),
                  pl.BlockSpec((1, D_MODEL), lambda i: (0, 0)),
                  pl.BlockSpec((1, D_MODEL), lambda i: (0, 0))],
        out_specs=pl.BlockSpec((tm, D_MODEL), lambda i: (i, 0)),
        out_shape=jax.ShapeDtypeStruct((n, D_MODEL), F32),
        compiler_params=_params("parallel"),
        name="mix_out_ln",
    )(ya, yb, x, w, g, b)


def _s5_disc_kernel(lre_ref, lim_ref, ldt_ref, bre_ref, bim_ref, coef_ref, bbre_ref, bbim_ref):
    lr, li = lre_ref[...], lim_ref[...]
    dt = jnp.exp(ldt_ref[...])
    mag = jnp.exp(lr * dt)
    ar = mag * jnp.cos(li * dt)
    ai = mag * jnp.sin(li * dt)
    nr, ni = ar - 1.0, ai
    den = lr * lr + li * li
    c_re = (nr * lr + ni * li) / den
    c_im = (ni * lr - nr * li) / den
    bw = S5_HID // S5_BLOCKS
    for j in range(S5_BLOCKS):
        cr, ci = c_re[:, j * bw:(j + 1) * bw], c_im[:, j * bw:(j + 1) * bw]
        bbre_ref[j] = (cr * bre_ref[j] - ci * bim_ref[j]).astype(BF16)
        bbim_ref[j] = (cr * bim_ref[j] + ci * bre_ref[j]).astype(BF16)

    def cmul(x, y):
        return x[0] * y[0] - x[1] * y[1], x[0] * y[1] + x[1] * y[0]

    width = lr.shape[-1]
    p = (ar, ai)
    for _ in range(S5_SEG - 1):
        p = cmul(p, (ar, ai))
    coef_ref[...] = jnp.zeros_like(coef_ref)
    for part, v in enumerate((ar, ai)):
        coef_ref[part] = jnp.broadcast_to(v, (SUBLANES, width))
    for k, s in enumerate((1, 2, 4)):
        for part in range(2):
            coef_ref[2 + 2 * k + part, s:SUBLANES, :] = jnp.broadcast_to(p[part], (SUBLANES - s, width))
        p = cmul(p, p)


def _s5_disc(lre, lim, ldt, b_re, b_im):
    return pl.pallas_call(
        _s5_disc_kernel,
        out_shape=[jax.ShapeDtypeStruct((8, SUBLANES, S5_HID), F32),
                   jax.ShapeDtypeStruct(b_re.shape, BF16), jax.ShapeDtypeStruct(b_im.shape, BF16)],
        name="s5_disc",
    )(lre, lim, ldt, b_re, b_im)


def _gelu_tanh(y):
    return 0.5 * y * (1.0 + jnp.tanh(math.sqrt(2.0 / math.pi) * (y + 0.044715 * (y * y * y))))


def _scan_layout(tt, group):
    p = jnp.arange(tt)
    rem = p % group
    src = (p - rem) + (rem % SUBLANES) * (group // SUBLANES) + rem // SUBLANES
    return (src[:, None] == jnp.arange(tt)[None, :]).astype(BF16)


def _cmul_add(ar, ai, xr, xi, br, bi):
    return ar * xr - ai * xi + br, ar * xi + ai * xr + bi


def _s5_kernel(u_ref, h0re_ref, h0im_ref, perm_ref, back_ref, coef_ref, bre_ref, bim_ref, cwre_ref, cwim_ref,
               d_ref, gw_ref, gb_ref, y_ref, sre_ref, sim_ref, hre, him, ysc, car_re, car_im,
               *, chain, tt):
    group = tt if chain else SLAB * SUBLANES
    seg = group // SUBLANES
    u = u_ref[...]
    ub = _dot(perm_ref[...], u.astype(BF16)).astype(BF16)
    bw = S5_HID // S5_BLOCKS
    gw = S5_WIDTH // S5_BLOCKS
    if chain:
        @pl.when(pl.program_id(1) == 0)
        def _():
            car_re[...] = jnp.broadcast_to(h0re_ref[...], car_re.shape)
            car_im[...] = jnp.broadcast_to(h0im_ref[...], car_im.shape)

    sub = lax.broadcasted_iota(jnp.int32, (SUBLANES, 1), 0)
    for c in range(S5_BLOCKS):
        sl = slice(c * bw, (c + 1) * bw)
        uc = ub[:, c * gw:(c + 1) * gw]
        hre[:, sl] = _dot(uc, bre_ref[c])
        him[:, sl] = _dot(uc, bim_ref[c])
        ar, ai = coef_ref[0, :, sl], coef_ref[1, :, sl]
        if not chain:
            for g in range(tt // group):
                srows = slice(g * SUBLANES, (g + 1) * SUBLANES)
                xr, xi = h0re_ref[srows, sl], h0im_ref[srows, sl]
                for j in range(SLAB - SLAB_REAL, SLAB):
                    rows = slice(g * group + j * SUBLANES, g * group + (j + 1) * SUBLANES)
                    xr, xi = _cmul_add(ar, ai, xr, xi, hre[rows, sl], him[rows, sl])
                    hre[rows, sl] = xr
                    him[rows, sl] = xi
                sre_ref[srows, sl] = xr
                sim_ref[srows, sl] = xi
        else:
            xr = xi = jnp.zeros((SUBLANES, bw), F32)
            for j in range(seg):
                rows = slice(j * SUBLANES, (j + 1) * SUBLANES)
                xr, xi = _cmul_add(ar, ai, xr, xi, hre[rows, sl], him[rows, sl])
                hre[rows, sl] = xr
                him[rows, sl] = xi
            kr = jnp.where(sub == 0, car_re[:, sl], pltpu.roll(xr, 1, 0))
            ki = jnp.where(sub == 0, car_im[:, sl], pltpu.roll(xi, 1, 0))
            for k, s in enumerate((1, 2, 4)):
                pr, pi = coef_ref[2 + 2 * k, :, sl], coef_ref[3 + 2 * k, :, sl]
                kr, ki = _cmul_add(pr, pi, pltpu.roll(kr, s, 0), pltpu.roll(ki, s, 0), kr, ki)
            outr, outi = _cmul_add(coef_ref[2, :, sl], coef_ref[3, :, sl], kr, ki, xr, xi)
            car_re[:, sl] = jnp.broadcast_to(outr[SUBLANES - 1:SUBLANES, :], outr.shape)
            car_im[:, sl] = jnp.broadcast_to(outi[SUBLANES - 1:SUBLANES, :], outi.shape)
            for j in range(seg):
                rows = slice(j * SUBLANES, (j + 1) * SUBLANES)
                kr, ki = ar * kr - ai * ki, ar * ki + ai * kr
                hre[rows, sl] += kr
                him[rows, sl] += ki
        ysc[:, c * gw:(c + 1) * gw] = (_dot(hre[:, sl].astype(BF16), cwre_ref[c])
                                       - _dot(him[:, sl].astype(BF16), cwim_ref[c]))

    if chain:
        @pl.when(pl.program_id(1) == pl.num_programs(1) - 1)
        def _():
            sre_ref[...] = car_re[0:1, :]
            sim_ref[...] = car_im[0:1, :]

    y = _dot_split(back_ref[...], ysc[...])
    z = _gelu_tanh(y + d_ref[...] * u)
    gl = _dot(z.astype(BF16), gw_ref[...]) + gb_ref[...]
    y_ref[...] = z * jax.nn.sigmoid(gl)


def _s5_mixer(proj, h0re, h0im, sw, *, chain, nseq, tt=S5_TILE):
    rows = proj.shape[0]
    const = lambda shape: pl.BlockSpec(shape, lambda *_: (0,) * len(shape))
    if chain:
        assert tt == S5_SEG * SUBLANES
        nt = rows // nseq // tt
        grid = (nseq, nt)
        u_spec = pl.BlockSpec((tt, S5_WIDTH), lambda b, t: (b * nt + t, 0))
        st_spec = pl.BlockSpec((None, 1, S5_HID), lambda b, t: (b, 0, 0))
        st_shape = jax.ShapeDtypeStruct((nseq, 1, S5_HID), F32)
        sem = ("parallel", "arbitrary")
    else:
        grid = (rows // tt,)
        u_spec = pl.BlockSpec((tt, S5_WIDTH), lambda i: (i, 0))
        st_spec = pl.BlockSpec((tt // SLAB, S5_HID), lambda i: (i, 0))
        st_shape = jax.ShapeDtypeStruct((rows // SLAB, S5_HID), F32)
        sem = ("parallel",)
    y_spec = u_spec
    bw = S5_HID // S5_BLOCKS
    gw = S5_WIDTH // S5_BLOCKS
    perm = _scan_layout(tt, tt if chain else SLAB * SUBLANES)
    return pl.pallas_call(
        functools.partial(_s5_kernel, chain=chain, tt=tt),
        grid=grid,
        in_specs=[u_spec, st_spec, st_spec,
                  const((tt, tt)), const((tt, tt)), const((8, SUBLANES, S5_HID)),
                  const((S5_BLOCKS, gw, bw)), const((S5_BLOCKS, gw, bw)),
                  const((S5_BLOCKS, bw, gw)), const((S5_BLOCKS, bw, gw)),
                  const((1, S5_WIDTH)), const((S5_WIDTH, S5_WIDTH)), const((1, S5_WIDTH))],
        out_specs=[y_spec, st_spec, st_spec],
        out_shape=[jax.ShapeDtypeStruct((rows, S5_WIDTH), F32), st_shape, st_shape],
        scratch_shapes=[pltpu.VMEM((tt, S5_HID), F32), pltpu.VMEM((tt, S5_HID), F32),
                        pltpu.VMEM((tt, S5_WIDTH), F32),
                        pltpu.VMEM((SUBLANES, S5_HID), F32), pltpu.VMEM((SUBLANES, S5_HID), F32)],
        compiler_params=_params(*sem),
        name="s5_chain" if chain else "s5_slab",
    )(proj, h0re, h0im, perm, perm.T, sw["coef"], sw["b_re"], sw["b_im"], sw["c_re"], sw["c_im"],
      sw["d"], sw["glu_w"], sw["glu_b"])


def _softplus(x):
    return jnp.maximum(x, 0.0) + jnp.log1p(jnp.exp(-jnp.abs(x)))


def _segment_masks(seg):
    ri = lax.broadcasted_iota(jnp.int32, (GDN_TILE, GDN_TILE), 0)
    ci = lax.broadcasted_iota(jnp.int32, (GDN_TILE, GDN_TILE), 1)
    same = (ri // seg) == (ci // seg)
    causal = (ri >= ci) & same
    strict = (ri > ci) & same
    return same, causal, strict


def _gdn_gates(bdt, alog_col, dtb_col, seg, slab):
    same, causal, strict = _segment_masks(seg)
    bt = jax.nn.sigmoid(bdt[:GDN_HEADS, :])
    gt = -jnp.exp(alog_col) * _softplus(bdt[GDN_HEADS:, :] + dtb_col)
    if slab:
        creal = (lax.broadcasted_iota(jnp.int32, (1, GDN_TILE), 1) % SLAB) >= SLAB - SLAB_REAL
        bt = jnp.where(creal, bt, 0.0)
        gt = jnp.where(creal, gt, 0.0)
    ri = lax.broadcasted_iota(jnp.int32, (GDN_TILE, GDN_TILE), 0)
    ci = lax.broadcasted_iota(jnp.int32, (GDN_TILE, GDN_TILE), 1)
    gates = jnp.concatenate([bt, gt, jnp.zeros((GDN_TILE - 2 * GDN_HEADS, GDN_TILE), F32)], axis=0)
    lhs = jnp.concatenate([_as_bf16(ri == ci), _as_bf16(causal), _as_bf16(same)], axis=0)
    cols = _dot_split(lhs, gates, NT_DIMS)
    beta, gc, gl = cols[:GDN_TILE], cols[GDN_TILE:2 * GDN_TILE], cols[2 * GDN_TILE:]
    gct = _dot_split(gt, _as_bf16((ri <= ci) & same))
    return beta, gc, gl, gct, causal, strict


def _merge_masks(top):
    ri = lax.broadcasted_iota(jnp.int32, (GDN_TILE, GDN_TILE), 0)
    ci = lax.broadcasted_iota(jnp.int32, (GDN_TILE, GDN_TILE), 1)
    masks = []
    s = 1
    while s < top:
        masks.append(((ri // (2 * s)) == (ci // (2 * s))) & ((ri // s) != (ci // s)))
        s *= 2
    return masks


def _unit_lower_inverse(ms, masks):
    es = [-jnp.where(masks[0], m, 0.0) for m in ms]
    for mask in masks[1:]:
        cs = [jnp.where(mask, m, 0.0) for m in ms]
        ebs = [e.astype(BF16) for e in es]
        xs = [c + _dot(eb, c.astype(BF16)) for c, eb in zip(cs, ebs)]
        es = [e - (x + _dot(x.astype(BF16), eb)) for e, x, eb in zip(es, xs, ebs)]
    return es


def _gdn_intra(qs, ks, vs, betas, gcols, grows, glcols, causal, strict, masks):
    n = range(len(qs))
    qn = [q * lax.rsqrt(jnp.sum(q * q, axis=-1, keepdims=True) + NORM_EPS) * (GDN_HEAD_DIM ** -0.5) for q in qs]
    kn = [k * lax.rsqrt(jnp.sum(k * k, axis=-1, keepdims=True) + NORM_EPS) for k in ks]
    decay = [jnp.exp(jnp.where(causal, gcols[i] - grows[i], -jnp.inf)) for i in n]
    kb = [kn[i] * betas[i] for i in n]
    knb = [k.astype(BF16) for k in kn]
    kk = [lax.dot_general(kb[i].astype(BF16), knb[i], NT_DIMS, preferred_element_type=F32) for i in n]
    ys = _unit_lower_inverse([kk[i] * jnp.where(strict, decay[i], 0.0) for i in n], masks)
    eg = [jnp.exp(g) for g in gcols]
    rhs = [jnp.concatenate([vs[i] * betas[i], kb[i] * eg[i]], axis=1) for i in n]
    uw = [rhs[i] + _dot(ys[i].astype(BF16), rhs[i].astype(BF16)) for i in n]
    attn = [lax.dot_general(qn[i].astype(BF16), knb[i], NT_DIMS, preferred_element_type=F32) * decay[i] for i in n]
    qg = [qn[i] * eg[i] for i in n]
    kd = [kn[i] * jnp.exp(glcols[i] - gcols[i]) for i in n]
    return [(uw[i][:, :GDN_HEAD_DIM], uw[i][:, GDN_HEAD_DIM:], attn[i], qg[i], kd[i]) for i in n]


def _gdn_out(o, gate, norm_w):
    o = o * lax.rsqrt(jnp.mean(o * o, axis=-1, keepdims=True) + NORM_EPS) * norm_w
    return o * _silu(gate)


def _conv4(win_ref, cw, width_slice):
    out = cw[0:1, :] * win_ref[pl.ds(SUBLANES - 3, GDN_TILE), width_slice]
    for j in range(1, GDN_CONV):
        out = out + cw[j:j + 1, :] * win_ref[pl.ds(SUBLANES - 3 + j, GDN_TILE), width_slice]
    return out


def _gdn_chain_kernel(q_ref, k_ref, v_ref, gate_ref, bdt_ref, buf_ref, s0_ref, cw_ref,
                      alc_ref, dbc_ref, nw_ref,
                      y_ref, sout_ref, bufout_ref, win, s_sc):
    t = pl.program_id(1)
    w3 = 3 * GDN_WIDTH

    @pl.when(t == 0)
    def _():
        win[0:SUBLANES, :] = jnp.zeros((SUBLANES, w3), F32)
        win[SUBLANES - 3:SUBLANES, :] = buf_ref[...]
        s_sc[...] = s0_ref[...]

    win[SUBLANES:, 0:GDN_WIDTH] = q_ref[...]
    win[SUBLANES:, GDN_WIDTH:2 * GDN_WIDTH] = k_ref[...]
    win[SUBLANES:, 2 * GDN_WIDTH:] = v_ref[...]

    beta, gc, gl, gct, causal, strict = _gdn_gates(bdt_ref[...], alc_ref[...], dbc_ref[...], GDN_TILE, False)
    masks = _merge_masks(GDN_TILE)
    heads = range(GDN_HEADS)
    qkv = [[], [], []]
    for part in range(3):
        for h in heads:
            cs = slice(part * GDN_WIDTH + h * GDN_HEAD_DIM, part * GDN_WIDTH + (h + 1) * GDN_HEAD_DIM)
            qkv[part].append(_silu(_conv4(win, cw_ref[:, cs], cs)))
    gcols = [gc[:, GDN_HEADS + h:GDN_HEADS + h + 1] for h in heads]
    glcols = [gl[:, GDN_HEADS + h:GDN_HEADS + h + 1] for h in heads]
    intra = _gdn_intra(qkv[0], qkv[1], qkv[2], [beta[:, h:h + 1] for h in heads], gcols,
                       [gct[h:h + 1, :] for h in heads], glcols, causal, strict, masks)
    s_old = [s_sc[h] for h in heads]
    a = [_dot(jnp.concatenate([intra[h][1], intra[h][3]], axis=0).astype(BF16), s_old[h].astype(BF16))
         for h in heads]
    vnb = [(intra[h][0] - a[h][:GDN_TILE]).astype(BF16) for h in heads]
    o = [a[h][GDN_TILE:] + _dot(intra[h][2].astype(BF16), vnb[h]) for h in heads]
    for h in heads:
        s_sc[h] = s_old[h] * jnp.exp(glcols[h][0:1, :]) + _dot(intra[h][4].T.astype(BF16), vnb[h])
    for h in heads:
        hs = slice(h * GDN_HEAD_DIM, (h + 1) * GDN_HEAD_DIM)
        y_ref[:, hs] = _gdn_out(o[h], gate_ref[:, hs], nw_ref[...])

    win[0:SUBLANES, :] = win[GDN_TILE:GDN_TILE + SUBLANES, :]

    @pl.when(t == pl.num_programs(1) - 1)
    def _():
        sout_ref[...] = s_sc[...]
        bufout_ref[...] = win[GDN_TILE + SUBLANES - 3:GDN_TILE + SUBLANES, :]


def _gdn_chain(proj, bdt, conv_buf, s0, gw, *, nseq):
    rows = proj.shape[0]
    nt = rows // nseq // GDN_TILE
    w3 = 3 * GDN_WIDTH
    const = lambda shape: pl.BlockSpec(shape, lambda *_: (0,) * len(shape))
    col = lambda cb: pl.BlockSpec((GDN_TILE, GDN_WIDTH), lambda b, t, cb=cb: (b * nt + t, cb))
    return pl.pallas_call(
        _gdn_chain_kernel,
        grid=(nseq, nt),
        in_specs=[col(1), col(2), col(3), col(4),
                  pl.BlockSpec((2 * GDN_HEADS, GDN_TILE), lambda b, t: (0, b * nt + t)),
                  pl.BlockSpec((None, GDN_CONV - 1, w3), lambda b, t: (b, 0, 0)),
                  pl.BlockSpec((None, GDN_HEADS, GDN_HEAD_DIM, GDN_HEAD_DIM), lambda b, t: (b, 0, 0, 0)),
                  const((GDN_CONV, w3)), const((GDN_HEADS, 1)), const((GDN_HEADS, 1)), const((1, GDN_HEAD_DIM))],
        out_specs=[pl.BlockSpec((GDN_TILE, GDN_WIDTH), lambda b, t: (b * nt + t, 0)),
                   pl.BlockSpec((None, GDN_HEADS, GDN_HEAD_DIM, GDN_HEAD_DIM), lambda b, t: (b, 0, 0, 0)),
                   pl.BlockSpec((None, GDN_CONV - 1, w3), lambda b, t: (b, 0, 0))],
        out_shape=[jax.ShapeDtypeStruct((rows, GDN_WIDTH), F32),
                   jax.ShapeDtypeStruct((nseq, GDN_HEADS, GDN_HEAD_DIM, GDN_HEAD_DIM), F32),
                   jax.ShapeDtypeStruct((nseq, GDN_CONV - 1, w3), F32)],
        scratch_shapes=[pltpu.VMEM((GDN_TILE + SUBLANES, w3), F32),
                        pltpu.VMEM((GDN_HEADS, GDN_HEAD_DIM, GDN_HEAD_DIM), F32)],
        compiler_params=_params("parallel", "arbitrary"),
        name="gdn_chain",
    )(proj, proj, proj, proj, bdt, conv_buf, s0, gw["conv_w"], gw["alog_col"], gw["dtb_col"], gw["norm_w"])


def _gdn_slab_kernel(q_ref, k_ref, v_ref, gate_ref, bdt_ref, bq_ref, bk_ref, bv_ref, s0_ref,
                     cwq_ref, cwk_ref, cwv_ref, alc_ref, dbc_ref, nw_ref,
                     y_ref, sout_ref, oq_ref, ok_ref, ov_ref,
                     win):
    h = pl.program_id(1)
    nslab = GDN_TILE // SLAB
    first = (pl.program_id(0) % (bq_ref.shape[1] // nslab)) * nslab
    real = (lax.broadcasted_iota(jnp.int32, (GDN_TILE, 1), 0) % SLAB) >= SLAB - SLAB_REAL
    qkv = []
    taps = range(GDN_CONV - 1)
    place = [_slab_row_selector(GDN_TILE, bq_ref.shape[1], 1 + i, first, transpose=True) for i in taps]
    take = [_slab_row_selector(GDN_TILE, nslab, SLAB - 3 + i, 0, transpose=False) for i in taps]
    for part, (x_ref, b_ref, cw_ref, o_ref) in enumerate(
            ((q_ref, bq_ref, cwq_ref, oq_ref), (k_ref, bk_ref, cwk_ref, ok_ref), (v_ref, bv_ref, cwv_ref, ov_ref))):
        cs = slice(part * GDN_HEAD_DIM, (part + 1) * GDN_HEAD_DIM)
        x = x_ref[...]
        for i in taps:
            x = x + _dot_split(place[i], b_ref[i])
        win[0:SUBLANES, cs] = jnp.zeros((SUBLANES, GDN_HEAD_DIM), F32)
        win[SUBLANES:, cs] = x
        conv = _conv4(win, cw_ref[...], cs)
        qkv.append(jnp.where(real, _silu(conv), 0.0))
        for i in taps:
            o_ref[i] = _dot_split(take[i], x)

    beta, gc, gl, gct, causal, strict = _gdn_gates(bdt_ref[...], alc_ref[...], dbc_ref[...], SLAB, True)
    lane = lax.broadcasted_iota(jnp.int32, (1, LANES), 1)
    sub = lax.broadcasted_iota(jnp.int32, (SUBLANES, 1), 0)
    pick = lambda a, idx: jnp.sum(jnp.where(lane == idx, a, 0.0), axis=1, keepdims=True)
    beta_col = pick(beta, h)
    gcol = pick(gc, GDN_HEADS + h)
    glcol = pick(gl, GDN_HEADS + h)
    grow = jnp.sum(jnp.where(sub == h, gct, 0.0), axis=0, keepdims=True)
    masks = _merge_masks(SLAB_REAL)
    (u, wk, attn, qg, kd), = _gdn_intra([qkv[0]], [qkv[1]], [qkv[2]], [beta_col], [gcol], [grow], [glcol],
                                        causal, strict, masks)
    res = []
    for i in range(nslab):
        rows = slice(i * SLAB, (i + 1) * SLAB)
        lhs = jnp.concatenate([wk[rows], qg[rows]], axis=0).astype(BF16)
        res.append(_dot(lhs, s0_ref[i, 0].astype(BF16)))
    v_new = u - jnp.concatenate([r[:SLAB] for r in res], axis=0)
    vnb = v_new.astype(BF16)
    o = jnp.concatenate([r[SLAB:] for r in res], axis=0) + _dot(attn.astype(BF16), vnb)
    y_ref[...] = _gdn_out(o, gate_ref[...], nw_ref[...])
    kdt = kd.T
    egl = jnp.exp(glcol)
    for i in range(nslab):
        in_slab = (lane // SLAB) == i
        upd = _dot(jnp.where(in_slab, kdt, 0.0).astype(BF16), vnb)
        sout_ref[i, 0] = s0_ref[i, 0] * egl[i * SLAB:i * SLAB + 1, :] + upd


def _gdn_slab(proj, bdt, conv_buf, s0, gw):
    rows = proj.shape[0]
    nb = rows // SLAB
    nslab = GDN_TILE // SLAB
    hb = GDN_WIDTH // GDN_HEAD_DIM
    const = lambda shape: pl.BlockSpec(shape, lambda *_: (0,) * len(shape))
    col = lambda g: pl.BlockSpec((GDN_TILE, GDN_HEAD_DIM), lambda i, h, g=g: (i, g * hb + h))
    per = LANES // nslab
    buf = lambda g: pl.BlockSpec((GDN_CONV - 1, LANES, GDN_HEAD_DIM), lambda i, h, g=g: (0, i // per, g * hb + h))
    cwb = lambda g: pl.BlockSpec((GDN_CONV, GDN_HEAD_DIM), lambda i, h, g=g: (0, g * hb + h))
    st = pl.BlockSpec((nslab, 1, GDN_HEAD_DIM, GDN_HEAD_DIM), lambda i, h: (i, h, 0, 0))
    obuf = pl.BlockSpec((GDN_CONV - 1, nslab, GDN_HEAD_DIM), lambda i, h: (0, i, h))
    tile = (GDN_TILE, GDN_HEAD_DIM)
    return pl.pallas_call(
        _gdn_slab_kernel,
        grid=(rows // GDN_TILE, GDN_HEADS),
        in_specs=[col(1), col(2), col(3), col(4),
                  pl.BlockSpec((2 * GDN_HEADS, GDN_TILE), lambda i, h: (0, i)),
                  buf(0), buf(1), buf(2), st, cwb(0), cwb(1), cwb(2),
                  const((GDN_HEADS, 1)), const((GDN_HEADS, 1)), const((1, GDN_HEAD_DIM))],
        out_specs=[pl.BlockSpec(tile, lambda i, h: (i, h)), st, obuf, obuf, obuf],
        out_shape=[jax.ShapeDtypeStruct((rows, GDN_WIDTH), F32),
                   jax.ShapeDtypeStruct(s0.shape, F32)]
                  + [jax.ShapeDtypeStruct((GDN_CONV - 1, nb, GDN_WIDTH), F32)] * 3,
        scratch_shapes=[pltpu.VMEM((GDN_TILE + SUBLANES, 3 * GDN_HEAD_DIM), F32)],
        compiler_params=_params("parallel", "arbitrary"),
        name="gdn_slab",
    )(proj, proj, proj, proj, bdt, conv_buf, conv_buf, conv_buf, s0,
      gw["conv_w"], gw["conv_w"], gw["conv_w"], gw["alog_col"], gw["dtb_col"], gw["norm_w"])


def _block_diag(w):
    per = S5_GROUPS // S5_BLOCKS
    g, a, b = w.shape
    w = w.reshape(S5_BLOCKS, per, a, b)
    eye = jnp.eye(per, dtype=w.dtype)
    return jnp.einsum("jgab,gk->jgakb", w, eye).reshape(S5_BLOCKS, per * a, per * b)


def _layer_weights(l, ln1_g, ln1_b, ffn1_w_in, ffn1_w_out, w_mix_in, s5_lambda_re, s5_lambda_im, s5_log_dt,
                   s5_b_re, s5_b_im, s5_c_re, s5_c_im, s5_d, s5_glu_w, s5_glu_b, gdn_conv_w, gdn_a_log,
                   gdn_dt_bias, gdn_norm_w, w_mix_out, ln2_g, ln2_b, ffn2_w_in, ffn2_w_out, ln3_g, ln3_b):
    row = lambda v: v[l].reshape(1, -1).astype(F32)
    w = {
        "ln1": (row(ln1_g), row(ln1_b)), "ln2": (row(ln2_g), row(ln2_b)), "ln3": (row(ln3_g), row(ln3_b)),
        "ffn1": (ffn1_w_in[l], ffn1_w_out[l]),
        "ffn2": (ffn2_w_in[l], ffn2_w_out[l]),
        "mix_in": (w_mix_in, l),
        "mix_tail_t": w_mix_in[l][:, MIX_MAIN:].T.astype(BF16),
        "mix_out": w_mix_out[l].astype(BF16),
    }
    coef, bb_re, bb_im = _s5_disc(
        s5_lambda_re[l].reshape(1, S5_HID).astype(F32), s5_lambda_im[l].reshape(1, S5_HID).astype(F32),
        jnp.repeat(s5_log_dt[l], S5_STATE).reshape(1, S5_HID).astype(F32),
        _block_diag(jnp.swapaxes(s5_b_re[l], 1, 2).astype(F32)),
        _block_diag(jnp.swapaxes(s5_b_im[l], 1, 2).astype(F32)))
    w["s5"] = {
        "coef": coef, "b_re": bb_re, "b_im": bb_im,
        "c_re": _block_diag(jnp.swapaxes(s5_c_re[l], 1, 2)).astype(BF16),
        "c_im": _block_diag(jnp.swapaxes(s5_c_im[l], 1, 2)).astype(BF16),
        "d": row(s5_d), "glu_w": s5_glu_w[l].astype(BF16), "glu_b": row(s5_glu_b),
    }
    w["gdn"] = {
        "conv_w": gdn_conv_w[l].astype(F32),
        "alog_col": gdn_a_log[l].reshape(GDN_HEADS, 1).astype(F32),
        "dtb_col": gdn_dt_bias[l].reshape(GDN_HEADS, 1).astype(F32),
        "norm_w": row(gdn_norm_w),
    }
    return w


def _prompt_layer(x, w, nseq):
    x = _ffn_ln(x, *w["ffn1"], *w["ln1"])
    proj, bdt = _mixin(x, *w["mix_in"], w["mix_tail_t"])
    z_s5 = jnp.zeros((nseq, 1, S5_HID), F32)
    y_s5, n_re, n_im = _s5_mixer(proj, z_s5, z_s5, w["s5"], chain=True, nseq=nseq)
    z_gdn = jnp.zeros((nseq, GDN_HEADS, GDN_HEAD_DIM, GDN_HEAD_DIM), F32)
    z_buf = jnp.zeros((nseq, GDN_CONV - 1, 3 * GDN_WIDTH), F32)
    y_gdn, n_s, n_buf = _gdn_chain(proj, bdt, z_buf, z_gdn, w["gdn"], nseq=nseq)
    x = _mixout_ln(y_s5, y_gdn, x, w["mix_out"], *w["ln2"])
    x = _ffn_ln(x, *w["ffn2"], *w["ln3"])
    shape = (nseq, S5_GROUPS, S5_STATE)
    return x, n_re.reshape(shape), n_im.reshape(shape), n_s, n_buf


def _sample_layer(x, s5_re, s5_im, gdn_s, conv_buf, w, nb, t):
    x = _ffn_ln(x, *w["ffn1"], *w["ln1"])
    xs = jnp.pad(x.reshape(nb, t, D_MODEL), ((0, 0), (SLAB - t, 0), (0, 0))).reshape(nb * SLAB, D_MODEL)
    proj, bdt = _mixin(xs, *w["mix_in"], w["mix_tail_t"])
    y_s5, n_re, n_im = _s5_mixer(proj, s5_re.reshape(nb, S5_HID).astype(F32),
                                 s5_im.reshape(nb, S5_HID).astype(F32), w["s5"], chain=False, nseq=nb)
    y_gdn, n_s, bq, bk, bv = _gdn_slab(proj, bdt, jnp.swapaxes(conv_buf.astype(F32), 0, 1),
                                       gdn_s.astype(F32), w["gdn"])
    xs = _mixout_ln(y_s5, y_gdn, xs, w["mix_out"], *w["ln2"])
    x = xs.reshape(nb, SLAB, D_MODEL)[:, SLAB - t:].reshape(nb * t, D_MODEL)
    x = _ffn_ln(x, *w["ffn2"], *w["ln3"])
    shape = (nb, S5_GROUPS, S5_STATE)
    n_buf = jnp.swapaxes(jnp.concatenate([bq, bk, bv], axis=-1), 0, 1)
    return x, n_re.reshape(shape), n_im.reshape(shape), n_s, n_buf


def kernel(x_prompt, x_sample, state_s5_re, state_s5_im, state_gdn, state_conv, ln1_g, ln1_b, ffn1_w_in, ffn1_w_out, w_mix_in, s5_lambda_re, s5_lambda_im, s5_log_dt, s5_b_re, s5_b_im, s5_c_re, s5_c_im, s5_d, s5_glu_w, s5_glu_b, gdn_conv_w, gdn_a_log, gdn_dt_bias, gdn_norm_w, w_mix_out, ln2_g, ln2_b, ffn2_w_in, ffn2_w_out, ln3_g, ln3_b):
    bp, tp, _ = x_prompt.shape
    bs, ts, _ = x_sample.shape
    assert ts == SLAB_REAL and tp % S5_TILE == 0 and (bs * SLAB) % S5_TILE == 0
    depth = ln1_g.shape[0]
    yp = x_prompt.astype(F32).reshape(bp * tp, D_MODEL)
    ys = x_sample.astype(F32).reshape(bs * ts, D_MODEL)
    outs = [[] for _ in range(8)]
    for l in range(depth):
        w = _layer_weights(l, ln1_g, ln1_b, ffn1_w_in, ffn1_w_out, w_mix_in, s5_lambda_re, s5_lambda_im,
                           s5_log_dt, s5_b_re, s5_b_im, s5_c_re, s5_c_im, s5_d, s5_glu_w, s5_glu_b,
                           gdn_conv_w, gdn_a_log, gdn_dt_bias, gdn_norm_w, w_mix_out, ln2_g, ln2_b,
                           ffn2_w_in, ffn2_w_out, ln3_g, ln3_b)
        yp, *p_state = _prompt_layer(yp, w, bp)
        ys, *s_state = _sample_layer(ys, state_s5_re[l], state_s5_im[l], state_gdn[l], state_conv[l], w, bs, ts)
        for acc, val in zip(outs, p_state + s_state):
            acc.append(val)
    return (yp.reshape(x_prompt.shape).astype(x_prompt.dtype), ys.reshape(x_sample.shape).astype(x_sample.dtype),
            *(o[0][None] if depth == 1 else jnp.stack(o) for o in outs))
```

```python
import functools
import math

import jax
import jax.numpy as jnp
from jax import lax
from jax.experimental import pallas as pl
from jax.experimental.pallas import tpu as pltpu

F32 = jnp.float32
BF16 = jnp.bfloat16

D_MODEL = 2048
S5_WIDTH = 1024
S5_GROUP = 16
S5_GROUPS = 64
S5_STATE = 64
S5_HID = S5_GROUPS * S5_STATE
GDN_WIDTH = 1024
GDN_HEAD_DIM = 128
GDN_HEADS = 8
GDN_CONV = 4
D_FF = 5632
MIX_MAIN = 5120
DEEP_ALPHA = 2.0 ** 0.25
LN_EPS = 1e-5
NORM_EPS = 1e-6

SUBLANES = 8
LANES = 128
SLAB = 8
SLAB_REAL = 4
GDN_TILE = 128
GDN_SLAB_HEADS = 4
FFN_TM = 1024
FFN_CHUNK = 512
S5_TILE = 256
S5_SEG = S5_TILE // SUBLANES
S5_BLOCKS = 8
VMEM_LIMIT = 56 * 1024 * 1024

NT_DIMS = (((1,), (1,)), ((), ()))


def _dot(a, b, **kw):
    return jnp.dot(a, b, preferred_element_type=F32, **kw)


def _silu(x):
    return x * jax.nn.sigmoid(x)


def _layer_norm(y, g, b):
    mu = jnp.mean(y, axis=-1, keepdims=True)
    d = y - mu
    var = jnp.mean(d * d, axis=-1, keepdims=True)
    return d * lax.rsqrt(var + LN_EPS) * g + b


def _slab_row_selector(rows, nseq, slab_row, first, *, transpose):
    shape = (rows, nseq) if transpose else (nseq, rows)
    r = lax.broadcasted_iota(jnp.int32, shape, 0 if transpose else 1)
    b = lax.broadcasted_iota(jnp.int32, shape, 1 if transpose else 0)
    return _as_bf16(r == SLAB * (b - first) + slab_row)


def _as_bf16(mask):
    return mask.astype(F32).astype(BF16)


def _split3(x):
    pieces = []
    for _ in range(3):
        pieces.append(x.astype(BF16))
        x = x - pieces[-1].astype(F32)
    return pieces


def _dot_split(a, b, dims=None):
    f32_is_lhs = a.dtype == F32
    acc = None
    for piece in _split3(a if f32_is_lhs else b):
        lhs, rhs = (piece, b) if f32_is_lhs else (a, piece)
        d = _dot(lhs, rhs) if dims is None else lax.dot_general(lhs, rhs, dims, preferred_element_type=F32)
        acc = d if acc is None else acc + d
    return acc


def _params(*sem):
    return pltpu.CompilerParams(dimension_semantics=sem, vmem_limit_bytes=VMEM_LIMIT)


def _ffn_kernel(x_ref, wg_ref, wu_ref, wo_ref, g_ref, b_ref, o_ref, xb_ref):
    j = pl.program_id(1)

    @pl.when(j == 0)
    def _():
        o_ref[...] = jnp.zeros_like(o_ref)
        xb_ref[...] = x_ref[...].astype(BF16)

    xb = xb_ref[...]
    gate = _dot(xb, wg_ref[...].astype(BF16))
    up = _dot(xb, wu_ref[...].astype(BF16))
    h = (_silu(gate) * up).astype(BF16)
    for c in range(0, D_MODEL, FFN_CHUNK):
        o_ref[:, c:c + FFN_CHUNK] += _dot(h, wo_ref[:, c:c + FFN_CHUNK].astype(BF16))

    @pl.when(j == pl.num_programs(1) - 1)
    def _():
        for r in range(0, o_ref.shape[0], FFN_CHUNK // 2):
            rows = slice(r, r + FFN_CHUNK // 2)
            y = DEEP_ALPHA * x_ref[rows, :] + 0.5 * o_ref[rows, :]
            o_ref[rows, :] = _layer_norm(y, g_ref[...], b_ref[...])


def _ffn_ln(x, w_in, w_out, g, b, *, tf=256):
    n = x.shape[0]
    tm = math.gcd(n, FFN_TM)
    nff = D_FF // tf
    return pl.pallas_call(
        _ffn_kernel,
        grid=(n // tm, nff),
        in_specs=[
            pl.BlockSpec((tm, D_MODEL), lambda i, j: (i, 0)),
            pl.BlockSpec((D_MODEL, tf), lambda i, j: (0, j)),
            pl.BlockSpec((D_MODEL, tf), lambda i, j: (0, j + nff)),
            pl.BlockSpec((tf, D_MODEL), lambda i, j: (j, 0)),
            pl.BlockSpec((1, D_MODEL), lambda i, j: (0, 0)),
            pl.BlockSpec((1, D_MODEL), lambda i, j: (0, 0)),
        ],
        out_specs=pl.BlockSpec((tm, D_MODEL), lambda i, j: (i, 0)),
        out_shape=jax.ShapeDtypeStruct((n, D_MODEL), F32),
        scratch_shapes=[pltpu.VMEM((tm, D_MODEL), BF16)],
        compiler_params=_params("parallel", "arbitrary"),
        name="ffn_ln",
    )(x, w_in, w_in, w_out, g, b)


def _mixin_kernel(x_ref, w_ref, wt_ref, o_ref, ot_ref, xb_ref):
    @pl.when(pl.program_id(1) == 0)
    def _():
        xb_ref[...] = x_ref[...].astype(BF16)
        ncols = ot_ref.shape[0]
        lane = lax.broadcasted_iota(jnp.int32, (1, LANES), 1)
        wt = jnp.where(lane < ncols, wt_ref[...], 0.0).astype(BF16)
        ot_ref[...] = _dot(xb_ref[...], wt).T[:ncols, :]

    o_ref[...] = _dot(xb_ref[...], w_ref[...].astype(BF16))


def _mixin(x, w, layer, *, tm=1024, tn=1024):
    n = x.shape[0]
    return pl.pallas_call(
        _mixin_kernel,
        grid=(n // tm, MIX_MAIN // tn),
        in_specs=[pl.BlockSpec((tm, D_MODEL), lambda i, j: (i, 0)),
                  pl.BlockSpec((None, D_MODEL, tn), lambda i, j: (layer, 0, j)),
                  pl.BlockSpec((None, D_MODEL, LANES), lambda i, j: (layer, 0, MIX_MAIN // LANES))],
        out_specs=[pl.BlockSpec((tm, tn), lambda i, j: (i, j)),
                   pl.BlockSpec((2 * GDN_HEADS, tm), lambda i, j: (0, i))],
        out_shape=[jax.ShapeDtypeStruct((n, MIX_MAIN), F32),
                   jax.ShapeDtypeStruct((2 * GDN_HEADS, n), F32)],
        scratch_shapes=[pltpu.VMEM((tm, D_MODEL), BF16)],
        compiler_params=_params("parallel", "arbitrary"),
        name="mix_in",
    )(x, w, w)


def _mixout_kernel(ya_ref, yb_ref, x_ref, w_ref, g_ref, b_ref, o_ref):
    mix = (_dot(ya_ref[...].astype(BF16), w_ref[0:S5_WIDTH, :])
           + _dot(yb_ref[...].astype(BF16), w_ref[S5_WIDTH:, :]))
    o_ref[...] = _layer_norm(DEEP_ALPHA * x_ref[...] + mix, g_ref[...], b_ref[...])


def _mixout_ln(ya, yb, x, w, g, b, *, tm=512):
    n = x.shape[0]
    return pl.pallas_call(
        _mixout_kernel,
        grid=(n // tm,),
        in_specs=[pl.BlockSpec((tm, S5_WIDTH), lambda i: (i, 0)),
                  pl.BlockSpec((tm, GDN_WIDTH), lambda i: (i, 0)),
                  pl.BlockSpec((tm, D_MODEL), lambda i: (i, 0)),
                  pl.BlockSpec((D_MODEL, D_MODEL), lambda i: (0, 0)),
                  pl.BlockSpec((1, D_MODEL), lambda i: (0, 0)),
                  pl.BlockSpec((1, D_MODEL), lambda i: (0, 0))],
        out_specs=pl.BlockSpec((tm, D_MODEL), lambda i: (i, 0)),
        out_shape=jax.ShapeDtypeStruct((n, D_MODEL), F32),
        compiler_params=_params("parallel"),
        name="mix_out_ln",
    )(ya, yb, x, w, g, b)


def _s5_disc_kernel(lre_ref, lim_ref, ldt_ref, bre_ref, bim_ref, coef_ref, bbre_ref, bbim_ref):
    lr, li = lre_ref[...], lim_ref[...]
    dt = jnp.exp(ldt_ref[...])
    mag = jnp.exp(lr * dt)
    ar = mag * jnp.cos(li * dt)
    ai = mag * jnp.sin(li * dt)
    nr, ni = ar - 1.0, ai
    den = lr * lr + li * li
    c_re = (nr * lr + ni * li) / den
    c_im = (ni * lr - nr * li) / den
    bw = S5_HID // S5_BLOCKS
    for j in range(S5_BLOCKS):
        cr, ci = c_re[:, j * bw:(j + 1) * bw], c_im[:, j * bw:(j + 1) * bw]
        bbre_ref[j] = (cr * bre_ref[j] - ci * bim_ref[j]).astype(BF16)
        bbim_ref[j] = (cr * bim_ref[j] + ci * bre_ref[j]).astype(BF16)

    def cmul(x, y):
        return x[0] * y[0] - x[1] * y[1], x[0] * y[1] + x[1] * y[0]

    width = lr.shape[-1]
    p = (ar, ai)
    for _ in range(S5_SEG - 1):
        p = cmul(p, (ar, ai))
    coef_ref[...] = jnp.zeros_like(coef_ref)
    for part, v in enumerate((ar, ai)):
        coef_ref[part] = jnp.broadcast_to(v, (SUBLANES, width))
    for k, s in enumerate((1, 2, 4)):
        for part in range(2):
            coef_ref[2 + 2 * k + part, s:SUBLANES, :] = jnp.broadcast_to(p[part], (SUBLANES - s, width))
        p = cmul(p, p)


def _s5_disc(lre, lim, ldt, b_re, b_im):
    return pl.pallas_call(
        _s5_disc_kernel,
        out_shape=[jax.ShapeDtypeStruct((8, SUBLANES, S5_HID), F32),
                   jax.ShapeDtypeStruct(b_re.shape, BF16), jax.ShapeDtypeStruct(b_im.shape, BF16)],
        name="s5_disc",
    )(lre, lim, ldt, b_re, b_im)


def _gelu_tanh(y):
    return 0.5 * y * (1.0 + jnp.tanh(math.sqrt(2.0 / math.pi) * (y + 0.044715 * (y * y * y))))


def _scan_layout(tt, group):
    p = jnp.arange(tt)
    rem = p % group
    src = (p - rem) + (rem % SUBLANES) * (group // SUBLANES) + rem // SUBLANES
    return (src[:, None] == jnp.arange(tt)[None, :]).astype(BF16)


def _cmul_add(ar, ai, xr, xi, br, bi):
    return ar * xr - ai * xi + br, ar * xi + ai * xr + bi


def _s5_kernel(u_ref, h0re_ref, h0im_ref, perm_ref, back_ref, coef_ref, bre_ref, bim_ref, cwre_ref, cwim_ref,
               d_ref, gw_ref, gb_ref, y_ref, sre_ref, sim_ref, hre, him, ysc, car_re, car_im,
               *, chain, tt):
    group = tt if chain else SLAB * SUBLANES
    seg = group // SUBLANES
    u = u_ref[...]
    ub = _dot(perm_ref[...], u.astype(BF16)).astype(BF16)
    bw = S5_HID // S5_BLOCKS
    gw = S5_WIDTH // S5_BLOCKS
    if chain:
        @pl.when(pl.program_id(1) == 0)
        def _():
            car_re[...] = jnp.broadcast_to(h0re_ref[...], car_re.shape)
            car_im[...] = jnp.broadcast_to(h0im_ref[...], car_im.shape)

    sub = lax.broadcasted_iota(jnp.int32, (SUBLANES, 1), 0)
    for c in range(S5_BLOCKS):
        sl = slice(c * bw, (c + 1) * bw)
        uc = ub[:, c * gw:(c + 1) * gw]
        hre[:, sl] = _dot(uc, bre_ref[c])
        him[:, sl] = _dot(uc, bim_ref[c])
        ar, ai = coef_ref[0, :, sl], coef_ref[1, :, sl]
        if not chain:
            for g in range(tt // group):
                srows = slice(g * SUBLANES, (g + 1) * SUBLANES)
                xr, xi = h0re_ref[srows, sl], h0im_ref[srows, sl]
                for j in range(SLAB - SLAB_REAL, SLAB):
                    rows = slice(g * group + j * SUBLANES, g * group + (j + 1) * SUBLANES)
                    xr, xi = _cmul_add(ar, ai, xr, xi, hre[rows, sl], him[rows, sl])
                    hre[rows, sl] = xr
                    him[rows, sl] = xi
                sre_ref[srows, sl] = xr
                sim_ref[srows, sl] = xi
        else:
            xr = xi = jnp.zeros((SUBLANES, bw), F32)
            for j in range(seg):
                rows = slice(j * SUBLANES, (j + 1) * SUBLANES)
                xr, xi = _cmul_add(ar, ai, xr, xi, hre[rows, sl], him[rows, sl])
                hre[rows, sl] = xr
                him[rows, sl] = xi
            kr = jnp.where(sub == 0, car_re[:, sl], pltpu.roll(xr, 1, 0))
            ki = jnp.where(sub == 0, car_im[:, sl], pltpu.roll(xi, 1, 0))
            for k, s in enumerate((1, 2, 4)):
                pr, pi = coef_ref[2 + 2 * k, :, sl], coef_ref[3 + 2 * k, :, sl]
                kr, ki = _cmul_add(pr, pi, pltpu.roll(kr, s, 0), pltpu.roll(ki, s, 0), kr, ki)
            outr, outi = _cmul_add(coef_ref[2, :, sl], coef_ref[3, :, sl], kr, ki, xr, xi)
            car_re[:, sl] = jnp.broadcast_to(outr[SUBLANES - 1:SUBLANES, :], outr.shape)
            car_im[:, sl] = jnp.broadcast_to(outi[SUBLANES - 1:SUBLANES, :], outi.shape)
            for j in range(seg):
                rows = slice(j * SUBLANES, (j + 1) * SUBLANES)
                kr, ki = ar * kr - ai * ki, ar * ki + ai * kr
                hre[rows, sl] += kr
                him[rows, sl] += ki
        ysc[:, c * gw:(c + 1) * gw] = (_dot(hre[:, sl].astype(BF16), cwre_ref[c])
                                       - _dot(him[:, sl].astype(BF16), cwim_ref[c]))

    if chain:
        @pl.when(pl.program_id(1) == pl.num_programs(1) - 1)
        def _():
            sre_ref[...] = car_re[0:1, :]
            sim_ref[...] = car_im[0:1, :]

    y = _dot_split(back_ref[...], ysc[...])
    z = _gelu_tanh(y + d_ref[...] * u)
    gl = _dot(z.astype(BF16), gw_ref[...]) + gb_ref[...]
    y_ref[...] = z * jax.nn.sigmoid(gl)


def _s5_mixer(proj, h0re, h0im, sw, *, chain, nseq, tt=S5_TILE):
    rows = proj.shape[0]
    const = lambda shape: pl.BlockSpec(shape, lambda *_: (0,) * len(shape))
    if chain:
        assert tt == S5_SEG * SUBLANES
        nt = rows // nseq // tt
        grid = (nseq, nt)
        u_spec = pl.BlockSpec((tt, S5_WIDTH), lambda b, t: (b * nt + t, 0))
        st_spec = pl.BlockSpec((None, 1, S5_HID), lambda b, t: (b, 0, 0))
        st_shape = jax.ShapeDtypeStruct((nseq, 1, S5_HID), F32)
        sem = ("parallel", "arbitrary")
    else:
        grid = (rows // tt,)
        u_spec = pl.BlockSpec((tt, S5_WIDTH), lambda i: (i, 0))
        st_spec = pl.BlockSpec((tt // SLAB, S5_HID), lambda i: (i, 0))
        st_shape = jax.ShapeDtypeStruct((rows // SLAB, S5_HID), F32)
        sem = ("parallel",)
    y_spec = u_spec
    bw = S5_HID // S5_BLOCKS
    gw = S5_WIDTH // S5_BLOCKS
    perm = _scan_layout(tt, tt if chain else SLAB * SUBLANES)
    return pl.pallas_call(
        functools.partial(_s5_kernel, chain=chain, tt=tt),
        grid=grid,
        in_specs=[u_spec, st_spec, st_spec,
                  const((tt, tt)), const((tt, tt)), const((8, SUBLANES, S5_HID)),
                  const((S5_BLOCKS, gw, bw)), const((S5_BLOCKS, gw, bw)),
                  const((S5_BLOCKS, bw, gw)), const((S5_BLOCKS, bw, gw)),
                  const((1, S5_WIDTH)), const((S5_WIDTH, S5_WIDTH)), const((1, S5_WIDTH))],
        out_specs=[y_spec, st_spec, st_spec],
        out_shape=[jax.ShapeDtypeStruct((rows, S5_WIDTH), F32), st_shape, st_shape],
        scratch_shapes=[pltpu.VMEM((tt, S5_HID), F32), pltpu.VMEM((tt, S5_HID), F32),
                        pltpu.VMEM((tt, S5_WIDTH), F32),
                        pltpu.VMEM((SUBLANES, S5_HID), F32), pltpu.VMEM((SUBLANES, S5_HID), F32)],
        compiler_params=_params(*sem),
        name="s5_chain" if chain else "s5_slab",
    )(proj, h0re, h0im, perm, perm.T, sw["coef"], sw["b_re"], sw["b_im"], sw["c_re"], sw["c_im"],
      sw["d"], sw["glu_w"], sw["glu_b"])


def _softplus(x):
    return jnp.maximum(x, 0.0) + jnp.log1p(jnp.exp(-jnp.abs(x)))


def _segment_masks(seg):
    ri = lax.broadcasted_iota(jnp.int32, (GDN_TILE, GDN_TILE), 0)
    ci = lax.broadcasted_iota(jnp.int32, (GDN_TILE, GDN_TILE), 1)
    same = (ri // seg) == (ci // seg)
    causal = (ri >= ci) & same
    strict = (ri > ci) & same
    return same, causal, strict


def _gdn_gates(bdt, alog_col, dtb_col, seg, slab):
    same, causal, strict = _segment_masks(seg)
    bt = jax.nn.sigmoid(bdt[:GDN_HEADS, :])
    gt = -jnp.exp(alog_col) * _softplus(bdt[GDN_HEADS:, :] + dtb_col)
    if slab:
        creal = (lax.broadcasted_iota(jnp.int32, (1, GDN_TILE), 1) % SLAB) >= SLAB - SLAB_REAL
        bt = jnp.where(creal, bt, 0.0)
        gt = jnp.where(creal, gt, 0.0)
    ri = lax.broadcasted_iota(jnp.int32, (GDN_TILE, GDN_TILE), 0)
    ci = lax.broadcasted_iota(jnp.int32, (GDN_TILE, GDN_TILE), 1)
    gates = jnp.concatenate([bt, gt, jnp.zeros((GDN_TILE - 2 * GDN_HEADS, GDN_TILE), F32)], axis=0)
    lhs = jnp.concatenate([_as_bf16(ri == ci), _as_bf16(causal), _as_bf16(same)], axis=0)
    cols = _dot_split(lhs, gates, NT_DIMS)
    beta, gc, gl = cols[:GDN_TILE], cols[GDN_TILE:2 * GDN_TILE], cols[2 * GDN_TILE:]
    gct = _dot_split(gt, _as_bf16((ri <= ci) & same))
    return beta, gc, gl, gct, causal, strict


def _merge_masks(top):
    ri = lax.broadcasted_iota(jnp.int32, (GDN_TILE, GDN_TILE), 0)
    ci = lax.broadcasted_iota(jnp.int32, (GDN_TILE, GDN_TILE), 1)
    masks = []
    s = 1
    while s < top:
        masks.append(((ri // (2 * s)) == (ci // (2 * s))) & ((ri // s) != (ci // s)))
        s *= 2
    return masks


def _unit_lower_inverse(ms, masks):
    es = [-jnp.where(masks[0], m, 0.0) for m in ms]
    for mask in masks[1:]:
        cs = [jnp.where(mask, m, 0.0) for m in ms]
        ebs = [e.astype(BF16) for e in es]
        xs = [c + _dot(eb, c.astype(BF16)) for c, eb in zip(cs, ebs)]
        es = [e - (x + _dot(x.astype(BF16), eb)) for e, x, eb in zip(es, xs, ebs)]
    return es


def _gdn_intra(qs, ks, vs, betas, gcols, grows, glcols, causal, strict, masks):
    n = range(len(qs))
    qn = [q * lax.rsqrt(jnp.sum(q * q, axis=-1, keepdims=True) + NORM_EPS) * (GDN_HEAD_DIM ** -0.5) for q in qs]
    kn = [k * lax.rsqrt(jnp.sum(k * k, axis=-1, keepdims=True) + NORM_EPS) for k in ks]
    decay = [jnp.exp(jnp.where(causal, gcols[i] - grows[i], -jnp.inf)) for i in n]
    kb = [kn[i] * betas[i] for i in n]
    knb = [k.astype(BF16) for k in kn]
    kk = [lax.dot_general(kb[i].astype(BF16), knb[i], NT_DIMS, preferred_element_type=F32) for i in n]
    ys = _unit_lower_inverse([kk[i] * jnp.where(strict, decay[i], 0.0) for i in n], masks)
    eg = [jnp.exp(g) for g in gcols]
    rhs = [jnp.concatenate([vs[i] * betas[i], kb[i] * eg[i]], axis=1) for i in n]
    uw = [rhs[i] + _dot(ys[i].astype(BF16), rhs[i].astype(BF16)) for i in n]
    attn = [lax.dot_general(qn[i].astype(BF16), knb[i], NT_DIMS, preferred_element_type=F32) * decay[i] for i in n]
    qg = [qn[i] * eg[i] for i in n]
    kd = [kn[i] * jnp.exp(glcols[i] - gcols[i]) for i in n]
    return [(uw[i][:, :GDN_HEAD_DIM], uw[i][:, GDN_HEAD_DIM:], attn[i], qg[i], kd[i]) for i in n]


def _gdn_out(o, gate, norm_w):
    o = o * lax.rsqrt(jnp.mean(o * o, axis=-1, keepdims=True) + NORM_EPS) * norm_w
    return o * _silu(gate)


def _conv4(win_ref, cw, width_slice):
    out = cw[0:1, :] * win_ref[pl.ds(SUBLANES - 3, GDN_TILE), width_slice]
    for j in range(1, GDN_CONV):
        out = out + cw[j:j + 1, :] * win_ref[pl.ds(SUBLANES - 3 + j, GDN_TILE), width_slice]
    return out


def _gdn_chain_kernel(q_ref, k_ref, v_ref, gate_ref, bdt_ref, buf_ref, s0_ref, cw_ref,
                      alc_ref, dbc_ref, nw_ref,
                      y_ref, sout_ref, bufout_ref, win, s_sc):
    t = pl.program_id(1)
    w3 = 3 * GDN_WIDTH

    @pl.when(t == 0)
    def _():
        win[0:SUBLANES, :] = jnp.zeros((SUBLANES, w3), F32)
        win[SUBLANES - 3:SUBLANES, :] = buf_ref[...]
        s_sc[...] = s0_ref[...]

    win[SUBLANES:, 0:GDN_WIDTH] = q_ref[...]
    win[SUBLANES:, GDN_WIDTH:2 * GDN_WIDTH] = k_ref[...]
    win[SUBLANES:, 2 * GDN_WIDTH:] = v_ref[...]

    beta, gc, gl, gct, causal, strict = _gdn_gates(bdt_ref[...], alc_ref[...], dbc_ref[...], GDN_TILE, False)
    masks = _merge_masks(GDN_TILE)
    heads = range(GDN_HEADS)
    qkv = [[], [], []]
    for part in range(3):
        for h in heads:
            cs = slice(part * GDN_WIDTH + h * GDN_HEAD_DIM, part * GDN_WIDTH + (h + 1) * GDN_HEAD_DIM)
            qkv[part].append(_silu(_conv4(win, cw_ref[:, cs], cs)))
    gcols = [gc[:, GDN_HEADS + h:GDN_HEADS + h + 1] for h in heads]
    glcols = [gl[:, GDN_HEADS + h:GDN_HEADS + h + 1] for h in heads]
    intra = _gdn_intra(qkv[0], qkv[1], qkv[2], [beta[:, h:h + 1] for h in heads], gcols,
                       [gct[h:h + 1, :] for h in heads], glcols, causal, strict, masks)
    s_old = [s_sc[h] for h in heads]
    a = [_dot(jnp.concatenate([intra[h][1], intra[h][3]], axis=0).astype(BF16), s_old[h].astype(BF16))
         for h in heads]
    vnb = [(intra[h][0] - a[h][:GDN_TILE]).astype(BF16) for h in heads]
    o = [a[h][GDN_TILE:] + _dot(intra[h][2].astype(BF16), vnb[h]) for h in heads]
    for h in heads:
        s_sc[h] = s_old[h] * jnp.exp(glcols[h][0:1, :]) + _dot(intra[h][4].T.astype(BF16), vnb[h])
    for h in heads:
        hs = slice(h * GDN_HEAD_DIM, (h + 1) * GDN_HEAD_DIM)
        y_ref[:, hs] = _gdn_out(o[h], gate_ref[:, hs], nw_ref[...])

    win[0:SUBLANES, :] = win[GDN_TILE:GDN_TILE + SUBLANES, :]

    @pl.when(t == pl.num_programs(1) - 1)
    def _():
        sout_ref[...] = s_sc[...]
        bufout_ref[...] = win[GDN_TILE + SUBLANES - 3:GDN_TILE + SUBLANES, :]


def _gdn_chain(proj, bdt, conv_buf, s0, gw, *, nseq):
    rows = proj.shape[0]
    nt = rows // nseq // GDN_TILE
    w3 = 3 * GDN_WIDTH
    const = lambda shape: pl.BlockSpec(shape, lambda *_: (0,) * len(shape))
    col = lambda cb: pl.BlockSpec((GDN_TILE, GDN_WIDTH), lambda b, t, cb=cb: (b * nt + t, cb))
    return pl.pallas_call(
        _gdn_chain_kernel,
        grid=(nseq, nt),
        in_specs=[col(1), col(2), col(3), col(4),
                  pl.BlockSpec((2 * GDN_HEADS, GDN_TILE), lambda b, t: (0, b * nt + t)),
                  pl.BlockSpec((None, GDN_CONV - 1, w3), lambda b, t: (b, 0, 0)),
                  pl.BlockSpec((None, GDN_HEADS, GDN_HEAD_DIM, GDN_HEAD_DIM), lambda b, t: (b, 0, 0, 0)),
                  const((GDN_CONV, w3)), const((GDN_HEADS, 1)), const((GDN_HEADS, 1)), const((1, GDN_HEAD_DIM))],
        out_specs=[pl.BlockSpec((GDN_TILE, GDN_WIDTH), lambda b, t: (b * nt + t, 0)),
                   pl.BlockSpec((None, GDN_HEADS, GDN_HEAD_DIM, GDN_HEAD_DIM), lambda b, t: (b, 0, 0, 0)),
                   pl.BlockSpec((None, GDN_CONV - 1, w3), lambda b, t: (b, 0, 0))],
        out_shape=[jax.ShapeDtypeStruct((rows, GDN_WIDTH), F32),
                   jax.ShapeDtypeStruct((nseq, GDN_HEADS, GDN_HEAD_DIM, GDN_HEAD_DIM), F32),
                   jax.ShapeDtypeStruct((nseq, GDN_CONV - 1, w3), F32)],
        scratch_shapes=[pltpu.VMEM((GDN_TILE + SUBLANES, w3), F32),
                        pltpu.VMEM((GDN_HEADS, GDN_HEAD_DIM, GDN_HEAD_DIM), F32)],
        compiler_params=_params("parallel", "arbitrary"),
        name="gdn_chain",
    )(proj, proj, proj, proj, bdt, conv_buf, s0, gw["conv_w"], gw["alog_col"], gw["dtb_col"], gw["norm_w"])


def _gdn_slab_kernel(q_ref, k_ref, v_ref, gate_ref, bdt_ref, bq_ref, bk_ref, bv_ref, s0_ref,
                     cwq_ref, cwk_ref, cwv_ref, alc_ref, dbc_ref, nw_ref,
                     y_ref, sout_ref, oq_ref, ok_ref, ov_ref,
                     win):
    hp = s0_ref.shape[1]
    hw = hp * GDN_HEAD_DIM
    heads = range(hp)
    head0 = pl.program_id(1) * hp
    nslab = GDN_TILE // SLAB
    first = (pl.program_id(0) % (bq_ref.shape[1] // nslab)) * nslab
    real = (lax.broadcasted_iota(jnp.int32, (GDN_TILE, 1), 0) % SLAB) >= SLAB - SLAB_REAL
    qkv = []
    taps = range(GDN_CONV - 1)
    place = [_slab_row_selector(GDN_TILE, bq_ref.shape[1], 1 + i, first, transpose=True) for i in taps]
    take = [_slab_row_selector(GDN_TILE, nslab, SLAB - 3 + i, 0, transpose=False) for i in taps]
    for part, (x_ref, b_ref, cw_ref, o_ref) in enumerate(
            ((q_ref, bq_ref, cwq_ref, oq_ref), (k_ref, bk_ref, cwk_ref, ok_ref), (v_ref, bv_ref, cwv_ref, ov_ref))):
        cs = slice(part * hw, (part + 1) * hw)
        x = x_ref[...]
        for i in taps:
            x = x + _dot_split(place[i], b_ref[i])
        win[0:SUBLANES, cs] = jnp.zeros((SUBLANES, hw), F32)
        win[SUBLANES:, cs] = x
        conv = jnp.where(real, _silu(_conv4(win, cw_ref[...], cs)), 0.0)
        qkv.append([conv[:, h * GDN_HEAD_DIM:(h + 1) * GDN_HEAD_DIM] for h in heads])
        pieces = _split3(x)
        for i in taps:
            o_ref[i] = sum(_dot(take[i], p) for p in pieces)

    beta, gc, gl, gct, causal, strict = _gdn_gates(bdt_ref[...], alc_ref[...], dbc_ref[...], SLAB, True)
    lane = lax.broadcasted_iota(jnp.int32, (1, LANES), 1)
    sub = lax.broadcasted_iota(jnp.int32, (SUBLANES, 1), 0)
    pick = lambda a, idx: jnp.sum(jnp.where(lane == idx, a, 0.0), axis=1, keepdims=True)
    gcols = [pick(gc, GDN_HEADS + head0 + h) for h in heads]
    glcols = [pick(gl, GDN_HEADS + head0 + h) for h in heads]
    intra = _gdn_intra(qkv[0], qkv[1], qkv[2], [pick(beta, head0 + h) for h in heads], gcols,
                       [jnp.sum(jnp.where(sub == head0 + h, gct, 0.0), axis=0, keepdims=True) for h in heads],
                       glcols, causal, strict, _merge_masks(SLAB_REAL))
    for h in heads:
        u, wk, attn, qg, kd = intra[h]
        res = []
        for i in range(nslab):
            rows = slice(i * SLAB, (i + 1) * SLAB)
            lhs = jnp.concatenate([wk[rows], qg[rows]], axis=0).astype(BF16)
            res.append(_dot(lhs, s0_ref[i, h].astype(BF16)))
        vnb = (u - jnp.concatenate([r[:SLAB] for r in res], axis=0)).astype(BF16)
        o = jnp.concatenate([r[SLAB:] for r in res], axis=0) + _dot(attn.astype(BF16), vnb)
        hs = slice(h * GDN_HEAD_DIM, (h + 1) * GDN_HEAD_DIM)
        y_ref[:, hs] = _gdn_out(o, gate_ref[:, hs], nw_ref[...])
        kdt = kd.T
        egl = jnp.exp(glcols[h])
        for i in range(nslab):
            in_slab = (lane // SLAB) == i
            upd = _dot(jnp.where(in_slab, kdt, 0.0).astype(BF16), vnb)
            sout_ref[i, h] = s0_ref[i, h] * egl[i * SLAB:i * SLAB + 1, :] + upd


def _gdn_slab(proj, bdt, conv_buf, s0, gw, *, hp=GDN_SLAB_HEADS):
    rows = proj.shape[0]
    nb = rows // SLAB
    nslab = GDN_TILE // SLAB
    hw = hp * GDN_HEAD_DIM
    hb = GDN_WIDTH // hw
    const = lambda shape: pl.BlockSpec(shape, lambda *_: (0,) * len(shape))
    col = lambda g: pl.BlockSpec((GDN_TILE, hw), lambda i, h, g=g: (i, g * hb + h))
    per = LANES // nslab
    buf = lambda g: pl.BlockSpec((GDN_CONV - 1, LANES, hw), lambda i, h, g=g: (0, i // per, g * hb + h))
    cwb = lambda g: pl.BlockSpec((GDN_CONV, hw), lambda i, h, g=g: (0, g * hb + h))
    st = pl.BlockSpec((nslab, hp, GDN_HEAD_DIM, GDN_HEAD_DIM), lambda i, h: (i, h, 0, 0))
    obuf = pl.BlockSpec((GDN_CONV - 1, nslab, hw), lambda i, h: (0, i, h))
    return pl.pallas_call(
        _gdn_slab_kernel,
        grid=(rows // GDN_TILE, GDN_HEADS // hp),
        in_specs=[col(1), col(2), col(3), col(4),
                  pl.BlockSpec((2 * GDN_HEADS, GDN_TILE), lambda i, h: (0, i)),
                  buf(0), buf(1), buf(2), st, cwb(0), cwb(1), cwb(2),
                  const((GDN_HEADS, 1)), const((GDN_HEADS, 1)), const((1, GDN_HEAD_DIM))],
        out_specs=[pl.BlockSpec((GDN_TILE, hw), lambda i, h: (i, h)), st, obuf, obuf, obuf],
        out_shape=[jax.ShapeDtypeStruct((rows, GDN_WIDTH), F32),
                   jax.ShapeDtypeStruct(s0.shape, F32)]
                  + [jax.ShapeDtypeStruct((GDN_CONV - 1, nb, GDN_WIDTH), F32)] * 3,
        scratch_shapes=[pltpu.VMEM((GDN_TILE + SUBLANES, 3 * hw), F32)],
        compiler_params=_params("parallel", "arbitrary"),
        name="gdn_slab",
    )(proj, proj, proj, proj, bdt, conv_buf, conv_buf, conv_buf, s0,
      gw["conv_w"], gw["conv_w"], gw["conv_w"], gw["alog_col"], gw["dtb_col"], gw["norm_w"])


def _block_diag(w):
    per = S5_GROUPS // S5_BLOCKS
    g, a, b = w.shape
    w = w.reshape(S5_BLOCKS, per, a, b)
    eye = jnp.eye(per, dtype=w.dtype)
    return jnp.einsum("jgab,gk->jgakb", w, eye).reshape(S5_BLOCKS, per * a, per * b)


def _layer_weights(l, ln1_g, ln1_b, ffn1_w_in, ffn1_w_out, w_mix_in, s5_lambda_re, s5_lambda_im, s5_log_dt,
                   s5_b_re, s5_b_im, s5_c_re, s5_c_im, s5_d, s5_glu_w, s5_glu_b, gdn_conv_w, gdn_a_log,
                   gdn_dt_bias, gdn_norm_w, w_mix_out, ln2_g, ln2_b, ffn2_w_in, ffn2_w_out, ln3_g, ln3_b):
    row = lambda v: v[l].reshape(1, -1).astype(F32)
    w = {
        "ln1": (row(ln1_g), row(ln1_b)), "ln2": (row(ln2_g), row(ln2_b)), "ln3": (row(ln3_g), row(ln3_b)),
        "ffn1": (ffn1_w_in[l], ffn1_w_out[l]),
        "ffn2": (ffn2_w_in[l], ffn2_w_out[l]),
        "mix_in": (w_mix_in, l),
        "mix_out": w_mix_out[l].astype(BF16),
    }
    coef, bb_re, bb_im = _s5_disc(
        s5_lambda_re[l].reshape(1, S5_HID).astype(F32), s5_lambda_im[l].reshape(1, S5_HID).astype(F32),
        jnp.repeat(s5_log_dt[l], S5_STATE).reshape(1, S5_HID).astype(F32),
        _block_diag(jnp.swapaxes(s5_b_re[l], 1, 2).astype(F32)),
        _block_diag(jnp.swapaxes(s5_b_im[l], 1, 2).astype(F32)))
    w["s5"] = {
        "coef": coef, "b_re": bb_re, "b_im": bb_im,
        "c_re": _block_diag(jnp.swapaxes(s5_c_re[l], 1, 2)).astype(BF16),
        "c_im": _block_diag(jnp.swapaxes(s5_c_im[l], 1, 2)).astype(BF16),
        "d": row(s5_d), "glu_w": s5_glu_w[l].astype(BF16), "glu_b": row(s5_glu_b),
    }
    w["gdn"] = {
        "conv_w": gdn_conv_w[l].astype(F32),
        "alog_col": gdn_a_log[l].reshape(GDN_HEADS, 1).astype(F32),
        "dtb_col": gdn_dt_bias[l].reshape(GDN_HEADS, 1).astype(F32),
        "norm_w": row(gdn_norm_w),
    }
    return w


def _prompt_layer(x, w, nseq):
    x = _ffn_ln(x, *w["ffn1"], *w["ln1"])
    proj, bdt = _mixin(x, *w["mix_in"])
    z_s5 = jnp.zeros((nseq, 1, S5_HID), F32)
    y_s5, n_re, n_im = _s5_mixer(proj, z_s5, z_s5, w["s5"], chain=True, nseq=nseq)
    z_gdn = jnp.zeros((nseq, GDN_HEADS, GDN_HEAD_DIM, GDN_HEAD_DIM), F32)
    z_buf = jnp.zeros((nseq, GDN_CONV - 1, 3 * GDN_WIDTH), F32)
    y_gdn, n_s, n_buf = _gdn_chain(proj, bdt, z_buf, z_gdn, w["gdn"], nseq=nseq)
    x = _mixout_ln(y_s5, y_gdn, x, w["mix_out"], *w["ln2"])
    x = _ffn_ln(x, *w["ffn2"], *w["ln3"])
    shape = (nseq, S5_GROUPS, S5_STATE)
    return x, n_re.reshape(shape), n_im.reshape(shape), n_s, n_buf


def _sample_layer(x, s5_re, s5_im, gdn_s, conv_buf, w, nb, t):
    x = _ffn_ln(x, *w["ffn1"], *w["ln1"])
    xs = jnp.pad(x.reshape(nb, t, D_MODEL), ((0, 0), (SLAB - t, 0), (0, 0))).reshape(nb * SLAB, D_MODEL)
    proj, bdt = _mixin(xs, *w["mix_in"])
    y_s5, n_re, n_im = _s5_mixer(proj, s5_re.reshape(nb, S5_HID).astype(F32),
                                 s5_im.reshape(nb, S5_HID).astype(F32), w["s5"], chain=False, nseq=nb)
    y_gdn, n_s, bq, bk, bv = _gdn_slab(proj, bdt, jnp.swapaxes(conv_buf.astype(F32), 0, 1),
                                       gdn_s.astype(F32), w["gdn"])
    xs = _mixout_ln(y_s5, y_gdn, xs, w["mix_out"], *w["ln2"])
    x = xs.reshape(nb, SLAB, D_MODEL)[:, SLAB - t:].reshape(nb * t, D_MODEL)
    x = _ffn_ln(x, *w["ffn2"], *w["ln3"])
    shape = (nb, S5_GROUPS, S5_STATE)
    n_buf = jnp.swapaxes(jnp.concatenate([bq, bk, bv], axis=-1), 0, 1)
    return x, n_re.reshape(shape), n_im.reshape(shape), n_s, n_buf


def kernel(x_prompt, x_sample, state_s5_re, state_s5_im, state_gdn, state_conv, ln1_g, ln1_b, ffn1_w_in, ffn1_w_out, w_mix_in, s5_lambda_re, s5_lambda_im, s5_log_dt, s5_b_re, s5_b_im, s5_c_re, s5_c_im, s5_d, s5_glu_w, s5_glu_b, gdn_conv_w, gdn_a_log, gdn_dt_bias, gdn_norm_w, w_mix_out, ln2_g, ln2_b, ffn2_w_in, ffn2_w_out, ln3_g, ln3_b):
    bp, tp, _ = x_prompt.shape
    bs, ts, _ = x_sample.shape
    assert ts == SLAB_REAL and tp % S5_TILE == 0 and (bs * SLAB) % S5_TILE == 0
    depth = ln1_g.shape[0]
    yp = x_prompt.astype(F32).reshape(bp * tp, D_MODEL)
    ys = x_sample.astype(F32).reshape(bs * ts, D_MODEL)
    outs = [[] for _ in range(8)]
    for l in range(depth):
        w = _layer_weights(l, ln1_g, ln1_b, ffn1_w_in, ffn1_w_out, w_mix_in, s5_lambda_re, s5_lambda_im,
                           s5_log_dt, s5_b_re, s5_b_im, s5_c_re, s5_c_im, s5_d, s5_glu_w, s5_glu_b,
                           gdn_conv_w, gdn_a_log, gdn_dt_bias, gdn_norm_w, w_mix_out, ln2_g, ln2_b,
                           ffn2_w_in, ffn2_w_out, ln3_g, ln3_b)
        yp, *p_state = _prompt_layer(yp, w, bp)
        ys, *s_state = _sample_layer(ys, state_s5_re[l], state_s5_im[l], state_gdn[l], state_conv[l], w, bs, ts)
        for acc, val in zip(outs, p_state + s_state):
            acc.append(val)
    return (yp.reshape(x_prompt.shape).astype(x_prompt.dtype), ys.reshape(x_sample.shape).astype(x_sample.dtype),
            *(o[0][None] if depth == 1 else jnp.stack(o) for o in outs))
```

```python
import functools
import math

import jax
import jax.numpy as jnp
from jax import lax
from jax.experimental import pallas as pl
from jax.experimental.pallas import tpu as pltpu

F32 = jnp.float32
BF16 = jnp.bfloat16

D_MODEL = 2048
S5_WIDTH = 1024
S5_GROUP = 16
S5_GROUPS = 64
S5_STATE = 64
S5_HID = S5_GROUPS * S5_STATE
GDN_WIDTH = 1024
GDN_HEAD_DIM = 128
GDN_HEADS = 8
GDN_CONV = 4
D_FF = 5632
MIX_MAIN = 5120
DEEP_ALPHA = 2.0 ** 0.25
LN_EPS = 1e-5
NORM_EPS = 1e-6

SUBLANES = 8
LANES = 128
SLAB = 8
SLAB_REAL = 4
GDN_TILE = 128
GDN_SLAB_HEADS = 8
GDN_CHAIN_GROUP = 8
FFN_TM = 1024
FFN_TF = 256
FFN_CHUNK = 512
S5_TILE = 256
S5_SEG = S5_TILE // SUBLANES
S5_BLOCKS = 8
VMEM_LIMIT = 56 * 1024 * 1024

NT_DIMS = (((1,), (1,)), ((), ()))


def _dot(a, b, **kw):
    return jnp.dot(a, b, preferred_element_type=F32, **kw)


def _silu(x):
    return x * jax.nn.sigmoid(x)


def _layer_norm(y, g, b):
    mu = jnp.mean(y, axis=-1, keepdims=True)
    d = y - mu
    var = jnp.mean(d * d, axis=-1, keepdims=True)
    return d * lax.rsqrt(var + LN_EPS) * g + b


def _slab_row_selector(rows, nseq, slab_row, first, *, transpose):
    shape = (rows, nseq) if transpose else (nseq, rows)
    r = lax.broadcasted_iota(jnp.int32, shape, 0 if transpose else 1)
    b = lax.broadcasted_iota(jnp.int32, shape, 1 if transpose else 0)
    return _as_bf16(r == SLAB * (b - first) + slab_row)


def _as_bf16(mask):
    return mask.astype(F32).astype(BF16)


def _split3(x):
    pieces = []
    for _ in range(3):
        pieces.append(x.astype(BF16))
        x = x - pieces[-1].astype(F32)
    return pieces


def _dot_split(a, b, dims=None):
    f32_is_lhs = a.dtype == F32
    acc = None
    for piece in _split3(a if f32_is_lhs else b):
        lhs, rhs = (piece, b) if f32_is_lhs else (a, piece)
        d = _dot(lhs, rhs) if dims is None else lax.dot_general(lhs, rhs, dims, preferred_element_type=F32)
        acc = d if acc is None else acc + d
    return acc


def _params(*sem):
    return pltpu.CompilerParams(dimension_semantics=sem, vmem_limit_bytes=VMEM_LIMIT)


def _ffn_kernel(x_ref, wg_ref, wu_ref, wo_ref, g_ref, b_ref, o_ref, xb_ref):
    j = pl.program_id(1)

    @pl.when(j == 0)
    def _():
        o_ref[...] = jnp.zeros_like(o_ref)
        xb_ref[...] = x_ref[...].astype(BF16)

    xb = xb_ref[...]
    gate = _dot(xb, wg_ref[...].astype(BF16))
    up = _dot(xb, wu_ref[...].astype(BF16))
    h = (_silu(gate) * up).astype(BF16)
    for c in range(0, D_MODEL, FFN_CHUNK):
        o_ref[:, c:c + FFN_CHUNK] += _dot(h, wo_ref[:, c:c + FFN_CHUNK].astype(BF16))

    @pl.when(j == pl.num_programs(1) - 1)
    def _():
        for r in range(0, o_ref.shape[0], FFN_CHUNK // 2):
            rows = slice(r, r + FFN_CHUNK // 2)
            y = DEEP_ALPHA * x_ref[rows, :] + 0.5 * o_ref[rows, :]
            o_ref[rows, :] = _layer_norm(y, g_ref[...], b_ref[...])


def _ffn_ln(x, w_in, w_out, g, b):
    n = x.shape[0]
    tm = math.gcd(n, FFN_TM)
    tf = FFN_TF * FFN_TM // tm
    nff = D_FF // tf
    return pl.pallas_call(
        _ffn_kernel,
        grid=(n // tm, nff),
        in_specs=[
            pl.BlockSpec((tm, D_MODEL), lambda i, j: (i, 0)),
            pl.BlockSpec((D_MODEL, tf), lambda i, j: (0, j)),
            pl.BlockSpec((D_MODEL, tf), lambda i, j: (0, j + nff)),
            pl.BlockSpec((tf, D_MODEL), lambda i, j: (j, 0)),
            pl.BlockSpec((1, D_MODEL), lambda i, j: (0, 0)),
            pl.BlockSpec((1, D_MODEL), lambda i, j: (0, 0)),
        ],
        out_specs=pl.BlockSpec((tm, D_MODEL), lambda i, j: (i, 0)),
        out_shape=jax.ShapeDtypeStruct((n, D_MODEL), F32),
        scratch_shapes=[pltpu.VMEM((tm, D_MODEL), BF16)],
        compiler_params=_params("parallel", "arbitrary"),
        name="ffn_ln",
    )(x, w_in, w_in, w_out, g, b)


def _mixin_kernel(x_ref, w_ref, wt_ref, o_ref, ot_ref, xb_ref):
    @pl.when(pl.program_id(1) == 0)
    def _():
        xb_ref[...] = x_ref[...].astype(BF16)
        ncols = ot_ref.shape[0]
        lane = lax.broadcasted_iota(jnp.int32, (1, LANES), 1)
        wt = jnp.where(lane < ncols, wt_ref[...], 0.0).astype(BF16)
        ot_ref[...] = _dot(xb_ref[...], wt).T[:ncols, :]

    o_ref[...] = _dot(xb_ref[...], w_ref[...].astype(BF16))


def _mixin(x, w, layer, *, tm=1024, tn=1024):
    n = x.shape[0]
    return pl.pallas_call(
        _mixin_kernel,
        grid=(n // tm, MIX_MAIN // tn),
        in_specs=[pl.BlockSpec((tm, D_MODEL), lambda i, j: (i, 0)),
                  pl.BlockSpec((None, D_MODEL, tn), lambda i, j: (layer, 0, j)),
                  pl.BlockSpec((None, D_MODEL, LANES), lambda i, j: (layer, 0, MIX_MAIN // LANES))],
        out_specs=[pl.BlockSpec((tm, tn), lambda i, j: (i, j)),
                   pl.BlockSpec((2 * GDN_HEADS, tm), lambda i, j: (0, i))],
        out_shape=[jax.ShapeDtypeStruct((n, MIX_MAIN), F32),
                   jax.ShapeDtypeStruct((2 * GDN_HEADS, n), F32)],
        scratch_shapes=[pltpu.VMEM((tm, D_MODEL), BF16)],
        compiler_params=_params("parallel", "arbitrary"),
        name="mix_in",
    )(x, w, w)


def _mixout_kernel(ya_ref, yb_ref, x_ref, w_ref, g_ref, b_ref, o_ref):
    mix = (_dot(ya_ref[...].astype(BF16), w_ref[0:S5_WIDTH, :])
           + _dot(yb_ref[...].astype(BF16), w_ref[S5_WIDTH:, :]))
    o_ref[...] = _layer_norm(DEEP_ALPHA * x_ref[...] + mix, g_ref[...], b_ref[...])


def _mixout_ln(ya, yb, x, w, g, b, *, tm=512):
    n = x.shape[0]
    return pl.pallas_call(
        _mixout_kernel,
        grid=(n // tm,),
        in_specs=[pl.BlockSpec((tm, S5_WIDTH), lambda i: (i, 0)),
                  pl.BlockSpec((tm, GDN_WIDTH), lambda i: (i, 0)),
                  pl.BlockSpec((tm, D_MODEL), lambda i: (i, 0)),
                  pl.BlockSpec((D_MODEL, D_MODEL), lambda i: (0, 0)),
                  pl.BlockSpec((1, D_MODEL), lambda i: (0, 0)),
                  pl.BlockSpec((1, D_MODEL), lambda i: (0, 0))],
        out_specs=pl.BlockSpec((tm, D_MODEL), lambda i: (i, 0)),
        out_shape=jax.ShapeDtypeStruct((n, D_MODEL), F32),
        compiler_params=_params("parallel"),
        name="mix_out_ln",
    )(ya, yb, x, w, g, b)


def _s5_disc_kernel(lre_ref, lim_ref, ldt_ref, bre_ref, bim_ref, coef_ref, bbre_ref, bbim_ref):
    lr, li = lre_ref[...], lim_ref[...]
    dt = jnp.exp(ldt_ref[...])
    mag = jnp.exp(lr * dt)
    ar = mag * jnp.cos(li * dt)
    ai = mag * jnp.sin(li * dt)
    nr, ni = ar - 1.0, ai
    den = lr * lr + li * li
    c_re = (nr * lr + ni * li) / den
    c_im = (ni * lr - nr * li) / den
    bw = S5_HID // S5_BLOCKS
    for j in range(S5_BLOCKS):
        cr, ci = c_re[:, j * bw:(j + 1) * bw], c_im[:, j * bw:(j + 1) * bw]
        bbre_ref[j] = (cr * bre_ref[j] - ci * bim_ref[j]).astype(BF16)
        bbim_ref[j] = (cr * bim_ref[j] + ci * bre_ref[j]).astype(BF16)

    def cmul(x, y):
        return x[0] * y[0] - x[1] * y[1], x[0] * y[1] + x[1] * y[0]

    width = lr.shape[-1]
    p = (ar, ai)
    for _ in range(S5_SEG - 1):
        p = cmul(p, (ar, ai))
    coef_ref[...] = jnp.zeros_like(coef_ref)
    for part, v in enumerate((ar, ai)):
        coef_ref[part] = jnp.broadcast_to(v, (SUBLANES, width))
    for k, s in enumerate((1, 2, 4)):
        for part in range(2):
            coef_ref[2 + 2 * k + part, s:SUBLANES, :] = jnp.broadcast_to(p[part], (SUBLANES - s, width))
        p = cmul(p, p)


def _s5_disc(lre, lim, ldt, b_re, b_im):
    return pl.pallas_call(
        _s5_disc_kernel,
        out_shape=[jax.ShapeDtypeStruct((8, SUBLANES, S5_HID), F32),
                   jax.ShapeDtypeStruct(b_re.shape, BF16), jax.ShapeDtypeStruct(b_im.shape, BF16)],
        name="s5_disc",
    )(lre, lim, ldt, b_re, b_im)


def _gelu_tanh(y):
    return 0.5 * y * (1.0 + jnp.tanh(math.sqrt(2.0 / math.pi) * (y + 0.044715 * (y * y * y))))


def _scan_layout(tt, group):
    p = jnp.arange(tt)
    rem = p % group
    src = (p - rem) + (rem % SUBLANES) * (group // SUBLANES) + rem // SUBLANES
    return (src[:, None] == jnp.arange(tt)[None, :]).astype(BF16)


def _cmul_add(ar, ai, xr, xi, br, bi):
    return ar * xr - ai * xi + br, ar * xi + ai * xr + bi


def _s5_kernel(u_ref, h0re_ref, h0im_ref, perm_ref, back_ref, coef_ref, bre_ref, bim_ref, cwre_ref, cwim_ref,
               d_ref, gw_ref, gb_ref, y_ref, sre_ref, sim_ref, hre, him, ysc, car_re, car_im,
               *, chain, tt):
    group = tt if chain else SLAB * SUBLANES
    seg = group // SUBLANES
    u = u_ref[...]
    ub = _dot(perm_ref[...], u.astype(BF16)).astype(BF16)
    bw = S5_HID // S5_BLOCKS
    gw = S5_WIDTH // S5_BLOCKS
    if chain:
        @pl.when(pl.program_id(1) == 0)
        def _():
            car_re[...] = jnp.broadcast_to(h0re_ref[...], car_re.shape)
            car_im[...] = jnp.broadcast_to(h0im_ref[...], car_im.shape)

    sub = lax.broadcasted_iota(jnp.int32, (SUBLANES, 1), 0)
    for c in range(S5_BLOCKS):
        sl = slice(c * bw, (c + 1) * bw)
        uc = ub[:, c * gw:(c + 1) * gw]
        hre[:, sl] = _dot(uc, bre_ref[c])
        him[:, sl] = _dot(uc, bim_ref[c])
        ar, ai = coef_ref[0, :, sl], coef_ref[1, :, sl]
        if not chain:
            for g in range(tt // group):
                srows = slice(g * SUBLANES, (g + 1) * SUBLANES)
                xr, xi = h0re_ref[srows, sl], h0im_ref[srows, sl]
                for j in range(SLAB - SLAB_REAL, SLAB):
                    rows = slice(g * group + j * SUBLANES, g * group + (j + 1) * SUBLANES)
                    xr, xi = _cmul_add(ar, ai, xr, xi, hre[rows, sl], him[rows, sl])
                    hre[rows, sl] = xr
                    him[rows, sl] = xi
                sre_ref[srows, sl] = xr
                sim_ref[srows, sl] = xi
        else:
            xr = xi = jnp.zeros((SUBLANES, bw), F32)
            for j in range(seg):
                rows = slice(j * SUBLANES, (j + 1) * SUBLANES)
                xr, xi = _cmul_add(ar, ai, xr, xi, hre[rows, sl], him[rows, sl])
                hre[rows, sl] = xr
                him[rows, sl] = xi
            kr = jnp.where(sub == 0, car_re[:, sl], pltpu.roll(xr, 1, 0))
            ki = jnp.where(sub == 0, car_im[:, sl], pltpu.roll(xi, 1, 0))
            for k, s in enumerate((1, 2, 4)):
                pr, pi = coef_ref[2 + 2 * k, :, sl], coef_ref[3 + 2 * k, :, sl]
                kr, ki = _cmul_add(pr, pi, pltpu.roll(kr, s, 0), pltpu.roll(ki, s, 0), kr, ki)
            outr, outi = _cmul_add(coef_ref[2, :, sl], coef_ref[3, :, sl], kr, ki, xr, xi)
            car_re[:, sl] = jnp.broadcast_to(outr[SUBLANES - 1:SUBLANES, :], outr.shape)
            car_im[:, sl] = jnp.broadcast_to(outi[SUBLANES - 1:SUBLANES, :], outi.shape)
            for j in range(seg):
                rows = slice(j * SUBLANES, (j + 1) * SUBLANES)
                kr, ki = ar * kr - ai * ki, ar * ki + ai * kr
                hre[rows, sl] += kr
                him[rows, sl] += ki
        ysc[:, c * gw:(c + 1) * gw] = (_dot(hre[:, sl].astype(BF16), cwre_ref[c])
                                       - _dot(him[:, sl].astype(BF16), cwim_ref[c]))

    if chain:
        @pl.when(pl.program_id(1) == pl.num_programs(1) - 1)
        def _():
            sre_ref[...] = car_re[0:1, :]
            sim_ref[...] = car_im[0:1, :]

    y = _dot_split(back_ref[...], ysc[...])
    z = _gelu_tanh(y + d_ref[...] * u)
    gl = _dot(z.astype(BF16), gw_ref[...]) + gb_ref[...]
    y_ref[...] = z * jax.nn.sigmoid(gl)


def _s5_mixer(proj, h0re, h0im, sw, *, chain, nseq, tt=S5_TILE):
    rows = proj.shape[0]
    const = lambda shape: pl.BlockSpec(shape, lambda *_: (0,) * len(shape))
    if chain:
        assert tt == S5_SEG * SUBLANES
        nt = rows // nseq // tt
        grid = (nseq, nt)
        u_spec = pl.BlockSpec((tt, S5_WIDTH), lambda b, t: (b * nt + t, 0))
        st_spec = pl.BlockSpec((None, 1, S5_HID), lambda b, t: (b, 0, 0))
        st_shape = jax.ShapeDtypeStruct((nseq, 1, S5_HID), F32)
        sem = ("parallel", "arbitrary")
    else:
        grid = (rows // tt,)
        u_spec = pl.BlockSpec((tt, S5_WIDTH), lambda i: (i, 0))
        st_spec = pl.BlockSpec((tt // SLAB, S5_HID), lambda i: (i, 0))
        st_shape = jax.ShapeDtypeStruct((rows // SLAB, S5_HID), F32)
        sem = ("parallel",)
    y_spec = u_spec
    bw = S5_HID // S5_BLOCKS
    gw = S5_WIDTH // S5_BLOCKS
    perm = _scan_layout(tt, tt if chain else SLAB * SUBLANES)
    return pl.pallas_call(
        functools.partial(_s5_kernel, chain=chain, tt=tt),
        grid=grid,
        in_specs=[u_spec, st_spec, st_spec,
                  const((tt, tt)), const((tt, tt)), const((8, SUBLANES, S5_HID)),
                  const((S5_BLOCKS, gw, bw)), const((S5_BLOCKS, gw, bw)),
                  const((S5_BLOCKS, bw, gw)), const((S5_BLOCKS, bw, gw)),
                  const((1, S5_WIDTH)), const((S5_WIDTH, S5_WIDTH)), const((1, S5_WIDTH))],
        out_specs=[y_spec, st_spec, st_spec],
        out_shape=[jax.ShapeDtypeStruct((rows, S5_WIDTH), F32), st_shape, st_shape],
        scratch_shapes=[pltpu.VMEM((tt, S5_HID), F32), pltpu.VMEM((tt, S5_HID), F32),
                        pltpu.VMEM((tt, S5_WIDTH), F32),
                        pltpu.VMEM((SUBLANES, S5_HID), F32), pltpu.VMEM((SUBLANES, S5_HID), F32)],
        compiler_params=_params(*sem),
        name="s5_chain" if chain else "s5_slab",
    )(proj, h0re, h0im, perm, perm.T, sw["coef"], sw["b_re"], sw["b_im"], sw["c_re"], sw["c_im"],
      sw["d"], sw["glu_w"], sw["glu_b"])


def _softplus(x):
    return jnp.maximum(x, 0.0) + jnp.log1p(jnp.exp(-jnp.abs(x)))


def _segment_masks(seg):
    ri = lax.broadcasted_iota(jnp.int32, (GDN_TILE, GDN_TILE), 0)
    ci = lax.broadcasted_iota(jnp.int32, (GDN_TILE, GDN_TILE), 1)
    same = (ri // seg) == (ci // seg)
    causal = (ri >= ci) & same
    strict = (ri > ci) & same
    return same, causal, strict


def _gate_matrices(seg):
    i = jnp.arange(GDN_TILE)
    same = (i[:, None] // seg) == (i[None, :] // seg)
    stack = jnp.concatenate([i[:, None] == i[None, :], (i[:, None] >= i[None, :]) & same, same], axis=0)
    return stack.astype(BF16), ((i[:, None] <= i[None, :]) & same).astype(BF16)


def _gdn_gates(bdt, alog_col, dtb_col, stack, upper, seg, slab):
    same, causal, strict = _segment_masks(seg)
    bt = jax.nn.sigmoid(bdt[:GDN_HEADS, :])
    gt = -jnp.exp(alog_col) * _softplus(bdt[GDN_HEADS:, :] + dtb_col)
    if slab:
        creal = (lax.broadcasted_iota(jnp.int32, (1, GDN_TILE), 1) % SLAB) >= SLAB - SLAB_REAL
        bt = jnp.where(creal, bt, 0.0)
        gt = jnp.where(creal, gt, 0.0)
    gates = jnp.concatenate([bt, gt, jnp.zeros((GDN_TILE - 2 * GDN_HEADS, GDN_TILE), F32)], axis=0)
    cols = _dot_split(stack, gates, NT_DIMS)
    beta, gc, gl = cols[:GDN_TILE], cols[GDN_TILE:2 * GDN_TILE], cols[2 * GDN_TILE:]
    gct = _dot_split(gt, upper)
    return beta, gc, gl, gct, causal, strict


def _merge_masks(top):
    ri = lax.broadcasted_iota(jnp.int32, (GDN_TILE, GDN_TILE), 0)
    ci = lax.broadcasted_iota(jnp.int32, (GDN_TILE, GDN_TILE), 1)
    masks = []
    s = 1
    while s < top:
        masks.append(((ri // (2 * s)) == (ci // (2 * s))) & ((ri // s) != (ci // s)))
        s *= 2
    return masks


def _unit_lower_inverse(ms, masks):
    es = [-jnp.where(masks[0], m, 0.0) for m in ms]
    for mask in masks[1:]:
        cs = [jnp.where(mask, m, 0.0) for m in ms]
        ebs = [e.astype(BF16) for e in es]
        xs = [c + _dot(eb, c.astype(BF16)) for c, eb in zip(cs, ebs)]
        es = [e - (x + _dot(x.astype(BF16), eb)) for e, x, eb in zip(es, xs, ebs)]
    return es


def _gdn_intra(qs, ks, vs, betas, gcols, grows, glcols, causal, strict, masks):
    n = range(len(qs))
    qn = [q * lax.rsqrt(jnp.sum(q * q, axis=-1, keepdims=True) + NORM_EPS) * (GDN_HEAD_DIM ** -0.5) for q in qs]
    kn = [k * lax.rsqrt(jnp.sum(k * k, axis=-1, keepdims=True) + NORM_EPS) for k in ks]
    decay = [jnp.exp(jnp.where(causal, gcols[i] - grows[i], -jnp.inf)) for i in n]
    kb = [kn[i] * betas[i] for i in n]
    knb = [k.astype(BF16) for k in kn]
    kk = [lax.dot_general(kb[i].astype(BF16), knb[i], NT_DIMS, preferred_element_type=F32) for i in n]
    ys = _unit_lower_inverse([kk[i] * jnp.where(strict, decay[i], 0.0) for i in n], masks)
    eg = [jnp.exp(g) for g in gcols]
    rhs = [jnp.concatenate([vs[i] * betas[i], kb[i] * eg[i]], axis=1) for i in n]
    uw = [rhs[i] + _dot(ys[i].astype(BF16), rhs[i].astype(BF16)) for i in n]
    attn = [lax.dot_general(qn[i].astype(BF16), knb[i], NT_DIMS, preferred_element_type=F32) * decay[i] for i in n]
    qg = [qn[i] * eg[i] for i in n]
    kd = [kn[i] * jnp.exp(glcols[i] - gcols[i]) for i in n]
    return [(uw[i][:, :GDN_HEAD_DIM], uw[i][:, GDN_HEAD_DIM:], attn[i], qg[i], kd[i]) for i in n]


def _gdn_out(o, gate, norm_w):
    o = o * lax.rsqrt(jnp.mean(o * o, axis=-1, keepdims=True) + NORM_EPS) * norm_w
    return o * _silu(gate)


def _conv4(win_ref, cw, width_slice):
    x = win_ref[:, width_slice]
    x1 = pltpu.roll(x, 1, 0)
    z = cw[1:2, :] * x + cw[0:1, :] * x1
    out = cw[3:4, :] * x + cw[2:3, :] * x1 + pltpu.roll(z, 2, 0)
    return out[SUBLANES:, :]


def _gdn_chain_kernel(q_ref, k_ref, v_ref, gate_ref, bdt_ref, buf_ref, s0_ref, cw_ref,
                      alc_ref, dbc_ref, nw_ref, gstack_ref, gupper_ref,
                      y_ref, sout_ref, bufout_ref, win, s_sc):
    t = pl.program_id(1)
    w3 = 3 * GDN_WIDTH

    @pl.when(t == 0)
    def _():
        win[0:SUBLANES, :] = jnp.zeros((SUBLANES, w3), F32)
        win[SUBLANES - 3:SUBLANES, :] = buf_ref[...]
        s_sc[...] = s0_ref[...]

    win[SUBLANES:, 0:GDN_WIDTH] = q_ref[...]
    win[SUBLANES:, GDN_WIDTH:2 * GDN_WIDTH] = k_ref[...]
    win[SUBLANES:, 2 * GDN_WIDTH:] = v_ref[...]

    beta, gc, gl, gct, causal, strict = _gdn_gates(bdt_ref[...], alc_ref[...], dbc_ref[...], gstack_ref[...],
                                                   gupper_ref[...], GDN_TILE, False)
    masks = _merge_masks(GDN_TILE)
    for h0 in range(0, GDN_HEADS, GDN_CHAIN_GROUP):
        heads = range(h0, h0 + GDN_CHAIN_GROUP)
        idx = range(GDN_CHAIN_GROUP)
        qkv = [[], [], []]
        for part in range(3):
            for h in heads:
                cs = slice(part * GDN_WIDTH + h * GDN_HEAD_DIM, part * GDN_WIDTH + (h + 1) * GDN_HEAD_DIM)
                qkv[part].append(_silu(_conv4(win, cw_ref[:, cs], cs)))
        gcols = [gc[:, GDN_HEADS + h:GDN_HEADS + h + 1] for h in heads]
        glcols = [gl[:, GDN_HEADS + h:GDN_HEADS + h + 1] for h in heads]
        intra = _gdn_intra(qkv[0], qkv[1], qkv[2], [beta[:, h:h + 1] for h in heads], gcols,
                           [gct[h:h + 1, :] for h in heads], glcols, causal, strict, masks)
        s_old = [s_sc[h] for h in heads]
        a = [_dot(jnp.concatenate([intra[i][1], intra[i][3]], axis=0).astype(BF16), s_old[i].astype(BF16))
             for i in idx]
        vnb = [(intra[i][0] - a[i][:GDN_TILE]).astype(BF16) for i in idx]
        o = [a[i][GDN_TILE:] + _dot(intra[i][2].astype(BF16), vnb[i]) for i in idx]
        for i, h in zip(idx, heads):
            s_sc[h] = s_old[i] * jnp.exp(glcols[i][0:1, :]) + _dot(intra[i][4].T.astype(BF16), vnb[i])
        for i, h in zip(idx, heads):
            hs = slice(h * GDN_HEAD_DIM, (h + 1) * GDN_HEAD_DIM)
            y_ref[:, hs] = _gdn_out(o[i], gate_ref[:, hs], nw_ref[...])

    win[0:SUBLANES, :] = win[GDN_TILE:GDN_TILE + SUBLANES, :]

    @pl.when(t == pl.num_programs(1) - 1)
    def _():
        sout_ref[...] = s_sc[...]
        bufout_ref[...] = win[GDN_TILE + SUBLANES - 3:GDN_TILE + SUBLANES, :]


def _gdn_chain(proj, bdt, conv_buf, s0, gw, *, nseq):
    rows = proj.shape[0]
    nt = rows // nseq // GDN_TILE
    w3 = 3 * GDN_WIDTH
    const = lambda shape: pl.BlockSpec(shape, lambda *_: (0,) * len(shape))
    col = lambda cb: pl.BlockSpec((GDN_TILE, GDN_WIDTH), lambda b, t, cb=cb: (b * nt + t, cb))
    return pl.pallas_call(
        _gdn_chain_kernel,
        grid=(nseq, nt),
        in_specs=[col(1), col(2), col(3), col(4),
                  pl.BlockSpec((2 * GDN_HEADS, GDN_TILE), lambda b, t: (0, b * nt + t)),
                  pl.BlockSpec((None, GDN_CONV - 1, w3), lambda b, t: (b, 0, 0)),
                  pl.BlockSpec((None, GDN_HEADS, GDN_HEAD_DIM, GDN_HEAD_DIM), lambda b, t: (b, 0, 0, 0)),
                  const((GDN_CONV, w3)), const((GDN_HEADS, 1)), const((GDN_HEADS, 1)), const((1, GDN_HEAD_DIM)),
                  const((3 * GDN_TILE, GDN_TILE)), const((GDN_TILE, GDN_TILE))],
        out_specs=[pl.BlockSpec((GDN_TILE, GDN_WIDTH), lambda b, t: (b * nt + t, 0)),
                   pl.BlockSpec((None, GDN_HEADS, GDN_HEAD_DIM, GDN_HEAD_DIM), lambda b, t: (b, 0, 0, 0)),
                   pl.BlockSpec((None, GDN_CONV - 1, w3), lambda b, t: (b, 0, 0))],
        out_shape=[jax.ShapeDtypeStruct((rows, GDN_WIDTH), F32),
                   jax.ShapeDtypeStruct((nseq, GDN_HEADS, GDN_HEAD_DIM, GDN_HEAD_DIM), F32),
                   jax.ShapeDtypeStruct((nseq, GDN_CONV - 1, w3), F32)],
        scratch_shapes=[pltpu.VMEM((GDN_TILE + SUBLANES, w3), F32),
                        pltpu.VMEM((GDN_HEADS, GDN_HEAD_DIM, GDN_HEAD_DIM), F32)],
        compiler_params=_params("parallel", "arbitrary"),
        name="gdn_chain",
    )(proj, proj, proj, proj, bdt, conv_buf, s0, gw["conv_w"], gw["alog_col"], gw["dtb_col"], gw["norm_w"],
      *_gate_matrices(GDN_TILE))


def _gdn_slab_kernel(q_ref, k_ref, v_ref, gate_ref, bdt_ref, bq_ref, bk_ref, bv_ref, s0_ref,
                     cwq_ref, cwk_ref, cwv_ref, alc_ref, dbc_ref, nw_ref, gstack_ref, gupper_ref,
                     y_ref, sout_ref, oq_ref, ok_ref, ov_ref,
                     win):
    hp = s0_ref.shape[1]
    hw = hp * GDN_HEAD_DIM
    heads = range(hp)
    head0 = pl.program_id(1) * hp
    nslab = GDN_TILE // SLAB
    first = (pl.program_id(0) % (bq_ref.shape[1] // nslab)) * nslab
    real = (lax.broadcasted_iota(jnp.int32, (GDN_TILE, 1), 0) % SLAB) >= SLAB - SLAB_REAL
    qkv = []
    taps = range(GDN_CONV - 1)
    place = [_slab_row_selector(GDN_TILE, bq_ref.shape[1], 1 + i, first, transpose=True) for i in taps]
    take = [_slab_row_selector(GDN_TILE, nslab, SLAB - 3 + i, 0, transpose=False) for i in taps]
    for part, (x_ref, b_ref, cw_ref, o_ref) in enumerate(
            ((q_ref, bq_ref, cwq_ref, oq_ref), (k_ref, bk_ref, cwk_ref, ok_ref), (v_ref, bv_ref, cwv_ref, ov_ref))):
        cs = slice(part * hw, (part + 1) * hw)
        x = x_ref[...]
        for i in taps:
            x = x + _dot_split(place[i], b_ref[i])
        win[0:SUBLANES, cs] = jnp.zeros((SUBLANES, hw), F32)
        win[SUBLANES:, cs] = x
        conv = jnp.where(real, _silu(_conv4(win, cw_ref[...], cs)), 0.0)
        qkv.append([conv[:, h * GDN_HEAD_DIM:(h + 1) * GDN_HEAD_DIM] for h in heads])
        pieces = _split3(x)
        for i in taps:
            o_ref[i] = sum(_dot(take[i], p) for p in pieces)

    beta, gc, gl, gct, causal, strict = _gdn_gates(bdt_ref[...], alc_ref[...], dbc_ref[...], gstack_ref[...],
                                                   gupper_ref[...], SLAB, True)
    lane = lax.broadcasted_iota(jnp.int32, (1, LANES), 1)
    sub = lax.broadcasted_iota(jnp.int32, (SUBLANES, 1), 0)
    pick = lambda a, idx: jnp.sum(jnp.where(lane == idx, a, 0.0), axis=1, keepdims=True)
    gcols = [pick(gc, GDN_HEADS + head0 + h) for h in heads]
    glcols = [pick(gl, GDN_HEADS + head0 + h) for h in heads]
    intra = _gdn_intra(qkv[0], qkv[1], qkv[2], [pick(beta, head0 + h) for h in heads], gcols,
                       [jnp.sum(jnp.where(sub == head0 + h, gct, 0.0), axis=0, keepdims=True) for h in heads],
                       glcols, causal, strict, _merge_masks(SLAB_REAL))
    for h in heads:
        u, wk, attn, qg, kd = intra[h]
        res = []
        for i in range(nslab):
            rows = slice(i * SLAB, (i + 1) * SLAB)
            lhs = jnp.concatenate([wk[rows], qg[rows]], axis=0).astype(BF16)
            res.append(_dot(lhs, s0_ref[i, h].astype(BF16)))
        vnb = (u - jnp.concatenate([r[:SLAB] for r in res], axis=0)).astype(BF16)
        o = jnp.concatenate([r[SLAB:] for r in res], axis=0) + _dot(attn.astype(BF16), vnb)
        hs = slice(h * GDN_HEAD_DIM, (h + 1) * GDN_HEAD_DIM)
        y_ref[:, hs] = _gdn_out(o, gate_ref[:, hs], nw_ref[...])
        kdt = kd.T
        egl = jnp.exp(glcols[h])
        for i in range(nslab):
            in_slab = (lane // SLAB) == i
            upd = _dot(jnp.where(in_slab, kdt, 0.0).astype(BF16), vnb)
            sout_ref[i, h] = s0_ref[i, h] * egl[i * SLAB:i * SLAB + 1, :] + upd


def _gdn_slab(proj, bdt, conv_buf, s0, gw, *, hp=GDN_SLAB_HEADS):
    rows = proj.shape[0]
    nb = rows // SLAB
    nslab = GDN_TILE // SLAB
    hw = hp * GDN_HEAD_DIM
    hb = GDN_WIDTH // hw
    const = lambda shape: pl.BlockSpec(shape, lambda *_: (0,) * len(shape))
    col = lambda g: pl.BlockSpec((GDN_TILE, hw), lambda i, h, g=g: (i, g * hb + h))
    per = LANES // nslab
    buf = lambda g: pl.BlockSpec((GDN_CONV - 1, LANES, hw), lambda i, h, g=g: (0, i // per, g * hb + h))
    cwb = lambda g: pl.BlockSpec((GDN_CONV, hw), lambda i, h, g=g: (0, g * hb + h))
    st = pl.BlockSpec((nslab, hp, GDN_HEAD_DIM, GDN_HEAD_DIM), lambda i, h: (i, h, 0, 0))
    obuf = pl.BlockSpec((GDN_CONV - 1, nslab, hw), lambda i, h: (0, i, h))
    return pl.pallas_call(
        _gdn_slab_kernel,
        grid=(rows // GDN_TILE, GDN_HEADS // hp),
        in_specs=[col(1), col(2), col(3), col(4),
                  pl.BlockSpec((2 * GDN_HEADS, GDN_TILE), lambda i, h: (0, i)),
                  buf(0), buf(1), buf(2), st, cwb(0), cwb(1), cwb(2),
                  const((GDN_HEADS, 1)), const((GDN_HEADS, 1)), const((1, GDN_HEAD_DIM)),
                  const((3 * GDN_TILE, GDN_TILE)), const((GDN_TILE, GDN_TILE))],
        out_specs=[pl.BlockSpec((GDN_TILE, hw), lambda i, h: (i, h)), st, obuf, obuf, obuf],
        out_shape=[jax.ShapeDtypeStruct((rows, GDN_WIDTH), F32),
                   jax.ShapeDtypeStruct(s0.shape, F32)]
                  + [jax.ShapeDtypeStruct((GDN_CONV - 1, nb, GDN_WIDTH), F32)] * 3,
        scratch_shapes=[pltpu.VMEM((GDN_TILE + SUBLANES, 3 * hw), F32)],
        compiler_params=_params("parallel", "arbitrary"),
        name="gdn_slab",
    )(proj, proj, proj, proj, bdt, conv_buf, conv_buf, conv_buf, s0,
      gw["conv_w"], gw["conv_w"], gw["conv_w"], gw["alog_col"], gw["dtb_col"], gw["norm_w"],
      *_gate_matrices(SLAB))


def _block_diag(w):
    per = S5_GROUPS // S5_BLOCKS
    g, a, b = w.shape
    w = w.reshape(S5_BLOCKS, per, a, b)
    eye = jnp.eye(per, dtype=w.dtype)
    return jnp.einsum("jgab,gk->jgakb", w, eye).reshape(S5_BLOCKS, per * a, per * b)


def _layer_weights(l, ln1_g, ln1_b, ffn1_w_in, ffn1_w_out, w_mix_in, s5_lambda_re, s5_lambda_im, s5_log_dt,
                   s5_b_re, s5_b_im, s5_c_re, s5_c_im, s5_d, s5_glu_w, s5_glu_b, gdn_conv_w, gdn_a_log,
                   gdn_dt_bias, gdn_norm_w, w_mix_out, ln2_g, ln2_b, ffn2_w_in, ffn2_w_out, ln3_g, ln3_b):
    row = lambda v: v[l].reshape(1, -1).astype(F32)
    w = {
        "ln1": (row(ln1_g), row(ln1_b)), "ln2": (row(ln2_g), row(ln2_b)), "ln3": (row(ln3_g), row(ln3_b)),
        "ffn1": (ffn1_w_in[l], ffn1_w_out[l]),
        "ffn2": (ffn2_w_in[l], ffn2_w_out[l]),
        "mix_in": (w_mix_in, l),
        "mix_out": w_mix_out[l].astype(BF16),
    }
    coef, bb_re, bb_im = _s5_disc(
        s5_lambda_re[l].reshape(1, S5_HID).astype(F32), s5_lambda_im[l].reshape(1, S5_HID).astype(F32),
        jnp.repeat(s5_log_dt[l], S5_STATE).reshape(1, S5_HID).astype(F32),
        _block_diag(jnp.swapaxes(s5_b_re[l], 1, 2).astype(F32)),
        _block_diag(jnp.swapaxes(s5_b_im[l], 1, 2).astype(F32)))
    w["s5"] = {
        "coef": coef, "b_re": bb_re, "b_im": bb_im,
        "c_re": _block_diag(jnp.swapaxes(s5_c_re[l], 1, 2)).astype(BF16),
        "c_im": _block_diag(jnp.swapaxes(s5_c_im[l], 1, 2)).astype(BF16),
        "d": row(s5_d), "glu_w": s5_glu_w[l].astype(BF16), "glu_b": row(s5_glu_b),
    }
    w["gdn"] = {
        "conv_w": gdn_conv_w[l].astype(F32),
        "alog_col": gdn_a_log[l].reshape(GDN_HEADS, 1).astype(F32),
        "dtb_col": gdn_dt_bias[l].reshape(GDN_HEADS, 1).astype(F32),
        "norm_w": row(gdn_norm_w),
    }
    return w


def _prompt_layer(x, w, nseq):
    x = _ffn_ln(x, *w["ffn1"], *w["ln1"])
    proj, bdt = _mixin(x, *w["mix_in"])
    z_s5 = jnp.zeros((nseq, 1, S5_HID), F32)
    y_s5, n_re, n_im = _s5_mixer(proj, z_s5, z_s5, w["s5"], chain=True, nseq=nseq)
    z_gdn = jnp.zeros((nseq, GDN_HEADS, GDN_HEAD_DIM, GDN_HEAD_DIM), F32)
    z_buf = jnp.zeros((nseq, GDN_CONV - 1, 3 * GDN_WIDTH), F32)
    y_gdn, n_s, n_buf = _gdn_chain(proj, bdt, z_buf, z_gdn, w["gdn"], nseq=nseq)
    x = _mixout_ln(y_s5, y_gdn, x, w["mix_out"], *w["ln2"])
    x = _ffn_ln(x, *w["ffn2"], *w["ln3"])
    shape = (nseq, S5_GROUPS, S5_STATE)
    return x, n_re.reshape(shape), n_im.reshape(shape), n_s, n_buf


def _sample_layer(x, s5_re, s5_im, gdn_s, conv_buf, w, nb, t):
    x = _ffn_ln(x, *w["ffn1"], *w["ln1"])
    xs = jnp.pad(x.reshape(nb, t, D_MODEL), ((0, 0), (SLAB - t, 0), (0, 0))).reshape(nb * SLAB, D_MODEL)
    proj, bdt = _mixin(xs, *w["mix_in"])
    y_s5, n_re, n_im = _s5_mixer(proj, s5_re.reshape(nb, S5_HID).astype(F32),
                                 s5_im.reshape(nb, S5_HID).astype(F32), w["s5"], chain=False, nseq=nb)
    y_gdn, n_s, bq, bk, bv = _gdn_slab(proj, bdt, jnp.swapaxes(conv_buf.astype(F32), 0, 1),
                                       gdn_s.astype(F32), w["gdn"])
    xs = _mixout_ln(y_s5, y_gdn, xs, w["mix_out"], *w["ln2"])
    x = xs.reshape(nb, SLAB, D_MODEL)[:, SLAB - t:].reshape(nb * t, D_MODEL)
    x = _ffn_ln(x, *w["ffn2"], *w["ln3"])
    shape = (nb, S5_GROUPS, S5_STATE)
    n_buf = jnp.swapaxes(jnp.concatenate([bq, bk, bv], axis=-1), 0, 1)
    return x, n_re.reshape(shape), n_im.reshape(shape), n_s, n_buf


def kernel(x_prompt, x_sample, state_s5_re, state_s5_im, state_gdn, state_conv, ln1_g, ln1_b, ffn1_w_in, ffn1_w_out, w_mix_in, s5_lambda_re, s5_lambda_im, s5_log_dt, s5_b_re, s5_b_im, s5_c_re, s5_c_im, s5_d, s5_glu_w, s5_glu_b, gdn_conv_w, gdn_a_log, gdn_dt_bias, gdn_norm_w, w_mix_out, ln2_g, ln2_b, ffn2_w_in, ffn2_w_out, ln3_g, ln3_b):
    bp, tp, _ = x_prompt.shape
    bs, ts, _ = x_sample.shape
    assert ts == SLAB_REAL and tp % S5_TILE == 0 and (bs * SLAB) % S5_TILE == 0
    depth = ln1_g.shape[0]
    yp = x_prompt.astype(F32).reshape(bp * tp, D_MODEL)
    ys = x_sample.astype(F32).reshape(bs * ts, D_MODEL)
    outs = [[] for _ in range(8)]
    for l in range(depth):
        w = _layer_weights(l, ln1_g, ln1_b, ffn1_w_in, ffn1_w_out, w_mix_in, s5_lambda_re, s5_lambda_im,
                           s5_log_dt, s5_b_re, s5_b_im, s5_c_re, s5_c_im, s5_d, s5_glu_w, s5_glu_b,
                           gdn_conv_w, gdn_a_log, gdn_dt_bias, gdn_norm_w, w_mix_out, ln2_g, ln2_b,
                           ffn2_w_in, ffn2_w_out, ln3_g, ln3_b)
        yp, *p_state = _prompt_layer(yp, w, bp)
        ys, *s_state = _sample_layer(ys, state_s5_re[l], state_s5_im[l], state_gdn[l], state_conv[l], w, bs, ts)
        for acc, val in zip(outs, p_state + s_state):
            acc.append(val)
    return (yp.reshape(x_prompt.shape).astype(x_prompt.dtype), ys.reshape(x_sample.shape).astype(x_sample.dtype),
            *(o[0][None] if depth == 1 else jnp.stack(o) for o in outs))
```

```python
import functools
import math

import jax
import jax.numpy as jnp
from jax import lax
from jax.experimental import pallas as pl
from jax.experimental.pallas import tpu as pltpu

F32 = jnp.float32
BF16 = jnp.bfloat16

D_MODEL = 2048
S5_WIDTH = 1024
S5_GROUP = 16
S5_GROUPS = 64
S5_STATE = 64
S5_HID = S5_GROUPS * S5_STATE
GDN_WIDTH = 1024
GDN_HEAD_DIM = 128
GDN_HEADS = 8
GDN_CONV = 4
D_FF = 5632
MIX_MAIN = 5120
DEEP_ALPHA = 2.0 ** 0.25
LN_EPS = 1e-5
NORM_EPS = 1e-6

SUBLANES = 8
LANES = 128
SLAB = 8
SLAB_REAL = 4
GDN_TILE = 128
GDN_SLAB_HEADS = 8
GDN_CHAIN_GROUP = 8
FFN_TM = 1024
FFN_TF = 256
FFN_CHUNK = 512
S5_TILE = 256
S5_SEG = S5_TILE // SUBLANES
S5_BLOCKS = 8
VMEM_LIMIT = 56 * 1024 * 1024

NT_DIMS = (((1,), (1,)), ((), ()))


def _dot(a, b, **kw):
    return jnp.dot(a, b, preferred_element_type=F32, **kw)


def _silu(x):
    return x * jax.nn.sigmoid(x)


def _layer_norm(y, g, b):
    mu = jnp.mean(y, axis=-1, keepdims=True)
    d = y - mu
    var = jnp.mean(d * d, axis=-1, keepdims=True)
    return d * lax.rsqrt(var + LN_EPS) * g + b


def _slab_row_selector(rows, nseq, slab_row, first, *, transpose):
    shape = (rows, nseq) if transpose else (nseq, rows)
    r = lax.broadcasted_iota(jnp.int32, shape, 0 if transpose else 1)
    b = lax.broadcasted_iota(jnp.int32, shape, 1 if transpose else 0)
    return _as_bf16(r == SLAB * (b - first) + slab_row)


def _as_bf16(mask):
    return mask.astype(F32).astype(BF16)


def _split3(x):
    pieces = []
    for _ in range(3):
        pieces.append(x.astype(BF16))
        x = x - pieces[-1].astype(F32)
    return pieces


def _dot_split(a, b, dims=None):
    f32_is_lhs = a.dtype == F32
    acc = None
    for piece in _split3(a if f32_is_lhs else b):
        lhs, rhs = (piece, b) if f32_is_lhs else (a, piece)
        d = _dot(lhs, rhs) if dims is None else lax.dot_general(lhs, rhs, dims, preferred_element_type=F32)
        acc = d if acc is None else acc + d
    return acc


def _params(*sem):
    return pltpu.CompilerParams(dimension_semantics=sem, vmem_limit_bytes=VMEM_LIMIT)


def _ffn_kernel(x_ref, wg_ref, wu_ref, wo_ref, g_ref, b_ref, o_ref, xb_ref):
    j = pl.program_id(1)

    @pl.when(j == 0)
    def _():
        o_ref[...] = jnp.zeros_like(o_ref)
        xb_ref[...] = x_ref[...].astype(BF16)

    xb = xb_ref[...]
    gate = _dot(xb, wg_ref[...].astype(BF16))
    up = _dot(xb, wu_ref[...].astype(BF16))
    h = (_silu(gate) * up).astype(BF16)
    for c in range(0, D_MODEL, FFN_CHUNK):
        o_ref[:, c:c + FFN_CHUNK] += _dot(h, wo_ref[:, c:c + FFN_CHUNK].astype(BF16))

    @pl.when(j == pl.num_programs(1) - 1)
    def _():
        for r in range(0, o_ref.shape[0], FFN_CHUNK // 2):
            rows = slice(r, r + FFN_CHUNK // 2)
            y = DEEP_ALPHA * x_ref[rows, :] + 0.5 * o_ref[rows, :]
            o_ref[rows, :] = _layer_norm(y, g_ref[...], b_ref[...])


def _ffn_ln(x, w_in, w_out, g, b):
    n = x.shape[0]
    tm = math.gcd(n, FFN_TM)
    tf = FFN_TF * FFN_TM // tm
    nff = D_FF // tf
    return pl.pallas_call(
        _ffn_kernel,
        grid=(n // tm, nff),
        in_specs=[
            pl.BlockSpec((tm, D_MODEL), lambda i, j: (i, 0)),
            pl.BlockSpec((D_MODEL, tf), lambda i, j: (0, j)),
            pl.BlockSpec((D_MODEL, tf), lambda i, j: (0, j + nff)),
            pl.BlockSpec((tf, D_MODEL), lambda i, j: (j, 0)),
            pl.BlockSpec((1, D_MODEL), lambda i, j: (0, 0)),
            pl.BlockSpec((1, D_MODEL), lambda i, j: (0, 0)),
        ],
        out_specs=pl.BlockSpec((tm, D_MODEL), lambda i, j: (i, 0)),
        out_shape=jax.ShapeDtypeStruct((n, D_MODEL), F32),
        scratch_shapes=[pltpu.VMEM((tm, D_MODEL), BF16)],
        compiler_params=_params("parallel", "arbitrary"),
        name="ffn_ln",
    )(x, w_in, w_in, w_out, g, b)


def _mixin_kernel(x_ref, w_ref, wt_ref, o_ref, ot_ref, xb_ref):
    @pl.when(pl.program_id(1) == 0)
    def _():
        xb_ref[...] = x_ref[...].astype(BF16)
        ncols = ot_ref.shape[0]
        lane = lax.broadcasted_iota(jnp.int32, (1, LANES), 1)
        wt = jnp.where(lane < ncols, wt_ref[...], 0.0).astype(BF16)
        ot_ref[...] = _dot(xb_ref[...], wt).T[:ncols, :]

    o_ref[...] = _dot(xb_ref[...], w_ref[...].astype(BF16))


def _mixin(x, w, layer, *, tm=1024, tn=1024):
    n = x.shape[0]
    return pl.pallas_call(
        _mixin_kernel,
        grid=(n // tm, MIX_MAIN // tn),
        in_specs=[pl.BlockSpec((tm, D_MODEL), lambda i, j: (i, 0)),
                  pl.BlockSpec((None, D_MODEL, tn), lambda i, j: (layer, 0, j)),
                  pl.BlockSpec((None, D_MODEL, LANES), lambda i, j: (layer, 0, MIX_MAIN // LANES))],
        out_specs=[pl.BlockSpec((tm, tn), lambda i, j: (i, j)),
                   pl.BlockSpec((2 * GDN_HEADS, tm), lambda i, j: (0, i))],
        out_shape=[jax.ShapeDtypeStruct((n, MIX_MAIN), F32),
                   jax.ShapeDtypeStruct((2 * GDN_HEADS, n), F32)],
        scratch_shapes=[pltpu.VMEM((tm, D_MODEL), BF16)],
        compiler_params=_params("parallel", "arbitrary"),
        name="mix_in",
    )(x, w, w)


def _mixout_kernel(ya_ref, yb_ref, x_ref, w_ref, g_ref, b_ref, o_ref):
    mix = (_dot(ya_ref[...].astype(BF16), w_ref[0:S5_WIDTH, :])
           + _dot(yb_ref[...].astype(BF16), w_ref[S5_WIDTH:, :]))
    o_ref[...] = _layer_norm(DEEP_ALPHA * x_ref[...] + mix, g_ref[...], b_ref[...])


def _mixout_ln(ya, yb, x, w, g, b, *, tm=512):
    n = x.shape[0]
    return pl.pallas_call(
        _mixout_kernel,
        grid=(n // tm,),
        in_specs=[pl.BlockSpec((tm, S5_WIDTH), lambda i: (i, 0)),
                  pl.BlockSpec((tm, GDN_WIDTH), lambda i: (i, 0)),
                  pl.BlockSpec((tm, D_MODEL), lambda i: (i, 0)),
                  pl.BlockSpec((D_MODEL, D_MODEL), lambda i: (0, 0)),
                  pl.BlockSpec((1, D_MODEL), lambda i: (0, 0)),
                  pl.BlockSpec((1, D_MODEL), lambda i: (0, 0))],
        out_specs=pl.BlockSpec((tm, D_MODEL), lambda i: (i, 0)),
        out_shape=jax.ShapeDtypeStruct((n, D_MODEL), F32),
        compiler_params=_params("parallel"),
        name="mix_out_ln",
    )(ya, yb, x, w, g, b)


def _s5_disc_kernel(lre_ref, lim_ref, ldt_ref, bre_ref, bim_ref, coef_ref, bbre_ref, bbim_ref):
    lr, li = lre_ref[...], lim_ref[...]
    dt = jnp.exp(ldt_ref[...])
    mag = jnp.exp(lr * dt)
    ar = mag * jnp.cos(li * dt)
    ai = mag * jnp.sin(li * dt)
    nr, ni = ar - 1.0, ai
    den = lr * lr + li * li
    c_re = (nr * lr + ni * li) / den
    c_im = (ni * lr - nr * li) / den
    bw = S5_HID // S5_BLOCKS
    for j in range(S5_BLOCKS):
        cr, ci = c_re[:, j * bw:(j + 1) * bw], c_im[:, j * bw:(j + 1) * bw]
        bbre_ref[j] = (cr * bre_ref[j] - ci * bim_ref[j]).astype(BF16)
        bbim_ref[j] = (cr * bim_ref[j] + ci * bre_ref[j]).astype(BF16)

    def cmul(x, y):
        return x[0] * y[0] - x[1] * y[1], x[0] * y[1] + x[1] * y[0]

    width = lr.shape[-1]
    p = (ar, ai)
    for _ in range(S5_SEG - 1):
        p = cmul(p, (ar, ai))
    coef_ref[...] = jnp.zeros_like(coef_ref)
    for part, v in enumerate((ar, ai)):
        coef_ref[part] = jnp.broadcast_to(v, (SUBLANES, width))
    for k, s in enumerate((1, 2, 4)):
        for part in range(2):
            coef_ref[2 + 2 * k + part, s:SUBLANES, :] = jnp.broadcast_to(p[part], (SUBLANES - s, width))
        p = cmul(p, p)


def _s5_disc(lre, lim, ldt, b_re, b_im):
    return pl.pallas_call(
        _s5_disc_kernel,
        out_shape=[jax.ShapeDtypeStruct((8, SUBLANES, S5_HID), F32),
                   jax.ShapeDtypeStruct(b_re.shape, BF16), jax.ShapeDtypeStruct(b_im.shape, BF16)],
        name="s5_disc",
    )(lre, lim, ldt, b_re, b_im)


def _gelu_tanh(y):
    return 0.5 * y * (1.0 + jnp.tanh(math.sqrt(2.0 / math.pi) * (y + 0.044715 * (y * y * y))))


def _scan_layout(tt, group):
    p = jnp.arange(tt)
    rem = p % group
    src = (p - rem) + (rem % SUBLANES) * (group // SUBLANES) + rem // SUBLANES
    return (src[:, None] == jnp.arange(tt)[None, :]).astype(BF16)


def _cmul_add(ar, ai, xr, xi, br, bi):
    return ar * xr - ai * xi + br, ar * xi + ai * xr + bi


def _s5_kernel(u_ref, h0re_ref, h0im_ref, perm_ref, back_ref, coef_ref, bre_ref, bim_ref, cwre_ref, cwim_ref,
               d_ref, gw_ref, gb_ref, y_ref, sre_ref, sim_ref, hre, him, ysc, car_re, car_im,
               *, chain, tt):
    group = tt if chain else SLAB * SUBLANES
    seg = group // SUBLANES
    u = u_ref[...]
    ub = _dot(perm_ref[...], u.astype(BF16)).astype(BF16)
    bw = S5_HID // S5_BLOCKS
    gw = S5_WIDTH // S5_BLOCKS
    if chain:
        @pl.when(pl.program_id(1) == 0)
        def _():
            car_re[...] = jnp.broadcast_to(h0re_ref[...], car_re.shape)
            car_im[...] = jnp.broadcast_to(h0im_ref[...], car_im.shape)

    sub = lax.broadcasted_iota(jnp.int32, (SUBLANES, 1), 0)
    for c in range(S5_BLOCKS):
        sl = slice(c * bw, (c + 1) * bw)
        uc = ub[:, c * gw:(c + 1) * gw]
        hre[:, sl] = _dot(uc, bre_ref[c])
        him[:, sl] = _dot(uc, bim_ref[c])
        ar, ai = coef_ref[0, :, sl], coef_ref[1, :, sl]
        if not chain:
            for g in range(tt // group):
                srows = slice(g * SUBLANES, (g + 1) * SUBLANES)
                xr, xi = h0re_ref[srows, sl], h0im_ref[srows, sl]
                for j in range(SLAB - SLAB_REAL, SLAB):
                    rows = slice(g * group + j * SUBLANES, g * group + (j + 1) * SUBLANES)
                    xr, xi = _cmul_add(ar, ai, xr, xi, hre[rows, sl], him[rows, sl])
                    hre[rows, sl] = xr
                    him[rows, sl] = xi
                sre_ref[srows, sl] = xr
                sim_ref[srows, sl] = xi
        else:
            xr = xi = jnp.zeros((SUBLANES, bw), F32)
            for j in range(seg):
                rows = slice(j * SUBLANES, (j + 1) * SUBLANES)
                xr, xi = _cmul_add(ar, ai, xr, xi, hre[rows, sl], him[rows, sl])
                hre[rows, sl] = xr
                him[rows, sl] = xi
            kr = jnp.where(sub == 0, car_re[:, sl], pltpu.roll(xr, 1, 0))
            ki = jnp.where(sub == 0, car_im[:, sl], pltpu.roll(xi, 1, 0))
            for k, s in enumerate((1, 2, 4)):
                pr, pi = coef_ref[2 + 2 * k, :, sl], coef_ref[3 + 2 * k, :, sl]
                kr, ki = _cmul_add(pr, pi, pltpu.roll(kr, s, 0), pltpu.roll(ki, s, 0), kr, ki)
            outr, outi = _cmul_add(coef_ref[2, :, sl], coef_ref[3, :, sl], kr, ki, xr, xi)
            car_re[:, sl] = jnp.broadcast_to(outr[SUBLANES - 1:SUBLANES, :], outr.shape)
            car_im[:, sl] = jnp.broadcast_to(outi[SUBLANES - 1:SUBLANES, :], outi.shape)
            for j in range(seg):
                rows = slice(j * SUBLANES, (j + 1) * SUBLANES)
                kr, ki = ar * kr - ai * ki, ar * ki + ai * kr
                hre[rows, sl] += kr
                him[rows, sl] += ki
        ysc[:, c * gw:(c + 1) * gw] = (_dot(hre[:, sl].astype(BF16), cwre_ref[c])
                                       - _dot(him[:, sl].astype(BF16), cwim_ref[c]))

    if chain:
        @pl.when(pl.program_id(1) == pl.num_programs(1) - 1)
        def _():
            sre_ref[...] = car_re[0:1, :]
            sim_ref[...] = car_im[0:1, :]

    y = _dot_split(back_ref[...], ysc[...])
    z = _gelu_tanh(y + d_ref[...] * u)
    gl = _dot(z.astype(BF16), gw_ref[...]) + gb_ref[...]
    y_ref[...] = z * jax.nn.sigmoid(gl)


def _s5_mixer(proj, h0re, h0im, sw, *, chain, nseq, tt=S5_TILE):
    rows = proj.shape[0]
    const = lambda shape: pl.BlockSpec(shape, lambda *_: (0,) * len(shape))
    if chain:
        assert tt == S5_SEG * SUBLANES
        nt = rows // nseq // tt
        grid = (nseq, nt)
        u_spec = pl.BlockSpec((tt, S5_WIDTH), lambda b, t: (b * nt + t, 0))
        st_spec = pl.BlockSpec((None, 1, S5_HID), lambda b, t: (b, 0, 0))
        st_shape = jax.ShapeDtypeStruct((nseq, 1, S5_HID), F32)
        sem = ("parallel", "arbitrary")
    else:
        grid = (rows // tt,)
        u_spec = pl.BlockSpec((tt, S5_WIDTH), lambda i: (i, 0))
        st_spec = pl.BlockSpec((tt // SLAB, S5_HID), lambda i: (i, 0))
        st_shape = jax.ShapeDtypeStruct((rows // SLAB, S5_HID), F32)
        sem = ("parallel",)
    y_spec = u_spec
    bw = S5_HID // S5_BLOCKS
    gw = S5_WIDTH // S5_BLOCKS
    perm = _scan_layout(tt, tt if chain else SLAB * SUBLANES)
    return pl.pallas_call(
        functools.partial(_s5_kernel, chain=chain, tt=tt),
        grid=grid,
        in_specs=[u_spec, st_spec, st_spec,
                  const((tt, tt)), const((tt, tt)), const((8, SUBLANES, S5_HID)),
                  const((S5_BLOCKS, gw, bw)), const((S5_BLOCKS, gw, bw)),
                  const((S5_BLOCKS, bw, gw)), const((S5_BLOCKS, bw, gw)),
                  const((1, S5_WIDTH)), const((S5_WIDTH, S5_WIDTH)), const((1, S5_WIDTH))],
        out_specs=[y_spec, st_spec, st_spec],
        out_shape=[jax.ShapeDtypeStruct((rows, S5_WIDTH), F32), st_shape, st_shape],
        scratch_shapes=[pltpu.VMEM((tt, S5_HID), F32), pltpu.VMEM((tt, S5_HID), F32),
                        pltpu.VMEM((tt, S5_WIDTH), F32),
                        pltpu.VMEM((SUBLANES, S5_HID), F32), pltpu.VMEM((SUBLANES, S5_HID), F32)],
        compiler_params=_params(*sem),
        name="s5_chain" if chain else "s5_slab",
    )(proj, h0re, h0im, perm, perm.T, sw["coef"], sw["b_re"], sw["b_im"], sw["c_re"], sw["c_im"],
      sw["d"], sw["glu_w"], sw["glu_b"])


def _softplus(x):
    return jnp.maximum(x, 0.0) + jnp.log1p(jnp.exp(-jnp.abs(x)))


def _segment_masks(seg):
    ri = lax.broadcasted_iota(jnp.int32, (GDN_TILE, GDN_TILE), 0)
    ci = lax.broadcasted_iota(jnp.int32, (GDN_TILE, GDN_TILE), 1)
    same = (ri // seg) == (ci // seg)
    causal = (ri >= ci) & same
    strict = (ri > ci) & same
    return same, causal, strict


def _gate_matrices(seg):
    i = jnp.arange(GDN_TILE)
    same = (i[:, None] // seg) == (i[None, :] // seg)
    stack = jnp.concatenate([i[:, None] == i[None, :], (i[:, None] >= i[None, :]) & same, same], axis=0)
    return stack.astype(BF16), ((i[:, None] <= i[None, :]) & same).astype(BF16)


def _gdn_gates(bdt, alog_col, dtb_col, stack, upper, seg, slab):
    same, causal, strict = _segment_masks(seg)
    bt = jax.nn.sigmoid(bdt[:GDN_HEADS, :])
    gt = -jnp.exp(alog_col) * _softplus(bdt[GDN_HEADS:, :] + dtb_col)
    if slab:
        creal = (lax.broadcasted_iota(jnp.int32, (1, GDN_TILE), 1) % SLAB) >= SLAB - SLAB_REAL
        bt = jnp.where(creal, bt, 0.0)
        gt = jnp.where(creal, gt, 0.0)
    gates = jnp.concatenate([bt, gt, jnp.zeros((GDN_TILE - 2 * GDN_HEADS, GDN_TILE), F32)], axis=0)
    cols = _dot_split(stack, gates, NT_DIMS)
    beta, gc, gl = cols[:GDN_TILE], cols[GDN_TILE:2 * GDN_TILE], cols[2 * GDN_TILE:]
    gct = _dot_split(gt, upper)
    return beta, gc, gl, gct, causal, strict


def _merge_masks(top):
    ri = lax.broadcasted_iota(jnp.int32, (GDN_TILE, GDN_TILE), 0)
    ci = lax.broadcasted_iota(jnp.int32, (GDN_TILE, GDN_TILE), 1)
    masks = []
    s = 1
    while s < top:
        masks.append(((ri // (2 * s)) == (ci // (2 * s))) & ((ri // s) != (ci // s)))
        s *= 2
    return masks


def _unit_lower_inverse(ms, masks):
    es = [-jnp.where(masks[0], m, 0.0) for m in ms]
    for mask in masks[1:]:
        cs = [jnp.where(mask, m, 0.0) for m in ms]
        ebs = [e.astype(BF16) for e in es]
        xs = [c + _dot(eb, c.astype(BF16)) for c, eb in zip(cs, ebs)]
        es = [e - (x + _dot(x.astype(BF16), eb)) for e, x, eb in zip(es, xs, ebs)]
    return es


def _gdn_intra(qs, ks, vs, betas, gcols, grows, glcols, causal, strict, masks):
    n = range(len(qs))
    qn = [q * lax.rsqrt(jnp.sum(q * q, axis=-1, keepdims=True) + NORM_EPS) * (GDN_HEAD_DIM ** -0.5) for q in qs]
    kn = [k * lax.rsqrt(jnp.sum(k * k, axis=-1, keepdims=True) + NORM_EPS) for k in ks]
    decay = [jnp.exp(jnp.where(causal, gcols[i] - grows[i], -jnp.inf)) for i in n]
    kb = [kn[i] * betas[i] for i in n]
    knb = [k.astype(BF16) for k in kn]
    kk = [lax.dot_general(kb[i].astype(BF16), knb[i], NT_DIMS, preferred_element_type=F32) for i in n]
    ms = [kk[i] * jnp.where(strict, decay[i], 0.0) for i in n]
    eg = [jnp.exp(g) for g in gcols]
    attn = [(lax.dot_general(qn[i].astype(BF16), knb[i], NT_DIMS, preferred_element_type=F32) * decay[i]).astype(BF16)
            for i in n]
    qg = [qn[i] * eg[i] for i in n]
    kdt = [(kn[i] * jnp.exp(glcols[i] - gcols[i])).T.astype(BF16) for i in n]
    rhs = [jnp.concatenate([vs[i] * betas[i], kb[i] * eg[i]], axis=1) for i in n]
    ys = _unit_lower_inverse(ms, masks)
    uw = [rhs[i] + _dot(ys[i].astype(BF16), rhs[i].astype(BF16)) for i in n]
    return [(uw[i][:, :GDN_HEAD_DIM], uw[i][:, GDN_HEAD_DIM:], attn[i], qg[i], kdt[i]) for i in n]


def _gdn_out(o, gate, norm_w):
    o = o * lax.rsqrt(jnp.mean(o * o, axis=-1, keepdims=True) + NORM_EPS) * norm_w
    return o * _silu(gate)


def _conv4(win_ref, cw, width_slice):
    x = win_ref[:, width_slice]
    x1 = pltpu.roll(x, 1, 0)
    z = cw[1:2, :] * x + cw[0:1, :] * x1
    out = cw[3:4, :] * x + cw[2:3, :] * x1 + pltpu.roll(z, 2, 0)
    return out[SUBLANES:, :]


def _gdn_chain_kernel(q_ref, k_ref, v_ref, gate_ref, bdt_ref, buf_ref, s0_ref, cw_ref,
                      alc_ref, dbc_ref, nw_ref, gstack_ref, gupper_ref,
                      y_ref, sout_ref, bufout_ref, win, s_sc):
    t = pl.program_id(1)
    w3 = 3 * GDN_WIDTH

    @pl.when(t == 0)
    def _():
        win[0:SUBLANES, :] = jnp.zeros((SUBLANES, w3), F32)
        win[SUBLANES - 3:SUBLANES, :] = buf_ref[...]
        s_sc[...] = s0_ref[...]

    win[SUBLANES:, 0:GDN_WIDTH] = q_ref[...]
    win[SUBLANES:, GDN_WIDTH:2 * GDN_WIDTH] = k_ref[...]
    win[SUBLANES:, 2 * GDN_WIDTH:] = v_ref[...]

    beta, gc, gl, gct, causal, strict = _gdn_gates(bdt_ref[...], alc_ref[...], dbc_ref[...], gstack_ref[...],
                                                   gupper_ref[...], GDN_TILE, False)
    masks = _merge_masks(GDN_TILE)
    for h0 in range(0, GDN_HEADS, GDN_CHAIN_GROUP):
        heads = range(h0, h0 + GDN_CHAIN_GROUP)
        idx = range(GDN_CHAIN_GROUP)
        qkv = [[], [], []]
        for part in range(3):
            for h in heads:
                cs = slice(part * GDN_WIDTH + h * GDN_HEAD_DIM, part * GDN_WIDTH + (h + 1) * GDN_HEAD_DIM)
                qkv[part].append(_silu(_conv4(win, cw_ref[:, cs], cs)))
        gcols = [gc[:, GDN_HEADS + h:GDN_HEADS + h + 1] for h in heads]
        glcols = [gl[:, GDN_HEADS + h:GDN_HEADS + h + 1] for h in heads]
        intra = _gdn_intra(qkv[0], qkv[1], qkv[2], [beta[:, h:h + 1] for h in heads], gcols,
                           [gct[h:h + 1, :] for h in heads], glcols, causal, strict, masks)
        s_old = [s_sc[h] for h in heads]
        a = [_dot(jnp.concatenate([intra[i][1], intra[i][3]], axis=0).astype(BF16), s_old[i].astype(BF16))
             for i in idx]
        vnb = [(intra[i][0] - a[i][:GDN_TILE]).astype(BF16) for i in idx]
        o = [a[i][GDN_TILE:] + _dot(intra[i][2], vnb[i]) for i in idx]
        for i, h in zip(idx, heads):
            s_sc[h] = s_old[i] * jnp.exp(glcols[i][0:1, :]) + _dot(intra[i][4], vnb[i])
        for i, h in zip(idx, heads):
            hs = slice(h * GDN_HEAD_DIM, (h + 1) * GDN_HEAD_DIM)
            y_ref[:, hs] = _gdn_out(o[i], gate_ref[:, hs], nw_ref[...])

    win[0:SUBLANES, :] = win[GDN_TILE:GDN_TILE + SUBLANES, :]

    @pl.when(t == pl.num_programs(1) - 1)
    def _():
        sout_ref[...] = s_sc[...]
        bufout_ref[...] = win[GDN_TILE + SUBLANES - 3:GDN_TILE + SUBLANES, :]


def _gdn_chain(proj, bdt, conv_buf, s0, gw, *, nseq):
    rows = proj.shape[0]
    nt = rows // nseq // GDN_TILE
    w3 = 3 * GDN_WIDTH
    const = lambda shape: pl.BlockSpec(shape, lambda *_: (0,) * len(shape))
    col = lambda cb: pl.BlockSpec((GDN_TILE, GDN_WIDTH), lambda b, t, cb=cb: (b * nt + t, cb))
    return pl.pallas_call(
        _gdn_chain_kernel,
        grid=(nseq, nt),
        in_specs=[col(1), col(2), col(3), col(4),
                  pl.BlockSpec((2 * GDN_HEADS, GDN_TILE), lambda b, t: (0, b * nt + t)),
                  pl.BlockSpec((None, GDN_CONV - 1, w3), lambda b, t: (b, 0, 0)),
                  pl.BlockSpec((None, GDN_HEADS, GDN_HEAD_DIM, GDN_HEAD_DIM), lambda b, t: (b, 0, 0, 0)),
                  const((GDN_CONV, w3)), const((GDN_HEADS, 1)), const((GDN_HEADS, 1)), const((1, GDN_HEAD_DIM)),
                  const((3 * GDN_TILE, GDN_TILE)), const((GDN_TILE, GDN_TILE))],
        out_specs=[pl.BlockSpec((GDN_TILE, GDN_WIDTH), lambda b, t: (b * nt + t, 0)),
                   pl.BlockSpec((None, GDN_HEADS, GDN_HEAD_DIM, GDN_HEAD_DIM), lambda b, t: (b, 0, 0, 0)),
                   pl.BlockSpec((None, GDN_CONV - 1, w3), lambda b, t: (b, 0, 0))],
        out_shape=[jax.ShapeDtypeStruct((rows, GDN_WIDTH), F32),
                   jax.ShapeDtypeStruct((nseq, GDN_HEADS, GDN_HEAD_DIM, GDN_HEAD_DIM), F32),
                   jax.ShapeDtypeStruct((nseq, GDN_CONV - 1, w3), F32)],
        scratch_shapes=[pltpu.VMEM((GDN_TILE + SUBLANES, w3), F32),
                        pltpu.VMEM((GDN_HEADS, GDN_HEAD_DIM, GDN_HEAD_DIM), F32)],
        compiler_params=_params("parallel", "arbitrary"),
        name="gdn_chain",
    )(proj, proj, proj, proj, bdt, conv_buf, s0, gw["conv_w"], gw["alog_col"], gw["dtb_col"], gw["norm_w"],
      *_gate_matrices(GDN_TILE))


def _gdn_slab_kernel(q_ref, k_ref, v_ref, gate_ref, bdt_ref, bq_ref, bk_ref, bv_ref, s0_ref,
                     cwq_ref, cwk_ref, cwv_ref, alc_ref, dbc_ref, nw_ref, gstack_ref, gupper_ref,
                     y_ref, sout_ref, oq_ref, ok_ref, ov_ref,
                     win):
    hp = s0_ref.shape[1]
    hw = hp * GDN_HEAD_DIM
    heads = range(hp)
    head0 = pl.program_id(1) * hp
    nslab = GDN_TILE // SLAB
    first = (pl.program_id(0) % (bq_ref.shape[1] // nslab)) * nslab
    real = (lax.broadcasted_iota(jnp.int32, (GDN_TILE, 1), 0) % SLAB) >= SLAB - SLAB_REAL
    qkv = []
    taps = range(GDN_CONV - 1)
    place = [_slab_row_selector(GDN_TILE, bq_ref.shape[1], 1 + i, first, transpose=True) for i in taps]
    take = [_slab_row_selector(GDN_TILE, nslab, SLAB - 3 + i, 0, transpose=False) for i in taps]
    for part, (x_ref, b_ref, cw_ref, o_ref) in enumerate(
            ((q_ref, bq_ref, cwq_ref, oq_ref), (k_ref, bk_ref, cwk_ref, ok_ref), (v_ref, bv_ref, cwv_ref, ov_ref))):
        cs = slice(part * hw, (part + 1) * hw)
        x = x_ref[...]
        for i in taps:
            x = x + _dot_split(place[i], b_ref[i])
        win[0:SUBLANES, cs] = jnp.zeros((SUBLANES, hw), F32)
        win[SUBLANES:, cs] = x
        conv = jnp.where(real, _silu(_conv4(win, cw_ref[...], cs)), 0.0)
        qkv.append([conv[:, h * GDN_HEAD_DIM:(h + 1) * GDN_HEAD_DIM] for h in heads])
        pieces = _split3(x)
        for i in taps:
            o_ref[i] = sum(_dot(take[i], p) for p in pieces)

    beta, gc, gl, gct, causal, strict = _gdn_gates(bdt_ref[...], alc_ref[...], dbc_ref[...], gstack_ref[...],
                                                   gupper_ref[...], SLAB, True)
    lane = lax.broadcasted_iota(jnp.int32, (1, LANES), 1)
    sub = lax.broadcasted_iota(jnp.int32, (SUBLANES, 1), 0)
    pick = lambda a, idx: jnp.sum(jnp.where(lane == idx, a, 0.0), axis=1, keepdims=True)
    gcols = [pick(gc, GDN_HEADS + head0 + h) for h in heads]
    glcols = [pick(gl, GDN_HEADS + head0 + h) for h in heads]
    intra = _gdn_intra(qkv[0], qkv[1], qkv[2], [pick(beta, head0 + h) for h in heads], gcols,
                       [jnp.sum(jnp.where(sub == head0 + h, gct, 0.0), axis=0, keepdims=True) for h in heads],
                       glcols, causal, strict, _merge_masks(SLAB_REAL))
    for h in heads:
        u, wk, attn, qg, kdt = intra[h]
        res = []
        for i in range(nslab):
            rows = slice(i * SLAB, (i + 1) * SLAB)
            lhs = jnp.concatenate([wk[rows], qg[rows]], axis=0).astype(BF16)
            res.append(_dot(lhs, s0_ref[i, h].astype(BF16)))
        vnb = (u - jnp.concatenate([r[:SLAB] for r in res], axis=0)).astype(BF16)
        o = jnp.concatenate([r[SLAB:] for r in res], axis=0) + _dot(attn, vnb)
        hs = slice(h * GDN_HEAD_DIM, (h + 1) * GDN_HEAD_DIM)
        y_ref[:, hs] = _gdn_out(o, gate_ref[:, hs], nw_ref[...])
        egl = jnp.exp(glcols[h])
        for i in range(nslab):
            in_slab = (lane // SLAB) == i
            upd = _dot(jnp.where(in_slab, kdt, jnp.zeros_like(kdt)), vnb)
            sout_ref[i, h] = s0_ref[i, h] * egl[i * SLAB:i * SLAB + 1, :] + upd


def _gdn_slab(proj, bdt, conv_buf, s0, gw, *, hp=GDN_SLAB_HEADS):
    rows = proj.shape[0]
    nb = rows // SLAB
    nslab = GDN_TILE // SLAB
    hw = hp * GDN_HEAD_DIM
    hb = GDN_WIDTH // hw
    const = lambda shape: pl.BlockSpec(shape, lambda *_: (0,) * len(shape))
    col = lambda g: pl.BlockSpec((GDN_TILE, hw), lambda i, h, g=g: (i, g * hb + h))
    per = LANES // nslab
    buf = lambda g: pl.BlockSpec((GDN_CONV - 1, LANES, hw), lambda i, h, g=g: (0, i // per, g * hb + h))
    cwb = lambda g: pl.BlockSpec((GDN_CONV, hw), lambda i, h, g=g: (0, g * hb + h))
    st = pl.BlockSpec((nslab, hp, GDN_HEAD_DIM, GDN_HEAD_DIM), lambda i, h: (i, h, 0, 0))
    obuf = pl.BlockSpec((GDN_CONV - 1, nslab, hw), lambda i, h: (0, i, h))
    return pl.pallas_call(
        _gdn_slab_kernel,
        grid=(rows // GDN_TILE, GDN_HEADS // hp),
        in_specs=[col(1), col(2), col(3), col(4),
                  pl.BlockSpec((2 * GDN_HEADS, GDN_TILE), lambda i, h: (0, i)),
                  buf(0), buf(1), buf(2), st, cwb(0), cwb(1), cwb(2),
                  const((GDN_HEADS, 1)), const((GDN_HEADS, 1)), const((1, GDN_HEAD_DIM)),
                  const((3 * GDN_TILE, GDN_TILE)), const((GDN_TILE, GDN_TILE))],
        out_specs=[pl.BlockSpec((GDN_TILE, hw), lambda i, h: (i, h)), st, obuf, obuf, obuf],
        out_shape=[jax.ShapeDtypeStruct((rows, GDN_WIDTH), F32),
                   jax.ShapeDtypeStruct(s0.shape, F32)]
                  + [jax.ShapeDtypeStruct((GDN_CONV - 1, nb, GDN_WIDTH), F32)] * 3,
        scratch_shapes=[pltpu.VMEM((GDN_TILE + SUBLANES, 3 * hw), F32)],
        compiler_params=_params("parallel", "arbitrary"),
        name="gdn_slab",
    )(proj, proj, proj, proj, bdt, conv_buf, conv_buf, conv_buf, s0,
      gw["conv_w"], gw["conv_w"], gw["conv_w"], gw["alog_col"], gw["dtb_col"], gw["norm_w"],
      *_gate_matrices(SLAB))


def _block_diag(w):
    per = S5_GROUPS // S5_BLOCKS
    g, a, b = w.shape
    w = w.reshape(S5_BLOCKS, per, a, b)
    eye = jnp.eye(per, dtype=w.dtype)
    return jnp.einsum("jgab,gk->jgakb", w, eye).reshape(S5_BLOCKS, per * a, per * b)


def _layer_weights(l, ln1_g, ln1_b, ffn1_w_in, ffn1_w_out, w_mix_in, s5_lambda_re, s5_lambda_im, s5_log_dt,
                   s5_b_re, s5_b_im, s5_c_re, s5_c_im, s5_d, s5_glu_w, s5_glu_b, gdn_conv_w, gdn_a_log,
                   gdn_dt_bias, gdn_norm_w, w_mix_out, ln2_g, ln2_b, ffn2_w_in, ffn2_w_out, ln3_g, ln3_b):
    row = lambda v: v[l].reshape(1, -1).astype(F32)
    w = {
        "ln1": (row(ln1_g), row(ln1_b)), "ln2": (row(ln2_g), row(ln2_b)), "ln3": (row(ln3_g), row(ln3_b)),
        "ffn1": (ffn1_w_in[l], ffn1_w_out[l]),
        "ffn2": (ffn2_w_in[l], ffn2_w_out[l]),
        "mix_in": (w_mix_in.astype(BF16), l),
        "mix_out": w_mix_out[l].astype(BF16),
    }
    coef, bb_re, bb_im = _s5_disc(
        s5_lambda_re[l].reshape(1, S5_HID).astype(F32), s5_lambda_im[l].reshape(1, S5_HID).astype(F32),
        jnp.repeat(s5_log_dt[l], S5_STATE).reshape(1, S5_HID).astype(F32),
        _block_diag(jnp.swapaxes(s5_b_re[l], 1, 2).astype(F32)),
        _block_diag(jnp.swapaxes(s5_b_im[l], 1, 2).astype(F32)))
    w["s5"] = {
        "coef": coef, "b_re": bb_re, "b_im": bb_im,
        "c_re": _block_diag(jnp.swapaxes(s5_c_re[l], 1, 2)).astype(BF16),
        "c_im": _block_diag(jnp.swapaxes(s5_c_im[l], 1, 2)).astype(BF16),
        "d": row(s5_d), "glu_w": s5_glu_w[l].astype(BF16), "glu_b": row(s5_glu_b),
    }
    w["gdn"] = {
        "conv_w": gdn_conv_w[l].astype(F32),
        "alog_col": gdn_a_log[l].reshape(GDN_HEADS, 1).astype(F32),
        "dtb_col": gdn_dt_bias[l].reshape(GDN_HEADS, 1).astype(F32),
        "norm_w": row(gdn_norm_w),
    }
    return w


def _prompt_layer(x, w, nseq):
    x = _ffn_ln(x, *w["ffn1"], *w["ln1"])
    proj, bdt = _mixin(x, *w["mix_in"])
    z_s5 = jnp.zeros((nseq, 1, S5_HID), F32)
    y_s5, n_re, n_im = _s5_mixer(proj, z_s5, z_s5, w["s5"], chain=True, nseq=nseq)
    z_gdn = jnp.zeros((nseq, GDN_HEADS, GDN_HEAD_DIM, GDN_HEAD_DIM), F32)
    z_buf = jnp.zeros((nseq, GDN_CONV - 1, 3 * GDN_WIDTH), F32)
    y_gdn, n_s, n_buf = _gdn_chain(proj, bdt, z_buf, z_gdn, w["gdn"], nseq=nseq)
    x = _mixout_ln(y_s5, y_gdn, x, w["mix_out"], *w["ln2"])
    x = _ffn_ln(x, *w["ffn2"], *w["ln3"])
    shape = (nseq, S5_GROUPS, S5_STATE)
    return x, n_re.reshape(shape), n_im.reshape(shape), n_s, n_buf


def _sample_layer(x, s5_re, s5_im, gdn_s, conv_buf, w, nb, t):
    x = _ffn_ln(x, *w["ffn1"], *w["ln1"])
    xs = jnp.pad(x.reshape(nb, t, D_MODEL), ((0, 0), (SLAB - t, 0), (0, 0))).reshape(nb * SLAB, D_MODEL)
    proj, bdt = _mixin(xs, *w["mix_in"])
    y_s5, n_re, n_im = _s5_mixer(proj, s5_re.reshape(nb, S5_HID).astype(F32),
                                 s5_im.reshape(nb, S5_HID).astype(F32), w["s5"], chain=False, nseq=nb)
    y_gdn, n_s, bq, bk, bv = _gdn_slab(proj, bdt, jnp.swapaxes(conv_buf.astype(F32), 0, 1),
                                       gdn_s.astype(F32), w["gdn"])
    xs = _mixout_ln(y_s5, y_gdn, xs, w["mix_out"], *w["ln2"])
    x = xs.reshape(nb, SLAB, D_MODEL)[:, SLAB - t:].reshape(nb * t, D_MODEL)
    x = _ffn_ln(x, *w["ffn2"], *w["ln3"])
    shape = (nb, S5_GROUPS, S5_STATE)
    n_buf = jnp.swapaxes(jnp.concatenate([bq, bk, bv], axis=-1), 0, 1)
    return x, n_re.reshape(shape), n_im.reshape(shape), n_s, n_buf


def kernel(x_prompt, x_sample, state_s5_re, state_s5_im, state_gdn, state_conv, ln1_g, ln1_b, ffn1_w_in, ffn1_w_out, w_mix_in, s5_lambda_re, s5_lambda_im, s5_log_dt, s5_b_re, s5_b_im, s5_c_re, s5_c_im, s5_d, s5_glu_w, s5_glu_b, gdn_conv_w, gdn_a_log, gdn_dt_bias, gdn_norm_w, w_mix_out, ln2_g, ln2_b, ffn2_w_in, ffn2_w_out, ln3_g, ln3_b):
    bp, tp, _ = x_prompt.shape
    bs, ts, _ = x_sample.shape
    assert ts == SLAB_REAL and tp % S5_TILE == 0 and (bs * SLAB) % S5_TILE == 0
    depth = ln1_g.shape[0]
    yp = x_prompt.astype(F32).reshape(bp * tp, D_MODEL)
    ys = x_sample.astype(F32).reshape(bs * ts, D_MODEL)
    outs = [[] for _ in range(8)]
    for l in range(depth):
        w = _layer_weights(l, ln1_g, ln1_b, ffn1_w_in, ffn1_w_out, w_mix_in, s5_lambda_re, s5_lambda_im,
                           s5_log_dt, s5_b_re, s5_b_im, s5_c_re, s5_c_im, s5_d, s5_glu_w, s5_glu_b,
                           gdn_conv_w, gdn_a_log, gdn_dt_bias, gdn_norm_w, w_mix_out, ln2_g, ln2_b,
                           ffn2_w_in, ffn2_w_out, ln3_g, ln3_b)
        yp, *p_state = _prompt_layer(yp, w, bp)
        ys, *s_state = _sample_layer(ys, state_s5_re[l], state_s5_im[l], state_gdn[l], state_conv[l], w, bs, ts)
        for acc, val in zip(outs, p_state + s_state):
            acc.append(val)
    return (yp.reshape(x_prompt.shape).astype(x_prompt.dtype), ys.reshape(x_sample.shape).astype(x_sample.dtype),
            *(o[0][None] if depth == 1 else jnp.stack(o) for o in outs))
```

```python
import functools
import math

import jax
import jax.numpy as jnp
from jax import lax
from jax.experimental import pallas as pl
from jax.experimental.pallas import tpu as pltpu

F32 = jnp.float32
BF16 = jnp.bfloat16

D_MODEL = 2048
S5_WIDTH = 1024
S5_GROUP = 16
S5_GROUPS = 64
S5_STATE = 64
S5_HID = S5_GROUPS * S5_STATE
GDN_WIDTH = 1024
GDN_HEAD_DIM = 128
GDN_HEADS = 8
GDN_CONV = 4
D_FF = 5632
MIX_MAIN = 5120
DEEP_ALPHA = 2.0 ** 0.25
LN_EPS = 1e-5
NORM_EPS = 1e-6

SUBLANES = 8
LANES = 128
SLAB = 8
SLAB_REAL = 4
GDN_TILE = 128
GDN_SLAB_HEADS = 8
GDN_CHAIN_GROUP = 8
FFN_TM = 1024
FFN_TF = 256
FFN_CHUNK = 512
S5_TILE = 256
S5_SEG = S5_TILE // SUBLANES
S5_BLOCKS = 8
VMEM_LIMIT = 56 * 1024 * 1024

NT_DIMS = (((1,), (1,)), ((), ()))


def _dot(a, b, **kw):
    return jnp.dot(a, b, preferred_element_type=F32, **kw)


def _silu(x):
    return x * jax.nn.sigmoid(x)


def _layer_norm(y, g, b):
    mu = jnp.mean(y, axis=-1, keepdims=True)
    d = y - mu
    var = jnp.mean(d * d, axis=-1, keepdims=True)
    return d * lax.rsqrt(var + LN_EPS) * g + b


def _slab_row_selector(rows, nseq, slab_row, first, *, transpose):
    shape = (rows, nseq) if transpose else (nseq, rows)
    r = lax.broadcasted_iota(jnp.int32, shape, 0 if transpose else 1)
    b = lax.broadcasted_iota(jnp.int32, shape, 1 if transpose else 0)
    return _as_bf16(r == SLAB * (b - first) + slab_row)


def _as_bf16(mask):
    return mask.astype(F32).astype(BF16)


def _split3(x):
    pieces = []
    for _ in range(3):
        pieces.append(x.astype(BF16))
        x = x - pieces[-1].astype(F32)
    return pieces


def _dot_split(a, b, dims=None):
    f32_is_lhs = a.dtype == F32
    acc = None
    for piece in _split3(a if f32_is_lhs else b):
        lhs, rhs = (piece, b) if f32_is_lhs else (a, piece)
        d = _dot(lhs, rhs) if dims is None else lax.dot_general(lhs, rhs, dims, preferred_element_type=F32)
        acc = d if acc is None else acc + d
    return acc


def _params(*sem):
    return pltpu.CompilerParams(dimension_semantics=sem, vmem_limit_bytes=VMEM_LIMIT)


def _ffn_kernel(x_ref, wg_ref, wu_ref, wo_ref, g_ref, b_ref, o_ref, xb_ref):
    j = pl.program_id(1)

    @pl.when(j == 0)
    def _():
        o_ref[...] = jnp.zeros_like(o_ref)
        xb_ref[...] = x_ref[...].astype(BF16)

    xb = xb_ref[...]
    gate = _dot(xb, wg_ref[...].astype(BF16))
    up = _dot(xb, wu_ref[...].astype(BF16))
    h = (_silu(gate) * up).astype(BF16)
    for c in range(0, D_MODEL, FFN_CHUNK):
        o_ref[:, c:c + FFN_CHUNK] += _dot(h, wo_ref[:, c:c + FFN_CHUNK].astype(BF16))

    @pl.when(j == pl.num_programs(1) - 1)
    def _():
        for r in range(0, o_ref.shape[0], FFN_CHUNK // 2):
            rows = slice(r, r + FFN_CHUNK // 2)
            y = DEEP_ALPHA * x_ref[rows, :] + 0.5 * o_ref[rows, :]
            o_ref[rows, :] = _layer_norm(y, g_ref[...], b_ref[...])


def _ffn_ln(x, w_in, w_out, g, b):
    n = x.shape[0]
    tm = math.gcd(n, FFN_TM)
    tf = FFN_TF * FFN_TM // tm
    nff = D_FF // tf
    return pl.pallas_call(
        _ffn_kernel,
        grid=(n // tm, nff),
        in_specs=[
            pl.BlockSpec((tm, D_MODEL), lambda i, j: (i, 0)),
            pl.BlockSpec((D_MODEL, tf), lambda i, j: (0, j)),
            pl.BlockSpec((D_MODEL, tf), lambda i, j: (0, j + nff)),
            pl.BlockSpec((tf, D_MODEL), lambda i, j: (j, 0)),
            pl.BlockSpec((1, D_MODEL), lambda i, j: (0, 0)),
            pl.BlockSpec((1, D_MODEL), lambda i, j: (0, 0)),
        ],
        out_specs=pl.BlockSpec((tm, D_MODEL), lambda i, j: (i, 0)),
        out_shape=jax.ShapeDtypeStruct((n, D_MODEL), F32),
        scratch_shapes=[pltpu.VMEM((tm, D_MODEL), BF16)],
        compiler_params=_params("parallel", "arbitrary"),
        name="ffn_ln",
    )(x, w_in, w_in, w_out, g, b)


def _mixin_kernel(x_ref, w_ref, wt_ref, o_ref, ot_ref, xb_ref):
    @pl.when(pl.program_id(1) == 0)
    def _():
        xb_ref[...] = x_ref[...].astype(BF16)
        ncols = ot_ref.shape[0]
        lane = lax.broadcasted_iota(jnp.int32, (1, LANES), 1)
        wt = jnp.where(lane < ncols, wt_ref[...], 0.0).astype(BF16)
        ot_ref[...] = _dot(xb_ref[...], wt).T[:ncols, :]

    o_ref[...] = _dot(xb_ref[...], w_ref[...].astype(BF16))


def _mixin(x, w, layer, *, tm=1024, tn=1024):
    n = x.shape[0]
    return pl.pallas_call(
        _mixin_kernel,
        grid=(n // tm, MIX_MAIN // tn),
        in_specs=[pl.BlockSpec((tm, D_MODEL), lambda i, j: (i, 0)),
                  pl.BlockSpec((None, D_MODEL, tn), lambda i, j: (layer, 0, j)),
                  pl.BlockSpec((None, D_MODEL, LANES), lambda i, j: (layer, 0, MIX_MAIN // LANES))],
        out_specs=[pl.BlockSpec((tm, tn), lambda i, j: (i, j)),
                   pl.BlockSpec((2 * GDN_HEADS, tm), lambda i, j: (0, i))],
        out_shape=[jax.ShapeDtypeStruct((n, MIX_MAIN), F32),
                   jax.ShapeDtypeStruct((2 * GDN_HEADS, n), F32)],
        scratch_shapes=[pltpu.VMEM((tm, D_MODEL), BF16)],
        compiler_params=_params("parallel", "arbitrary"),
        name="mix_in",
    )(x, w, w)


def _mixout_kernel(ya_ref, yb_ref, x_ref, w_ref, g_ref, b_ref, o_ref):
    mix = (_dot(ya_ref[...].astype(BF16), w_ref[0:S5_WIDTH, :])
           + _dot(yb_ref[...].astype(BF16), w_ref[S5_WIDTH:, :]))
    o_ref[...] = _layer_norm(DEEP_ALPHA * x_ref[...] + mix, g_ref[...], b_ref[...])


def _mixout_ln(ya, yb, x, w, g, b, *, tm=512):
    n = x.shape[0]
    return pl.pallas_call(
        _mixout_kernel,
        grid=(n // tm,),
        in_specs=[pl.BlockSpec((tm, S5_WIDTH), lambda i: (i, 0)),
                  pl.BlockSpec((tm, GDN_WIDTH), lambda i: (i, 0)),
                  pl.BlockSpec((tm, D_MODEL), lambda i: (i, 0)),
                  pl.BlockSpec((D_MODEL, D_MODEL), lambda i: (0, 0)),
                  pl.BlockSpec((1, D_MODEL), lambda i: (0, 0)),
                  pl.BlockSpec((1, D_MODEL), lambda i: (0, 0))],
        out_specs=pl.BlockSpec((tm, D_MODEL), lambda i: (i, 0)),
        out_shape=jax.ShapeDtypeStruct((n, D_MODEL), F32),
        compiler_params=_params("parallel"),
        name="mix_out_ln",
    )(ya, yb, x, w, g, b)


def _s5_disc_kernel(lre_ref, lim_ref, ldt_ref, bre_ref, bim_ref, coef_ref, bbre_ref, bbim_ref):
    lr, li = lre_ref[...], lim_ref[...]
    dt = jnp.exp(ldt_ref[...])
    mag = jnp.exp(lr * dt)
    ar = mag * jnp.cos(li * dt)
    ai = mag * jnp.sin(li * dt)
    nr, ni = ar - 1.0, ai
    den = lr * lr + li * li
    c_re = (nr * lr + ni * li) / den
    c_im = (ni * lr - nr * li) / den
    bw = S5_HID // S5_BLOCKS
    for j in range(S5_BLOCKS):
        cr, ci = c_re[:, j * bw:(j + 1) * bw], c_im[:, j * bw:(j + 1) * bw]
        bbre_ref[j] = (cr * bre_ref[j] - ci * bim_ref[j]).astype(BF16)
        bbim_ref[j] = (cr * bim_ref[j] + ci * bre_ref[j]).astype(BF16)

    def cmul(x, y):
        return x[0] * y[0] - x[1] * y[1], x[0] * y[1] + x[1] * y[0]

    width = lr.shape[-1]
    p = (ar, ai)
    for _ in range(S5_SEG - 1):
        p = cmul(p, (ar, ai))
    coef_ref[...] = jnp.zeros_like(coef_ref)
    for part, v in enumerate((ar, ai)):
        coef_ref[part] = jnp.broadcast_to(v, (SUBLANES, width))
    for k, s in enumerate((1, 2, 4)):
        for part in range(2):
            coef_ref[2 + 2 * k + part, s:SUBLANES, :] = jnp.broadcast_to(p[part], (SUBLANES - s, width))
        p = cmul(p, p)


def _s5_disc(lre, lim, ldt, b_re, b_im):
    return pl.pallas_call(
        _s5_disc_kernel,
        out_shape=[jax.ShapeDtypeStruct((8, SUBLANES, S5_HID), F32),
                   jax.ShapeDtypeStruct(b_re.shape, BF16), jax.ShapeDtypeStruct(b_im.shape, BF16)],
        name="s5_disc",
    )(lre, lim, ldt, b_re, b_im)


def _gelu_tanh(y):
    return 0.5 * y * (1.0 + jnp.tanh(math.sqrt(2.0 / math.pi) * (y + 0.044715 * (y * y * y))))


def _scan_layout(tt, group):
    p = jnp.arange(tt)
    rem = p % group
    src = (p - rem) + (rem % SUBLANES) * (group // SUBLANES) + rem // SUBLANES
    return (src[:, None] == jnp.arange(tt)[None, :]).astype(BF16)


def _cmul_add(ar, ai, xr, xi, br, bi):
    return ar * xr - ai * xi + br, ar * xi + ai * xr + bi


def _s5_kernel(u_ref, h0re_ref, h0im_ref, perm_ref, back_ref, coef_ref, bre_ref, bim_ref, cwre_ref, cwim_ref,
               d_ref, gw_ref, gb_ref, y_ref, sre_ref, sim_ref, hre, him, ysc, car_re, car_im,
               *, chain, tt):
    group = tt if chain else SLAB * SUBLANES
    seg = group // SUBLANES
    u = u_ref[...]
    ub = _dot(perm_ref[...], u.astype(BF16)).astype(BF16)
    bw = S5_HID // S5_BLOCKS
    gw = S5_WIDTH // S5_BLOCKS
    if chain:
        @pl.when(pl.program_id(1) == 0)
        def _():
            car_re[...] = jnp.broadcast_to(h0re_ref[...], car_re.shape)
            car_im[...] = jnp.broadcast_to(h0im_ref[...], car_im.shape)

    sub = lax.broadcasted_iota(jnp.int32, (SUBLANES, 1), 0)
    def project_in(c):
        uc = ub[:, c * gw:(c + 1) * gw]
        hre[c] = _dot(uc, bre_ref[c])
        him[c] = _dot(uc, bim_ref[c])

    def project_out(c):
        ysc[:, c * gw:(c + 1) * gw] = (_dot(hre[c].astype(BF16), cwre_ref[c])
                                       - _dot(him[c].astype(BF16), cwim_ref[c]))

    def scan(c, between):
        sl = slice(c * bw, (c + 1) * bw)
        hr, hi = hre.at[c], him.at[c]
        ar, ai = coef_ref[0, :, sl], coef_ref[1, :, sl]
        if not chain:
            for g in range(tt // group):
                srows = slice(g * SUBLANES, (g + 1) * SUBLANES)
                xr, xi = h0re_ref[srows, sl], h0im_ref[srows, sl]
                for j in range(SLAB - SLAB_REAL, SLAB):
                    rows = slice(g * group + j * SUBLANES, g * group + (j + 1) * SUBLANES)
                    xr, xi = _cmul_add(ar, ai, xr, xi, hr[rows, :], hi[rows, :])
                    hr[rows, :] = xr
                    hi[rows, :] = xi
                sre_ref[srows, sl] = xr
                sim_ref[srows, sl] = xi
            between()
        else:
            xr = xi = jnp.zeros((SUBLANES, bw), F32)
            for j in range(seg):
                rows = slice(j * SUBLANES, (j + 1) * SUBLANES)
                xr, xi = _cmul_add(ar, ai, xr, xi, hr[rows, :], hi[rows, :])
                hr[rows, :] = xr
                hi[rows, :] = xi
            kr = jnp.where(sub == 0, car_re[:, sl], pltpu.roll(xr, 1, 0))
            ki = jnp.where(sub == 0, car_im[:, sl], pltpu.roll(xi, 1, 0))
            for k, s in enumerate((1, 2, 4)):
                pr, pi = coef_ref[2 + 2 * k, :, sl], coef_ref[3 + 2 * k, :, sl]
                kr, ki = _cmul_add(pr, pi, pltpu.roll(kr, s, 0), pltpu.roll(ki, s, 0), kr, ki)
            outr, outi = _cmul_add(coef_ref[2, :, sl], coef_ref[3, :, sl], kr, ki, xr, xi)
            car_re[:, sl] = jnp.broadcast_to(outr[SUBLANES - 1:SUBLANES, :], outr.shape)
            car_im[:, sl] = jnp.broadcast_to(outi[SUBLANES - 1:SUBLANES, :], outi.shape)
            between()
            for j in range(seg):
                rows = slice(j * SUBLANES, (j + 1) * SUBLANES)
                kr, ki = ar * kr - ai * ki, ar * ki + ai * kr
                hr[rows, :] += kr
                hi[rows, :] += ki

    for c in range(S5_BLOCKS + 2):
        emit_out = (lambda c=c: project_out(c - 2)) if c >= 2 else (lambda: None)
        if c < S5_BLOCKS:
            project_in(c)
        if 1 <= c <= S5_BLOCKS:
            scan(c - 1, emit_out)
        else:
            emit_out()

    if chain:
        @pl.when(pl.program_id(1) == pl.num_programs(1) - 1)
        def _():
            sre_ref[...] = car_re[0:1, :]
            sim_ref[...] = car_im[0:1, :]

    y = _dot_split(back_ref[...], ysc[...])
    z = _gelu_tanh(y + d_ref[...] * u)
    gl = _dot(z.astype(BF16), gw_ref[...]) + gb_ref[...]
    y_ref[...] = z * jax.nn.sigmoid(gl)


def _s5_mixer(proj, h0re, h0im, sw, *, chain, nseq, tt=S5_TILE):
    rows = proj.shape[0]
    const = lambda shape: pl.BlockSpec(shape, lambda *_: (0,) * len(shape))
    if chain:
        assert tt == S5_SEG * SUBLANES
        nt = rows // nseq // tt
        grid = (nseq, nt)
        u_spec = pl.BlockSpec((tt, S5_WIDTH), lambda b, t: (b * nt + t, 0))
        st_spec = pl.BlockSpec((None, 1, S5_HID), lambda b, t: (b, 0, 0))
        st_shape = jax.ShapeDtypeStruct((nseq, 1, S5_HID), F32)
        sem = ("parallel", "arbitrary")
    else:
        grid = (rows // tt,)
        u_spec = pl.BlockSpec((tt, S5_WIDTH), lambda i: (i, 0))
        st_spec = pl.BlockSpec((tt // SLAB, S5_HID), lambda i: (i, 0))
        st_shape = jax.ShapeDtypeStruct((rows // SLAB, S5_HID), F32)
        sem = ("parallel",)
    y_spec = u_spec
    bw = S5_HID // S5_BLOCKS
    gw = S5_WIDTH // S5_BLOCKS
    perm = _scan_layout(tt, tt if chain else SLAB * SUBLANES)
    return pl.pallas_call(
        functools.partial(_s5_kernel, chain=chain, tt=tt),
        grid=grid,
        in_specs=[u_spec, st_spec, st_spec,
                  const((tt, tt)), const((tt, tt)), const((8, SUBLANES, S5_HID)),
                  const((S5_BLOCKS, gw, bw)), const((S5_BLOCKS, gw, bw)),
                  const((S5_BLOCKS, bw, gw)), const((S5_BLOCKS, bw, gw)),
                  const((1, S5_WIDTH)), const((S5_WIDTH, S5_WIDTH)), const((1, S5_WIDTH))],
        out_specs=[y_spec, st_spec, st_spec],
        out_shape=[jax.ShapeDtypeStruct((rows, S5_WIDTH), F32), st_shape, st_shape],
        scratch_shapes=[pltpu.VMEM((S5_BLOCKS, tt, bw), F32), pltpu.VMEM((S5_BLOCKS, tt, bw), F32),
                        pltpu.VMEM((tt, S5_WIDTH), F32),
                        pltpu.VMEM((SUBLANES, S5_HID), F32), pltpu.VMEM((SUBLANES, S5_HID), F32)],
        compiler_params=_params(*sem),
        name="s5_chain" if chain else "s5_slab",
    )(proj, h0re, h0im, perm, perm.T, sw["coef"], sw["b_re"], sw["b_im"], sw["c_re"], sw["c_im"],
      sw["d"], sw["glu_w"], sw["glu_b"])


def _softplus(x):
    return jnp.maximum(x, 0.0) + jnp.log1p(jnp.exp(-jnp.abs(x)))


def _segment_masks(seg):
    ri = lax.broadcasted_iota(jnp.int32, (GDN_TILE, GDN_TILE), 0)
    ci = lax.broadcasted_iota(jnp.int32, (GDN_TILE, GDN_TILE), 1)
    same = (ri // seg) == (ci // seg)
    causal = (ri >= ci) & same
    strict = (ri > ci) & same
    return same, causal, strict


def _gate_matrices(seg):
    i = jnp.arange(GDN_TILE)
    same = (i[:, None] // seg) == (i[None, :] // seg)
    stack = jnp.concatenate([i[:, None] == i[None, :], (i[:, None] >= i[None, :]) & same, same], axis=0)
    return stack.astype(BF16), ((i[:, None] <= i[None, :]) & same).astype(BF16)


def _gdn_gates(bdt, alog_col, dtb_col, stack, upper, seg, slab):
    same, causal, strict = _segment_masks(seg)
    bt = jax.nn.sigmoid(bdt[:GDN_HEADS, :])
    gt = -jnp.exp(alog_col) * _softplus(bdt[GDN_HEADS:, :] + dtb_col)
    if slab:
        creal = (lax.broadcasted_iota(jnp.int32, (1, GDN_TILE), 1) % SLAB) >= SLAB - SLAB_REAL
        bt = jnp.where(creal, bt, 0.0)
        gt = jnp.where(creal, gt, 0.0)
    gates = jnp.concatenate([bt, gt, jnp.zeros((GDN_TILE - 2 * GDN_HEADS, GDN_TILE), F32)], axis=0)
    cols = _dot_split(stack, gates, NT_DIMS)
    beta, gc, gl = cols[:GDN_TILE], cols[GDN_TILE:2 * GDN_TILE], cols[2 * GDN_TILE:]
    gct = _dot_split(gt, upper)
    return beta, gc, gl, gct, causal, strict


def _merge_masks(top):
    ri = lax.broadcasted_iota(jnp.int32, (GDN_TILE, GDN_TILE), 0)
    ci = lax.broadcasted_iota(jnp.int32, (GDN_TILE, GDN_TILE), 1)
    masks = []
    s = 1
    while s < top:
        masks.append(((ri // (2 * s)) == (ci // (2 * s))) & ((ri // s) != (ci // s)))
        s *= 2
    return masks


def _unit_lower_inverse(ms, masks):
    es = [-jnp.where(masks[0], m, 0.0) for m in ms]
    for mask in masks[1:]:
        cs = [jnp.where(mask, m, 0.0) for m in ms]
        ebs = [e.astype(BF16) for e in es]
        xs = [c + _dot(eb, c.astype(BF16)) for c, eb in zip(cs, ebs)]
        es = [e - (x + _dot(x.astype(BF16), eb)) for e, x, eb in zip(es, xs, ebs)]
    return es


def _gdn_intra(qs, ks, vs, betas, gcols, grows, glcols, causal, strict, masks):
    n = range(len(qs))
    qn = [q * lax.rsqrt(jnp.sum(q * q, axis=-1, keepdims=True) + NORM_EPS) * (GDN_HEAD_DIM ** -0.5) for q in qs]
    kn = [k * lax.rsqrt(jnp.sum(k * k, axis=-1, keepdims=True) + NORM_EPS) for k in ks]
    decay = [jnp.exp(jnp.where(causal, gcols[i] - grows[i], -jnp.inf)) for i in n]
    kb = [kn[i] * betas[i] for i in n]
    knb = [k.astype(BF16) for k in kn]
    kk = [lax.dot_general(kb[i].astype(BF16), knb[i], NT_DIMS, preferred_element_type=F32) for i in n]
    ms = [kk[i] * jnp.where(strict, decay[i], 0.0) for i in n]
    eg = [jnp.exp(g) for g in gcols]
    attn = [(lax.dot_general(qn[i].astype(BF16), knb[i], NT_DIMS, preferred_element_type=F32) * decay[i]).astype(BF16)
            for i in n]
    qg = [qn[i] * eg[i] for i in n]
    kdt = [(kn[i] * jnp.exp(glcols[i] - gcols[i])).T.astype(BF16) for i in n]
    rhs = [jnp.concatenate([vs[i] * betas[i], kb[i] * eg[i]], axis=1) for i in n]
    ys = _unit_lower_inverse(ms, masks)
    uw = [rhs[i] + _dot(ys[i].astype(BF16), rhs[i].astype(BF16)) for i in n]
    return [(uw[i][:, :GDN_HEAD_DIM], uw[i][:, GDN_HEAD_DIM:], attn[i], qg[i], kdt[i]) for i in n]


def _gdn_out(o, gate, norm_w):
    o = o * lax.rsqrt(jnp.mean(o * o, axis=-1, keepdims=True) + NORM_EPS) * norm_w
    return o * _silu(gate)


def _conv4(win_ref, cw, width_slice):
    x = win_ref[:, width_slice]
    x1 = pltpu.roll(x, 1, 0)
    z = cw[1:2, :] * x + cw[0:1, :] * x1
    out = cw[3:4, :] * x + cw[2:3, :] * x1 + pltpu.roll(z, 2, 0)
    return out[SUBLANES:, :]


def _gdn_chain_kernel(q_ref, k_ref, v_ref, gate_ref, bdt_ref, buf_ref, s0_ref, cw_ref,
                      alc_ref, dbc_ref, nw_ref, gstack_ref, gupper_ref,
                      y_ref, sout_ref, bufout_ref, win, s_sc):
    t = pl.program_id(1)
    w3 = 3 * GDN_WIDTH

    @pl.when(t == 0)
    def _():
        win[0:SUBLANES, :] = jnp.zeros((SUBLANES, w3), F32)
        win[SUBLANES - 3:SUBLANES, :] = buf_ref[...]
        s_sc[...] = s0_ref[...]

    win[SUBLANES:, 0:GDN_WIDTH] = q_ref[...]
    win[SUBLANES:, GDN_WIDTH:2 * GDN_WIDTH] = k_ref[...]
    win[SUBLANES:, 2 * GDN_WIDTH:] = v_ref[...]

    beta, gc, gl, gct, causal, strict = _gdn_gates(bdt_ref[...], alc_ref[...], dbc_ref[...], gstack_ref[...],
                                                   gupper_ref[...], GDN_TILE, False)
    masks = _merge_masks(GDN_TILE)
    for h0 in range(0, GDN_HEADS, GDN_CHAIN_GROUP):
        heads = range(h0, h0 + GDN_CHAIN_GROUP)
        idx = range(GDN_CHAIN_GROUP)
        qkv = [[], [], []]
        for part in range(3):
            for h in heads:
                cs = slice(part * GDN_WIDTH + h * GDN_HEAD_DIM, part * GDN_WIDTH + (h + 1) * GDN_HEAD_DIM)
                qkv[part].append(_silu(_conv4(win, cw_ref[:, cs], cs)))
        gcols = [gc[:, GDN_HEADS + h:GDN_HEADS + h + 1] for h in heads]
        glcols = [gl[:, GDN_HEADS + h:GDN_HEADS + h + 1] for h in heads]
        intra = _gdn_intra(qkv[0], qkv[1], qkv[2], [beta[:, h:h + 1] for h in heads], gcols,
                           [gct[h:h + 1, :] for h in heads], glcols, causal, strict, masks)
        s_old = [s_sc[h] for h in heads]
        a = [_dot(jnp.concatenate([intra[i][1], intra[i][3]], axis=0).astype(BF16), s_old[i].astype(BF16))
             for i in idx]
        vnb = [(intra[i][0] - a[i][:GDN_TILE]).astype(BF16) for i in idx]
        o = [a[i][GDN_TILE:] + _dot(intra[i][2], vnb[i]) for i in idx]
        for i, h in zip(idx, heads):
            s_sc[h] = s_old[i] * jnp.exp(glcols[i][0:1, :]) + _dot(intra[i][4], vnb[i])
        for i, h in zip(idx, heads):
            hs = slice(h * GDN_HEAD_DIM, (h + 1) * GDN_HEAD_DIM)
            y_ref[:, hs] = _gdn_out(o[i], gate_ref[:, hs], nw_ref[...])

    win[0:SUBLANES, :] = win[GDN_TILE:GDN_TILE + SUBLANES, :]

    @pl.when(t == pl.num_programs(1) - 1)
    def _():
        sout_ref[...] = s_sc[...]
        bufout_ref[...] = win[GDN_TILE + SUBLANES - 3:GDN_TILE + SUBLANES, :]


def _gdn_chain(proj, bdt, conv_buf, s0, gw, *, nseq):
    rows = proj.shape[0]
    nt = rows // nseq // GDN_TILE
    w3 = 3 * GDN_WIDTH
    const = lambda shape: pl.BlockSpec(shape, lambda *_: (0,) * len(shape))
    col = lambda cb: pl.BlockSpec((GDN_TILE, GDN_WIDTH), lambda b, t, cb=cb: (b * nt + t, cb))
    return pl.pallas_call(
        _gdn_chain_kernel,
        grid=(nseq, nt),
        in_specs=[col(1), col(2), col(3), col(4),
                  pl.BlockSpec((2 * GDN_HEADS, GDN_TILE), lambda b, t: (0, b * nt + t)),
                  pl.BlockSpec((None, GDN_CONV - 1, w3), lambda b, t: (b, 0, 0)),
                  pl.BlockSpec((None, GDN_HEADS, GDN_HEAD_DIM, GDN_HEAD_DIM), lambda b, t: (b, 0, 0, 0)),
                  const((GDN_CONV, w3)), const((GDN_HEADS, 1)), const((GDN_HEADS, 1)), const((1, GDN_HEAD_DIM)),
                  const((3 * GDN_TILE, GDN_TILE)), const((GDN_TILE, GDN_TILE))],
        out_specs=[pl.BlockSpec((GDN_TILE, GDN_WIDTH), lambda b, t: (b * nt + t, 0)),
                   pl.BlockSpec((None, GDN_HEADS, GDN_HEAD_DIM, GDN_HEAD_DIM), lambda b, t: (b, 0, 0, 0)),
                   pl.BlockSpec((None, GDN_CONV - 1, w3), lambda b, t: (b, 0, 0))],
        out_shape=[jax.ShapeDtypeStruct((rows, GDN_WIDTH), F32),
                   jax.ShapeDtypeStruct((nseq, GDN_HEADS, GDN_HEAD_DIM, GDN_HEAD_DIM), F32),
                   jax.ShapeDtypeStruct((nseq, GDN_CONV - 1, w3), F32)],
        scratch_shapes=[pltpu.VMEM((GDN_TILE + SUBLANES, w3), F32),
                        pltpu.VMEM((GDN_HEADS, GDN_HEAD_DIM, GDN_HEAD_DIM), F32)],
        compiler_params=_params("parallel", "arbitrary"),
        name="gdn_chain",
    )(proj, proj, proj, proj, bdt, conv_buf, s0, gw["conv_w"], gw["alog_col"], gw["dtb_col"], gw["norm_w"],
      *_gate_matrices(GDN_TILE))


def _gdn_slab_kernel(q_ref, k_ref, v_ref, gate_ref, bdt_ref, bq_ref, bk_ref, bv_ref, s0_ref,
                     cwq_ref, cwk_ref, cwv_ref, alc_ref, dbc_ref, nw_ref, gstack_ref, gupper_ref,
                     y_ref, sout_ref, oq_ref, ok_ref, ov_ref,
                     win):
    hp = s0_ref.shape[1]
    hw = hp * GDN_HEAD_DIM
    heads = range(hp)
    head0 = pl.program_id(1) * hp
    nslab = GDN_TILE // SLAB
    first = (pl.program_id(0) % (bq_ref.shape[1] // nslab)) * nslab
    real = (lax.broadcasted_iota(jnp.int32, (GDN_TILE, 1), 0) % SLAB) >= SLAB - SLAB_REAL
    qkv = []
    taps = range(GDN_CONV - 1)
    place = [_slab_row_selector(GDN_TILE, bq_ref.shape[1], 1 + i, first, transpose=True) for i in taps]
    take = [_slab_row_selector(GDN_TILE, nslab, SLAB - 3 + i, 0, transpose=False) for i in taps]
    for part, (x_ref, b_ref, cw_ref, o_ref) in enumerate(
            ((q_ref, bq_ref, cwq_ref, oq_ref), (k_ref, bk_ref, cwk_ref, ok_ref), (v_ref, bv_ref, cwv_ref, ov_ref))):
        cs = slice(part * hw, (part + 1) * hw)
        x = x_ref[...]
        for i in taps:
            x = x + _dot_split(place[i], b_ref[i])
        win[0:SUBLANES, cs] = jnp.zeros((SUBLANES, hw), F32)
        win[SUBLANES:, cs] = x
        conv = jnp.where(real, _silu(_conv4(win, cw_ref[...], cs)), 0.0)
        qkv.append([conv[:, h * GDN_HEAD_DIM:(h + 1) * GDN_HEAD_DIM] for h in heads])
        pieces = _split3(x)
        for i in taps:
            o_ref[i] = sum(_dot(take[i], p) for p in pieces)

    beta, gc, gl, gct, causal, strict = _gdn_gates(bdt_ref[...], alc_ref[...], dbc_ref[...], gstack_ref[...],
                                                   gupper_ref[...], SLAB, True)
    lane = lax.broadcasted_iota(jnp.int32, (1, LANES), 1)
    sub = lax.broadcasted_iota(jnp.int32, (SUBLANES, 1), 0)
    pick = lambda a, idx: jnp.sum(jnp.where(lane == idx, a, 0.0), axis=1, keepdims=True)
    gcols = [pick(gc, GDN_HEADS + head0 + h) for h in heads]
    glcols = [pick(gl, GDN_HEADS + head0 + h) for h in heads]
    intra = _gdn_intra(qkv[0], qkv[1], qkv[2], [pick(beta, head0 + h) for h in heads], gcols,
                       [jnp.sum(jnp.where(sub == head0 + h, gct, 0.0), axis=0, keepdims=True) for h in heads],
                       glcols, causal, strict, _merge_masks(SLAB_REAL))
    for h in heads:
        u, wk, attn, qg, kdt = intra[h]
        res = []
        for i in range(nslab):
            rows = slice(i * SLAB, (i + 1) * SLAB)
            lhs = jnp.concatenate([wk[rows], qg[rows]], axis=0).astype(BF16)
            res.append(_dot(lhs, s0_ref[i, h].astype(BF16)))
        vnb = (u - jnp.concatenate([r[:SLAB] for r in res], axis=0)).astype(BF16)
        o = jnp.concatenate([r[SLAB:] for r in res], axis=0) + _dot(attn, vnb)
        hs = slice(h * GDN_HEAD_DIM, (h + 1) * GDN_HEAD_DIM)
        y_ref[:, hs] = _gdn_out(o, gate_ref[:, hs], nw_ref[...])
        egl = jnp.exp(glcols[h])
        for i in range(nslab):
            in_slab = (lane // SLAB) == i
            upd = _dot(jnp.where(in_slab, kdt, jnp.zeros_like(kdt)), vnb)
            sout_ref[i, h] = s0_ref[i, h] * egl[i * SLAB:i * SLAB + 1, :] + upd


def _gdn_slab(proj, bdt, conv_buf, s0, gw, *, hp=GDN_SLAB_HEADS):
    rows = proj.shape[0]
    nb = rows // SLAB
    nslab = GDN_TILE // SLAB
    hw = hp * GDN_HEAD_DIM
    hb = GDN_WIDTH // hw
    const = lambda shape: pl.BlockSpec(shape, lambda *_: (0,) * len(shape))
    col = lambda g: pl.BlockSpec((GDN_TILE, hw), lambda i, h, g=g: (i, g * hb + h))
    per = LANES // nslab
    buf = lambda g: pl.BlockSpec((GDN_CONV - 1, LANES, hw), lambda i, h, g=g: (0, i // per, g * hb + h))
    cwb = lambda g: pl.BlockSpec((GDN_CONV, hw), lambda i, h, g=g: (0, g * hb + h))
    st = pl.BlockSpec((nslab, hp, GDN_HEAD_DIM, GDN_HEAD_DIM), lambda i, h: (i, h, 0, 0))
    obuf = pl.BlockSpec((GDN_CONV - 1, nslab, hw), lambda i, h: (0, i, h))
    return pl.pallas_call(
        _gdn_slab_kernel,
        grid=(rows // GDN_TILE, GDN_HEADS // hp),
        in_specs=[col(1), col(2), col(3), col(4),
                  pl.BlockSpec((2 * GDN_HEADS, GDN_TILE), lambda i, h: (0, i)),
                  buf(0), buf(1), buf(2), st, cwb(0), cwb(1), cwb(2),
                  const((GDN_HEADS, 1)), const((GDN_HEADS, 1)), const((1, GDN_HEAD_DIM)),
                  const((3 * GDN_TILE, GDN_TILE)), const((GDN_TILE, GDN_TILE))],
        out_specs=[pl.BlockSpec((GDN_TILE, hw), lambda i, h: (i, h)), st, obuf, obuf, obuf],
        out_shape=[jax.ShapeDtypeStruct((rows, GDN_WIDTH), F32),
                   jax.ShapeDtypeStruct(s0.shape, F32)]
                  + [jax.ShapeDtypeStruct((GDN_CONV - 1, nb, GDN_WIDTH), F32)] * 3,
        scratch_shapes=[pltpu.VMEM((GDN_TILE + SUBLANES, 3 * hw), F32)],
        compiler_params=_params("parallel", "arbitrary"),
        name="gdn_slab",
    )(proj, proj, proj, proj, bdt, conv_buf, conv_buf, conv_buf, s0,
      gw["conv_w"], gw["conv_w"], gw["conv_w"], gw["alog_col"], gw["dtb_col"], gw["norm_w"],
      *_gate_matrices(SLAB))


def _block_diag(w):
    per = S5_GROUPS // S5_BLOCKS
    g, a, b = w.shape
    w = w.reshape(S5_BLOCKS, per, a, b)
    eye = jnp.eye(per, dtype=w.dtype)
    return jnp.einsum("jgab,gk->jgakb", w, eye).reshape(S5_BLOCKS, per * a, per * b)


def _layer_weights(l, ln1_g, ln1_b, ffn1_w_in, ffn1_w_out, w_mix_in, s5_lambda_re, s5_lambda_im, s5_log_dt,
                   s5_b_re, s5_b_im, s5_c_re, s5_c_im, s5_d, s5_glu_w, s5_glu_b, gdn_conv_w, gdn_a_log,
                   gdn_dt_bias, gdn_norm_w, w_mix_out, ln2_g, ln2_b, ffn2_w_in, ffn2_w_out, ln3_g, ln3_b):
    row = lambda v: v[l].reshape(1, -1).astype(F32)
    w = {
        "ln1": (row(ln1_g), row(ln1_b)), "ln2": (row(ln2_g), row(ln2_b)), "ln3": (row(ln3_g), row(ln3_b)),
        "ffn1": (ffn1_w_in[l], ffn1_w_out[l]),
        "ffn2": (ffn2_w_in[l], ffn2_w_out[l]),
        "mix_in": (w_mix_in.astype(BF16), l),
        "mix_out": w_mix_out[l].astype(BF16),
    }
    coef, bb_re, bb_im = _s5_disc(
        s5_lambda_re[l].reshape(1, S5_HID).astype(F32), s5_lambda_im[l].reshape(1, S5_HID).astype(F32),
        jnp.repeat(s5_log_dt[l], S5_STATE).reshape(1, S5_HID).astype(F32),
        _block_diag(jnp.swapaxes(s5_b_re[l], 1, 2).astype(F32)),
        _block_diag(jnp.swapaxes(s5_b_im[l], 1, 2).astype(F32)))
    w["s5"] = {
        "coef": coef, "b_re": bb_re, "b_im": bb_im,
        "c_re": _block_diag(jnp.swapaxes(s5_c_re[l], 1, 2)).astype(BF16),
        "c_im": _block_diag(jnp.swapaxes(s5_c_im[l], 1, 2)).astype(BF16),
        "d": row(s5_d), "glu_w": s5_glu_w[l].astype(BF16), "glu_b": row(s5_glu_b),
    }
    w["gdn"] = {
        "conv_w": gdn_conv_w[l].astype(F32),
        "alog_col": gdn_a_log[l].reshape(GDN_HEADS, 1).astype(F32),
        "dtb_col": gdn_dt_bias[l].reshape(GDN_HEADS, 1).astype(F32),
        "norm_w": row(gdn_norm_w),
    }
    return w


def _prompt_layer(x, w, nseq):
    x = _ffn_ln(x, *w["ffn1"], *w["ln1"])
    proj, bdt = _mixin(x, *w["mix_in"])
    z_s5 = jnp.zeros((nseq, 1, S5_HID), F32)
    y_s5, n_re, n_im = _s5_mixer(proj, z_s5, z_s5, w["s5"], chain=True, nseq=nseq)
    z_gdn = jnp.zeros((nseq, GDN_HEADS, GDN_HEAD_DIM, GDN_HEAD_DIM), F32)
    z_buf = jnp.zeros((nseq, GDN_CONV - 1, 3 * GDN_WIDTH), F32)
    y_gdn, n_s, n_buf = _gdn_chain(proj, bdt, z_buf, z_gdn, w["gdn"], nseq=nseq)
    x = _mixout_ln(y_s5, y_gdn, x, w["mix_out"], *w["ln2"])
    x = _ffn_ln(x, *w["ffn2"], *w["ln3"])
    shape = (nseq, S5_GROUPS, S5_STATE)
    return x, n_re.reshape(shape), n_im.reshape(shape), n_s, n_buf


def _sample_layer(x, s5_re, s5_im, gdn_s, conv_buf, w, nb, t):
    x = _ffn_ln(x, *w["ffn1"], *w["ln1"])
    xs = jnp.pad(x.reshape(nb, t, D_MODEL), ((0, 0), (SLAB - t, 0), (0, 0))).reshape(nb * SLAB, D_MODEL)
    proj, bdt = _mixin(xs, *w["mix_in"])
    y_s5, n_re, n_im = _s5_mixer(proj, s5_re.reshape(nb, S5_HID).astype(F32),
                                 s5_im.reshape(nb, S5_HID).astype(F32), w["s5"], chain=False, nseq=nb)
    y_gdn, n_s, bq, bk, bv = _gdn_slab(proj, bdt, jnp.swapaxes(conv_buf.astype(F32), 0, 1),
                                       gdn_s.astype(F32), w["gdn"])
    xs = _mixout_ln(y_s5, y_gdn, xs, w["mix_out"], *w["ln2"])
    x = xs.reshape(nb, SLAB, D_MODEL)[:, SLAB - t:].reshape(nb * t, D_MODEL)
    x = _ffn_ln(x, *w["ffn2"], *w["ln3"])
    shape = (nb, S5_GROUPS, S5_STATE)
    n_buf = jnp.swapaxes(jnp.concatenate([bq, bk, bv], axis=-1), 0, 1)
    return x, n_re.reshape(shape), n_im.reshape(shape), n_s, n_buf


def kernel(x_prompt, x_sample, state_s5_re, state_s5_im, state_gdn, state_conv, ln1_g, ln1_b, ffn1_w_in, ffn1_w_out, w_mix_in, s5_lambda_re, s5_lambda_im, s5_log_dt, s5_b_re, s5_b_im, s5_c_re, s5_c_im, s5_d, s5_glu_w, s5_glu_b, gdn_conv_w, gdn_a_log, gdn_dt_bias, gdn_norm_w, w_mix_out, ln2_g, ln2_b, ffn2_w_in, ffn2_w_out, ln3_g, ln3_b):
    bp, tp, _ = x_prompt.shape
    bs, ts, _ = x_sample.shape
    assert ts == SLAB_REAL and tp % S5_TILE == 0 and (bs * SLAB) % S5_TILE == 0
    depth = ln1_g.shape[0]
    yp = x_prompt.astype(F32).reshape(bp * tp, D_MODEL)
    ys = x_sample.astype(F32).reshape(bs * ts, D_MODEL)
    outs = [[] for _ in range(8)]
    for l in range(depth):
        w = _layer_weights(l, ln1_g, ln1_b, ffn1_w_in, ffn1_w_out, w_mix_in, s5_lambda_re, s5_lambda_im,
                           s5_log_dt, s5_b_re, s5_b_im, s5_c_re, s5_c_im, s5_d, s5_glu_w, s5_glu_b,
                           gdn_conv_w, gdn_a_log, gdn_dt_bias, gdn_norm_w, w_mix_out, ln2_g, ln2_b,
                           ffn2_w_in, ffn2_w_out, ln3_g, ln3_b)
        yp, *p_state = _prompt_layer(yp, w, bp)
        ys, *s_state = _sample_layer(ys, state_s5_re[l], state_s5_im[l], state_gdn[l], state_conv[l], w, bs, ts)
        for acc, val in zip(outs, p_state + s_state):
            acc.append(val)
    return (yp.reshape(x_prompt.shape).astype(x_prompt.dtype), ys.reshape(x_sample.shape).astype(x_sample.dtype),
            *(o[0][None] if depth == 1 else jnp.stack(o) for o in outs))
```

```python
import functools
import math

import jax
import jax.numpy as jnp
from jax import lax
from jax.experimental import pallas as pl
from jax.experimental.pallas import tpu as pltpu

F32 = jnp.float32
BF16 = jnp.bfloat16

D_MODEL = 2048
S5_WIDTH = 1024
S5_GROUP = 16
S5_GROUPS = 64
S5_STATE = 64
S5_HID = S5_GROUPS * S5_STATE
GDN_WIDTH = 1024
GDN_HEAD_DIM = 128
GDN_HEADS = 8
GDN_CONV = 4
D_FF = 5632
MIX_MAIN = 5120
DEEP_ALPHA = 2.0 ** 0.25
LN_EPS = 1e-5
NORM_EPS = 1e-6

SUBLANES = 8
LANES = 128
SLAB = 8
SLAB_REAL = 4
GDN_TILE = 128
GDN_SLAB_HEADS = 8
GDN_CHAIN_CHUNKS = 2
FFN_TM = 1024
FFN_TF = 256
FFN_CHUNK = 512
S5_TILE = 256
S5_SEG = S5_TILE // SUBLANES
S5_BLOCKS = 8
VMEM_LIMIT = 56 * 1024 * 1024

NT_DIMS = (((1,), (1,)), ((), ()))


def _dot(a, b, **kw):
    return jnp.dot(a, b, preferred_element_type=F32, **kw)


def _silu(x):
    return x * jax.nn.sigmoid(x)


def _layer_norm(y, g, b):
    mu = jnp.mean(y, axis=-1, keepdims=True)
    d = y - mu
    var = jnp.mean(d * d, axis=-1, keepdims=True)
    return d * lax.rsqrt(var + LN_EPS) * g + b


def _slab_row_selector(rows, nseq, slab_row, first, *, transpose):
    shape = (rows, nseq) if transpose else (nseq, rows)
    r = lax.broadcasted_iota(jnp.int32, shape, 0 if transpose else 1)
    b = lax.broadcasted_iota(jnp.int32, shape, 1 if transpose else 0)
    return _as_bf16(r == SLAB * (b - first) + slab_row)


def _as_bf16(mask):
    return mask.astype(F32).astype(BF16)


def _split3(x):
    pieces = []
    for _ in range(3):
        pieces.append(x.astype(BF16))
        x = x - pieces[-1].astype(F32)
    return pieces


def _dot_split(a, b, dims=None):
    f32_is_lhs = a.dtype == F32
    acc = None
    for piece in _split3(a if f32_is_lhs else b):
        lhs, rhs = (piece, b) if f32_is_lhs else (a, piece)
        d = _dot(lhs, rhs) if dims is None else lax.dot_general(lhs, rhs, dims, preferred_element_type=F32)
        acc = d if acc is None else acc + d
    return acc


def _params(*sem):
    return pltpu.CompilerParams(dimension_semantics=sem, vmem_limit_bytes=VMEM_LIMIT)


def _ffn_kernel(x_ref, wg_ref, wu_ref, wo_ref, g_ref, b_ref, o_ref, xb_ref):
    j = pl.program_id(1)

    @pl.when(j == 0)
    def _():
        o_ref[...] = jnp.zeros_like(o_ref)
        xb_ref[...] = x_ref[...].astype(BF16)

    xb = xb_ref[...]
    gate = _dot(xb, wg_ref[...].astype(BF16))
    up = _dot(xb, wu_ref[...].astype(BF16))
    h = (_silu(gate) * up).astype(BF16)
    for c in range(0, D_MODEL, FFN_CHUNK):
        o_ref[:, c:c + FFN_CHUNK] += _dot(h, wo_ref[:, c:c + FFN_CHUNK].astype(BF16))

    @pl.when(j == pl.num_programs(1) - 1)
    def _():
        for r in range(0, o_ref.shape[0], FFN_CHUNK // 2):
            rows = slice(r, r + FFN_CHUNK // 2)
            y = DEEP_ALPHA * x_ref[rows, :] + 0.5 * o_ref[rows, :]
            o_ref[rows, :] = _layer_norm(y, g_ref[...], b_ref[...])


def _ffn_ln(x, w_in, w_out, g, b):
    n = x.shape[0]
    tm = math.gcd(n, FFN_TM)
    tf = FFN_TF * FFN_TM // tm
    nff = D_FF // tf
    return pl.pallas_call(
        _ffn_kernel,
        grid=(n // tm, nff),
        in_specs=[
            pl.BlockSpec((tm, D_MODEL), lambda i, j: (i, 0)),
            pl.BlockSpec((D_MODEL, tf), lambda i, j: (0, j)),
            pl.BlockSpec((D_MODEL, tf), lambda i, j: (0, j + nff)),
            pl.BlockSpec((tf, D_MODEL), lambda i, j: (j, 0)),
            pl.BlockSpec((1, D_MODEL), lambda i, j: (0, 0)),
            pl.BlockSpec((1, D_MODEL), lambda i, j: (0, 0)),
        ],
        out_specs=pl.BlockSpec((tm, D_MODEL), lambda i, j: (i, 0)),
        out_shape=jax.ShapeDtypeStruct((n, D_MODEL), F32),
        scratch_shapes=[pltpu.VMEM((tm, D_MODEL), BF16)],
        compiler_params=_params("parallel", "arbitrary"),
        name="ffn_ln",
    )(x, w_in, w_in, w_out, g, b)


def _mixin_kernel(x_ref, w_ref, wt_ref, o_ref, ot_ref, xb_ref):
    @pl.when(pl.program_id(1) == 0)
    def _():
        xb_ref[...] = x_ref[...].astype(BF16)
        ncols = ot_ref.shape[0]
        lane = lax.broadcasted_iota(jnp.int32, (1, LANES), 1)
        wt = jnp.where(lane < ncols, wt_ref[...], 0.0).astype(BF16)
        ot_ref[...] = _dot(xb_ref[...], wt).T[:ncols, :]

    o_ref[...] = _dot(xb_ref[...], w_ref[...].astype(BF16))


def _mixin(x, w, layer, *, tm=1024, tn=1024):
    n = x.shape[0]
    return pl.pallas_call(
        _mixin_kernel,
        grid=(n // tm, MIX_MAIN // tn),
        in_specs=[pl.BlockSpec((tm, D_MODEL), lambda i, j: (i, 0)),
                  pl.BlockSpec((None, D_MODEL, tn), lambda i, j: (layer, 0, j)),
                  pl.BlockSpec((None, D_MODEL, LANES), lambda i, j: (layer, 0, MIX_MAIN // LANES))],
        out_specs=[pl.BlockSpec((tm, tn), lambda i, j: (i, j)),
                   pl.BlockSpec((2 * GDN_HEADS, tm), lambda i, j: (0, i))],
        out_shape=[jax.ShapeDtypeStruct((n, MIX_MAIN), F32),
                   jax.ShapeDtypeStruct((2 * GDN_HEADS, n), F32)],
        scratch_shapes=[pltpu.VMEM((tm, D_MODEL), BF16)],
        compiler_params=_params("parallel", "arbitrary"),
        name="mix_in",
    )(x, w, w)


def _mixout_kernel(ya_ref, yb_ref, x_ref, w_ref, g_ref, b_ref, o_ref):
    mix = (_dot(ya_ref[...].astype(BF16), w_ref[0:S5_WIDTH, :])
           + _dot(yb_ref[...].astype(BF16), w_ref[S5_WIDTH:, :]))
    o_ref[...] = _layer_norm(DEEP_ALPHA * x_ref[...] + mix, g_ref[...], b_ref[...])


def _mixout_ln(ya, yb, x, w, g, b, *, tm=512):
    n = x.shape[0]
    return pl.pallas_call(
        _mixout_kernel,
        grid=(n // tm,),
        in_specs=[pl.BlockSpec((tm, S5_WIDTH), lambda i: (i, 0)),
                  pl.BlockSpec((tm, GDN_WIDTH), lambda i: (i, 0)),
                  pl.BlockSpec((tm, D_MODEL), lambda i: (i, 0)),
                  pl.BlockSpec((D_MODEL, D_MODEL), lambda i: (0, 0)),
                  pl.BlockSpec((1, D_MODEL), lambda i: (0, 0)),
                  pl.BlockSpec((1, D_MODEL), lambda i: (0, 0))],
        out_specs=pl.BlockSpec((tm, D_MODEL), lambda i: (i, 0)),
        out_shape=jax.ShapeDtypeStruct((n, D_MODEL), F32),
        compiler_params=_params("parallel"),
        name="mix_out_ln",
    )(ya, yb, x, w, g, b)


def _s5_disc_kernel(lre_ref, lim_ref, ldt_ref, bre_ref, bim_ref, coef_ref, bbre_ref, bbim_ref):
    lr, li = lre_ref[...], lim_ref[...]
    dt = jnp.exp(ldt_ref[...])
    mag = jnp.exp(lr * dt)
    ar = mag * jnp.cos(li * dt)
    ai = mag * jnp.sin(li * dt)
    nr, ni = ar - 1.0, ai
    den = lr * lr + li * li
    c_re = (nr * lr + ni * li) / den
    c_im = (ni * lr - nr * li) / den
    bw = S5_HID // S5_BLOCKS
    for j in range(S5_BLOCKS):
        cr, ci = c_re[:, j * bw:(j + 1) * bw], c_im[:, j * bw:(j + 1) * bw]
        bbre_ref[j] = (cr * bre_ref[j] - ci * bim_ref[j]).astype(BF16)
        bbim_ref[j] = (cr * bim_ref[j] + ci * bre_ref[j]).astype(BF16)

    def cmul(x, y):
        return x[0] * y[0] - x[1] * y[1], x[0] * y[1] + x[1] * y[0]

    width = lr.shape[-1]
    p = (ar, ai)
    for _ in range(S5_SEG - 1):
        p = cmul(p, (ar, ai))
    coef_ref[...] = jnp.zeros_like(coef_ref)
    for part, v in enumerate((ar, ai)):
        coef_ref[part] = jnp.broadcast_to(v, (SUBLANES, width))
    for k, s in enumerate((1, 2, 4)):
        for part in range(2):
            coef_ref[2 + 2 * k + part, s:SUBLANES, :] = jnp.broadcast_to(p[part], (SUBLANES - s, width))
        p = cmul(p, p)


def _s5_disc(lre, lim, ldt, b_re, b_im):
    return pl.pallas_call(
        _s5_disc_kernel,
        out_shape=[jax.ShapeDtypeStruct((8, SUBLANES, S5_HID), F32),
                   jax.ShapeDtypeStruct(b_re.shape, BF16), jax.ShapeDtypeStruct(b_im.shape, BF16)],
        name="s5_disc",
    )(lre, lim, ldt, b_re, b_im)


def _gelu_tanh(y):
    return 0.5 * y * (1.0 + jnp.tanh(math.sqrt(2.0 / math.pi) * (y + 0.044715 * (y * y * y))))


def _scan_layout(tt, group):
    p = jnp.arange(tt)
    rem = p % group
    src = (p - rem) + (rem % SUBLANES) * (group // SUBLANES) + rem // SUBLANES
    return (src[:, None] == jnp.arange(tt)[None, :]).astype(BF16)


def _cmul_add(ar, ai, xr, xi, br, bi):
    return ar * xr - ai * xi + br, ar * xi + ai * xr + bi


def _s5_kernel(u_ref, h0re_ref, h0im_ref, perm_ref, back_ref, coef_ref, bre_ref, bim_ref, cwre_ref, cwim_ref,
               d_ref, gw_ref, gb_ref, y_ref, sre_ref, sim_ref, hre, him, ysc, car_re, car_im,
               *, chain, tt):
    group = tt if chain else SLAB * SUBLANES
    seg = group // SUBLANES
    u = u_ref[...]
    ub = _dot(perm_ref[...], u.astype(BF16)).astype(BF16)
    bw = S5_HID // S5_BLOCKS
    gw = S5_WIDTH // S5_BLOCKS
    if chain:
        @pl.when(pl.program_id(1) == 0)
        def _():
            car_re[...] = jnp.broadcast_to(h0re_ref[...], car_re.shape)
            car_im[...] = jnp.broadcast_to(h0im_ref[...], car_im.shape)

    sub = lax.broadcasted_iota(jnp.int32, (SUBLANES, 1), 0)
    def project_in(c):
        uc = ub[:, c * gw:(c + 1) * gw]
        hre[c] = _dot(uc, bre_ref[c])
        him[c] = _dot(uc, bim_ref[c])

    def project_out(c):
        ysc[:, c * gw:(c + 1) * gw] = (_dot(hre[c].astype(BF16), cwre_ref[c])
                                       - _dot(him[c].astype(BF16), cwim_ref[c]))

    def scan(c, between):
        sl = slice(c * bw, (c + 1) * bw)
        hr, hi = hre.at[c], him.at[c]
        ar, ai = coef_ref[0, :, sl], coef_ref[1, :, sl]
        if not chain:
            for g in range(tt // group):
                srows = slice(g * SUBLANES, (g + 1) * SUBLANES)
                xr, xi = h0re_ref[srows, sl], h0im_ref[srows, sl]
                for j in range(SLAB - SLAB_REAL, SLAB):
                    rows = slice(g * group + j * SUBLANES, g * group + (j + 1) * SUBLANES)
                    xr, xi = _cmul_add(ar, ai, xr, xi, hr[rows, :], hi[rows, :])
                    hr[rows, :] = xr
                    hi[rows, :] = xi
                sre_ref[srows, sl] = xr
                sim_ref[srows, sl] = xi
            between()
        else:
            xr = xi = jnp.zeros((SUBLANES, bw), F32)
            for j in range(seg):
                rows = slice(j * SUBLANES, (j + 1) * SUBLANES)
                xr, xi = _cmul_add(ar, ai, xr, xi, hr[rows, :], hi[rows, :])
                hr[rows, :] = xr
                hi[rows, :] = xi
            kr = jnp.where(sub == 0, car_re[:, sl], pltpu.roll(xr, 1, 0))
            ki = jnp.where(sub == 0, car_im[:, sl], pltpu.roll(xi, 1, 0))
            for k, s in enumerate((1, 2, 4)):
                pr, pi = coef_ref[2 + 2 * k, :, sl], coef_ref[3 + 2 * k, :, sl]
                kr, ki = _cmul_add(pr, pi, pltpu.roll(kr, s, 0), pltpu.roll(ki, s, 0), kr, ki)
            outr, outi = _cmul_add(coef_ref[2, :, sl], coef_ref[3, :, sl], kr, ki, xr, xi)
            car_re[:, sl] = jnp.broadcast_to(outr[SUBLANES - 1:SUBLANES, :], outr.shape)
            car_im[:, sl] = jnp.broadcast_to(outi[SUBLANES - 1:SUBLANES, :], outi.shape)
            between()
            for j in range(seg):
                rows = slice(j * SUBLANES, (j + 1) * SUBLANES)
                kr, ki = ar * kr - ai * ki, ar * ki + ai * kr
                hr[rows, :] += kr
                hi[rows, :] += ki

    for c in range(S5_BLOCKS + 2):
        emit_out = (lambda c=c: project_out(c - 2)) if c >= 2 else (lambda: None)
        if c < S5_BLOCKS:
            project_in(c)
        if 1 <= c <= S5_BLOCKS:
            scan(c - 1, emit_out)
        else:
            emit_out()

    if chain:
        @pl.when(pl.program_id(1) == pl.num_programs(1) - 1)
        def _():
            sre_ref[...] = car_re[0:1, :]
            sim_ref[...] = car_im[0:1, :]

    y = _dot_split(back_ref[...], ysc[...])
    z = _gelu_tanh(y + d_ref[...] * u)
    gl = _dot(z.astype(BF16), gw_ref[...]) + gb_ref[...]
    y_ref[...] = z * jax.nn.sigmoid(gl)


def _s5_mixer(proj, h0re, h0im, sw, *, chain, nseq, tt=S5_TILE):
    rows = proj.shape[0]
    const = lambda shape: pl.BlockSpec(shape, lambda *_: (0,) * len(shape))
    if chain:
        assert tt == S5_SEG * SUBLANES
        nt = rows // nseq // tt
        grid = (nseq, nt)
        u_spec = pl.BlockSpec((tt, S5_WIDTH), lambda b, t: (b * nt + t, 0))
        st_spec = pl.BlockSpec((None, 1, S5_HID), lambda b, t: (b, 0, 0))
        st_shape = jax.ShapeDtypeStruct((nseq, 1, S5_HID), F32)
        sem = ("parallel", "arbitrary")
    else:
        grid = (rows // tt,)
        u_spec = pl.BlockSpec((tt, S5_WIDTH), lambda i: (i, 0))
        st_spec = pl.BlockSpec((tt // SLAB, S5_HID), lambda i: (i, 0))
        st_shape = jax.ShapeDtypeStruct((rows // SLAB, S5_HID), F32)
        sem = ("parallel",)
    y_spec = u_spec
    bw = S5_HID // S5_BLOCKS
    gw = S5_WIDTH // S5_BLOCKS
    perm = _scan_layout(tt, tt if chain else SLAB * SUBLANES)
    return pl.pallas_call(
        functools.partial(_s5_kernel, chain=chain, tt=tt),
        grid=grid,
        in_specs=[u_spec, st_spec, st_spec,
                  const((tt, tt)), const((tt, tt)), const((8, SUBLANES, S5_HID)),
                  const((S5_BLOCKS, gw, bw)), const((S5_BLOCKS, gw, bw)),
                  const((S5_BLOCKS, bw, gw)), const((S5_BLOCKS, bw, gw)),
                  const((1, S5_WIDTH)), const((S5_WIDTH, S5_WIDTH)), const((1, S5_WIDTH))],
        out_specs=[y_spec, st_spec, st_spec],
        out_shape=[jax.ShapeDtypeStruct((rows, S5_WIDTH), F32), st_shape, st_shape],
        scratch_shapes=[pltpu.VMEM((S5_BLOCKS, tt, bw), F32), pltpu.VMEM((S5_BLOCKS, tt, bw), F32),
                        pltpu.VMEM((tt, S5_WIDTH), F32),
                        pltpu.VMEM((SUBLANES, S5_HID), F32), pltpu.VMEM((SUBLANES, S5_HID), F32)],
        compiler_params=_params(*sem),
        name="s5_chain" if chain else "s5_slab",
    )(proj, h0re, h0im, perm, perm.T, sw["coef"], sw["b_re"], sw["b_im"], sw["c_re"], sw["c_im"],
      sw["d"], sw["glu_w"], sw["glu_b"])


def _softplus(x):
    return jnp.maximum(x, 0.0) + jnp.log1p(jnp.exp(-jnp.abs(x)))


def _segment_masks(seg):
    ri = lax.broadcasted_iota(jnp.int32, (GDN_TILE, GDN_TILE), 0)
    ci = lax.broadcasted_iota(jnp.int32, (GDN_TILE, GDN_TILE), 1)
    same = (ri // seg) == (ci // seg)
    causal = (ri >= ci) & same
    strict = (ri > ci) & same
    return same, causal, strict


def _gate_matrices(seg):
    i = jnp.arange(GDN_TILE)
    same = (i[:, None] // seg) == (i[None, :] // seg)
    stack = jnp.concatenate([i[:, None] == i[None, :], (i[:, None] >= i[None, :]) & same, same], axis=0)
    return stack.astype(BF16), ((i[:, None] <= i[None, :]) & same).astype(BF16)


def _gdn_gates(bdt, alog_col, dtb_col, stack, upper, seg, slab):
    same, causal, strict = _segment_masks(seg)
    bt = jax.nn.sigmoid(bdt[:GDN_HEADS, :])
    gt = -jnp.exp(alog_col) * _softplus(bdt[GDN_HEADS:, :] + dtb_col)
    if slab:
        creal = (lax.broadcasted_iota(jnp.int32, (1, GDN_TILE), 1) % SLAB) >= SLAB - SLAB_REAL
        bt = jnp.where(creal, bt, 0.0)
        gt = jnp.where(creal, gt, 0.0)
    gates = jnp.concatenate([bt, gt, jnp.zeros((GDN_TILE - 2 * GDN_HEADS, GDN_TILE), F32)], axis=0)
    cols = _dot_split(stack, gates, NT_DIMS)
    beta, gc, gl = cols[:GDN_TILE], cols[GDN_TILE:2 * GDN_TILE], cols[2 * GDN_TILE:]
    gct = _dot_split(gt, upper)
    return beta, gc, gl, gct, causal, strict


def _merge_masks(top):
    ri = lax.broadcasted_iota(jnp.int32, (GDN_TILE, GDN_TILE), 0)
    ci = lax.broadcasted_iota(jnp.int32, (GDN_TILE, GDN_TILE), 1)
    masks = []
    s = 1
    while s < top:
        masks.append(((ri // (2 * s)) == (ci // (2 * s))) & ((ri // s) != (ci // s)))
        s *= 2
    return masks


def _unit_lower_inverse(ms, masks):
    es = [-jnp.where(masks[0], m, 0.0) for m in ms]
    for mask in masks[1:]:
        cs = [jnp.where(mask, m, 0.0) for m in ms]
        ebs = [e.astype(BF16) for e in es]
        xs = [c + _dot(eb, c.astype(BF16)) for c, eb in zip(cs, ebs)]
        es = [e - (x + _dot(x.astype(BF16), eb)) for e, x, eb in zip(es, xs, ebs)]
    return es


def _gdn_intra(qs, ks, vs, betas, gcols, grows, glcols, causal, strict, masks):
    n = range(len(qs))
    qn = [q * lax.rsqrt(jnp.sum(q * q, axis=-1, keepdims=True) + NORM_EPS) * (GDN_HEAD_DIM ** -0.5) for q in qs]
    kn = [k * lax.rsqrt(jnp.sum(k * k, axis=-1, keepdims=True) + NORM_EPS) for k in ks]
    decay = [jnp.exp(jnp.where(causal, gcols[i] - grows[i], -jnp.inf)) for i in n]
    kb = [kn[i] * betas[i] for i in n]
    knb = [k.astype(BF16) for k in kn]
    kk = [lax.dot_general(kb[i].astype(BF16), knb[i], NT_DIMS, preferred_element_type=F32) for i in n]
    ms = [kk[i] * jnp.where(strict, decay[i], 0.0) for i in n]
    eg = [jnp.exp(g) for g in gcols]
    attn = [(lax.dot_general(qn[i].astype(BF16), knb[i], NT_DIMS, preferred_element_type=F32) * decay[i]).astype(BF16)
            for i in n]
    qg = [qn[i] * eg[i] for i in n]
    kdt = [(kn[i] * jnp.exp(glcols[i] - gcols[i])).T.astype(BF16) for i in n]
    rhs = [jnp.concatenate([vs[i] * betas[i], kb[i] * eg[i]], axis=1) for i in n]
    ys = _unit_lower_inverse(ms, masks)
    uw = [rhs[i] + _dot(ys[i].astype(BF16), rhs[i].astype(BF16)) for i in n]
    return [(uw[i][:, :GDN_HEAD_DIM], uw[i][:, GDN_HEAD_DIM:], attn[i], qg[i], kdt[i]) for i in n]


def _gdn_out(o, gate, norm_w):
    o = o * lax.rsqrt(jnp.mean(o * o, axis=-1, keepdims=True) + NORM_EPS) * norm_w
    return o * _silu(gate)


def _conv4(win_ref, cw, width_slice):
    x = win_ref[:, width_slice]
    x1 = pltpu.roll(x, 1, 0)
    z = cw[1:2, :] * x + cw[0:1, :] * x1
    out = cw[3:4, :] * x + cw[2:3, :] * x1 + pltpu.roll(z, 2, 0)
    return out[SUBLANES:, :]


def _gdn_chain_kernel(q_ref, k_ref, v_ref, gate_ref, bdt_ref, buf_ref, s0_ref, cw_ref,
                      alc_ref, dbc_ref, nw_ref, gstack_ref, gupper_ref,
                      y_ref, sout_ref, bufout_ref, win, s_sc):
    t = pl.program_id(1)
    w3 = 3 * GDN_WIDTH

    @pl.when(t == 0)
    def _():
        win[0:SUBLANES, :] = jnp.zeros((SUBLANES, w3), F32)
        win[SUBLANES - 3:SUBLANES, :] = buf_ref[...]
        s_sc[...] = s0_ref[...]

    rows = q_ref.shape[0]
    nchunk = rows // GDN_TILE
    win[SUBLANES:, 0:GDN_WIDTH] = q_ref[...]
    win[SUBLANES:, GDN_WIDTH:2 * GDN_WIDTH] = k_ref[...]
    win[SUBLANES:, 2 * GDN_WIDTH:] = v_ref[...]

    masks = _merge_masks(GDN_TILE)
    heads = range(GDN_HEADS)
    qkv = [[], [], []]
    for part in range(3):
        conv = []
        for h in heads:
            cs = slice(part * GDN_WIDTH + h * GDN_HEAD_DIM, part * GDN_WIDTH + (h + 1) * GDN_HEAD_DIM)
            conv.append(_silu(_conv4(win, cw_ref[:, cs], cs)))
        qkv[part] = [x[c * GDN_TILE:(c + 1) * GDN_TILE] for c in range(nchunk) for x in conv]
    betas, gcols, glcols, grows = [], [], [], []
    for c in range(nchunk):
        beta, gc, gl, gct, causal, strict = _gdn_gates(
            bdt_ref[:, c * GDN_TILE:(c + 1) * GDN_TILE], alc_ref[...], dbc_ref[...], gstack_ref[...],
            gupper_ref[...], GDN_TILE, False)
        betas += [beta[:, h:h + 1] for h in heads]
        gcols += [gc[:, GDN_HEADS + h:GDN_HEADS + h + 1] for h in heads]
        glcols += [gl[:, GDN_HEADS + h:GDN_HEADS + h + 1] for h in heads]
        grows += [gct[h:h + 1, :] for h in heads]
    intra = _gdn_intra(qkv[0], qkv[1], qkv[2], betas, gcols, grows, glcols, causal, strict, masks)
    for c in range(nchunk):
        at = lambda h, c=c: c * GDN_HEADS + h
        crow = slice(c * GDN_TILE, (c + 1) * GDN_TILE)
        s_old = [s_sc[h] for h in heads]
        a = [_dot(jnp.concatenate([intra[at(h)][1], intra[at(h)][3]], axis=0).astype(BF16), s_old[h].astype(BF16))
             for h in heads]
        vnb = [(intra[at(h)][0] - a[h][:GDN_TILE]).astype(BF16) for h in heads]
        o = [a[h][GDN_TILE:] + _dot(intra[at(h)][2], vnb[h]) for h in heads]
        for h in heads:
            s_sc[h] = s_old[h] * jnp.exp(glcols[at(h)][0:1, :]) + _dot(intra[at(h)][4], vnb[h])
        for h in heads:
            hs = slice(h * GDN_HEAD_DIM, (h + 1) * GDN_HEAD_DIM)
            y_ref[crow, hs] = _gdn_out(o[h], gate_ref[crow, hs], nw_ref[...])

    win[0:SUBLANES, :] = win[rows:rows + SUBLANES, :]

    @pl.when(t == pl.num_programs(1) - 1)
    def _():
        sout_ref[...] = s_sc[...]
        bufout_ref[...] = win[rows + SUBLANES - 3:rows + SUBLANES, :]


def _gdn_chain(proj, bdt, conv_buf, s0, gw, *, nseq):
    rows = proj.shape[0]
    tr = GDN_CHAIN_CHUNKS * GDN_TILE
    nt = rows // nseq // tr
    w3 = 3 * GDN_WIDTH
    const = lambda shape: pl.BlockSpec(shape, lambda *_: (0,) * len(shape))
    col = lambda cb: pl.BlockSpec((tr, GDN_WIDTH), lambda b, t, cb=cb: (b * nt + t, cb))
    return pl.pallas_call(
        _gdn_chain_kernel,
        grid=(nseq, nt),
        in_specs=[col(1), col(2), col(3), col(4),
                  pl.BlockSpec((2 * GDN_HEADS, tr), lambda b, t: (0, b * nt + t)),
                  pl.BlockSpec((None, GDN_CONV - 1, w3), lambda b, t: (b, 0, 0)),
                  pl.BlockSpec((None, GDN_HEADS, GDN_HEAD_DIM, GDN_HEAD_DIM), lambda b, t: (b, 0, 0, 0)),
                  const((GDN_CONV, w3)), const((GDN_HEADS, 1)), const((GDN_HEADS, 1)), const((1, GDN_HEAD_DIM)),
                  const((3 * GDN_TILE, GDN_TILE)), const((GDN_TILE, GDN_TILE))],
        out_specs=[pl.BlockSpec((tr, GDN_WIDTH), lambda b, t: (b * nt + t, 0)),
                   pl.BlockSpec((None, GDN_HEADS, GDN_HEAD_DIM, GDN_HEAD_DIM), lambda b, t: (b, 0, 0, 0)),
                   pl.BlockSpec((None, GDN_CONV - 1, w3), lambda b, t: (b, 0, 0))],
        out_shape=[jax.ShapeDtypeStruct((rows, GDN_WIDTH), F32),
                   jax.ShapeDtypeStruct((nseq, GDN_HEADS, GDN_HEAD_DIM, GDN_HEAD_DIM), F32),
                   jax.ShapeDtypeStruct((nseq, GDN_CONV - 1, w3), F32)],
        scratch_shapes=[pltpu.VMEM((tr + SUBLANES, w3), F32),
                        pltpu.VMEM((GDN_HEADS, GDN_HEAD_DIM, GDN_HEAD_DIM), F32)],
        compiler_params=_params("parallel", "arbitrary"),
        name="gdn_chain",
    )(proj, proj, proj, proj, bdt, conv_buf, s0, gw["conv_w"], gw["alog_col"], gw["dtb_col"], gw["norm_w"],
      *_gate_matrices(GDN_TILE))


def _gdn_slab_kernel(q_ref, k_ref, v_ref, gate_ref, bdt_ref, bq_ref, bk_ref, bv_ref, s0_ref,
                     cwq_ref, cwk_ref, cwv_ref, alc_ref, dbc_ref, nw_ref, gstack_ref, gupper_ref,
                     y_ref, sout_ref, oq_ref, ok_ref, ov_ref,
                     win):
    hp = s0_ref.shape[1]
    hw = hp * GDN_HEAD_DIM
    heads = range(hp)
    head0 = pl.program_id(1) * hp
    nslab = GDN_TILE // SLAB
    first = (pl.program_id(0) % (bq_ref.shape[1] // nslab)) * nslab
    real = (lax.broadcasted_iota(jnp.int32, (GDN_TILE, 1), 0) % SLAB) >= SLAB - SLAB_REAL
    qkv = []
    taps = range(GDN_CONV - 1)
    place = [_slab_row_selector(GDN_TILE, bq_ref.shape[1], 1 + i, first, transpose=True) for i in taps]
    take = [_slab_row_selector(GDN_TILE, nslab, SLAB - 3 + i, 0, transpose=False) for i in taps]
    for part, (x_ref, b_ref, cw_ref, o_ref) in enumerate(
            ((q_ref, bq_ref, cwq_ref, oq_ref), (k_ref, bk_ref, cwk_ref, ok_ref), (v_ref, bv_ref, cwv_ref, ov_ref))):
        cs = slice(part * hw, (part + 1) * hw)
        x = x_ref[...]
        for i in taps:
            x = x + _dot_split(place[i], b_ref[i])
        win[0:SUBLANES, cs] = jnp.zeros((SUBLANES, hw), F32)
        win[SUBLANES:, cs] = x
        conv = jnp.where(real, _silu(_conv4(win, cw_ref[...], cs)), 0.0)
        qkv.append([conv[:, h * GDN_HEAD_DIM:(h + 1) * GDN_HEAD_DIM] for h in heads])
        pieces = _split3(x)
        for i in taps:
            o_ref[i] = sum(_dot(take[i], p) for p in pieces)

    beta, gc, gl, gct, causal, strict = _gdn_gates(bdt_ref[...], alc_ref[...], dbc_ref[...], gstack_ref[...],
                                                   gupper_ref[...], SLAB, True)
    lane = lax.broadcasted_iota(jnp.int32, (1, LANES), 1)
    sub = lax.broadcasted_iota(jnp.int32, (SUBLANES, 1), 0)
    pick = lambda a, idx: jnp.sum(jnp.where(lane == idx, a, 0.0), axis=1, keepdims=True)
    gcols = [pick(gc, GDN_HEADS + head0 + h) for h in heads]
    glcols = [pick(gl, GDN_HEADS + head0 + h) for h in heads]
    intra = _gdn_intra(qkv[0], qkv[1], qkv[2], [pick(beta, head0 + h) for h in heads], gcols,
                       [jnp.sum(jnp.where(sub == head0 + h, gct, 0.0), axis=0, keepdims=True) for h in heads],
                       glcols, causal, strict, _merge_masks(SLAB_REAL))
    for h in heads:
        u, wk, attn, qg, kdt = intra[h]
        res = []
        for i in range(nslab):
            rows = slice(i * SLAB, (i + 1) * SLAB)
            lhs = jnp.concatenate([wk[rows], qg[rows]], axis=0).astype(BF16)
            res.append(_dot(lhs, s0_ref[i, h].astype(BF16)))
        vnb = (u - jnp.concatenate([r[:SLAB] for r in res], axis=0)).astype(BF16)
        o = jnp.concatenate([r[SLAB:] for r in res], axis=0) + _dot(attn, vnb)
        hs = slice(h * GDN_HEAD_DIM, (h + 1) * GDN_HEAD_DIM)
        y_ref[:, hs] = _gdn_out(o, gate_ref[:, hs], nw_ref[...])
        egl = jnp.exp(glcols[h])
        for i in range(nslab):
            in_slab = (lane // SLAB) == i
            upd = _dot(jnp.where(in_slab, kdt, jnp.zeros_like(kdt)), vnb)
            sout_ref[i, h] = s0_ref[i, h] * egl[i * SLAB:i * SLAB + 1, :] + upd


def _gdn_slab(proj, bdt, conv_buf, s0, gw, *, hp=GDN_SLAB_HEADS):
    rows = proj.shape[0]
    nb = rows // SLAB
    nslab = GDN_TILE // SLAB
    hw = hp * GDN_HEAD_DIM
    hb = GDN_WIDTH // hw
    const = lambda shape: pl.BlockSpec(shape, lambda *_: (0,) * len(shape))
    col = lambda g: pl.BlockSpec((GDN_TILE, hw), lambda i, h, g=g: (i, g * hb + h))
    per = LANES // nslab
    buf = lambda g: pl.BlockSpec((GDN_CONV - 1, LANES, hw), lambda i, h, g=g: (0, i // per, g * hb + h))
    cwb = lambda g: pl.BlockSpec((GDN_CONV, hw), lambda i, h, g=g: (0, g * hb + h))
    st = pl.BlockSpec((nslab, hp, GDN_HEAD_DIM, GDN_HEAD_DIM), lambda i, h: (i, h, 0, 0))
    obuf = pl.BlockSpec((GDN_CONV - 1, nslab, hw), lambda i, h: (0, i, h))
    return pl.pallas_call(
        _gdn_slab_kernel,
        grid=(rows // GDN_TILE, GDN_HEADS // hp),
        in_specs=[col(1), col(2), col(3), col(4),
                  pl.BlockSpec((2 * GDN_HEADS, GDN_TILE), lambda i, h: (0, i)),
                  buf(0), buf(1), buf(2), st, cwb(0), cwb(1), cwb(2),
                  const((GDN_HEADS, 1)), const((GDN_HEADS, 1)), const((1, GDN_HEAD_DIM)),
                  const((3 * GDN_TILE, GDN_TILE)), const((GDN_TILE, GDN_TILE))],
        out_specs=[pl.BlockSpec((GDN_TILE, hw), lambda i, h: (i, h)), st, obuf, obuf, obuf],
        out_shape=[jax.ShapeDtypeStruct((rows, GDN_WIDTH), F32),
                   jax.ShapeDtypeStruct(s0.shape, F32)]
                  + [jax.ShapeDtypeStruct((GDN_CONV - 1, nb, GDN_WIDTH), F32)] * 3,
        scratch_shapes=[pltpu.VMEM((GDN_TILE + SUBLANES, 3 * hw), F32)],
        compiler_params=_params("parallel", "arbitrary"),
        name="gdn_slab",
    )(proj, proj, proj, proj, bdt, conv_buf, conv_buf, conv_buf, s0,
      gw["conv_w"], gw["conv_w"], gw["conv_w"], gw["alog_col"], gw["dtb_col"], gw["norm_w"],
      *_gate_matrices(SLAB))


def _block_diag(w):
    per = S5_GROUPS // S5_BLOCKS
    g, a, b = w.shape
    w = w.reshape(S5_BLOCKS, per, a, b)
    eye = jnp.eye(per, dtype=w.dtype)
    return jnp.einsum("jgab,gk->jgakb", w, eye).reshape(S5_BLOCKS, per * a, per * b)


def _layer_weights(l, ln1_g, ln1_b, ffn1_w_in, ffn1_w_out, w_mix_in, s5_lambda_re, s5_lambda_im, s5_log_dt,
                   s5_b_re, s5_b_im, s5_c_re, s5_c_im, s5_d, s5_glu_w, s5_glu_b, gdn_conv_w, gdn_a_log,
                   gdn_dt_bias, gdn_norm_w, w_mix_out, ln2_g, ln2_b, ffn2_w_in, ffn2_w_out, ln3_g, ln3_b):
    row = lambda v: v[l].reshape(1, -1).astype(F32)
    w = {
        "ln1": (row(ln1_g), row(ln1_b)), "ln2": (row(ln2_g), row(ln2_b)), "ln3": (row(ln3_g), row(ln3_b)),
        "ffn1": (ffn1_w_in[l], ffn1_w_out[l]),
        "ffn2": (ffn2_w_in[l], ffn2_w_out[l]),
        "mix_in": (w_mix_in.astype(BF16), l),
        "mix_out": w_mix_out[l].astype(BF16),
    }
    coef, bb_re, bb_im = _s5_disc(
        s5_lambda_re[l].reshape(1, S5_HID).astype(F32), s5_lambda_im[l].reshape(1, S5_HID).astype(F32),
        jnp.repeat(s5_log_dt[l], S5_STATE).reshape(1, S5_HID).astype(F32),
        _block_diag(jnp.swapaxes(s5_b_re[l], 1, 2).astype(F32)),
        _block_diag(jnp.swapaxes(s5_b_im[l], 1, 2).astype(F32)))
    w["s5"] = {
        "coef": coef, "b_re": bb_re, "b_im": bb_im,
        "c_re": _block_diag(jnp.swapaxes(s5_c_re[l], 1, 2)).astype(BF16),
        "c_im": _block_diag(jnp.swapaxes(s5_c_im[l], 1, 2)).astype(BF16),
        "d": row(s5_d), "glu_w": s5_glu_w[l].astype(BF16), "glu_b": row(s5_glu_b),
    }
    w["gdn"] = {
        "conv_w": gdn_conv_w[l].astype(F32),
        "alog_col": gdn_a_log[l].reshape(GDN_HEADS, 1).astype(F32),
        "dtb_col": gdn_dt_bias[l].reshape(GDN_HEADS, 1).astype(F32),
        "norm_w": row(gdn_norm_w),
    }
    return w


def _prompt_layer(x, w, nseq):
    x = _ffn_ln(x, *w["ffn1"], *w["ln1"])
    proj, bdt = _mixin(x, *w["mix_in"])
    z_s5 = jnp.zeros((nseq, 1, S5_HID), F32)
    y_s5, n_re, n_im = _s5_mixer(proj, z_s5, z_s5, w["s5"], chain=True, nseq=nseq)
    z_gdn = jnp.zeros((nseq, GDN_HEADS, GDN_HEAD_DIM, GDN_HEAD_DIM), F32)
    z_buf = jnp.zeros((nseq, GDN_CONV - 1, 3 * GDN_WIDTH), F32)
    y_gdn, n_s, n_buf = _gdn_chain(proj, bdt, z_buf, z_gdn, w["gdn"], nseq=nseq)
    x = _mixout_ln(y_s5, y_gdn, x, w["mix_out"], *w["ln2"])
    x = _ffn_ln(x, *w["ffn2"], *w["ln3"])
    shape = (nseq, S5_GROUPS, S5_STATE)
    return x, n_re.reshape(shape), n_im.reshape(shape), n_s, n_buf


def _sample_layer(x, s5_re, s5_im, gdn_s, conv_buf, w, nb, t):
    x = _ffn_ln(x, *w["ffn1"], *w["ln1"])
    xs = jnp.pad(x.reshape(nb, t, D_MODEL), ((0, 0), (SLAB - t, 0), (0, 0))).reshape(nb * SLAB, D_MODEL)
    proj, bdt = _mixin(xs, *w["mix_in"])
    y_s5, n_re, n_im = _s5_mixer(proj, s5_re.reshape(nb, S5_HID).astype(F32),
                                 s5_im.reshape(nb, S5_HID).astype(F32), w["s5"], chain=False, nseq=nb)
    y_gdn, n_s, bq, bk, bv = _gdn_slab(proj, bdt, jnp.swapaxes(conv_buf.astype(F32), 0, 1),
                                       gdn_s.astype(F32), w["gdn"])
    xs = _mixout_ln(y_s5, y_gdn, xs, w["mix_out"], *w["ln2"])
    x = xs.reshape(nb, SLAB, D_MODEL)[:, SLAB - t:].reshape(nb * t, D_MODEL)
    x = _ffn_ln(x, *w["ffn2"], *w["ln3"])
    shape = (nb, S5_GROUPS, S5_STATE)
    n_buf = jnp.swapaxes(jnp.concatenate([bq, bk, bv], axis=-1), 0, 1)
    return x, n_re.reshape(shape), n_im.reshape(shape), n_s, n_buf


def kernel(x_prompt, x_sample, state_s5_re, state_s5_im, state_gdn, state_conv, ln1_g, ln1_b, ffn1_w_in, ffn1_w_out, w_mix_in, s5_lambda_re, s5_lambda_im, s5_log_dt, s5_b_re, s5_b_im, s5_c_re, s5_c_im, s5_d, s5_glu_w, s5_glu_b, gdn_conv_w, gdn_a_log, gdn_dt_bias, gdn_norm_w, w_mix_out, ln2_g, ln2_b, ffn2_w_in, ffn2_w_out, ln3_g, ln3_b):
    bp, tp, _ = x_prompt.shape
    bs, ts, _ = x_sample.shape
    assert ts == SLAB_REAL and tp % S5_TILE == 0 and (bs * SLAB) % S5_TILE == 0
    depth = ln1_g.shape[0]
    yp = x_prompt.astype(F32).reshape(bp * tp, D_MODEL)
    ys = x_sample.astype(F32).reshape(bs * ts, D_MODEL)
    outs = [[] for _ in range(8)]
    for l in range(depth):
        w = _layer_weights(l, ln1_g, ln1_b, ffn1_w_in, ffn1_w_out, w_mix_in, s5_lambda_re, s5_lambda_im,
                           s5_log_dt, s5_b_re, s5_b_im, s5_c_re, s5_c_im, s5_d, s5_glu_w, s5_glu_b,
                           gdn_conv_w, gdn_a_log, gdn_dt_bias, gdn_norm_w, w_mix_out, ln2_g, ln2_b,
                           ffn2_w_in, ffn2_w_out, ln3_g, ln3_b)
        yp, *p_state = _prompt_layer(yp, w, bp)
        ys, *s_state = _sample_layer(ys, state_s5_re[l], state_s5_im[l], state_gdn[l], state_conv[l], w, bs, ts)
        for acc, val in zip(outs, p_state + s_state):
            acc.append(val)
    return (yp.reshape(x_prompt.shape).astype(x_prompt.dtype), ys.reshape(x_sample.shape).astype(x_sample.dtype),
            *(o[0][None] if depth == 1 else jnp.stack(o) for o in outs))
```

```python
import functools
import math

import jax
import jax.numpy as jnp
from jax import lax
from jax.experimental import pallas as pl
from jax.experimental.pallas import tpu as pltpu

F32 = jnp.float32
BF16 = jnp.bfloat16

D_MODEL = 2048
S5_WIDTH = 1024
S5_GROUP = 16
S5_GROUPS = 64
S5_STATE = 64
S5_HID = S5_GROUPS * S5_STATE
GDN_WIDTH = 1024
GDN_HEAD_DIM = 128
GDN_HEADS = 8
GDN_CONV = 4
D_FF = 5632
MIX_MAIN = 5120
DEEP_ALPHA = 2.0 ** 0.25
LN_EPS = 1e-5
NORM_EPS = 1e-6

SUBLANES = 8
LANES = 128
SLAB = 8
SLAB_REAL = 4
GDN_TILE = 128
GDN_SLAB_HEADS = 8
GDN_CHAIN_CHUNKS = 2
FFN_TM = 1024
FFN_TF = 256
FFN_CHUNK = 512
S5_TILE = 256
S5_SEG = S5_TILE // SUBLANES
S5_BLOCKS = 8
S5_SCAN_GROUP = 4
VMEM_LIMIT = 56 * 1024 * 1024

NT_DIMS = (((1,), (1,)), ((), ()))


def _dot(a, b, **kw):
    return jnp.dot(a, b, preferred_element_type=F32, **kw)


def _silu(x):
    return x * jax.nn.sigmoid(x)


def _layer_norm(y, g, b):
    mu = jnp.mean(y, axis=-1, keepdims=True)
    d = y - mu
    var = jnp.mean(d * d, axis=-1, keepdims=True)
    return d * lax.rsqrt(var + LN_EPS) * g + b


def _slab_row_selector(rows, nseq, slab_row, first, *, transpose):
    shape = (rows, nseq) if transpose else (nseq, rows)
    r = lax.broadcasted_iota(jnp.int32, shape, 0 if transpose else 1)
    b = lax.broadcasted_iota(jnp.int32, shape, 1 if transpose else 0)
    return _as_bf16(r == SLAB * (b - first) + slab_row)


def _as_bf16(mask):
    return mask.astype(F32).astype(BF16)


def _split3(x):
    pieces = []
    for _ in range(3):
        pieces.append(x.astype(BF16))
        x = x - pieces[-1].astype(F32)
    return pieces


def _dot_split(a, b, dims=None):
    f32_is_lhs = a.dtype == F32
    acc = None
    for piece in _split3(a if f32_is_lhs else b):
        lhs, rhs = (piece, b) if f32_is_lhs else (a, piece)
        d = _dot(lhs, rhs) if dims is None else lax.dot_general(lhs, rhs, dims, preferred_element_type=F32)
        acc = d if acc is None else acc + d
    return acc


def _params(*sem):
    return pltpu.CompilerParams(dimension_semantics=sem, vmem_limit_bytes=VMEM_LIMIT)


def _ffn_kernel(x_ref, wg_ref, wu_ref, wo_ref, g_ref, b_ref, o_ref, xb_ref):
    j = pl.program_id(1)

    @pl.when(j == 0)
    def _():
        o_ref[...] = jnp.zeros_like(o_ref)
        xb_ref[...] = x_ref[...].astype(BF16)

    xb = xb_ref[...]
    gate = _dot(xb, wg_ref[...].astype(BF16))
    up = _dot(xb, wu_ref[...].astype(BF16))
    h = (_silu(gate) * up).astype(BF16)
    for c in range(0, D_MODEL, FFN_CHUNK):
        o_ref[:, c:c + FFN_CHUNK] += _dot(h, wo_ref[:, c:c + FFN_CHUNK].astype(BF16))

    @pl.when(j == pl.num_programs(1) - 1)
    def _():
        for r in range(0, o_ref.shape[0], FFN_CHUNK // 2):
            rows = slice(r, r + FFN_CHUNK // 2)
            y = DEEP_ALPHA * x_ref[rows, :] + 0.5 * o_ref[rows, :]
            o_ref[rows, :] = _layer_norm(y, g_ref[...], b_ref[...])


def _ffn_ln(x, w_in, w_out, g, b):
    n = x.shape[0]
    tm = math.gcd(n, FFN_TM)
    tf = FFN_TF * FFN_TM // tm
    nff = D_FF // tf
    return pl.pallas_call(
        _ffn_kernel,
        grid=(n // tm, nff),
        in_specs=[
            pl.BlockSpec((tm, D_MODEL), lambda i, j: (i, 0)),
            pl.BlockSpec((D_MODEL, tf), lambda i, j: (0, j)),
            pl.BlockSpec((D_MODEL, tf), lambda i, j: (0, j + nff)),
            pl.BlockSpec((tf, D_MODEL), lambda i, j: (j, 0)),
            pl.BlockSpec((1, D_MODEL), lambda i, j: (0, 0)),
            pl.BlockSpec((1, D_MODEL), lambda i, j: (0, 0)),
        ],
        out_specs=pl.BlockSpec((tm, D_MODEL), lambda i, j: (i, 0)),
        out_shape=jax.ShapeDtypeStruct((n, D_MODEL), F32),
        scratch_shapes=[pltpu.VMEM((tm, D_MODEL), BF16)],
        compiler_params=_params("parallel", "arbitrary"),
        name="ffn_ln",
    )(x, w_in, w_in, w_out, g, b)


def _mixin_kernel(x_ref, w_ref, wt_ref, o_ref, ot_ref, xb_ref):
    @pl.when(pl.program_id(1) == 0)
    def _():
        xb_ref[...] = x_ref[...].astype(BF16)
        ncols = ot_ref.shape[0]
        lane = lax.broadcasted_iota(jnp.int32, (1, LANES), 1)
        wt = jnp.where(lane < ncols, wt_ref[...], 0.0).astype(BF16)
        ot_ref[...] = _dot(xb_ref[...], wt).T[:ncols, :]

    o_ref[...] = _dot(xb_ref[...], w_ref[...].astype(BF16))


def _mixin(x, w, layer, *, tm=1024, tn=1024):
    n = x.shape[0]
    return pl.pallas_call(
        _mixin_kernel,
        grid=(n // tm, MIX_MAIN // tn),
        in_specs=[pl.BlockSpec((tm, D_MODEL), lambda i, j: (i, 0)),
                  pl.BlockSpec((None, D_MODEL, tn), lambda i, j: (layer, 0, j)),
                  pl.BlockSpec((None, D_MODEL, LANES), lambda i, j: (layer, 0, MIX_MAIN // LANES))],
        out_specs=[pl.BlockSpec((tm, tn), lambda i, j: (i, j)),
                   pl.BlockSpec((2 * GDN_HEADS, tm), lambda i, j: (0, i))],
        out_shape=[jax.ShapeDtypeStruct((n, MIX_MAIN), F32),
                   jax.ShapeDtypeStruct((2 * GDN_HEADS, n), F32)],
        scratch_shapes=[pltpu.VMEM((tm, D_MODEL), BF16)],
        compiler_params=_params("parallel", "arbitrary"),
        name="mix_in",
    )(x, w, w)


def _mixout_kernel(ya_ref, yb_ref, x_ref, w_ref, g_ref, b_ref, o_ref):
    mix = (_dot(ya_ref[...].astype(BF16), w_ref[0:S5_WIDTH, :])
           + _dot(yb_ref[...].astype(BF16), w_ref[S5_WIDTH:, :]))
    o_ref[...] = _layer_norm(DEEP_ALPHA * x_ref[...] + mix, g_ref[...], b_ref[...])


def _mixout_ln(ya, yb, x, w, g, b, *, tm=512):
    n = x.shape[0]
    return pl.pallas_call(
        _mixout_kernel,
        grid=(n // tm,),
        in_specs=[pl.BlockSpec((tm, S5_WIDTH), lambda i: (i, 0)),
                  pl.BlockSpec((tm, GDN_WIDTH), lambda i: (i, 0)),
                  pl.BlockSpec((tm, D_MODEL), lambda i: (i, 0)),
                  pl.BlockSpec((D_MODEL, D_MODEL), lambda i: (0, 0)),
                  pl.BlockSpec((1, D_MODEL), lambda i: (0, 0)),
                  pl.BlockSpec((1, D_MODEL), lambda i: (0, 0))],
        out_specs=pl.BlockSpec((tm, D_MODEL), lambda i: (i, 0)),
        out_shape=jax.ShapeDtypeStruct((n, D_MODEL), F32),
        compiler_params=_params("parallel"),
        name="mix_out_ln",
    )(ya, yb, x, w, g, b)


def _s5_disc_kernel(lre_ref, lim_ref, ldt_ref, bre_ref, bim_ref, coef_ref, bbre_ref, bbim_ref):
    lr, li = lre_ref[...], lim_ref[...]
    dt = jnp.exp(ldt_ref[...])
    mag = jnp.exp(lr * dt)
    ar = mag * jnp.cos(li * dt)
    ai = mag * jnp.sin(li * dt)
    nr, ni = ar - 1.0, ai
    den = lr * lr + li * li
    c_re = (nr * lr + ni * li) / den
    c_im = (ni * lr - nr * li) / den
    bw = S5_HID // S5_BLOCKS
    for j in range(S5_BLOCKS):
        cr, ci = c_re[:, j * bw:(j + 1) * bw], c_im[:, j * bw:(j + 1) * bw]
        bbre_ref[j] = (cr * bre_ref[j] - ci * bim_ref[j]).astype(BF16)
        bbim_ref[j] = (cr * bim_ref[j] + ci * bre_ref[j]).astype(BF16)

    def cmul(x, y):
        return x[0] * y[0] - x[1] * y[1], x[0] * y[1] + x[1] * y[0]

    width = lr.shape[-1]
    p = (ar, ai)
    for _ in range(S5_SEG - 1):
        p = cmul(p, (ar, ai))
    coef_ref[...] = jnp.zeros_like(coef_ref)
    for part, v in enumerate((ar, ai)):
        coef_ref[part] = jnp.broadcast_to(v, (SUBLANES, width))
    for k, s in enumerate((1, 2, 4)):
        for part in range(2):
            coef_ref[2 + 2 * k + part, s:SUBLANES, :] = jnp.broadcast_to(p[part], (SUBLANES - s, width))
        p = cmul(p, p)


def _s5_disc(lre, lim, ldt, b_re, b_im):
    return pl.pallas_call(
        _s5_disc_kernel,
        out_shape=[jax.ShapeDtypeStruct((8, SUBLANES, S5_HID), F32),
                   jax.ShapeDtypeStruct(b_re.shape, BF16), jax.ShapeDtypeStruct(b_im.shape, BF16)],
        name="s5_disc",
    )(lre, lim, ldt, b_re, b_im)


def _gelu_tanh(y):
    return 0.5 * y * (1.0 + jnp.tanh(math.sqrt(2.0 / math.pi) * (y + 0.044715 * (y * y * y))))


def _scan_layout(tt, group):
    p = jnp.arange(tt)
    rem = p % group
    src = (p - rem) + (rem % SUBLANES) * (group // SUBLANES) + rem // SUBLANES
    return (src[:, None] == jnp.arange(tt)[None, :]).astype(BF16)


def _cmul_add(ar, ai, xr, xi, br, bi):
    return ar * xr - ai * xi + br, ar * xi + ai * xr + bi


def _s5_kernel(u_ref, h0re_ref, h0im_ref, perm_ref, back_ref, coef_ref, bre_ref, bim_ref, cwre_ref, cwim_ref,
               d_ref, gw_ref, gb_ref, y_ref, sre_ref, sim_ref, hre, him, ysc, car_re, car_im,
               *, chain, tt):
    group = tt if chain else SLAB * SUBLANES
    seg = group // SUBLANES
    u = u_ref[...]
    ub = _dot(perm_ref[...], u.astype(BF16)).astype(BF16)
    bw = S5_HID // S5_BLOCKS
    gw = S5_WIDTH // S5_BLOCKS
    if chain:
        @pl.when(pl.program_id(1) == 0)
        def _():
            car_re[...] = jnp.broadcast_to(h0re_ref[...], car_re.shape)
            car_im[...] = jnp.broadcast_to(h0im_ref[...], car_im.shape)

    sub = lax.broadcasted_iota(jnp.int32, (SUBLANES, 1), 0)
    def project_in(c):
        uc = ub[:, c * gw:(c + 1) * gw]
        hre[c] = _dot(uc, bre_ref[c])
        him[c] = _dot(uc, bim_ref[c])

    def project_out(c):
        ysc[:, c * gw:(c + 1) * gw] = (_dot(hre[c].astype(BF16), cwre_ref[c])
                                       - _dot(him[c].astype(BF16), cwim_ref[c]))

    def scan(blocks):
        sls = [slice(c * bw, (c + 1) * bw) for c in blocks]
        refs = [(hre.at[c], him.at[c]) for c in blocks]
        coefs = [(coef_ref[0, :, sl], coef_ref[1, :, sl]) for sl in sls]
        n = range(len(blocks))
        if not chain:
            for g in range(tt // group):
                srows = slice(g * SUBLANES, (g + 1) * SUBLANES)
                x = [(h0re_ref[srows, sl], h0im_ref[srows, sl]) for sl in sls]
                for j in range(SLAB - SLAB_REAL, SLAB):
                    rows = slice(g * group + j * SUBLANES, g * group + (j + 1) * SUBLANES)
                    for i in n:
                        hr, hi = refs[i]
                        x[i] = _cmul_add(*coefs[i], *x[i], hr[rows, :], hi[rows, :])
                        hr[rows, :], hi[rows, :] = x[i]
                for i in n:
                    sre_ref[srows, sls[i]], sim_ref[srows, sls[i]] = x[i]
            return
        x = [(jnp.zeros((SUBLANES, bw), F32),) * 2 for _ in n]
        for j in range(seg):
            rows = slice(j * SUBLANES, (j + 1) * SUBLANES)
            for i in n:
                hr, hi = refs[i]
                x[i] = _cmul_add(*coefs[i], *x[i], hr[rows, :], hi[rows, :])
                hr[rows, :], hi[rows, :] = x[i]
        ks = []
        for i in n:
            sl = sls[i]
            kr = jnp.where(sub == 0, car_re[:, sl], pltpu.roll(x[i][0], 1, 0))
            ki = jnp.where(sub == 0, car_im[:, sl], pltpu.roll(x[i][1], 1, 0))
            for k, s in enumerate((1, 2, 4)):
                pr, pi = coef_ref[2 + 2 * k, :, sl], coef_ref[3 + 2 * k, :, sl]
                kr, ki = _cmul_add(pr, pi, pltpu.roll(kr, s, 0), pltpu.roll(ki, s, 0), kr, ki)
            outr, outi = _cmul_add(coef_ref[2, :, sl], coef_ref[3, :, sl], kr, ki, *x[i])
            car_re[:, sl] = jnp.broadcast_to(outr[SUBLANES - 1:SUBLANES, :], outr.shape)
            car_im[:, sl] = jnp.broadcast_to(outi[SUBLANES - 1:SUBLANES, :], outi.shape)
            ks.append((kr, ki))
        for j in range(seg):
            rows = slice(j * SUBLANES, (j + 1) * SUBLANES)
            for i in n:
                (ar, ai), (kr, ki), (hr, hi) = coefs[i], ks[i], refs[i]
                ks[i] = (ar * kr - ai * ki, ar * ki + ai * kr)
                hr[rows, :] += ks[i][0]
                hi[rows, :] += ks[i][1]

    groups = [list(range(c, c + S5_SCAN_GROUP)) for c in range(0, S5_BLOCKS, S5_SCAN_GROUP)]
    for step in range(len(groups) + 2):
        if step < len(groups):
            for c in groups[step]:
                project_in(c)
        if 1 <= step <= len(groups):
            scan(groups[step - 1])
        if step >= 2:
            for c in groups[step - 2]:
                project_out(c)

    if chain:
        @pl.when(pl.program_id(1) == pl.num_programs(1) - 1)
        def _():
            sre_ref[...] = car_re[0:1, :]
            sim_ref[...] = car_im[0:1, :]

    y = _dot_split(back_ref[...], ysc[...])
    z = _gelu_tanh(y + d_ref[...] * u)
    gl = _dot(z.astype(BF16), gw_ref[...]) + gb_ref[...]
    y_ref[...] = z * jax.nn.sigmoid(gl)


def _s5_mixer(proj, h0re, h0im, sw, *, chain, nseq, tt=S5_TILE):
    rows = proj.shape[0]
    const = lambda shape: pl.BlockSpec(shape, lambda *_: (0,) * len(shape))
    if chain:
        assert tt == S5_SEG * SUBLANES
        nt = rows // nseq // tt
        grid = (nseq, nt)
        u_spec = pl.BlockSpec((tt, S5_WIDTH), lambda b, t: (b * nt + t, 0))
        st_spec = pl.BlockSpec((None, 1, S5_HID), lambda b, t: (b, 0, 0))
        st_shape = jax.ShapeDtypeStruct((nseq, 1, S5_HID), F32)
        sem = ("parallel", "arbitrary")
    else:
        grid = (rows // tt,)
        u_spec = pl.BlockSpec((tt, S5_WIDTH), lambda i: (i, 0))
        st_spec = pl.BlockSpec((tt // SLAB, S5_HID), lambda i: (i, 0))
        st_shape = jax.ShapeDtypeStruct((rows // SLAB, S5_HID), F32)
        sem = ("parallel",)
    y_spec = u_spec
    bw = S5_HID // S5_BLOCKS
    gw = S5_WIDTH // S5_BLOCKS
    perm = _scan_layout(tt, tt if chain else SLAB * SUBLANES)
    return pl.pallas_call(
        functools.partial(_s5_kernel, chain=chain, tt=tt),
        grid=grid,
        in_specs=[u_spec, st_spec, st_spec,
                  const((tt, tt)), const((tt, tt)), const((8, SUBLANES, S5_HID)),
                  const((S5_BLOCKS, gw, bw)), const((S5_BLOCKS, gw, bw)),
                  const((S5_BLOCKS, bw, gw)), const((S5_BLOCKS, bw, gw)),
                  const((1, S5_WIDTH)), const((S5_WIDTH, S5_WIDTH)), const((1, S5_WIDTH))],
        out_specs=[y_spec, st_spec, st_spec],
        out_shape=[jax.ShapeDtypeStruct((rows, S5_WIDTH), F32), st_shape, st_shape],
        scratch_shapes=[pltpu.VMEM((S5_BLOCKS, tt, bw), F32), pltpu.VMEM((S5_BLOCKS, tt, bw), F32),
                        pltpu.VMEM((tt, S5_WIDTH), F32),
                        pltpu.VMEM((SUBLANES, S5_HID), F32), pltpu.VMEM((SUBLANES, S5_HID), F32)],
        compiler_params=_params(*sem),
        name="s5_chain" if chain else "s5_slab",
    )(proj, h0re, h0im, perm, perm.T, sw["coef"], sw["b_re"], sw["b_im"], sw["c_re"], sw["c_im"],
      sw["d"], sw["glu_w"], sw["glu_b"])


def _softplus(x):
    return jnp.maximum(x, 0.0) + jnp.log1p(jnp.exp(-jnp.abs(x)))


def _segment_masks(seg):
    ri = lax.broadcasted_iota(jnp.int32, (GDN_TILE, GDN_TILE), 0)
    ci = lax.broadcasted_iota(jnp.int32, (GDN_TILE, GDN_TILE), 1)
    same = (ri // seg) == (ci // seg)
    causal = (ri >= ci) & same
    strict = (ri > ci) & same
    return same, causal, strict


def _gate_matrices(seg):
    i = jnp.arange(GDN_TILE)
    same = (i[:, None] // seg) == (i[None, :] // seg)
    stack = jnp.concatenate([i[:, None] == i[None, :], (i[:, None] >= i[None, :]) & same, same], axis=0)
    return stack.astype(BF16), ((i[:, None] <= i[None, :]) & same).astype(BF16)


def _gdn_gates(bdt, alog_col, dtb_col, stack, upper, seg, slab):
    same, causal, strict = _segment_masks(seg)
    bt = jax.nn.sigmoid(bdt[:GDN_HEADS, :])
    gt = -jnp.exp(alog_col) * _softplus(bdt[GDN_HEADS:, :] + dtb_col)
    if slab:
        creal = (lax.broadcasted_iota(jnp.int32, (1, GDN_TILE), 1) % SLAB) >= SLAB - SLAB_REAL
        bt = jnp.where(creal, bt, 0.0)
        gt = jnp.where(creal, gt, 0.0)
    gates = jnp.concatenate([bt, gt, jnp.zeros((GDN_TILE - 2 * GDN_HEADS, GDN_TILE), F32)], axis=0)
    cols = _dot_split(stack, gates, NT_DIMS)
    beta, gc, gl = cols[:GDN_TILE], cols[GDN_TILE:2 * GDN_TILE], cols[2 * GDN_TILE:]
    gct = _dot_split(gt, upper)
    return beta, gc, gl, gct, causal, strict


def _merge_masks(top):
    ri = lax.broadcasted_iota(jnp.int32, (GDN_TILE, GDN_TILE), 0)
    ci = lax.broadcasted_iota(jnp.int32, (GDN_TILE, GDN_TILE), 1)
    masks = []
    s = 1
    while s < top:
        masks.append(((ri // (2 * s)) == (ci // (2 * s))) & ((ri // s) != (ci // s)))
        s *= 2
    return masks


def _unit_lower_inverse(ms, masks):
    es = [-jnp.where(masks[0], m, 0.0) for m in ms]
    for mask in masks[1:]:
        cs = [jnp.where(mask, m, 0.0) for m in ms]
        ebs = [e.astype(BF16) for e in es]
        xs = [c + _dot(eb, c.astype(BF16)) for c, eb in zip(cs, ebs)]
        es = [e - (x + _dot(x.astype(BF16), eb)) for e, x, eb in zip(es, xs, ebs)]
    return es


def _gdn_intra(qs, ks, vs, betas, gcols, grows, glcols, causal, strict, masks):
    n = range(len(qs))
    qn = [q * lax.rsqrt(jnp.sum(q * q, axis=-1, keepdims=True) + NORM_EPS) * (GDN_HEAD_DIM ** -0.5) for q in qs]
    kn = [k * lax.rsqrt(jnp.sum(k * k, axis=-1, keepdims=True) + NORM_EPS) for k in ks]
    decay = [jnp.exp(jnp.where(causal, gcols[i] - grows[i], -jnp.inf)) for i in n]
    kb = [kn[i] * betas[i] for i in n]
    knb = [k.astype(BF16) for k in kn]
    kk = [lax.dot_general(kb[i].astype(BF16), knb[i], NT_DIMS, preferred_element_type=F32) for i in n]
    ms = [kk[i] * jnp.where(strict, decay[i], 0.0) for i in n]
    eg = [jnp.exp(g) for g in gcols]
    attn = [(lax.dot_general(qn[i].astype(BF16), knb[i], NT_DIMS, preferred_element_type=F32) * decay[i]).astype(BF16)
            for i in n]
    qg = [qn[i] * eg[i] for i in n]
    kdt = [(kn[i] * jnp.exp(glcols[i] - gcols[i])).T.astype(BF16) for i in n]
    rhs = [jnp.concatenate([vs[i] * betas[i], kb[i] * eg[i]], axis=1) for i in n]
    ys = _unit_lower_inverse(ms, masks)
    uw = [rhs[i] + _dot(ys[i].astype(BF16), rhs[i].astype(BF16)) for i in n]
    return [(uw[i][:, :GDN_HEAD_DIM], uw[i][:, GDN_HEAD_DIM:], attn[i], qg[i], kdt[i]) for i in n]


def _gdn_out(o, gate, norm_w):
    o = o * lax.rsqrt(jnp.mean(o * o, axis=-1, keepdims=True) + NORM_EPS) * norm_w
    return o * _silu(gate)


def _conv4(win_ref, cw, width_slice):
    x = win_ref[:, width_slice]
    x1 = pltpu.roll(x, 1, 0)
    z = cw[1:2, :] * x + cw[0:1, :] * x1
    out = cw[3:4, :] * x + cw[2:3, :] * x1 + pltpu.roll(z, 2, 0)
    return out[SUBLANES:, :]


def _gdn_chain_kernel(q_ref, k_ref, v_ref, gate_ref, bdt_ref, buf_ref, s0_ref, cw_ref,
                      alc_ref, dbc_ref, nw_ref, gstack_ref, gupper_ref,
                      y_ref, sout_ref, bufout_ref, win, s_sc):
    t = pl.program_id(1)
    w3 = 3 * GDN_WIDTH

    @pl.when(t == 0)
    def _():
        win[0:SUBLANES, :] = jnp.zeros((SUBLANES, w3), F32)
        win[SUBLANES - 3:SUBLANES, :] = buf_ref[...]
        s_sc[...] = s0_ref[...]

    rows = q_ref.shape[0]
    nchunk = rows // GDN_TILE
    win[SUBLANES:, 0:GDN_WIDTH] = q_ref[...]
    win[SUBLANES:, GDN_WIDTH:2 * GDN_WIDTH] = k_ref[...]
    win[SUBLANES:, 2 * GDN_WIDTH:] = v_ref[...]

    masks = _merge_masks(GDN_TILE)
    heads = range(GDN_HEADS)
    qkv = [[], [], []]
    for part in range(3):
        conv = []
        for h in heads:
            cs = slice(part * GDN_WIDTH + h * GDN_HEAD_DIM, part * GDN_WIDTH + (h + 1) * GDN_HEAD_DIM)
            conv.append(_silu(_conv4(win, cw_ref[:, cs], cs)))
        qkv[part] = [x[c * GDN_TILE:(c + 1) * GDN_TILE] for c in range(nchunk) for x in conv]
    betas, gcols, glcols, grows = [], [], [], []
    for c in range(nchunk):
        beta, gc, gl, gct, causal, strict = _gdn_gates(
            bdt_ref[:, c * GDN_TILE:(c + 1) * GDN_TILE], alc_ref[...], dbc_ref[...], gstack_ref[...],
            gupper_ref[...], GDN_TILE, False)
        betas += [beta[:, h:h + 1] for h in heads]
        gcols += [gc[:, GDN_HEADS + h:GDN_HEADS + h + 1] for h in heads]
        glcols += [gl[:, GDN_HEADS + h:GDN_HEADS + h + 1] for h in heads]
        grows += [gct[h:h + 1, :] for h in heads]
    intra = _gdn_intra(qkv[0], qkv[1], qkv[2], betas, gcols, grows, glcols, causal, strict, masks)
    for c in range(nchunk):
        at = lambda h, c=c: c * GDN_HEADS + h
        crow = slice(c * GDN_TILE, (c + 1) * GDN_TILE)
        s_old = [s_sc[h] for h in heads]
        a = [_dot(jnp.concatenate([intra[at(h)][1], intra[at(h)][3]], axis=0).astype(BF16), s_old[h].astype(BF16))
             for h in heads]
        vnb = [(intra[at(h)][0] - a[h][:GDN_TILE]).astype(BF16) for h in heads]
        o = [a[h][GDN_TILE:] + _dot(intra[at(h)][2], vnb[h]) for h in heads]
        for h in heads:
            s_sc[h] = s_old[h] * jnp.exp(glcols[at(h)][0:1, :]) + _dot(intra[at(h)][4], vnb[h])
        for h in heads:
            hs = slice(h * GDN_HEAD_DIM, (h + 1) * GDN_HEAD_DIM)
            y_ref[crow, hs] = _gdn_out(o[h], gate_ref[crow, hs], nw_ref[...])

    win[0:SUBLANES, :] = win[rows:rows + SUBLANES, :]

    @pl.when(t == pl.num_programs(1) - 1)
    def _():
        sout_ref[...] = s_sc[...]
        bufout_ref[...] = win[rows + SUBLANES - 3:rows + SUBLANES, :]


def _gdn_chain(proj, bdt, conv_buf, s0, gw, *, nseq):
    rows = proj.shape[0]
    tr = GDN_CHAIN_CHUNKS * GDN_TILE
    nt = rows // nseq // tr
    w3 = 3 * GDN_WIDTH
    const = lambda shape: pl.BlockSpec(shape, lambda *_: (0,) * len(shape))
    col = lambda cb: pl.BlockSpec((tr, GDN_WIDTH), lambda b, t, cb=cb: (b * nt + t, cb))
    return pl.pallas_call(
        _gdn_chain_kernel,
        grid=(nseq, nt),
        in_specs=[col(1), col(2), col(3), col(4),
                  pl.BlockSpec((2 * GDN_HEADS, tr), lambda b, t: (0, b * nt + t)),
                  pl.BlockSpec((None, GDN_CONV - 1, w3), lambda b, t: (b, 0, 0)),
                  pl.BlockSpec((None, GDN_HEADS, GDN_HEAD_DIM, GDN_HEAD_DIM), lambda b, t: (b, 0, 0, 0)),
                  const((GDN_CONV, w3)), const((GDN_HEADS, 1)), const((GDN_HEADS, 1)), const((1, GDN_HEAD_DIM)),
                  const((3 * GDN_TILE, GDN_TILE)), const((GDN_TILE, GDN_TILE))],
        out_specs=[pl.BlockSpec((tr, GDN_WIDTH), lambda b, t: (b * nt + t, 0)),
                   pl.BlockSpec((None, GDN_HEADS, GDN_HEAD_DIM, GDN_HEAD_DIM), lambda b, t: (b, 0, 0, 0)),
                   pl.BlockSpec((None, GDN_CONV - 1, w3), lambda b, t: (b, 0, 0))],
        out_shape=[jax.ShapeDtypeStruct((rows, GDN_WIDTH), F32),
                   jax.ShapeDtypeStruct((nseq, GDN_HEADS, GDN_HEAD_DIM, GDN_HEAD_DIM), F32),
                   jax.ShapeDtypeStruct((nseq, GDN_CONV - 1, w3), F32)],
        scratch_shapes=[pltpu.VMEM((tr + SUBLANES, w3), F32),
                        pltpu.VMEM((GDN_HEADS, GDN_HEAD_DIM, GDN_HEAD_DIM), F32)],
        compiler_params=_params("parallel", "arbitrary"),
        name="gdn_chain",
    )(proj, proj, proj, proj, bdt, conv_buf, s0, gw["conv_w"], gw["alog_col"], gw["dtb_col"], gw["norm_w"],
      *_gate_matrices(GDN_TILE))


def _gdn_slab_kernel(q_ref, k_ref, v_ref, gate_ref, bdt_ref, bq_ref, bk_ref, bv_ref, s0_ref,
                     cwq_ref, cwk_ref, cwv_ref, alc_ref, dbc_ref, nw_ref, gstack_ref, gupper_ref,
                     y_ref, sout_ref, oq_ref, ok_ref, ov_ref,
                     win):
    hp = s0_ref.shape[1]
    hw = hp * GDN_HEAD_DIM
    heads = range(hp)
    head0 = pl.program_id(1) * hp
    nslab = GDN_TILE // SLAB
    first = (pl.program_id(0) % (bq_ref.shape[1] // nslab)) * nslab
    real = (lax.broadcasted_iota(jnp.int32, (GDN_TILE, 1), 0) % SLAB) >= SLAB - SLAB_REAL
    qkv = []
    taps = range(GDN_CONV - 1)
    place = [_slab_row_selector(GDN_TILE, bq_ref.shape[1], 1 + i, first, transpose=True) for i in taps]
    take = [_slab_row_selector(GDN_TILE, nslab, SLAB - 3 + i, 0, transpose=False) for i in taps]
    for part, (x_ref, b_ref, cw_ref, o_ref) in enumerate(
            ((q_ref, bq_ref, cwq_ref, oq_ref), (k_ref, bk_ref, cwk_ref, ok_ref), (v_ref, bv_ref, cwv_ref, ov_ref))):
        cs = slice(part * hw, (part + 1) * hw)
        x = x_ref[...]
        for i in taps:
            x = x + _dot_split(place[i], b_ref[i])
        win[0:SUBLANES, cs] = jnp.zeros((SUBLANES, hw), F32)
        win[SUBLANES:, cs] = x
        conv = jnp.where(real, _silu(_conv4(win, cw_ref[...], cs)), 0.0)
        qkv.append([conv[:, h * GDN_HEAD_DIM:(h + 1) * GDN_HEAD_DIM] for h in heads])
        pieces = _split3(x)
        for i in taps:
            o_ref[i] = sum(_dot(take[i], p) for p in pieces)

    beta, gc, gl, gct, causal, strict = _gdn_gates(bdt_ref[...], alc_ref[...], dbc_ref[...], gstack_ref[...],
                                                   gupper_ref[...], SLAB, True)
    lane = lax.broadcasted_iota(jnp.int32, (1, LANES), 1)
    sub = lax.broadcasted_iota(jnp.int32, (SUBLANES, 1), 0)
    pick = lambda a, idx: jnp.sum(jnp.where(lane == idx, a, 0.0), axis=1, keepdims=True)
    gcols = [pick(gc, GDN_HEADS + head0 + h) for h in heads]
    glcols = [pick(gl, GDN_HEADS + head0 + h) for h in heads]
    intra = _gdn_intra(qkv[0], qkv[1], qkv[2], [pick(beta, head0 + h) for h in heads], gcols,
                       [jnp.sum(jnp.where(sub == head0 + h, gct, 0.0), axis=0, keepdims=True) for h in heads],
                       glcols, causal, strict, _merge_masks(SLAB_REAL))
    for h in heads:
        u, wk, attn, qg, kdt = intra[h]
        res = []
        for i in range(nslab):
            rows = slice(i * SLAB, (i + 1) * SLAB)
            lhs = jnp.concatenate([wk[rows], qg[rows]], axis=0).astype(BF16)
            res.append(_dot(lhs, s0_ref[i, h].astype(BF16)))
        vnb = (u - jnp.concatenate([r[:SLAB] for r in res], axis=0)).astype(BF16)
        o = jnp.concatenate([r[SLAB:] for r in res], axis=0) + _dot(attn, vnb)
        hs = slice(h * GDN_HEAD_DIM, (h + 1) * GDN_HEAD_DIM)
        y_ref[:, hs] = _gdn_out(o, gate_ref[:, hs], nw_ref[...])
        egl = jnp.exp(glcols[h])
        for i in range(nslab):
            in_slab = (lane // SLAB) == i
            upd = _dot(jnp.where(in_slab, kdt, jnp.zeros_like(kdt)), vnb)
            sout_ref[i, h] = s0_ref[i, h] * egl[i * SLAB:i * SLAB + 1, :] + upd


def _gdn_slab(proj, bdt, conv_buf, s0, gw, *, hp=GDN_SLAB_HEADS):
    rows = proj.shape[0]
    nb = rows // SLAB
    nslab = GDN_TILE // SLAB
    hw = hp * GDN_HEAD_DIM
    hb = GDN_WIDTH // hw
    const = lambda shape: pl.BlockSpec(shape, lambda *_: (0,) * len(shape))
    col = lambda g: pl.BlockSpec((GDN_TILE, hw), lambda i, h, g=g: (i, g * hb + h))
    per = LANES // nslab
    buf = lambda g: pl.BlockSpec((GDN_CONV - 1, LANES, hw), lambda i, h, g=g: (0, i // per, g * hb + h))
    cwb = lambda g: pl.BlockSpec((GDN_CONV, hw), lambda i, h, g=g: (0, g * hb + h))
    st = pl.BlockSpec((nslab, hp, GDN_HEAD_DIM, GDN_HEAD_DIM), lambda i, h: (i, h, 0, 0))
    obuf = pl.BlockSpec((GDN_CONV - 1, nslab, hw), lambda i, h: (0, i, h))
    return pl.pallas_call(
        _gdn_slab_kernel,
        grid=(rows // GDN_TILE, GDN_HEADS // hp),
        in_specs=[col(1), col(2), col(3), col(4),
                  pl.BlockSpec((2 * GDN_HEADS, GDN_TILE), lambda i, h: (0, i)),
                  buf(0), buf(1), buf(2), st, cwb(0), cwb(1), cwb(2),
                  const((GDN_HEADS, 1)), const((GDN_HEADS, 1)), const((1, GDN_HEAD_DIM)),
                  const((3 * GDN_TILE, GDN_TILE)), const((GDN_TILE, GDN_TILE))],
        out_specs=[pl.BlockSpec((GDN_TILE, hw), lambda i, h: (i, h)), st, obuf, obuf, obuf],
        out_shape=[jax.ShapeDtypeStruct((rows, GDN_WIDTH), F32),
                   jax.ShapeDtypeStruct(s0.shape, F32)]
                  + [jax.ShapeDtypeStruct((GDN_CONV - 1, nb, GDN_WIDTH), F32)] * 3,
        scratch_shapes=[pltpu.VMEM((GDN_TILE + SUBLANES, 3 * hw), F32)],
        compiler_params=_params("parallel", "arbitrary"),
        name="gdn_slab",
    )(proj, proj, proj, proj, bdt, conv_buf, conv_buf, conv_buf, s0,
      gw["conv_w"], gw["conv_w"], gw["conv_w"], gw["alog_col"], gw["dtb_col"], gw["norm_w"],
      *_gate_matrices(SLAB))


def _block_diag(w):
    per = S5_GROUPS // S5_BLOCKS
    g, a, b = w.shape
    w = w.reshape(S5_BLOCKS, per, a, b)
    eye = jnp.eye(per, dtype=w.dtype)
    return jnp.einsum("jgab,gk->jgakb", w, eye).reshape(S5_BLOCKS, per * a, per * b)


def _layer_weights(l, ln1_g, ln1_b, ffn1_w_in, ffn1_w_out, w_mix_in, s5_lambda_re, s5_lambda_im, s5_log_dt,
                   s5_b_re, s5_b_im, s5_c_re, s5_c_im, s5_d, s5_glu_w, s5_glu_b, gdn_conv_w, gdn_a_log,
                   gdn_dt_bias, gdn_norm_w, w_mix_out, ln2_g, ln2_b, ffn2_w_in, ffn2_w_out, ln3_g, ln3_b):
    row = lambda v: v[l].reshape(1, -1).astype(F32)
    w = {
        "ln1": (row(ln1_g), row(ln1_b)), "ln2": (row(ln2_g), row(ln2_b)), "ln3": (row(ln3_g), row(ln3_b)),
        "ffn1": (ffn1_w_in[l], ffn1_w_out[l]),
        "ffn2": (ffn2_w_in[l], ffn2_w_out[l]),
        "mix_in": (w_mix_in.astype(BF16), l),
        "mix_out": w_mix_out[l].astype(BF16),
    }
    coef, bb_re, bb_im = _s5_disc(
        s5_lambda_re[l].reshape(1, S5_HID).astype(F32), s5_lambda_im[l].reshape(1, S5_HID).astype(F32),
        jnp.repeat(s5_log_dt[l], S5_STATE).reshape(1, S5_HID).astype(F32),
        _block_diag(jnp.swapaxes(s5_b_re[l], 1, 2).astype(F32)),
        _block_diag(jnp.swapaxes(s5_b_im[l], 1, 2).astype(F32)))
    w["s5"] = {
        "coef": coef, "b_re": bb_re, "b_im": bb_im,
        "c_re": _block_diag(jnp.swapaxes(s5_c_re[l], 1, 2)).astype(BF16),
        "c_im": _block_diag(jnp.swapaxes(s5_c_im[l], 1, 2)).astype(BF16),
        "d": row(s5_d), "glu_w": s5_glu_w[l].astype(BF16), "glu_b": row(s5_glu_b),
    }
    w["gdn"] = {
        "conv_w": gdn_conv_w[l].astype(F32),
        "alog_col": gdn_a_log[l].reshape(GDN_HEADS, 1).astype(F32),
        "dtb_col": gdn_dt_bias[l].reshape(GDN_HEADS, 1).astype(F32),
        "norm_w": row(gdn_norm_w),
    }
    return w


def _prompt_layer(x, w, nseq):
    x = _ffn_ln(x, *w["ffn1"], *w["ln1"])
    proj, bdt = _mixin(x, *w["mix_in"])
    z_s5 = jnp.zeros((nseq, 1, S5_HID), F32)
    y_s5, n_re, n_im = _s5_mixer(proj, z_s5, z_s5, w["s5"], chain=True, nseq=nseq)
    z_gdn = jnp.zeros((nseq, GDN_HEADS, GDN_HEAD_DIM, GDN_HEAD_DIM), F32)
    z_buf = jnp.zeros((nseq, GDN_CONV - 1, 3 * GDN_WIDTH), F32)
    y_gdn, n_s, n_buf = _gdn_chain(proj, bdt, z_buf, z_gdn, w["gdn"], nseq=nseq)
    x = _mixout_ln(y_s5, y_gdn, x, w["mix_out"], *w["ln2"])
    x = _ffn_ln(x, *w["ffn2"], *w["ln3"])
    shape = (nseq, S5_GROUPS, S5_STATE)
    return x, n_re.reshape(shape), n_im.reshape(shape), n_s, n_buf


def _sample_layer(x, s5_re, s5_im, gdn_s, conv_buf, w, nb, t):
    x = _ffn_ln(x, *w["ffn1"], *w["ln1"])
    xs = jnp.pad(x.reshape(nb, t, D_MODEL), ((0, 0), (SLAB - t, 0), (0, 0))).reshape(nb * SLAB, D_MODEL)
    proj, bdt = _mixin(xs, *w["mix_in"])
    y_s5, n_re, n_im = _s5_mixer(proj, s5_re.reshape(nb, S5_HID).astype(F32),
                                 s5_im.reshape(nb, S5_HID).astype(F32), w["s5"], chain=False, nseq=nb)
    y_gdn, n_s, bq, bk, bv = _gdn_slab(proj, bdt, jnp.swapaxes(conv_buf.astype(F32), 0, 1),
                                       gdn_s.astype(F32), w["gdn"])
    xs = _mixout_ln(y_s5, y_gdn, xs, w["mix_out"], *w["ln2"])
    x = xs.reshape(nb, SLAB, D_MODEL)[:, SLAB - t:].reshape(nb * t, D_MODEL)
    x = _ffn_ln(x, *w["ffn2"], *w["ln3"])
    shape = (nb, S5_GROUPS, S5_STATE)
    n_buf = jnp.swapaxes(jnp.concatenate([bq, bk, bv], axis=-1), 0, 1)
    return x, n_re.reshape(shape), n_im.reshape(shape), n_s, n_buf


def kernel(x_prompt, x_sample, state_s5_re, state_s5_im, state_gdn, state_conv, ln1_g, ln1_b, ffn1_w_in, ffn1_w_out, w_mix_in, s5_lambda_re, s5_lambda_im, s5_log_dt, s5_b_re, s5_b_im, s5_c_re, s5_c_im, s5_d, s5_glu_w, s5_glu_b, gdn_conv_w, gdn_a_log, gdn_dt_bias, gdn_norm_w, w_mix_out, ln2_g, ln2_b, ffn2_w_in, ffn2_w_out, ln3_g, ln3_b):
    bp, tp, _ = x_prompt.shape
    bs, ts, _ = x_sample.shape
    assert ts == SLAB_REAL and tp % S5_TILE == 0 and (bs * SLAB) % S5_TILE == 0
    depth = ln1_g.shape[0]
    yp = x_prompt.astype(F32).reshape(bp * tp, D_MODEL)
    ys = x_sample.astype(F32).reshape(bs * ts, D_MODEL)
    outs = [[] for _ in range(8)]
    for l in range(depth):
        w = _layer_weights(l, ln1_g, ln1_b, ffn1_w_in, ffn1_w_out, w_mix_in, s5_lambda_re, s5_lambda_im,
                           s5_log_dt, s5_b_re, s5_b_im, s5_c_re, s5_c_im, s5_d, s5_glu_w, s5_glu_b,
                           gdn_conv_w, gdn_a_log, gdn_dt_bias, gdn_norm_w, w_mix_out, ln2_g, ln2_b,
                           ffn2_w_in, ffn2_w_out, ln3_g, ln3_b)
        yp, *p_state = _prompt_layer(yp, w, bp)
        ys, *s_state = _sample_layer(ys, state_s5_re[l], state_s5_im[l], state_gdn[l], state_conv[l], w, bs, ts)
        for acc, val in zip(outs, p_state + s_state):
            acc.append(val)
    return (yp.reshape(x_prompt.shape).astype(x_prompt.dtype), ys.reshape(x_sample.shape).astype(x_sample.dtype),
            *(o[0][None] if depth == 1 else jnp.stack(o) for o in outs))
```

```python
import functools
import math

import jax
import jax.numpy as jnp
from jax import lax
from jax.experimental import pallas as pl
from jax.experimental.pallas import tpu as pltpu

F32 = jnp.float32
BF16 = jnp.bfloat16

D_MODEL = 2048
S5_WIDTH = 1024
S5_GROUPS = 64
S5_STATE = 64
S5_HID = S5_GROUPS * S5_STATE
GDN_WIDTH = 1024
GDN_HEAD_DIM = 128
GDN_HEADS = 8
GDN_CONV = 4
D_FF = 5632
MIX_MAIN = 5120
DEEP_ALPHA = 2.0 ** 0.25
LN_EPS = 1e-5
NORM_EPS = 1e-6

SUBLANES = 8
LANES = 128
SLAB = 8
SLAB_REAL = 4
GDN_TILE = 128
GDN_SLAB_HEADS = 8
GDN_CHAIN_CHUNKS = 2
FFN_TM = 1024
FFN_TF = 256
FFN_CHUNK = 512
S5_TILE = 256
S5_SEG = S5_TILE // SUBLANES
S5_BLOCKS = 8
S5_SCAN_GROUP = 4
VMEM_LIMIT = 56 * 1024 * 1024

NT_DIMS = (((1,), (1,)), ((), ()))


def _dot(a, b, **kw):
    return jnp.dot(a, b, preferred_element_type=F32, **kw)


def _silu(x):
    return x * jax.nn.sigmoid(x)


def _layer_norm(y, g, b):
    mu = jnp.mean(y, axis=-1, keepdims=True)
    d = y - mu
    var = jnp.mean(d * d, axis=-1, keepdims=True)
    return d * lax.rsqrt(var + LN_EPS) * g + b


def _slab_row_selector(rows, nseq, slab_row, first, *, transpose):
    shape = (rows, nseq) if transpose else (nseq, rows)
    r = lax.broadcasted_iota(jnp.int32, shape, 0 if transpose else 1)
    b = lax.broadcasted_iota(jnp.int32, shape, 1 if transpose else 0)
    return _as_bf16(r == SLAB * (b - first) + slab_row)


def _as_bf16(mask):
    return mask.astype(F32).astype(BF16)


def _split3(x):
    pieces = []
    for _ in range(3):
        pieces.append(x.astype(BF16))
        x = x - pieces[-1].astype(F32)
    return pieces


def _dot_split(a, b, dims=None):
    f32_is_lhs = a.dtype == F32
    acc = None
    for piece in _split3(a if f32_is_lhs else b):
        lhs, rhs = (piece, b) if f32_is_lhs else (a, piece)
        d = _dot(lhs, rhs) if dims is None else lax.dot_general(lhs, rhs, dims, preferred_element_type=F32)
        acc = d if acc is None else acc + d
    return acc


def _params(*sem):
    return pltpu.CompilerParams(dimension_semantics=sem, vmem_limit_bytes=VMEM_LIMIT)


def _ffn_kernel(x_ref, wg_ref, wu_ref, wo_ref, g_ref, b_ref, o_ref, xb_ref):
    j = pl.program_id(1)

    @pl.when(j == 0)
    def _():
        o_ref[...] = jnp.zeros_like(o_ref)
        xb_ref[...] = x_ref[...].astype(BF16)

    xb = xb_ref[...]
    gate = _dot(xb, wg_ref[...].astype(BF16))
    up = _dot(xb, wu_ref[...].astype(BF16))
    h = (_silu(gate) * up).astype(BF16)
    for c in range(0, D_MODEL, FFN_CHUNK):
        o_ref[:, c:c + FFN_CHUNK] += _dot(h, wo_ref[:, c:c + FFN_CHUNK].astype(BF16))

    @pl.when(j == pl.num_programs(1) - 1)
    def _():
        for r in range(0, o_ref.shape[0], FFN_CHUNK // 2):
            rows = slice(r, r + FFN_CHUNK // 2)
            y = DEEP_ALPHA * x_ref[rows, :] + 0.5 * o_ref[rows, :]
            o_ref[rows, :] = _layer_norm(y, g_ref[...], b_ref[...])


def _ffn_ln(x, w_in, w_out, g, b):
    n = x.shape[0]
    tm = math.gcd(n, FFN_TM)
    tf = FFN_TF * FFN_TM // tm
    nff = D_FF // tf
    return pl.pallas_call(
        _ffn_kernel,
        grid=(n // tm, nff),
        in_specs=[
            pl.BlockSpec((tm, D_MODEL), lambda i, j: (i, 0)),
            pl.BlockSpec((D_MODEL, tf), lambda i, j: (0, j)),
            pl.BlockSpec((D_MODEL, tf), lambda i, j: (0, j + nff)),
            pl.BlockSpec((tf, D_MODEL), lambda i, j: (j, 0)),
            pl.BlockSpec((1, D_MODEL), lambda i, j: (0, 0)),
            pl.BlockSpec((1, D_MODEL), lambda i, j: (0, 0)),
        ],
        out_specs=pl.BlockSpec((tm, D_MODEL), lambda i, j: (i, 0)),
        out_shape=jax.ShapeDtypeStruct((n, D_MODEL), F32),
        scratch_shapes=[pltpu.VMEM((tm, D_MODEL), BF16)],
        compiler_params=_params("parallel", "arbitrary"),
        name="ffn_ln",
    )(x, w_in, w_in, w_out, g, b)


def _mixin_kernel(x_ref, w_ref, wt_ref, o_ref, ot_ref, xb_ref):
    @pl.when(pl.program_id(1) == 0)
    def _():
        xb_ref[...] = x_ref[...].astype(BF16)
        ncols = ot_ref.shape[0]
        lane = lax.broadcasted_iota(jnp.int32, (1, LANES), 1)
        wt = jnp.where(lane < ncols, wt_ref[...], 0.0).astype(BF16)
        ot_ref[...] = _dot(xb_ref[...], wt).T[:ncols, :]

    o_ref[...] = _dot(xb_ref[...], w_ref[...].astype(BF16))


def _mixin(x, w, layer, *, tm=1024, tn=1024):
    n = x.shape[0]
    return pl.pallas_call(
        _mixin_kernel,
        grid=(n // tm, MIX_MAIN // tn),
        in_specs=[pl.BlockSpec((tm, D_MODEL), lambda i, j: (i, 0)),
                  pl.BlockSpec((None, D_MODEL, tn), lambda i, j: (layer, 0, j)),
                  pl.BlockSpec((None, D_MODEL, LANES), lambda i, j: (layer, 0, MIX_MAIN // LANES))],
        out_specs=[pl.BlockSpec((tm, tn), lambda i, j: (i, j)),
                   pl.BlockSpec((2 * GDN_HEADS, tm), lambda i, j: (0, i))],
        out_shape=[jax.ShapeDtypeStruct((n, MIX_MAIN), F32),
                   jax.ShapeDtypeStruct((2 * GDN_HEADS, n), F32)],
        scratch_shapes=[pltpu.VMEM((tm, D_MODEL), BF16)],
        compiler_params=_params("parallel", "arbitrary"),
        name="mix_in",
    )(x, w, w)


def _mixout_kernel(ya_ref, yb_ref, x_ref, w_ref, g_ref, b_ref, o_ref):
    mix = (_dot(ya_ref[...].astype(BF16), w_ref[0:S5_WIDTH, :])
           + _dot(yb_ref[...].astype(BF16), w_ref[S5_WIDTH:, :]))
    o_ref[...] = _layer_norm(DEEP_ALPHA * x_ref[...] + mix, g_ref[...], b_ref[...])


def _mixout_ln(ya, yb, x, w, g, b, *, tm=512):
    n = x.shape[0]
    return pl.pallas_call(
        _mixout_kernel,
        grid=(n // tm,),
        in_specs=[pl.BlockSpec((tm, S5_WIDTH), lambda i: (i, 0)),
                  pl.BlockSpec((tm, GDN_WIDTH), lambda i: (i, 0)),
                  pl.BlockSpec((tm, D_MODEL), lambda i: (i, 0)),
                  pl.BlockSpec((D_MODEL, D_MODEL), lambda i: (0, 0)),
                  pl.BlockSpec((1, D_MODEL), lambda i: (0, 0)),
                  pl.BlockSpec((1, D_MODEL), lambda i: (0, 0))],
        out_specs=pl.BlockSpec((tm, D_MODEL), lambda i: (i, 0)),
        out_shape=jax.ShapeDtypeStruct((n, D_MODEL), F32),
        compiler_params=_params("parallel"),
        name="mix_out_ln",
    )(ya, yb, x, w, g, b)


def _s5_disc_kernel(lre_ref, lim_ref, ldt_ref, bre_ref, bim_ref, coef_ref, bbre_ref, bbim_ref):
    lr, li = lre_ref[...], lim_ref[...]
    dt = jnp.exp(ldt_ref[...])
    mag = jnp.exp(lr * dt)
    ar = mag * jnp.cos(li * dt)
    ai = mag * jnp.sin(li * dt)
    nr, ni = ar - 1.0, ai
    den = lr * lr + li * li
    c_re = (nr * lr + ni * li) / den
    c_im = (ni * lr - nr * li) / den
    bw = S5_HID // S5_BLOCKS
    for j in range(S5_BLOCKS):
        cr, ci = c_re[:, j * bw:(j + 1) * bw], c_im[:, j * bw:(j + 1) * bw]
        bbre_ref[j] = (cr * bre_ref[j] - ci * bim_ref[j]).astype(BF16)
        bbim_ref[j] = (cr * bim_ref[j] + ci * bre_ref[j]).astype(BF16)

    def cmul(x, y):
        return x[0] * y[0] - x[1] * y[1], x[0] * y[1] + x[1] * y[0]

    width = lr.shape[-1]
    p = (ar, ai)
    for _ in range(S5_SEG - 1):
        p = cmul(p, (ar, ai))
    coef_ref[...] = jnp.zeros_like(coef_ref)
    for part, v in enumerate((ar, ai)):
        coef_ref[part] = jnp.broadcast_to(v, (SUBLANES, width))
    for k, s in enumerate((1, 2, 4)):
        for part in range(2):
            coef_ref[2 + 2 * k + part, s:SUBLANES, :] = jnp.broadcast_to(p[part], (SUBLANES - s, width))
        p = cmul(p, p)


def _s5_disc(lre, lim, ldt, b_re, b_im):
    return pl.pallas_call(
        _s5_disc_kernel,
        out_shape=[jax.ShapeDtypeStruct((8, SUBLANES, S5_HID), F32),
                   jax.ShapeDtypeStruct(b_re.shape, BF16), jax.ShapeDtypeStruct(b_im.shape, BF16)],
        name="s5_disc",
    )(lre, lim, ldt, b_re, b_im)


def _gelu_tanh(y):
    return 0.5 * y * (1.0 + jnp.tanh(math.sqrt(2.0 / math.pi) * (y + 0.044715 * (y * y * y))))


def _scan_layout(tt, group):
    p = jnp.arange(tt)
    rem = p % group
    src = (p - rem) + (rem % SUBLANES) * (group // SUBLANES) + rem // SUBLANES
    return (src[:, None] == jnp.arange(tt)[None, :]).astype(BF16)


def _cmul_add(ar, ai, xr, xi, br, bi):
    return ar * xr - ai * xi + br, ar * xi + ai * xr + bi


def _s5_kernel(u_ref, h0re_ref, h0im_ref, perm_ref, back_ref, coef_ref, bre_ref, bim_ref, cwre_ref, cwim_ref,
               d_ref, gw_ref, gb_ref, y_ref, sre_ref, sim_ref, hre, him, ysc, car_re, car_im,
               *, chain, tt):
    group = tt if chain else SLAB * SUBLANES
    seg = group // SUBLANES
    u = u_ref[...]
    ub = _dot(perm_ref[...], u.astype(BF16)).astype(BF16)
    bw = S5_HID // S5_BLOCKS
    gw = S5_WIDTH // S5_BLOCKS
    if chain:
        @pl.when(pl.program_id(1) == 0)
        def _():
            car_re[...] = jnp.broadcast_to(h0re_ref[...], car_re.shape)
            car_im[...] = jnp.broadcast_to(h0im_ref[...], car_im.shape)

    sub = lax.broadcasted_iota(jnp.int32, (SUBLANES, 1), 0)
    def project_in(c):
        uc = ub[:, c * gw:(c + 1) * gw]
        hre[c] = _dot(uc, bre_ref[c])
        him[c] = _dot(uc, bim_ref[c])

    def project_out(c):
        ysc[:, c * gw:(c + 1) * gw] = (_dot(hre[c].astype(BF16), cwre_ref[c])
                                       - _dot(him[c].astype(BF16), cwim_ref[c]))

    def scan(blocks):
        sls = [slice(c * bw, (c + 1) * bw) for c in blocks]
        refs = [(hre.at[c], him.at[c]) for c in blocks]
        coefs = [(coef_ref[0, :, sl], coef_ref[1, :, sl]) for sl in sls]
        n = range(len(blocks))
        if not chain:
            for g in range(tt // group):
                srows = slice(g * SUBLANES, (g + 1) * SUBLANES)
                x = [(h0re_ref[srows, sl], h0im_ref[srows, sl]) for sl in sls]
                for j in range(SLAB - SLAB_REAL, SLAB):
                    rows = slice(g * group + j * SUBLANES, g * group + (j + 1) * SUBLANES)
                    for i in n:
                        hr, hi = refs[i]
                        x[i] = _cmul_add(*coefs[i], *x[i], hr[rows, :], hi[rows, :])
                        hr[rows, :], hi[rows, :] = x[i]
                for i in n:
                    sre_ref[srows, sls[i]], sim_ref[srows, sls[i]] = x[i]
            return
        x = [(jnp.zeros((SUBLANES, bw), F32),) * 2 for _ in n]
        for j in range(seg):
            rows = slice(j * SUBLANES, (j + 1) * SUBLANES)
            for i in n:
                hr, hi = refs[i]
                x[i] = _cmul_add(*coefs[i], *x[i], hr[rows, :], hi[rows, :])
                hr[rows, :], hi[rows, :] = x[i]
        ks = []
        for i in n:
            sl = sls[i]
            kr = jnp.where(sub == 0, car_re[:, sl], pltpu.roll(x[i][0], 1, 0))
            ki = jnp.where(sub == 0, car_im[:, sl], pltpu.roll(x[i][1], 1, 0))
            for k, s in enumerate((1, 2, 4)):
                pr, pi = coef_ref[2 + 2 * k, :, sl], coef_ref[3 + 2 * k, :, sl]
                kr, ki = _cmul_add(pr, pi, pltpu.roll(kr, s, 0), pltpu.roll(ki, s, 0), kr, ki)
            outr, outi = _cmul_add(coef_ref[2, :, sl], coef_ref[3, :, sl], kr, ki, *x[i])
            car_re[:, sl] = jnp.broadcast_to(outr[SUBLANES - 1:SUBLANES, :], outr.shape)
            car_im[:, sl] = jnp.broadcast_to(outi[SUBLANES - 1:SUBLANES, :], outi.shape)
            ks.append((kr, ki))
        for j in range(seg):
            rows = slice(j * SUBLANES, (j + 1) * SUBLANES)
            for i in n:
                (ar, ai), (kr, ki), (hr, hi) = coefs[i], ks[i], refs[i]
                ks[i] = (ar * kr - ai * ki, ar * ki + ai * kr)
                hr[rows, :] += ks[i][0]
                hi[rows, :] += ks[i][1]

    groups = [list(range(c, c + S5_SCAN_GROUP)) for c in range(0, S5_BLOCKS, S5_SCAN_GROUP)]
    for step in range(len(groups) + 2):
        if step < len(groups):
            for c in groups[step]:
                project_in(c)
        if 1 <= step <= len(groups):
            scan(groups[step - 1])
        if step >= 2:
            for c in groups[step - 2]:
                project_out(c)

    if chain:
        @pl.when(pl.program_id(1) == pl.num_programs(1) - 1)
        def _():
            sre_ref[...] = car_re[0:1, :]
            sim_ref[...] = car_im[0:1, :]

    y = _dot_split(back_ref[...], ysc[...])
    z = _gelu_tanh(y + d_ref[...] * u)
    gl = _dot(z.astype(BF16), gw_ref[...]) + gb_ref[...]
    y_ref[...] = z * jax.nn.sigmoid(gl)


def _s5_mixer(proj, h0re, h0im, sw, *, chain, nseq, tt=S5_TILE):
    rows = proj.shape[0]
    const = lambda shape: pl.BlockSpec(shape, lambda *_: (0,) * len(shape))
    if chain:
        assert tt == S5_SEG * SUBLANES
        nt = rows // nseq // tt
        grid = (nseq, nt)
        u_spec = pl.BlockSpec((tt, S5_WIDTH), lambda b, t: (b * nt + t, 0))
        st_spec = pl.BlockSpec((None, 1, S5_HID), lambda b, t: (b, 0, 0))
        st_shape = jax.ShapeDtypeStruct((nseq, 1, S5_HID), F32)
        sem = ("parallel", "arbitrary")
    else:
        grid = (rows // tt,)
        u_spec = pl.BlockSpec((tt, S5_WIDTH), lambda i: (i, 0))
        st_spec = pl.BlockSpec((tt // SLAB, S5_HID), lambda i: (i, 0))
        st_shape = jax.ShapeDtypeStruct((rows // SLAB, S5_HID), F32)
        sem = ("parallel",)
    y_spec = u_spec
    bw = S5_HID // S5_BLOCKS
    gw = S5_WIDTH // S5_BLOCKS
    perm = _scan_layout(tt, tt if chain else SLAB * SUBLANES)
    return pl.pallas_call(
        functools.partial(_s5_kernel, chain=chain, tt=tt),
        grid=grid,
        in_specs=[u_spec, st_spec, st_spec,
                  const((tt, tt)), const((tt, tt)), const((8, SUBLANES, S5_HID)),
                  const((S5_BLOCKS, gw, bw)), const((S5_BLOCKS, gw, bw)),
                  const((S5_BLOCKS, bw, gw)), const((S5_BLOCKS, bw, gw)),
                  const((1, S5_WIDTH)), const((S5_WIDTH, S5_WIDTH)), const((1, S5_WIDTH))],
        out_specs=[y_spec, st_spec, st_spec],
        out_shape=[jax.ShapeDtypeStruct((rows, S5_WIDTH), F32), st_shape, st_shape],
        scratch_shapes=[pltpu.VMEM((S5_BLOCKS, tt, bw), F32), pltpu.VMEM((S5_BLOCKS, tt, bw), F32),
                        pltpu.VMEM((tt, S5_WIDTH), F32),
                        pltpu.VMEM((SUBLANES, S5_HID), F32), pltpu.VMEM((SUBLANES, S5_HID), F32)],
        compiler_params=_params(*sem),
        name="s5_chain" if chain else "s5_slab",
    )(proj, h0re, h0im, perm, perm.T, sw["coef"], sw["b_re"], sw["b_im"], sw["c_re"], sw["c_im"],
      sw["d"], sw["glu_w"], sw["glu_b"])


def _softplus(x):
    return jnp.maximum(x, 0.0) + jnp.log1p(jnp.exp(-jnp.abs(x)))


def _segment_masks(seg):
    ri = lax.broadcasted_iota(jnp.int32, (GDN_TILE, GDN_TILE), 0)
    ci = lax.broadcasted_iota(jnp.int32, (GDN_TILE, GDN_TILE), 1)
    same = (ri // seg) == (ci // seg)
    causal = (ri >= ci) & same
    strict = (ri > ci) & same
    return same, causal, strict


def _gate_matrices(seg):
    i = jnp.arange(GDN_TILE)
    same = (i[:, None] // seg) == (i[None, :] // seg)
    stack = jnp.concatenate([i[:, None] == i[None, :], (i[:, None] >= i[None, :]) & same, same], axis=0)
    return stack.astype(BF16), ((i[:, None] <= i[None, :]) & same).astype(BF16)


def _gdn_gates(bdt, alog_col, dtb_col, stack, upper, seg, slab):
    same, causal, strict = _segment_masks(seg)
    bt = jax.nn.sigmoid(bdt[:GDN_HEADS, :])
    gt = -jnp.exp(alog_col) * _softplus(bdt[GDN_HEADS:, :] + dtb_col)
    if slab:
        creal = (lax.broadcasted_iota(jnp.int32, (1, GDN_TILE), 1) % SLAB) >= SLAB - SLAB_REAL
        bt = jnp.where(creal, bt, 0.0)
        gt = jnp.where(creal, gt, 0.0)
    gates = jnp.concatenate([bt, gt, jnp.zeros((GDN_TILE - 2 * GDN_HEADS, GDN_TILE), F32)], axis=0)
    cols = _dot_split(stack, gates, NT_DIMS)
    beta, gc, gl = cols[:GDN_TILE], cols[GDN_TILE:2 * GDN_TILE], cols[2 * GDN_TILE:]
    gct = _dot_split(gt, upper)
    return beta, gc, gl, gct, causal, strict


def _merge_masks(top):
    ri = lax.broadcasted_iota(jnp.int32, (GDN_TILE, GDN_TILE), 0)
    ci = lax.broadcasted_iota(jnp.int32, (GDN_TILE, GDN_TILE), 1)
    masks = []
    s = 1
    while s < top:
        masks.append(((ri // (2 * s)) == (ci // (2 * s))) & ((ri // s) != (ci // s)))
        s *= 2
    return masks


def _unit_lower_inverse(ms, masks):
    es = [-jnp.where(masks[0], m, 0.0) for m in ms]
    for mask in masks[1:]:
        cs = [jnp.where(mask, m, 0.0) for m in ms]
        ebs = [e.astype(BF16) for e in es]
        xs = [c + _dot(eb, c.astype(BF16)) for c, eb in zip(cs, ebs)]
        es = [e - (x + _dot(x.astype(BF16), eb)) for e, x, eb in zip(es, xs, ebs)]
    return es


def _gdn_intra(qs, ks, vs, betas, gcols, grows, glcols, causal, strict, masks):
    n = range(len(qs))
    qn = [q * lax.rsqrt(jnp.sum(q * q, axis=-1, keepdims=True) + NORM_EPS) * (GDN_HEAD_DIM ** -0.5) for q in qs]
    kn = [k * lax.rsqrt(jnp.sum(k * k, axis=-1, keepdims=True) + NORM_EPS) for k in ks]
    decay = [jnp.exp(jnp.where(causal, gcols[i] - grows[i], -jnp.inf)) for i in n]
    kb = [kn[i] * betas[i] for i in n]
    knb = [k.astype(BF16) for k in kn]
    kk = [lax.dot_general(kb[i].astype(BF16), knb[i], NT_DIMS, preferred_element_type=F32) for i in n]
    ms = [kk[i] * jnp.where(strict, decay[i], 0.0) for i in n]
    eg = [jnp.exp(g) for g in gcols]
    attn = [(lax.dot_general(qn[i].astype(BF16), knb[i], NT_DIMS, preferred_element_type=F32) * decay[i]).astype(BF16)
            for i in n]
    qg = [qn[i] * eg[i] for i in n]
    kdt = [(kn[i] * jnp.exp(glcols[i] - gcols[i])).T.astype(BF16) for i in n]
    rhs = [jnp.concatenate([vs[i] * betas[i], kb[i] * eg[i]], axis=1) for i in n]
    ys = _unit_lower_inverse(ms, masks)
    uw = [rhs[i] + _dot(ys[i].astype(BF16), rhs[i].astype(BF16)) for i in n]
    return [(uw[i][:, :GDN_HEAD_DIM], uw[i][:, GDN_HEAD_DIM:], attn[i], qg[i], kdt[i]) for i in n]


def _gdn_out(o, gate, norm_w):
    o = o * lax.rsqrt(jnp.mean(o * o, axis=-1, keepdims=True) + NORM_EPS) * norm_w
    return o * _silu(gate)


def _conv4(win_ref, cw, width_slice):
    x = win_ref[:, width_slice]
    x1 = pltpu.roll(x, 1, 0)
    z = cw[1:2, :] * x + cw[0:1, :] * x1
    out = cw[3:4, :] * x + cw[2:3, :] * x1 + pltpu.roll(z, 2, 0)
    return out[SUBLANES:, :]


def _gdn_chain_kernel(q_ref, k_ref, v_ref, gate_ref, bdt_ref, buf_ref, s0_ref, cw_ref,
                      alc_ref, dbc_ref, nw_ref, gstack_ref, gupper_ref,
                      y_ref, sout_ref, bufout_ref, win, s_sc):
    t = pl.program_id(1)
    w3 = 3 * GDN_WIDTH

    @pl.when(t == 0)
    def _():
        win[0:SUBLANES, :] = jnp.zeros((SUBLANES, w3), F32)
        win[SUBLANES - 3:SUBLANES, :] = buf_ref[...]
        s_sc[...] = s0_ref[...]

    rows = q_ref.shape[0]
    nchunk = rows // GDN_TILE
    win[SUBLANES:, 0:GDN_WIDTH] = q_ref[...]
    win[SUBLANES:, GDN_WIDTH:2 * GDN_WIDTH] = k_ref[...]
    win[SUBLANES:, 2 * GDN_WIDTH:] = v_ref[...]

    masks = _merge_masks(GDN_TILE)
    heads = range(GDN_HEADS)
    qkv = [[], [], []]
    for part in range(3):
        conv = []
        for h in heads:
            cs = slice(part * GDN_WIDTH + h * GDN_HEAD_DIM, part * GDN_WIDTH + (h + 1) * GDN_HEAD_DIM)
            conv.append(_silu(_conv4(win, cw_ref[:, cs], cs)))
        qkv[part] = [x[c * GDN_TILE:(c + 1) * GDN_TILE] for c in range(nchunk) for x in conv]
    betas, gcols, glcols, grows = [], [], [], []
    for c in range(nchunk):
        beta, gc, gl, gct, causal, strict = _gdn_gates(
            bdt_ref[:, c * GDN_TILE:(c + 1) * GDN_TILE], alc_ref[...], dbc_ref[...], gstack_ref[...],
            gupper_ref[...], GDN_TILE, False)
        betas += [beta[:, h:h + 1] for h in heads]
        gcols += [gc[:, GDN_HEADS + h:GDN_HEADS + h + 1] for h in heads]
        glcols += [gl[:, GDN_HEADS + h:GDN_HEADS + h + 1] for h in heads]
        grows += [gct[h:h + 1, :] for h in heads]
    intra = _gdn_intra(qkv[0], qkv[1], qkv[2], betas, gcols, grows, glcols, causal, strict, masks)
    for c in range(nchunk):
        at = lambda h, c=c: c * GDN_HEADS + h
        crow = slice(c * GDN_TILE, (c + 1) * GDN_TILE)
        s_old = [s_sc[h] for h in heads]
        a = [_dot(jnp.concatenate([intra[at(h)][1], intra[at(h)][3]], axis=0).astype(BF16), s_old[h].astype(BF16))
             for h in heads]
        vnb = [(intra[at(h)][0] - a[h][:GDN_TILE]).astype(BF16) for h in heads]
        o = [a[h][GDN_TILE:] + _dot(intra[at(h)][2], vnb[h]) for h in heads]
        for h in heads:
            s_sc[h] = s_old[h] * jnp.exp(glcols[at(h)][0:1, :]) + _dot(intra[at(h)][4], vnb[h])
        for h in heads:
            hs = slice(h * GDN_HEAD_DIM, (h + 1) * GDN_HEAD_DIM)
            y_ref[crow, hs] = _gdn_out(o[h], gate_ref[crow, hs], nw_ref[...])

    win[0:SUBLANES, :] = win[rows:rows + SUBLANES, :]

    @pl.when(t == pl.num_programs(1) - 1)
    def _():
        sout_ref[...] = s_sc[...]
        bufout_ref[...] = win[rows + SUBLANES - 3:rows + SUBLANES, :]


def _gdn_chain(proj, bdt, conv_buf, s0, gw, *, nseq):
    rows = proj.shape[0]
    tr = GDN_CHAIN_CHUNKS * GDN_TILE
    nt = rows // nseq // tr
    w3 = 3 * GDN_WIDTH
    const = lambda shape: pl.BlockSpec(shape, lambda *_: (0,) * len(shape))
    col = lambda cb: pl.BlockSpec((tr, GDN_WIDTH), lambda b, t, cb=cb: (b * nt + t, cb))
    return pl.pallas_call(
        _gdn_chain_kernel,
        grid=(nseq, nt),
        in_specs=[col(1), col(2), col(3), col(4),
                  pl.BlockSpec((2 * GDN_HEADS, tr), lambda b, t: (0, b * nt + t)),
                  pl.BlockSpec((None, GDN_CONV - 1, w3), lambda b, t: (b, 0, 0)),
                  pl.BlockSpec((None, GDN_HEADS, GDN_HEAD_DIM, GDN_HEAD_DIM), lambda b, t: (b, 0, 0, 0)),
                  const((GDN_CONV, w3)), const((GDN_HEADS, 1)), const((GDN_HEADS, 1)), const((1, GDN_HEAD_DIM)),
                  const((3 * GDN_TILE, GDN_TILE)), const((GDN_TILE, GDN_TILE))],
        out_specs=[pl.BlockSpec((tr, GDN_WIDTH), lambda b, t: (b * nt + t, 0)),
                   pl.BlockSpec((None, GDN_HEADS, GDN_HEAD_DIM, GDN_HEAD_DIM), lambda b, t: (b, 0, 0, 0)),
                   pl.BlockSpec((None, GDN_CONV - 1, w3), lambda b, t: (b, 0, 0))],
        out_shape=[jax.ShapeDtypeStruct((rows, GDN_WIDTH), F32),
                   jax.ShapeDtypeStruct((nseq, GDN_HEADS, GDN_HEAD_DIM, GDN_HEAD_DIM), F32),
                   jax.ShapeDtypeStruct((nseq, GDN_CONV - 1, w3), F32)],
        scratch_shapes=[pltpu.VMEM((tr + SUBLANES, w3), F32),
                        pltpu.VMEM((GDN_HEADS, GDN_HEAD_DIM, GDN_HEAD_DIM), F32)],
        compiler_params=_params("parallel", "arbitrary"),
        name="gdn_chain",
    )(proj, proj, proj, proj, bdt, conv_buf, s0, gw["conv_w"], gw["alog_col"], gw["dtb_col"], gw["norm_w"],
      *_gate_matrices(GDN_TILE))


def _gdn_slab_kernel(q_ref, k_ref, v_ref, gate_ref, bdt_ref, bq_ref, bk_ref, bv_ref, s0_ref,
                     cwq_ref, cwk_ref, cwv_ref, alc_ref, dbc_ref, nw_ref, gstack_ref, gupper_ref,
                     y_ref, sout_ref, oq_ref, ok_ref, ov_ref,
                     win):
    hp = s0_ref.shape[1]
    hw = hp * GDN_HEAD_DIM
    heads = range(hp)
    head0 = pl.program_id(1) * hp
    nslab = GDN_TILE // SLAB
    first = (pl.program_id(0) % (bq_ref.shape[1] // nslab)) * nslab
    real = (lax.broadcasted_iota(jnp.int32, (GDN_TILE, 1), 0) % SLAB) >= SLAB - SLAB_REAL
    qkv = []
    taps = range(GDN_CONV - 1)
    place = [_slab_row_selector(GDN_TILE, bq_ref.shape[1], 1 + i, first, transpose=True) for i in taps]
    take = [_slab_row_selector(GDN_TILE, nslab, SLAB - 3 + i, 0, transpose=False) for i in taps]
    for part, (x_ref, b_ref, cw_ref, o_ref) in enumerate(
            ((q_ref, bq_ref, cwq_ref, oq_ref), (k_ref, bk_ref, cwk_ref, ok_ref), (v_ref, bv_ref, cwv_ref, ov_ref))):
        cs = slice(part * hw, (part + 1) * hw)
        x = x_ref[...]
        for i in taps:
            x = x + _dot_split(place[i], b_ref[i])
        win[0:SUBLANES, cs] = jnp.zeros((SUBLANES, hw), F32)
        win[SUBLANES:, cs] = x
        conv = jnp.where(real, _silu(_conv4(win, cw_ref[...], cs)), 0.0)
        qkv.append([conv[:, h * GDN_HEAD_DIM:(h + 1) * GDN_HEAD_DIM] for h in heads])
        pieces = _split3(x)
        for i in taps:
            o_ref[i] = sum(_dot(take[i], p) for p in pieces)

    beta, gc, gl, gct, causal, strict = _gdn_gates(bdt_ref[...], alc_ref[...], dbc_ref[...], gstack_ref[...],
                                                   gupper_ref[...], SLAB, True)
    lane = lax.broadcasted_iota(jnp.int32, (1, LANES), 1)
    sub = lax.broadcasted_iota(jnp.int32, (SUBLANES, 1), 0)
    pick = lambda a, idx: jnp.sum(jnp.where(lane == idx, a, 0.0), axis=1, keepdims=True)
    gcols = [pick(gc, GDN_HEADS + head0 + h) for h in heads]
    glcols = [pick(gl, GDN_HEADS + head0 + h) for h in heads]
    intra = _gdn_intra(qkv[0], qkv[1], qkv[2], [pick(beta, head0 + h) for h in heads], gcols,
                       [jnp.sum(jnp.where(sub == head0 + h, gct, 0.0), axis=0, keepdims=True) for h in heads],
                       glcols, causal, strict, _merge_masks(SLAB_REAL))
    for h in heads:
        u, wk, attn, qg, kdt = intra[h]
        res = []
        for i in range(nslab):
            rows = slice(i * SLAB, (i + 1) * SLAB)
            lhs = jnp.concatenate([wk[rows], qg[rows]], axis=0).astype(BF16)
            res.append(_dot(lhs, s0_ref[i, h].astype(BF16)))
        vnb = (u - jnp.concatenate([r[:SLAB] for r in res], axis=0)).astype(BF16)
        o = jnp.concatenate([r[SLAB:] for r in res], axis=0) + _dot(attn, vnb)
        hs = slice(h * GDN_HEAD_DIM, (h + 1) * GDN_HEAD_DIM)
        y_ref[:, hs] = _gdn_out(o, gate_ref[:, hs], nw_ref[...])
        egl = jnp.exp(glcols[h])
        for i in range(nslab):
            in_slab = (lane // SLAB) == i
            upd = _dot(jnp.where(in_slab, kdt, jnp.zeros_like(kdt)), vnb)
            sout_ref[i, h] = s0_ref[i, h] * egl[i * SLAB:i * SLAB + 1, :] + upd


def _gdn_slab(proj, bdt, conv_buf, s0, gw, *, hp=GDN_SLAB_HEADS):
    rows = proj.shape[0]
    nb = rows // SLAB
    nslab = GDN_TILE // SLAB
    hw = hp * GDN_HEAD_DIM
    hb = GDN_WIDTH // hw
    const = lambda shape: pl.BlockSpec(shape, lambda *_: (0,) * len(shape))
    col = lambda g: pl.BlockSpec((GDN_TILE, hw), lambda i, h, g=g: (i, g * hb + h))
    per = LANES // nslab
    buf = lambda g: pl.BlockSpec((GDN_CONV - 1, LANES, hw), lambda i, h, g=g: (0, i // per, g * hb + h))
    cwb = lambda g: pl.BlockSpec((GDN_CONV, hw), lambda i, h, g=g: (0, g * hb + h))
    st = pl.BlockSpec((nslab, hp, GDN_HEAD_DIM, GDN_HEAD_DIM), lambda i, h: (i, h, 0, 0))
    obuf = pl.BlockSpec((GDN_CONV - 1, nslab, hw), lambda i, h: (0, i, h))
    return pl.pallas_call(
        _gdn_slab_kernel,
        grid=(rows // GDN_TILE, GDN_HEADS // hp),
        in_specs=[col(1), col(2), col(3), col(4),
                  pl.BlockSpec((2 * GDN_HEADS, GDN_TILE), lambda i, h: (0, i)),
                  buf(0), buf(1), buf(2), st, cwb(0), cwb(1), cwb(2),
                  const((GDN_HEADS, 1)), const((GDN_HEADS, 1)), const((1, GDN_HEAD_DIM)),
                  const((3 * GDN_TILE, GDN_TILE)), const((GDN_TILE, GDN_TILE))],
        out_specs=[pl.BlockSpec((GDN_TILE, hw), lambda i, h: (i, h)), st, obuf, obuf, obuf],
        out_shape=[jax.ShapeDtypeStruct((rows, GDN_WIDTH), F32),
                   jax.ShapeDtypeStruct(s0.shape, F32)]
                  + [jax.ShapeDtypeStruct((GDN_CONV - 1, nb, GDN_WIDTH), F32)] * 3,
        scratch_shapes=[pltpu.VMEM((GDN_TILE + SUBLANES, 3 * hw), F32)],
        compiler_params=_params("parallel", "arbitrary"),
        name="gdn_slab",
    )(proj, proj, proj, proj, bdt, conv_buf, conv_buf, conv_buf, s0,
      gw["conv_w"], gw["conv_w"], gw["conv_w"], gw["alog_col"], gw["dtb_col"], gw["norm_w"],
      *_gate_matrices(SLAB))


def _block_diag(w):
    per = S5_GROUPS // S5_BLOCKS
    g, a, b = w.shape
    w = w.reshape(S5_BLOCKS, per, a, b)
    eye = jnp.eye(per, dtype=w.dtype)
    return jnp.einsum("jgab,gk->jgakb", w, eye).reshape(S5_BLOCKS, per * a, per * b)


def _layer_weights(l, ln1_g, ln1_b, ffn1_w_in, ffn1_w_out, w_mix_in, s5_lambda_re, s5_lambda_im, s5_log_dt,
                   s5_b_re, s5_b_im, s5_c_re, s5_c_im, s5_d, s5_glu_w, s5_glu_b, gdn_conv_w, gdn_a_log,
                   gdn_dt_bias, gdn_norm_w, w_mix_out, ln2_g, ln2_b, ffn2_w_in, ffn2_w_out, ln3_g, ln3_b):
    row = lambda v: v[l].reshape(1, -1).astype(F32)
    w = {
        "ln1": (row(ln1_g), row(ln1_b)), "ln2": (row(ln2_g), row(ln2_b)), "ln3": (row(ln3_g), row(ln3_b)),
        "ffn1": (ffn1_w_in[l], ffn1_w_out[l]),
        "ffn2": (ffn2_w_in[l], ffn2_w_out[l]),
        "mix_in": (w_mix_in.astype(BF16), l),
        "mix_out": w_mix_out[l].astype(BF16),
    }
    coef, bb_re, bb_im = _s5_disc(
        s5_lambda_re[l].reshape(1, S5_HID).astype(F32), s5_lambda_im[l].reshape(1, S5_HID).astype(F32),
        jnp.repeat(s5_log_dt[l], S5_STATE).reshape(1, S5_HID).astype(F32),
        _block_diag(jnp.swapaxes(s5_b_re[l], 1, 2).astype(F32)),
        _block_diag(jnp.swapaxes(s5_b_im[l], 1, 2).astype(F32)))
    w["s5"] = {
        "coef": coef, "b_re": bb_re, "b_im": bb_im,
        "c_re": _block_diag(jnp.swapaxes(s5_c_re[l], 1, 2)).astype(BF16),
        "c_im": _block_diag(jnp.swapaxes(s5_c_im[l], 1, 2)).astype(BF16),
        "d": row(s5_d), "glu_w": s5_glu_w[l].astype(BF16), "glu_b": row(s5_glu_b),
    }
    w["gdn"] = {
        "conv_w": gdn_conv_w[l].astype(F32),
        "alog_col": gdn_a_log[l].reshape(GDN_HEADS, 1).astype(F32),
        "dtb_col": gdn_dt_bias[l].reshape(GDN_HEADS, 1).astype(F32),
        "norm_w": row(gdn_norm_w),
    }
    return w


def _prompt_layer(x, w, nseq):
    x = _ffn_ln(x, *w["ffn1"], *w["ln1"])
    proj, bdt = _mixin(x, *w["mix_in"])
    z_s5 = jnp.zeros((nseq, 1, S5_HID), F32)
    y_s5, n_re, n_im = _s5_mixer(proj, z_s5, z_s5, w["s5"], chain=True, nseq=nseq)
    z_gdn = jnp.zeros((nseq, GDN_HEADS, GDN_HEAD_DIM, GDN_HEAD_DIM), F32)
    z_buf = jnp.zeros((nseq, GDN_CONV - 1, 3 * GDN_WIDTH), F32)
    y_gdn, n_s, n_buf = _gdn_chain(proj, bdt, z_buf, z_gdn, w["gdn"], nseq=nseq)
    x = _mixout_ln(y_s5, y_gdn, x, w["mix_out"], *w["ln2"])
    x = _ffn_ln(x, *w["ffn2"], *w["ln3"])
    shape = (nseq, S5_GROUPS, S5_STATE)
    return x, n_re.reshape(shape), n_im.reshape(shape), n_s, n_buf


def _sample_layer(x, s5_re, s5_im, gdn_s, conv_buf, w, nb, t):
    x = _ffn_ln(x, *w["ffn1"], *w["ln1"])
    xs = jnp.pad(x.reshape(nb, t, D_MODEL), ((0, 0), (SLAB - t, 0), (0, 0))).reshape(nb * SLAB, D_MODEL)
    proj, bdt = _mixin(xs, *w["mix_in"])
    y_s5, n_re, n_im = _s5_mixer(proj, s5_re.reshape(nb, S5_HID).astype(F32),
                                 s5_im.reshape(nb, S5_HID).astype(F32), w["s5"], chain=False, nseq=nb)
    y_gdn, n_s, bq, bk, bv = _gdn_slab(proj, bdt, jnp.swapaxes(conv_buf.astype(F32), 0, 1),
                                       gdn_s.astype(F32), w["gdn"])
    xs = _mixout_ln(y_s5, y_gdn, xs, w["mix_out"], *w["ln2"])
    x = xs.reshape(nb, SLAB, D_MODEL)[:, SLAB - t:].reshape(nb * t, D_MODEL)
    x = _ffn_ln(x, *w["ffn2"], *w["ln3"])
    shape = (nb, S5_GROUPS, S5_STATE)
    n_buf = jnp.swapaxes(jnp.concatenate([bq, bk, bv], axis=-1), 0, 1)
    return x, n_re.reshape(shape), n_im.reshape(shape), n_s, n_buf


def kernel(x_prompt, x_sample, state_s5_re, state_s5_im, state_gdn, state_conv, ln1_g, ln1_b, ffn1_w_in, ffn1_w_out, w_mix_in, s5_lambda_re, s5_lambda_im, s5_log_dt, s5_b_re, s5_b_im, s5_c_re, s5_c_im, s5_d, s5_glu_w, s5_glu_b, gdn_conv_w, gdn_a_log, gdn_dt_bias, gdn_norm_w, w_mix_out, ln2_g, ln2_b, ffn2_w_in, ffn2_w_out, ln3_g, ln3_b):
    bp, tp, _ = x_prompt.shape
    bs, ts, _ = x_sample.shape
    assert ts == SLAB_REAL and tp % S5_TILE == 0 and (bs * SLAB) % S5_TILE == 0
    depth = ln1_g.shape[0]
    yp = x_prompt.astype(F32).reshape(bp * tp, D_MODEL)
    ys = x_sample.astype(F32).reshape(bs * ts, D_MODEL)
    outs = [[] for _ in range(8)]
    for l in range(depth):
        w = _layer_weights(l, ln1_g, ln1_b, ffn1_w_in, ffn1_w_out, w_mix_in, s5_lambda_re, s5_lambda_im,
                           s5_log_dt, s5_b_re, s5_b_im, s5_c_re, s5_c_im, s5_d, s5_glu_w, s5_glu_b,
                           gdn_conv_w, gdn_a_log, gdn_dt_bias, gdn_norm_w, w_mix_out, ln2_g, ln2_b,
                           ffn2_w_in, ffn2_w_out, ln3_g, ln3_b)
        yp, *p_state = _prompt_layer(yp, w, bp)
        ys, *s_state = _sample_layer(ys, state_s5_re[l], state_s5_im[l], state_gdn[l], state_conv[l], w, bs, ts)
        for acc, val in zip(outs, p_state + s_state):
            acc.append(val)
    return (yp.reshape(x_prompt.shape).astype(x_prompt.dtype), ys.reshape(x_sample.shape).astype(x_sample.dtype),
            *(o[0][None] if depth == 1 else jnp.stack(o) for o in outs))
```

```python
import functools
import math

import jax
import jax.numpy as jnp
from jax import lax
from jax.experimental import pallas as pl
from jax.experimental.pallas import tpu as pltpu

F32 = jnp.float32
BF16 = jnp.bfloat16

D_MODEL = 2048
S5_WIDTH = 1024
S5_GROUPS = 64
S5_STATE = 64
S5_HID = S5_GROUPS * S5_STATE
GDN_WIDTH = 1024
GDN_HEAD_DIM = 128
GDN_HEADS = 8
GDN_CONV = 4
D_FF = 5632
MIX_MAIN = 5120
DEEP_ALPHA = 2.0 ** 0.25
LN_EPS = 1e-5
NORM_EPS = 1e-6

SUBLANES = 8
LANES = 128
SLAB = 8
SLAB_REAL = 4
GDN_TILE = 128
GDN_SLAB_HEADS = 8
GDN_CHAIN_CHUNKS = 2
FFN_TM = 1024
FFN_TF = 256
FFN_CHUNK = 512
S5_TILE = 256
S5_SEG = S5_TILE // SUBLANES
S5_BLOCKS = 8
S5_SCAN_GROUP = 4
VMEM_LIMIT = 56 * 1024 * 1024
FFN_VMEM_LIMIT = 60 * 1024 * 1024

NT_DIMS = (((1,), (1,)), ((), ()))


def _dot(a, b, **kw):
    return jnp.dot(a, b, preferred_element_type=F32, **kw)


def _silu(x):
    return x * jax.nn.sigmoid(x)


def _layer_norm(y, g, b):
    mu = jnp.mean(y, axis=-1, keepdims=True)
    d = y - mu
    var = jnp.mean(d * d, axis=-1, keepdims=True)
    return d * lax.rsqrt(var + LN_EPS) * g + b


def _slab_row_selector(rows, nseq, slab_row, first, *, transpose):
    shape = (rows, nseq) if transpose else (nseq, rows)
    r = lax.broadcasted_iota(jnp.int32, shape, 0 if transpose else 1)
    b = lax.broadcasted_iota(jnp.int32, shape, 1 if transpose else 0)
    return _as_bf16(r == SLAB * (b - first) + slab_row)


def _as_bf16(mask):
    return mask.astype(F32).astype(BF16)


def _split3(x):
    pieces = []
    for _ in range(3):
        pieces.append(x.astype(BF16))
        x = x - pieces[-1].astype(F32)
    return pieces


def _dot_split(a, b, dims=None):
    f32_is_lhs = a.dtype == F32
    acc = None
    for piece in _split3(a if f32_is_lhs else b):
        lhs, rhs = (piece, b) if f32_is_lhs else (a, piece)
        d = _dot(lhs, rhs) if dims is None else lax.dot_general(lhs, rhs, dims, preferred_element_type=F32)
        acc = d if acc is None else acc + d
    return acc


def _params(*sem, vmem_limit=VMEM_LIMIT):
    return pltpu.CompilerParams(dimension_semantics=sem, vmem_limit_bytes=vmem_limit)


def _ffn_kernel(x_ref, wg_ref, wu_ref, wo_ref, g_ref, b_ref, o_ref, *rest):
    *wb_refs, xb_ref = rest
    j = pl.program_id(1)

    @pl.when(j == 0)
    def _():
        o_ref[...] = jnp.zeros_like(o_ref)
        xb_ref[...] = x_ref[...].astype(BF16)

    if wb_refs:
        for src, dst in zip((wg_ref, wu_ref, wo_ref), wb_refs):
            dst[...] = src[...].astype(BF16)
        wg_ref, wu_ref, wo_ref = wb_refs
    xb = xb_ref[...]
    for f in range(0, wg_ref.shape[1], FFN_TF):
        fs = slice(f, f + FFN_TF)
        gate = _dot(xb, wg_ref[:, fs].astype(BF16))
        up = _dot(xb, wu_ref[:, fs].astype(BF16))
        h = (_silu(gate) * up).astype(BF16)
        for c in range(0, D_MODEL, FFN_CHUNK):
            o_ref[:, c:c + FFN_CHUNK] += _dot(h, wo_ref[fs, c:c + FFN_CHUNK].astype(BF16))

    @pl.when(j == pl.num_programs(1) - 1)
    def _():
        for r in range(0, o_ref.shape[0], FFN_CHUNK // 2):
            rows = slice(r, r + FFN_CHUNK // 2)
            y = DEEP_ALPHA * x_ref[rows, :] + 0.5 * o_ref[rows, :]
            o_ref[rows, :] = _layer_norm(y, g_ref[...], b_ref[...])


def _ffn_ln(x, w_gate, w_up, up_col0, w_out, g, b, *, emit_bf16=False):
    n = x.shape[0]
    tm = math.gcd(n, FFN_TM)
    narrow = w_gate.dtype == BF16
    tf = 2 * FFN_TF if (narrow or tm < FFN_TM) else FFN_TF
    nff, up0 = D_FF // tf, up_col0 // tf
    assert not emit_bf16 or (n == tm and not narrow)
    w_gate_spec = pl.BlockSpec((D_MODEL, tf), lambda i, j: (0, j))
    w_out_spec = pl.BlockSpec((tf, D_MODEL), lambda i, j: (j, 0))
    res = pl.pallas_call(
        _ffn_kernel,
        grid=(n // tm, nff),
        in_specs=[
            pl.BlockSpec((tm, D_MODEL), lambda i, j: (i, 0)),
            w_gate_spec,
            pl.BlockSpec((D_MODEL, tf), lambda i, j: (0, j + up0)),
            w_out_spec,
            pl.BlockSpec((1, D_MODEL), lambda i, j: (0, 0)),
            pl.BlockSpec((1, D_MODEL), lambda i, j: (0, 0)),
        ],
        out_specs=[pl.BlockSpec((tm, D_MODEL), lambda i, j: (i, 0))]
                  + ([w_gate_spec, w_gate_spec, w_out_spec] if emit_bf16 else []),
        out_shape=[jax.ShapeDtypeStruct((n, D_MODEL), F32)]
                  + ([jax.ShapeDtypeStruct((D_MODEL, D_FF), BF16)] * 2
                     + [jax.ShapeDtypeStruct((D_FF, D_MODEL), BF16)] if emit_bf16 else []),
        scratch_shapes=[pltpu.VMEM((tm, D_MODEL), BF16)],
        compiler_params=_params("parallel", "arbitrary", vmem_limit=FFN_VMEM_LIMIT),
        name="ffn_ln",
    )(x, w_gate, w_up, w_out, g, b)
    if emit_bf16:
        return res[0], (res[1], res[2], 0, res[3])
    return res[0]


def _mixin_kernel(x_ref, w_ref, wt_ref, o_ref, ot_ref, xb_ref):
    @pl.when(pl.program_id(1) == 0)
    def _():
        xb_ref[...] = x_ref[...].astype(BF16)
        ncols = ot_ref.shape[0]
        lane = lax.broadcasted_iota(jnp.int32, (1, LANES), 1)
        wt = jnp.where(lane < ncols, wt_ref[...], 0.0).astype(BF16)
        ot_ref[...] = _dot(xb_ref[...], wt).T[:ncols, :]

    o_ref[...] = _dot(xb_ref[...], w_ref[...].astype(BF16))


def _mixin(x, w, layer, *, tm=1024, tn=1024):
    n = x.shape[0]
    return pl.pallas_call(
        _mixin_kernel,
        grid=(n // tm, MIX_MAIN // tn),
        in_specs=[pl.BlockSpec((tm, D_MODEL), lambda i, j: (i, 0)),
                  pl.BlockSpec((None, D_MODEL, tn), lambda i, j: (layer, 0, j)),
                  pl.BlockSpec((None, D_MODEL, LANES), lambda i, j: (layer, 0, MIX_MAIN // LANES))],
        out_specs=[pl.BlockSpec((tm, tn), lambda i, j: (i, j)),
                   pl.BlockSpec((2 * GDN_HEADS, tm), lambda i, j: (0, i))],
        out_shape=[jax.ShapeDtypeStruct((n, MIX_MAIN), F32),
                   jax.ShapeDtypeStruct((2 * GDN_HEADS, n), F32)],
        scratch_shapes=[pltpu.VMEM((tm, D_MODEL), BF16)],
        compiler_params=_params("parallel", "arbitrary"),
        name="mix_in",
    )(x, w, w)


def _mixout_kernel(ya_ref, yb_ref, x_ref, w_ref, g_ref, b_ref, o_ref):
    mix = (_dot(ya_ref[...].astype(BF16), w_ref[0:S5_WIDTH, :])
           + _dot(yb_ref[...].astype(BF16), w_ref[S5_WIDTH:, :]))
    o_ref[...] = _layer_norm(DEEP_ALPHA * x_ref[...] + mix, g_ref[...], b_ref[...])


def _mixout_ln(ya, yb, x, w, g, b, *, tm=512):
    n = x.shape[0]
    return pl.pallas_call(
        _mixout_kernel,
        grid=(n // tm,),
        in_specs=[pl.BlockSpec((tm, S5_WIDTH), lambda i: (i, 0)),
                  pl.BlockSpec((tm, GDN_WIDTH), lambda i: (i, 0)),
                  pl.BlockSpec((tm, D_MODEL), lambda i: (i, 0)),
                  pl.BlockSpec((D_MODEL, D_MODEL), lambda i: (0, 0)),
                  pl.BlockSpec((1, D_MODEL), lambda i: (0, 0)),
                  pl.BlockSpec((1, D_MODEL), lambda i: (0, 0))],
        out_specs=pl.BlockSpec((tm, D_MODEL), lambda i: (i, 0)),
        out_shape=jax.ShapeDtypeStruct((n, D_MODEL), F32),
        compiler_params=_params("parallel"),
        name="mix_out_ln",
    )(ya, yb, x, w, g, b)


def _s5_disc_kernel(lre_ref, lim_ref, ldt_ref, bre_ref, bim_ref, coef_ref, bbre_ref, bbim_ref):
    lr, li = lre_ref[...], lim_ref[...]
    dt = jnp.exp(ldt_ref[...])
    mag = jnp.exp(lr * dt)
    ar = mag * jnp.cos(li * dt)
    ai = mag * jnp.sin(li * dt)
    nr, ni = ar - 1.0, ai
    den = lr * lr + li * li
    c_re = (nr * lr + ni * li) / den
    c_im = (ni * lr - nr * li) / den
    bw = S5_HID // S5_BLOCKS
    for j in range(S5_BLOCKS):
        cr, ci = c_re[:, j * bw:(j + 1) * bw], c_im[:, j * bw:(j + 1) * bw]
        bbre_ref[j] = (cr * bre_ref[j] - ci * bim_ref[j]).astype(BF16)
        bbim_ref[j] = (cr * bim_ref[j] + ci * bre_ref[j]).astype(BF16)

    def cmul(x, y):
        return x[0] * y[0] - x[1] * y[1], x[0] * y[1] + x[1] * y[0]

    width = lr.shape[-1]
    p = (ar, ai)
    for _ in range(S5_SEG - 1):
        p = cmul(p, (ar, ai))
    coef_ref[...] = jnp.zeros_like(coef_ref)
    for part, v in enumerate((ar, ai)):
        coef_ref[part] = jnp.broadcast_to(v, (SUBLANES, width))
    for k, s in enumerate((1, 2, 4)):
        for part in range(2):
            coef_ref[2 + 2 * k + part, s:SUBLANES, :] = jnp.broadcast_to(p[part], (SUBLANES - s, width))
        p = cmul(p, p)


def _s5_disc(lre, lim, ldt, b_re, b_im):
    return pl.pallas_call(
        _s5_disc_kernel,
        out_shape=[jax.ShapeDtypeStruct((8, SUBLANES, S5_HID), F32),
                   jax.ShapeDtypeStruct(b_re.shape, BF16), jax.ShapeDtypeStruct(b_im.shape, BF16)],
        name="s5_disc",
    )(lre, lim, ldt, b_re, b_im)


def _gelu_tanh(y):
    return 0.5 * y * (1.0 + jnp.tanh(math.sqrt(2.0 / math.pi) * (y + 0.044715 * (y * y * y))))


def _scan_layout(tt, group):
    p = jnp.arange(tt)
    rem = p % group
    src = (p - rem) + (rem % SUBLANES) * (group // SUBLANES) + rem // SUBLANES
    return (src[:, None] == jnp.arange(tt)[None, :]).astype(BF16)


def _cmul_add(ar, ai, xr, xi, br, bi):
    return ar * xr - ai * xi + br, ar * xi + ai * xr + bi


def _s5_kernel(u_ref, h0re_ref, h0im_ref, perm_ref, back_ref, coef_ref, bre_ref, bim_ref, cwre_ref, cwim_ref,
               d_ref, gw_ref, gb_ref, y_ref, sre_ref, sim_ref, hre, him, ysc, car_re, car_im,
               *, chain, tt):
    group = tt if chain else SLAB * SUBLANES
    seg = group // SUBLANES
    u = u_ref[...]
    ub = _dot(perm_ref[...], u.astype(BF16)).astype(BF16)
    bw = S5_HID // S5_BLOCKS
    gw = S5_WIDTH // S5_BLOCKS
    if chain:
        @pl.when(pl.program_id(1) == 0)
        def _():
            car_re[...] = jnp.broadcast_to(h0re_ref[...], car_re.shape)
            car_im[...] = jnp.broadcast_to(h0im_ref[...], car_im.shape)

    sub = lax.broadcasted_iota(jnp.int32, (SUBLANES, 1), 0)
    def project_in(c):
        uc = ub[:, c * gw:(c + 1) * gw]
        hre[c] = _dot(uc, bre_ref[c])
        him[c] = _dot(uc, bim_ref[c])

    def project_out(c):
        ysc[:, c * gw:(c + 1) * gw] = (_dot(hre[c].astype(BF16), cwre_ref[c])
                                       - _dot(him[c].astype(BF16), cwim_ref[c]))

    def scan(blocks):
        sls = [slice(c * bw, (c + 1) * bw) for c in blocks]
        refs = [(hre.at[c], him.at[c]) for c in blocks]
        coefs = [(coef_ref[0, :, sl], coef_ref[1, :, sl]) for sl in sls]
        n = range(len(blocks))
        if not chain:
            for g in range(tt // group):
                srows = slice(g * SUBLANES, (g + 1) * SUBLANES)
                x = [(h0re_ref[srows, sl], h0im_ref[srows, sl]) for sl in sls]
                for j in range(SLAB - SLAB_REAL, SLAB):
                    rows = slice(g * group + j * SUBLANES, g * group + (j + 1) * SUBLANES)
                    for i in n:
                        hr, hi = refs[i]
                        x[i] = _cmul_add(*coefs[i], *x[i], hr[rows, :], hi[rows, :])
                        hr[rows, :], hi[rows, :] = x[i]
                for i in n:
                    sre_ref[srows, sls[i]], sim_ref[srows, sls[i]] = x[i]
            return
        x = [(jnp.zeros((SUBLANES, bw), F32),) * 2 for _ in n]
        for j in range(seg):
            rows = slice(j * SUBLANES, (j + 1) * SUBLANES)
            for i in n:
                hr, hi = refs[i]
                x[i] = _cmul_add(*coefs[i], *x[i], hr[rows, :], hi[rows, :])
                hr[rows, :], hi[rows, :] = x[i]
        ks = []
        for i in n:
            sl = sls[i]
            kr = jnp.where(sub == 0, car_re[:, sl], pltpu.roll(x[i][0], 1, 0))
            ki = jnp.where(sub == 0, car_im[:, sl], pltpu.roll(x[i][1], 1, 0))
            for k, s in enumerate((1, 2, 4)):
                pr, pi = coef_ref[2 + 2 * k, :, sl], coef_ref[3 + 2 * k, :, sl]
                kr, ki = _cmul_add(pr, pi, pltpu.roll(kr, s, 0), pltpu.roll(ki, s, 0), kr, ki)
            outr, outi = _cmul_add(coef_ref[2, :, sl], coef_ref[3, :, sl], kr, ki, *x[i])
            car_re[:, sl] = jnp.broadcast_to(outr[SUBLANES - 1:SUBLANES, :], outr.shape)
            car_im[:, sl] = jnp.broadcast_to(outi[SUBLANES - 1:SUBLANES, :], outi.shape)
            ks.append((kr, ki))
        for j in range(seg):
            rows = slice(j * SUBLANES, (j + 1) * SUBLANES)
            for i in n:
                (ar, ai), (kr, ki), (hr, hi) = coefs[i], ks[i], refs[i]
                ks[i] = (ar * kr - ai * ki, ar * ki + ai * kr)
                hr[rows, :] += ks[i][0]
                hi[rows, :] += ks[i][1]

    groups = [list(range(c, c + S5_SCAN_GROUP)) for c in range(0, S5_BLOCKS, S5_SCAN_GROUP)]
    for step in range(len(groups) + 2):
        if step < len(groups):
            for c in groups[step]:
                project_in(c)
        if 1 <= step <= len(groups):
            scan(groups[step - 1])
        if step >= 2:
            for c in groups[step - 2]:
                project_out(c)

    if chain:
        @pl.when(pl.program_id(1) == pl.num_programs(1) - 1)
        def _():
            sre_ref[...] = car_re[0:1, :]
            sim_ref[...] = car_im[0:1, :]

    y = _dot_split(back_ref[...], ysc[...])
    z = _gelu_tanh(y + d_ref[...] * u)
    gl = _dot(z.astype(BF16), gw_ref[...]) + gb_ref[...]
    y_ref[...] = z * jax.nn.sigmoid(gl)


def _s5_mixer(proj, h0re, h0im, sw, *, chain, nseq, tt=S5_TILE):
    rows = proj.shape[0]
    const = lambda shape: pl.BlockSpec(shape, lambda *_: (0,) * len(shape))
    if chain:
        assert tt == S5_SEG * SUBLANES
        nt = rows // nseq // tt
        grid = (nseq, nt)
        u_spec = pl.BlockSpec((tt, S5_WIDTH), lambda b, t: (b * nt + t, 0))
        st_spec = pl.BlockSpec((None, 1, S5_HID), lambda b, t: (b, 0, 0))
        st_shape = jax.ShapeDtypeStruct((nseq, 1, S5_HID), F32)
        sem = ("parallel", "arbitrary")
    else:
        grid = (rows // tt,)
        u_spec = pl.BlockSpec((tt, S5_WIDTH), lambda i: (i, 0))
        st_spec = pl.BlockSpec((tt // SLAB, S5_HID), lambda i: (i, 0))
        st_shape = jax.ShapeDtypeStruct((rows // SLAB, S5_HID), F32)
        sem = ("parallel",)
    y_spec = u_spec
    bw = S5_HID // S5_BLOCKS
    gw = S5_WIDTH // S5_BLOCKS
    perm = _scan_layout(tt, tt if chain else SLAB * SUBLANES)
    return pl.pallas_call(
        functools.partial(_s5_kernel, chain=chain, tt=tt),
        grid=grid,
        in_specs=[u_spec, st_spec, st_spec,
                  const((tt, tt)), const((tt, tt)), const((8, SUBLANES, S5_HID)),
                  const((S5_BLOCKS, gw, bw)), const((S5_BLOCKS, gw, bw)),
                  const((S5_BLOCKS, bw, gw)), const((S5_BLOCKS, bw, gw)),
                  const((1, S5_WIDTH)), const((S5_WIDTH, S5_WIDTH)), const((1, S5_WIDTH))],
        out_specs=[y_spec, st_spec, st_spec],
        out_shape=[jax.ShapeDtypeStruct((rows, S5_WIDTH), F32), st_shape, st_shape],
        scratch_shapes=[pltpu.VMEM((S5_BLOCKS, tt, bw), F32), pltpu.VMEM((S5_BLOCKS, tt, bw), F32),
                        pltpu.VMEM((tt, S5_WIDTH), F32),
                        pltpu.VMEM((SUBLANES, S5_HID), F32), pltpu.VMEM((SUBLANES, S5_HID), F32)],
        compiler_params=_params(*sem),
        name="s5_chain" if chain else "s5_slab",
    )(proj, h0re, h0im, perm, perm.T, sw["coef"], sw["b_re"], sw["b_im"], sw["c_re"], sw["c_im"],
      sw["d"], sw["glu_w"], sw["glu_b"])


def _softplus(x):
    return jnp.maximum(x, 0.0) + jnp.log1p(jnp.exp(-jnp.abs(x)))


def _segment_masks(seg):
    ri = lax.broadcasted_iota(jnp.int32, (GDN_TILE, GDN_TILE), 0)
    ci = lax.broadcasted_iota(jnp.int32, (GDN_TILE, GDN_TILE), 1)
    same = (ri // seg) == (ci // seg)
    causal = (ri >= ci) & same
    strict = (ri > ci) & same
    return same, causal, strict


def _gate_matrices(seg):
    i = jnp.arange(GDN_TILE)
    same = (i[:, None] // seg) == (i[None, :] // seg)
    stack = jnp.concatenate([i[:, None] == i[None, :], (i[:, None] >= i[None, :]) & same, same], axis=0)
    return stack.astype(BF16), ((i[:, None] <= i[None, :]) & same).astype(BF16)


def _gdn_gates(bdt, alog_col, dtb_col, stack, upper, seg, slab):
    same, causal, strict = _segment_masks(seg)
    bt = jax.nn.sigmoid(bdt[:GDN_HEADS, :])
    gt = -jnp.exp(alog_col) * _softplus(bdt[GDN_HEADS:, :] + dtb_col)
    if slab:
        creal = (lax.broadcasted_iota(jnp.int32, (1, GDN_TILE), 1) % SLAB) >= SLAB - SLAB_REAL
        bt = jnp.where(creal, bt, 0.0)
        gt = jnp.where(creal, gt, 0.0)
    gates = jnp.concatenate([bt, gt, jnp.zeros((GDN_TILE - 2 * GDN_HEADS, GDN_TILE), F32)], axis=0)
    cols = _dot_split(stack, gates, NT_DIMS)
    beta, gc, gl = cols[:GDN_TILE], cols[GDN_TILE:2 * GDN_TILE], cols[2 * GDN_TILE:]
    gct = _dot_split(gt, upper)
    return beta, gc, gl, gct, causal, strict


def _merge_masks(top):
    ri = lax.broadcasted_iota(jnp.int32, (GDN_TILE, GDN_TILE), 0)
    ci = lax.broadcasted_iota(jnp.int32, (GDN_TILE, GDN_TILE), 1)
    masks = []
    s = 1
    while s < top:
        masks.append(((ri // (2 * s)) == (ci // (2 * s))) & ((ri // s) != (ci // s)))
        s *= 2
    return masks


def _unit_lower_inverse(ms, masks):
    es = [-jnp.where(masks[0], m, 0.0) for m in ms]
    for mask in masks[1:]:
        cs = [jnp.where(mask, m, 0.0) for m in ms]
        ebs = [e.astype(BF16) for e in es]
        xs = [c + _dot(eb, c.astype(BF16)) for c, eb in zip(cs, ebs)]
        es = [e - (x + _dot(x.astype(BF16), eb)) for e, x, eb in zip(es, xs, ebs)]
    return es


def _gdn_intra(qs, ks, vs, betas, gcols, grows, glcols, causal, strict, masks):
    n = range(len(qs))
    qn = [q * lax.rsqrt(jnp.sum(q * q, axis=-1, keepdims=True) + NORM_EPS) * (GDN_HEAD_DIM ** -0.5) for q in qs]
    kn = [k * lax.rsqrt(jnp.sum(k * k, axis=-1, keepdims=True) + NORM_EPS) for k in ks]
    decay = [jnp.exp(jnp.where(causal, gcols[i] - grows[i], -jnp.inf)) for i in n]
    kb = [kn[i] * betas[i] for i in n]
    knb = [k.astype(BF16) for k in kn]
    kk = [lax.dot_general(kb[i].astype(BF16), knb[i], NT_DIMS, preferred_element_type=F32) for i in n]
    ms = [kk[i] * jnp.where(strict, decay[i], 0.0) for i in n]
    eg = [jnp.exp(g) for g in gcols]
    attn = [(lax.dot_general(qn[i].astype(BF16), knb[i], NT_DIMS, preferred_element_type=F32) * decay[i]).astype(BF16)
            for i in n]
    qg = [qn[i] * eg[i] for i in n]
    kdt = [(kn[i] * jnp.exp(glcols[i] - gcols[i])).T.astype(BF16) for i in n]
    rhs = [jnp.concatenate([vs[i] * betas[i], kb[i] * eg[i]], axis=1) for i in n]
    ys = _unit_lower_inverse(ms, masks)
    uw = [rhs[i] + _dot(ys[i].astype(BF16), rhs[i].astype(BF16)) for i in n]
    return [(uw[i][:, :GDN_HEAD_DIM], uw[i][:, GDN_HEAD_DIM:], attn[i], qg[i], kdt[i]) for i in n]


def _gdn_out(o, gate, norm_w):
    o = o * lax.rsqrt(jnp.mean(o * o, axis=-1, keepdims=True) + NORM_EPS) * norm_w
    return o * _silu(gate)


def _conv4(win_ref, cw, width_slice):
    x = win_ref[:, width_slice]
    x1 = pltpu.roll(x, 1, 0)
    z = cw[1:2, :] * x + cw[0:1, :] * x1
    out = cw[3:4, :] * x + cw[2:3, :] * x1 + pltpu.roll(z, 2, 0)
    return out[SUBLANES:, :]


def _gdn_chain_kernel(q_ref, k_ref, v_ref, gate_ref, bdt_ref, buf_ref, s0_ref, cw_ref,
                      alc_ref, dbc_ref, nw_ref, gstack_ref, gupper_ref,
                      y_ref, sout_ref, bufout_ref, win, s_sc):
    t = pl.program_id(1)
    w3 = 3 * GDN_WIDTH

    @pl.when(t == 0)
    def _():
        win[0:SUBLANES, :] = jnp.zeros((SUBLANES, w3), F32)
        win[SUBLANES - 3:SUBLANES, :] = buf_ref[...]
        s_sc[...] = s0_ref[...]

    rows = q_ref.shape[0]
    nchunk = rows // GDN_TILE
    win[SUBLANES:, 0:GDN_WIDTH] = q_ref[...]
    win[SUBLANES:, GDN_WIDTH:2 * GDN_WIDTH] = k_ref[...]
    win[SUBLANES:, 2 * GDN_WIDTH:] = v_ref[...]

    masks = _merge_masks(GDN_TILE)
    heads = range(GDN_HEADS)
    qkv = [[], [], []]
    for part in range(3):
        conv = []
        for h in heads:
            cs = slice(part * GDN_WIDTH + h * GDN_HEAD_DIM, part * GDN_WIDTH + (h + 1) * GDN_HEAD_DIM)
            conv.append(_silu(_conv4(win, cw_ref[:, cs], cs)))
        qkv[part] = [x[c * GDN_TILE:(c + 1) * GDN_TILE] for c in range(nchunk) for x in conv]
    betas, gcols, glcols, grows = [], [], [], []
    for c in range(nchunk):
        beta, gc, gl, gct, causal, strict = _gdn_gates(
            bdt_ref[:, c * GDN_TILE:(c + 1) * GDN_TILE], alc_ref[...], dbc_ref[...], gstack_ref[...],
            gupper_ref[...], GDN_TILE, False)
        betas += [beta[:, h:h + 1] for h in heads]
        gcols += [gc[:, GDN_HEADS + h:GDN_HEADS + h + 1] for h in heads]
        glcols += [gl[:, GDN_HEADS + h:GDN_HEADS + h + 1] for h in heads]
        grows += [gct[h:h + 1, :] for h in heads]
    intra = _gdn_intra(qkv[0], qkv[1], qkv[2], betas, gcols, grows, glcols, causal, strict, masks)
    for c in range(nchunk):
        at = lambda h, c=c: c * GDN_HEADS + h
        crow = slice(c * GDN_TILE, (c + 1) * GDN_TILE)
        s_old = [s_sc[h] for h in heads]
        a = [_dot(jnp.concatenate([intra[at(h)][1], intra[at(h)][3]], axis=0).astype(BF16), s_old[h].astype(BF16))
             for h in heads]
        vnb = [(intra[at(h)][0] - a[h][:GDN_TILE]).astype(BF16) for h in heads]
        o = [a[h][GDN_TILE:] + _dot(intra[at(h)][2], vnb[h]) for h in heads]
        for h in heads:
            s_sc[h] = s_old[h] * jnp.exp(glcols[at(h)][0:1, :]) + _dot(intra[at(h)][4], vnb[h])
        for h in heads:
            hs = slice(h * GDN_HEAD_DIM, (h + 1) * GDN_HEAD_DIM)
            y_ref[crow, hs] = _gdn_out(o[h], gate_ref[crow, hs], nw_ref[...])

    win[0:SUBLANES, :] = win[rows:rows + SUBLANES, :]

    @pl.when(t == pl.num_programs(1) - 1)
    def _():
        sout_ref[...] = s_sc[...]
        bufout_ref[...] = win[rows + SUBLANES - 3:rows + SUBLANES, :]


def _gdn_chain(proj, bdt, conv_buf, s0, gw, *, nseq):
    rows = proj.shape[0]
    tr = GDN_CHAIN_CHUNKS * GDN_TILE
    nt = rows // nseq // tr
    w3 = 3 * GDN_WIDTH
    const = lambda shape: pl.BlockSpec(shape, lambda *_: (0,) * len(shape))
    col = lambda cb: pl.BlockSpec((tr, GDN_WIDTH), lambda b, t, cb=cb: (b * nt + t, cb))
    return pl.pallas_call(
        _gdn_chain_kernel,
        grid=(nseq, nt),
        in_specs=[col(1), col(2), col(3), col(4),
                  pl.BlockSpec((2 * GDN_HEADS, tr), lambda b, t: (0, b * nt + t)),
                  pl.BlockSpec((None, GDN_CONV - 1, w3), lambda b, t: (b, 0, 0)),
                  pl.BlockSpec((None, GDN_HEADS, GDN_HEAD_DIM, GDN_HEAD_DIM), lambda b, t: (b, 0, 0, 0)),
                  const((GDN_CONV, w3)), const((GDN_HEADS, 1)), const((GDN_HEADS, 1)), const((1, GDN_HEAD_DIM)),
                  const((3 * GDN_TILE, GDN_TILE)), const((GDN_TILE, GDN_TILE))],
        out_specs=[pl.BlockSpec((tr, GDN_WIDTH), lambda b, t: (b * nt + t, 0)),
                   pl.BlockSpec((None, GDN_HEADS, GDN_HEAD_DIM, GDN_HEAD_DIM), lambda b, t: (b, 0, 0, 0)),
                   pl.BlockSpec((None, GDN_CONV - 1, w3), lambda b, t: (b, 0, 0))],
        out_shape=[jax.ShapeDtypeStruct((rows, GDN_WIDTH), F32),
                   jax.ShapeDtypeStruct((nseq, GDN_HEADS, GDN_HEAD_DIM, GDN_HEAD_DIM), F32),
                   jax.ShapeDtypeStruct((nseq, GDN_CONV - 1, w3), F32)],
        scratch_shapes=[pltpu.VMEM((tr + SUBLANES, w3), F32),
                        pltpu.VMEM((GDN_HEADS, GDN_HEAD_DIM, GDN_HEAD_DIM), F32)],
        compiler_params=_params("parallel", "arbitrary"),
        name="gdn_chain",
    )(proj, proj, proj, proj, bdt, conv_buf, s0, gw["conv_w"], gw["alog_col"], gw["dtb_col"], gw["norm_w"],
      *_gate_matrices(GDN_TILE))


def _gdn_slab_kernel(q_ref, k_ref, v_ref, gate_ref, bdt_ref, bq_ref, bk_ref, bv_ref, s0_ref,
                     cwq_ref, cwk_ref, cwv_ref, alc_ref, dbc_ref, nw_ref, gstack_ref, gupper_ref,
                     y_ref, sout_ref, oq_ref, ok_ref, ov_ref,
                     win):
    hp = s0_ref.shape[1]
    hw = hp * GDN_HEAD_DIM
    heads = range(hp)
    head0 = pl.program_id(1) * hp
    nslab = GDN_TILE // SLAB
    first = (pl.program_id(0) % (bq_ref.shape[1] // nslab)) * nslab
    real = (lax.broadcasted_iota(jnp.int32, (GDN_TILE, 1), 0) % SLAB) >= SLAB - SLAB_REAL
    qkv = []
    taps = range(GDN_CONV - 1)
    place = [_slab_row_selector(GDN_TILE, bq_ref.shape[1], 1 + i, first, transpose=True) for i in taps]
    take = [_slab_row_selector(GDN_TILE, nslab, SLAB - 3 + i, 0, transpose=False) for i in taps]
    for part, (x_ref, b_ref, cw_ref, o_ref) in enumerate(
            ((q_ref, bq_ref, cwq_ref, oq_ref), (k_ref, bk_ref, cwk_ref, ok_ref), (v_ref, bv_ref, cwv_ref, ov_ref))):
        cs = slice(part * hw, (part + 1) * hw)
        x = x_ref[...]
        for i in taps:
            x = x + _dot_split(place[i], b_ref[i])
        win[0:SUBLANES, cs] = jnp.zeros((SUBLANES, hw), F32)
        win[SUBLANES:, cs] = x
        conv = jnp.where(real, _silu(_conv4(win, cw_ref[...], cs)), 0.0)
        qkv.append([conv[:, h * GDN_HEAD_DIM:(h + 1) * GDN_HEAD_DIM] for h in heads])
        pieces = _split3(x)
        for i in taps:
            o_ref[i] = sum(_dot(take[i], p) for p in pieces)

    beta, gc, gl, gct, causal, strict = _gdn_gates(bdt_ref[...], alc_ref[...], dbc_ref[...], gstack_ref[...],
                                                   gupper_ref[...], SLAB, True)
    lane = lax.broadcasted_iota(jnp.int32, (1, LANES), 1)
    sub = lax.broadcasted_iota(jnp.int32, (SUBLANES, 1), 0)
    pick = lambda a, idx: jnp.sum(jnp.where(lane == idx, a, 0.0), axis=1, keepdims=True)
    gcols = [pick(gc, GDN_HEADS + head0 + h) for h in heads]
    glcols = [pick(gl, GDN_HEADS + head0 + h) for h in heads]
    intra = _gdn_intra(qkv[0], qkv[1], qkv[2], [pick(beta, head0 + h) for h in heads], gcols,
                       [jnp.sum(jnp.where(sub == head0 + h, gct, 0.0), axis=0, keepdims=True) for h in heads],
                       glcols, causal, strict, _merge_masks(SLAB_REAL))
    for h in heads:
        u, wk, attn, qg, kdt = intra[h]
        res = []
        for i in range(nslab):
            rows = slice(i * SLAB, (i + 1) * SLAB)
            lhs = jnp.concatenate([wk[rows], qg[rows]], axis=0).astype(BF16)
            res.append(_dot(lhs, s0_ref[i, h].astype(BF16)))
        vnb = (u - jnp.concatenate([r[:SLAB] for r in res], axis=0)).astype(BF16)
        o = jnp.concatenate([r[SLAB:] for r in res], axis=0) + _dot(attn, vnb)
        hs = slice(h * GDN_HEAD_DIM, (h + 1) * GDN_HEAD_DIM)
        y_ref[:, hs] = _gdn_out(o, gate_ref[:, hs], nw_ref[...])
        egl = jnp.exp(glcols[h])
        for i in range(nslab):
            in_slab = (lane // SLAB) == i
            upd = _dot(jnp.where(in_slab, kdt, jnp.zeros_like(kdt)), vnb)
            sout_ref[i, h] = s0_ref[i, h] * egl[i * SLAB:i * SLAB + 1, :] + upd


def _gdn_slab(proj, bdt, conv_buf, s0, gw, *, hp=GDN_SLAB_HEADS):
    rows = proj.shape[0]
    nb = rows // SLAB
    nslab = GDN_TILE // SLAB
    hw = hp * GDN_HEAD_DIM
    hb = GDN_WIDTH // hw
    const = lambda shape: pl.BlockSpec(shape, lambda *_: (0,) * len(shape))
    col = lambda g: pl.BlockSpec((GDN_TILE, hw), lambda i, h, g=g: (i, g * hb + h))
    per = LANES // nslab
    buf = lambda g: pl.BlockSpec((GDN_CONV - 1, LANES, hw), lambda i, h, g=g: (0, i // per, g * hb + h))
    cwb = lambda g: pl.BlockSpec((GDN_CONV, hw), lambda i, h, g=g: (0, g * hb + h))
    st = pl.BlockSpec((nslab, hp, GDN_HEAD_DIM, GDN_HEAD_DIM), lambda i, h: (i, h, 0, 0))
    obuf = pl.BlockSpec((GDN_CONV - 1, nslab, hw), lambda i, h: (0, i, h))
    return pl.pallas_call(
        _gdn_slab_kernel,
        grid=(rows // GDN_TILE, GDN_HEADS // hp),
        in_specs=[col(1), col(2), col(3), col(4),
                  pl.BlockSpec((2 * GDN_HEADS, GDN_TILE), lambda i, h: (0, i)),
                  buf(0), buf(1), buf(2), st, cwb(0), cwb(1), cwb(2),
                  const((GDN_HEADS, 1)), const((GDN_HEADS, 1)), const((1, GDN_HEAD_DIM)),
                  const((3 * GDN_TILE, GDN_TILE)), const((GDN_TILE, GDN_TILE))],
        out_specs=[pl.BlockSpec((GDN_TILE, hw), lambda i, h: (i, h)), st, obuf, obuf, obuf],
        out_shape=[jax.ShapeDtypeStruct((rows, GDN_WIDTH), F32),
                   jax.ShapeDtypeStruct(s0.shape, F32)]
                  + [jax.ShapeDtypeStruct((GDN_CONV - 1, nb, GDN_WIDTH), F32)] * 3,
        scratch_shapes=[pltpu.VMEM((GDN_TILE + SUBLANES, 3 * hw), F32)],
        compiler_params=_params("parallel", "arbitrary"),
        name="gdn_slab",
    )(proj, proj, proj, proj, bdt, conv_buf, conv_buf, conv_buf, s0,
      gw["conv_w"], gw["conv_w"], gw["conv_w"], gw["alog_col"], gw["dtb_col"], gw["norm_w"],
      *_gate_matrices(SLAB))


def _block_diag(w):
    per = S5_GROUPS // S5_BLOCKS
    g, a, b = w.shape
    w = w.reshape(S5_BLOCKS, per, a, b)
    eye = jnp.eye(per, dtype=w.dtype)
    return jnp.einsum("jgab,gk->jgakb", w, eye).reshape(S5_BLOCKS, per * a, per * b)


def _layer_weights(l, ln1_g, ln1_b, ffn1_w_in, ffn1_w_out, w_mix_in, s5_lambda_re, s5_lambda_im, s5_log_dt,
                   s5_b_re, s5_b_im, s5_c_re, s5_c_im, s5_d, s5_glu_w, s5_glu_b, gdn_conv_w, gdn_a_log,
                   gdn_dt_bias, gdn_norm_w, w_mix_out, ln2_g, ln2_b, ffn2_w_in, ffn2_w_out, ln3_g, ln3_b):
    row = lambda v: v[l].reshape(1, -1).astype(F32)
    w = {
        "ln1": (row(ln1_g), row(ln1_b)), "ln2": (row(ln2_g), row(ln2_b)), "ln3": (row(ln3_g), row(ln3_b)),
        "ffn1": (ffn1_w_in[l], ffn1_w_in[l], D_FF, ffn1_w_out[l]),
        "ffn2": (ffn2_w_in[l], ffn2_w_in[l], D_FF, ffn2_w_out[l]),
        "mix_in": (w_mix_in.astype(BF16), l),
        "mix_out": w_mix_out[l].astype(BF16),
    }
    coef, bb_re, bb_im = _s5_disc(
        s5_lambda_re[l].reshape(1, S5_HID).astype(F32), s5_lambda_im[l].reshape(1, S5_HID).astype(F32),
        jnp.repeat(s5_log_dt[l], S5_STATE).reshape(1, S5_HID).astype(F32),
        _block_diag(jnp.swapaxes(s5_b_re[l], 1, 2).astype(F32)),
        _block_diag(jnp.swapaxes(s5_b_im[l], 1, 2).astype(F32)))
    w["s5"] = {
        "coef": coef, "b_re": bb_re, "b_im": bb_im,
        "c_re": _block_diag(jnp.swapaxes(s5_c_re[l], 1, 2)).astype(BF16),
        "c_im": _block_diag(jnp.swapaxes(s5_c_im[l], 1, 2)).astype(BF16),
        "d": row(s5_d), "glu_w": s5_glu_w[l].astype(BF16), "glu_b": row(s5_glu_b),
    }
    w["gdn"] = {
        "conv_w": gdn_conv_w[l].astype(F32),
        "alog_col": gdn_a_log[l].reshape(GDN_HEADS, 1).astype(F32),
        "dtb_col": gdn_dt_bias[l].reshape(GDN_HEADS, 1).astype(F32),
        "norm_w": row(gdn_norm_w),
    }
    return w


def _prompt_layer(x, w, nseq, ffn1_bf16, ffn2_bf16):
    x = _ffn_ln(x, *ffn1_bf16, *w["ln1"])
    proj, bdt = _mixin(x, *w["mix_in"])
    z_s5 = jnp.zeros((nseq, 1, S5_HID), F32)
    y_s5, n_re, n_im = _s5_mixer(proj, z_s5, z_s5, w["s5"], chain=True, nseq=nseq)
    z_gdn = jnp.zeros((nseq, GDN_HEADS, GDN_HEAD_DIM, GDN_HEAD_DIM), F32)
    z_buf = jnp.zeros((nseq, GDN_CONV - 1, 3 * GDN_WIDTH), F32)
    y_gdn, n_s, n_buf = _gdn_chain(proj, bdt, z_buf, z_gdn, w["gdn"], nseq=nseq)
    x = _mixout_ln(y_s5, y_gdn, x, w["mix_out"], *w["ln2"])
    x = _ffn_ln(x, *ffn2_bf16, *w["ln3"])
    shape = (nseq, S5_GROUPS, S5_STATE)
    return x, n_re.reshape(shape), n_im.reshape(shape), n_s, n_buf


def _sample_layer(x, s5_re, s5_im, gdn_s, conv_buf, w, nb, t):
    x, ffn1_bf16 = _ffn_ln(x, *w["ffn1"], *w["ln1"], emit_bf16=True)
    xs = jnp.pad(x.reshape(nb, t, D_MODEL), ((0, 0), (SLAB - t, 0), (0, 0))).reshape(nb * SLAB, D_MODEL)
    proj, bdt = _mixin(xs, *w["mix_in"])
    y_s5, n_re, n_im = _s5_mixer(proj, s5_re.reshape(nb, S5_HID).astype(F32),
                                 s5_im.reshape(nb, S5_HID).astype(F32), w["s5"], chain=False, nseq=nb)
    y_gdn, n_s, bq, bk, bv = _gdn_slab(proj, bdt, jnp.swapaxes(conv_buf.astype(F32), 0, 1),
                                       gdn_s.astype(F32), w["gdn"])
    xs = _mixout_ln(y_s5, y_gdn, xs, w["mix_out"], *w["ln2"])
    x = xs.reshape(nb, SLAB, D_MODEL)[:, SLAB - t:].reshape(nb * t, D_MODEL)
    x, ffn2_bf16 = _ffn_ln(x, *w["ffn2"], *w["ln3"], emit_bf16=True)
    shape = (nb, S5_GROUPS, S5_STATE)
    n_buf = jnp.swapaxes(jnp.concatenate([bq, bk, bv], axis=-1), 0, 1)
    return (x, n_re.reshape(shape), n_im.reshape(shape), n_s, n_buf), (ffn1_bf16, ffn2_bf16)


def kernel(x_prompt, x_sample, state_s5_re, state_s5_im, state_gdn, state_conv, ln1_g, ln1_b, ffn1_w_in, ffn1_w_out, w_mix_in, s5_lambda_re, s5_lambda_im, s5_log_dt, s5_b_re, s5_b_im, s5_c_re, s5_c_im, s5_d, s5_glu_w, s5_glu_b, gdn_conv_w, gdn_a_log, gdn_dt_bias, gdn_norm_w, w_mix_out, ln2_g, ln2_b, ffn2_w_in, ffn2_w_out, ln3_g, ln3_b):
    bp, tp, _ = x_prompt.shape
    bs, ts, _ = x_sample.shape
    assert ts == SLAB_REAL and tp % S5_TILE == 0 and (bs * SLAB) % S5_TILE == 0
    depth = ln1_g.shape[0]
    yp = x_prompt.astype(F32).reshape(bp * tp, D_MODEL)
    ys = x_sample.astype(F32).reshape(bs * ts, D_MODEL)
    outs = [[] for _ in range(8)]
    for l in range(depth):
        w = _layer_weights(l, ln1_g, ln1_b, ffn1_w_in, ffn1_w_out, w_mix_in, s5_lambda_re, s5_lambda_im,
                           s5_log_dt, s5_b_re, s5_b_im, s5_c_re, s5_c_im, s5_d, s5_glu_w, s5_glu_b,
                           gdn_conv_w, gdn_a_log, gdn_dt_bias, gdn_norm_w, w_mix_out, ln2_g, ln2_b,
                           ffn2_w_in, ffn2_w_out, ln3_g, ln3_b)
        (ys, *s_state), ffn_bf16 = _sample_layer(ys, state_s5_re[l], state_s5_im[l], state_gdn[l], state_conv[l],
                                                 w, bs, ts)
        yp, *p_state = _prompt_layer(yp, w, bp, *ffn_bf16)
        for acc, val in zip(outs, p_state + s_state):
            acc.append(val)
    return (yp.reshape(x_prompt.shape).astype(x_prompt.dtype), ys.reshape(x_sample.shape).astype(x_sample.dtype),
            *(o[0][None] if depth == 1 else jnp.stack(o) for o in outs))
```
